```python
import math
import jax
import jax.numpy as jnp
from jax import lax
import numpy as np

D_MODEL = 1024
BATCH = 8
SEQ = 4096
DEPTH = 4

GRID_W = 64
CTX_LEN = 256
N_MIXERS = 3
MIX_SHORTCONV = 0
MIX_HYENA = 1
MIX_ATTN = 2
CONV_W = 3
EPS = 1e-6
HY_ORDER = 2
HY_EMB = 33
HY_BANDS = (HY_EMB - 1) // 2
HY_FILTER_HIDDEN = 64
HY_FAST_DECAY = 0.3
HY_SLOW_DECAY = 1.5
HY_TARGET = 1e-2
HEAD_DIM = 128
N_Q_HEADS = D_MODEL // HEAD_DIM
N_KV_HEADS = 2
Q_PER_KV = N_Q_HEADS // N_KV_HEADS
AXIS_DIM = HEAD_DIM // 2
ROPE_THETA = 10000.0
Q_BLOCK = 128
N_EXPERTS = 16
EC_CAPACITY_FACTOR = 2
D_EXPERT = D_MODEL

kernel_name = "hybrid_conv_hyena_gqa_ec_moe_dit"


def _layers_of(kind):
    return list(range(kind, DEPTH, N_MIXERS))


def rms_norm(x, g):
    xf = x.astype(jnp.float32)
    y = xf * lax.rsqrt(jnp.mean(xf * xf, axis=-1, keepdims=True) + EPS)
    return (y * g.astype(jnp.float32)).astype(x.dtype)


def modulate(x, shift, scale):
    return x * (1.0 + scale) + shift


def dwconv_centred(x, w):
    L = x.shape[1]
    p = CONV_W // 2
    xp = jnp.pad(x, ((0, 0), (p, p), (0, 0)))
    y = xp[:, 0:L] * w[0]
    for k in range(1, CONV_W):
        y = y + xp[:, k:k + L] * w[k]
    return y


def short_conv_mixer(h, w_in, conv_w, w_out):
    gate_b, gate_c, v = jnp.split(h @ w_in, 3, axis=-1)
    return (gate_b * dwconv_centred(gate_c * v, conv_w)) @ w_out


def hyena_filter_spectrum(L, w1, b1, w2, b2, w3, sin_freq):
    f32 = jnp.float32
    t = jnp.linspace(0.0, 1.0, L, dtype=f32)[:, None]
    bands = jnp.linspace(1e-4, HY_BANDS - 1, HY_BANDS, dtype=f32)[None, :]
    ang = (2.0 * math.pi / L) * jnp.arange(L, dtype=f32)[:, None] * bands
    z = jnp.concatenate([t, jnp.cos(ang), -jnp.sin(ang)], axis=-1)
    freq = sin_freq.astype(f32)
    a = jnp.sin(freq * (z @ w1.astype(f32) + b1.astype(f32)))
    a = jnp.sin(freq * (a @ w2.astype(f32) + b2.astype(f32)))
    h = (a @ w3.astype(f32)).reshape(L, 2, HY_ORDER, D_MODEL)
    deltas = jnp.abs(jnp.linspace(math.log(HY_TARGET) / HY_SLOW_DECAY,
                                  math.log(HY_TARGET) / HY_FAST_DECAY, D_MODEL, dtype=f32))
    h = h * jnp.exp(-t * deltas[None, :])[:, None, None, :]
    fwd = h[:, 0]
    bwd = h[1:, 1][::-1]
    full = jnp.concatenate([fwd, jnp.zeros((1, HY_ORDER, D_MODEL), f32), bwd], axis=0)
    full = full / jnp.sum(jnp.abs(full), axis=0, keepdims=True)
    return jnp.fft.rfft(full, axis=0)


def long_conv(u, filt_f, skip):
    L = u.shape[1]
    uf = u.astype(jnp.float32)
    y = jnp.fft.irfft(jnp.fft.rfft(uf, n=2 * L, axis=1) * filt_f[None], n=2 * L, axis=1)[:, :L]
    return (y + uf * skip.astype(jnp.float32)).astype(u.dtype)


def hyena_mixer(h, w_in, conv_w, f_w1, f_b1, f_w2, f_b2, f_w3, sin_freq, skip, w_out):
    L = h.shape[1]
    parts = jnp.split(dwconv_centred(h @ w_in, conv_w), HY_ORDER + 1, axis=-1)
    z, gates = parts[0], parts[1:]
    filt = hyena_filter_spectrum(L, f_w1, f_b1, f_w2, f_b2, f_w3, sin_freq)
    for o in range(HY_ORDER):
        z = gates[o] * long_conv(z, filt[:, o], skip[o])
    return z @ w_out


def axial_rope_tables(n_tokens, dtype):
    f32 = jnp.float32
    rows = n_tokens // GRID_W
    row = jnp.repeat(jnp.arange(rows, dtype=f32), GRID_W)
    col = jnp.tile(jnp.arange(GRID_W, dtype=f32), rows)
    inv = ROPE_THETA ** (-jnp.arange(0, AXIS_DIM, 2, dtype=f32) / AXIS_DIM)

    def axis_angles(pos):
        a = pos[:, None] * inv[None, :]
        return jnp.concatenate([a, a], axis=-1)

    ang = jnp.concatenate([axis_angles(row), axis_angles(col)], axis=-1)
    return jnp.cos(ang).astype(dtype), jnp.sin(ang).astype(dtype)


def _rotate_axial(x):
    q = AXIS_DIM // 2
    xr, xc = x[..., :AXIS_DIM], x[..., AXIS_DIM:]
    return jnp.concatenate([-xr[..., q:], xr[..., :q], -xc[..., q:], xc[..., :q]], axis=-1)


def apply_rope(x, cos, sin):
    return x * cos + _rotate_axial(x) * sin


def _attend(q, k, v):
    s = jnp.einsum('bqkgd,bskd->bkgqs', q, k, preferred_element_type=jnp.float32) * (HEAD_DIM ** -0.5)
    p = jax.nn.softmax(s, axis=-1)
    return jnp.einsum('bkgqs,bskd->bqkgd', p.astype(v.dtype), v)


def gqa_mixer(h_lat, h_ctx, w_qkv, q_g, k_g, w_o, ctx_out):
    B, L, _ = h_lat.shape
    Lc = h_ctx.shape[1]
    QD = N_Q_HEADS * HEAD_DIM
    KVD = N_KV_HEADS * HEAD_DIM

    def heads_q(t, n):
        return rms_norm(t.reshape(B, n, N_KV_HEADS, Q_PER_KV, HEAD_DIM), q_g)

    def heads_kv(t, n):
        k = rms_norm(t[..., :KVD].reshape(B, n, N_KV_HEADS, HEAD_DIM), k_g)
        v = t[..., KVD:].reshape(B, n, N_KV_HEADS, HEAD_DIM)
        return k, v

    qkv_l = h_lat @ w_qkv
    q_l = heads_q(qkv_l[..., :QD], L)
    k_l, v_l = heads_kv(qkv_l[..., QD:], L)
    cos, sin = axial_rope_tables(L, q_l.dtype)
    q_l = apply_rope(q_l, cos[None, :, None, None], sin[None, :, None, None])
    k_l = apply_rope(k_l, cos[None, :, None], sin[None, :, None])
    if ctx_out:
        qkv_c = h_ctx @ w_qkv
        q_c = heads_q(qkv_c[..., :QD], Lc)
        kv_c = qkv_c[..., QD:]
    else:
        kv_c = h_ctx @ w_qkv[:, QD:]
    k_c, v_c = heads_kv(kv_c, Lc)

    k_all = jnp.concatenate([k_c, k_l], axis=1)
    v_all = jnp.concatenate([v_c, v_l], axis=1)
    n_blk = L // Q_BLOCK
    qb = jnp.moveaxis(q_l.reshape(B, n_blk, Q_BLOCK, N_KV_HEADS, Q_PER_KV, HEAD_DIM), 1, 0)
    o_l = lax.map(lambda qblk: _attend(qblk, k_all, v_all), qb)
    y_l = jnp.moveaxis(o_l, 0, 1).reshape(B, L, QD) @ w_o
    y_c = _attend(q_c, k_c, v_c).reshape(B, Lc, QD) @ w_o if ctx_out else None
    return y_l, y_c


def expert_choice_ffn(h, w_router, w_gate, w_up, w_down):
    B, N, D = h.shape
    cap = EC_CAPACITY_FACTOR * N // N_EXPERTS
    aff = jax.nn.softmax((h @ w_router).astype(jnp.float32), axis=-1)
    g, idx = lax.top_k(jnp.swapaxes(aff, 1, 2), cap)
    xs = jax.vmap(lambda hb, ib: hb[ib])(h, idx)
    hid = jax.nn.silu(jnp.einsum('becd,edf->becf', xs, w_gate)) * jnp.einsum('becd,edf->becf', xs, w_up)
    ys = jnp.einsum('becf,efd->becd', hid, w_down) * g[..., None].astype(h.dtype)
    return jax.vmap(lambda yb, ib: jnp.zeros((N, D), yb.dtype).at[ib.reshape(-1)].add(yb.reshape(-1, D)))(ys, idx)


def setup_inputs(seed: int = 0) -> dict:
    key = jax.random.key(seed)
    ks = iter(jax.random.split(key, 40))
    D, F = D_MODEL, D_EXPERT
    nA, nB, nC = (len(_layers_of(k)) for k in range(N_MIXERS))
    QKV = (N_Q_HEADS + 2 * N_KV_HEADS) * HEAD_DIM

    def nrm(shape, s):
        return jax.random.normal(next(ks), shape, jnp.float32) * s

    return {
        "x": nrm((BATCH, SEQ, D), 1.0),
        "c": nrm((BATCH, D), 1.0),
        "ctx": nrm((BATCH, CTX_LEN, D), 1.0),
        "c_ctx": nrm((D,), 1.0),
        "ada_w": nrm((DEPTH, D, 6 * D), 0.5 * D ** -0.5),
        "ada_b": nrm((DEPTH, 6 * D), 0.01),
        "norm_g": 1.0 + nrm((DEPTH, 2, D), 0.02),
        "sc_w_in": nrm((nA, D, 3 * D), D ** -0.5),
        "sc_conv": nrm((nA, CONV_W, D), CONV_W ** -0.5),
        "sc_w_out": nrm((nA, D, D), D ** -0.5),
        "hy_w_in": nrm((nB, D, (HY_ORDER + 1) * D), D ** -0.5),
        "hy_conv": nrm((nB, CONV_W, (HY_ORDER + 1) * D), CONV_W ** -0.5),
        "hy_f_w1": nrm((nB, HY_EMB, HY_FILTER_HIDDEN), HY_EMB ** -0.5),
        "hy_f_b1": nrm((nB, HY_FILTER_HIDDEN), 0.02),
        "hy_f_w2": nrm((nB, HY_FILTER_HIDDEN, HY_FILTER_HIDDEN), HY_FILTER_HIDDEN ** -0.5),
        "hy_f_b2": nrm((nB, HY_FILTER_HIDDEN), 0.02),
        "hy_f_w3": nrm((nB, HY_FILTER_HIDDEN, 2 * HY_ORDER * D), HY_FILTER_HIDDEN ** -0.5),
        "hy_sin_freq": 1.0 + nrm((nB, HY_FILTER_HIDDEN), 0.02),
        "hy_skip": nrm((nB, HY_ORDER, D), 1.0),
        "hy_w_out": nrm((nB, D, D), D ** -0.5),
        "at_w_qkv": nrm((nC, D, QKV), D ** -0.5),
        "at_q_g": 1.0 + nrm((nC, HEAD_DIM), 0.02),
        "at_k_g": 1.0 + nrm((nC, HEAD_DIM), 0.02),
        "at_w_o": nrm((nC, N_Q_HEADS * HEAD_DIM, D), (N_Q_HEADS * HEAD_DIM) ** -0.5),
        "moe_router": nrm((DEPTH, D, N_EXPERTS), D ** -0.5),
        "moe_w_gate": nrm((DEPTH, N_EXPERTS, D, F), D ** -0.5),
        "moe_w_up": nrm((DEPTH, N_EXPERTS, D, F), D ** -0.5),
        "moe_w_down": nrm((DEPTH, N_EXPERTS, F, D), F ** -0.5),
    }


def reference(x, c, ctx, c_ctx, ada_w, ada_b, norm_g, sc_w_in, sc_conv, sc_w_out,
              hy_w_in, hy_conv, hy_f_w1, hy_f_b1, hy_f_w2, hy_f_b2, hy_f_w3, hy_sin_freq, hy_skip, hy_w_out,
              at_w_qkv, at_q_g, at_k_g, at_w_o, moe_router, moe_w_gate, moe_w_up, moe_w_down):
    last_attn = max(_layers_of(MIX_ATTN), default=-1)
    lat, cx = x, ctx
    silu_c = jax.nn.silu(c)
    silu_cc = jax.nn.silu(c_ctx)
    for i in range(DEPTH):
        kind = i % N_MIXERS
        j = i // N_MIXERS
        need_ctx = i <= last_attn
        upd_ctx = i < last_attn

        m_l = (silu_c @ ada_w[i] + ada_b[i])[:, None, :]
        sh1, sc1, g1, sh2, sc2, g2 = jnp.split(m_l, 6, axis=-1)
        hl = modulate(rms_norm(lat, norm_g[i, 0]), sh1, sc1)
        if need_ctx:
            m_c = silu_cc @ ada_w[i] + ada_b[i]
            csh1, csc1, cg1, csh2, csc2, cg2 = jnp.split(m_c, 6, axis=-1)
            hc = modulate(rms_norm(cx, norm_g[i, 0]), csh1, csc1)

        yc = None
        if kind == MIX_SHORTCONV:
            yl = short_conv_mixer(hl, sc_w_in[j], sc_conv[j], sc_w_out[j])
            if upd_ctx:
                yc = short_conv_mixer(hc, sc_w_in[j], sc_conv[j], sc_w_out[j])
        elif kind == MIX_HYENA:
            hy_args = (hy_w_in[j], hy_conv[j], hy_f_w1[j], hy_f_b1[j], hy_f_w2[j], hy_f_b2[j],
                       hy_f_w3[j], hy_sin_freq[j], hy_skip[j], hy_w_out[j])
            yl = hyena_mixer(hl, *hy_args)
            if upd_ctx:
                yc = hyena_mixer(hc, *hy_args)
        else:
            yl, yc = gqa_mixer(hl, hc, at_w_qkv[j], at_q_g[j], at_k_g[j], at_w_o[j], upd_ctx)

        lat = lat + g1 * yl
        hl2 = modulate(rms_norm(lat, norm_g[i, 1]), sh2, sc2)
        lat = lat + g2 * expert_choice_ffn(hl2, moe_router[i], moe_w_gate[i], moe_w_up[i], moe_w_down[i])
        if upd_ctx:
            cx = cx + cg1 * yc
            hc2 = modulate(rms_norm(cx, norm_g[i, 1]), csh2, csc2)
            cx = cx + cg2 * expert_choice_ffn(hc2, moe_router[i], moe_w_gate[i], moe_w_up[i], moe_w_down[i])
    return lat
```

```python
import functools
import math

import jax
import jax.numpy as jnp
import numpy as np
from jax import lax
from jax.experimental import pallas as pl
from jax.experimental.pallas import tpu as pltpu

F32 = jnp.float32
BF16 = jnp.bfloat16
I32 = jnp.int32
HIGHEST = lax.Precision.HIGHEST

N_MIXERS = 3
MIX_SHORTCONV, MIX_HYENA, MIX_ATTN = 0, 1, 2
CONV_W = 3
EPS = 1e-6
HY_ORDER = 2
HY_EMB = 33
HY_BANDS = (HY_EMB - 1) // 2
HY_FAST_DECAY = 0.3
HY_SLOW_DECAY = 1.5
HY_TARGET = 1e-2
HEAD_DIM = 128
N_KV_HEADS = 2
GRID_W = 64
ROPE_THETA = 10000.0
EC_CAPACITY_FACTOR = 2

LANES = 128
SUBLANES = 8
BF16_ROWS = 16
VMEM_LIMIT_BYTES = 56 * 1024 * 1024
MOD_ROWS = 16
FFT_N2 = 128
HY_DT = 128


def _cparams(*sem):
    return pltpu.CompilerParams(dimension_semantics=sem, vmem_limit_bytes=VMEM_LIMIT_BYTES)


def _row_tile(n, cap=512):
    t = min(n, cap)
    assert n % t == 0
    return t


def _dot(a, b):
    return jnp.dot(a, b, preferred_element_type=F32)


def _ada_kernel(s_ref, w_ref, b_ref, o_ref):
    s = s_ref[...]
    s = s / (1.0 + jnp.exp(-s))
    o_ref[0] = jnp.dot(s, w_ref[0], preferred_element_type=F32, precision=HIGHEST) + b_ref[0]


def ada_modulation(cond, ada_w, ada_b):
    depth, d, n = ada_w.shape
    tn = 1536 if n % 1536 == 0 else n
    return pl.pallas_call(
        _ada_kernel,
        grid=(depth, n // tn),
        in_specs=[pl.BlockSpec((MOD_ROWS, d), lambda i, j: (0, 0)),
                  pl.BlockSpec((1, d, tn), lambda i, j: (i, 0, j)),
                  pl.BlockSpec((1, 1, tn), lambda i, j: (i, 0, j))],
        out_specs=pl.BlockSpec((1, MOD_ROWS, tn), lambda i, j: (i, 0, j)),
        out_shape=jax.ShapeDtypeStruct((depth, MOD_ROWS, n), F32),
        compiler_params=_cparams("arbitrary", "arbitrary"),
        name="ada_modulation",
    )(cond, ada_w, ada_b.reshape(depth, 1, n))


def _norm_mod(x, g, sh, sc):
    y = x * lax.rsqrt(jnp.mean(x * x, axis=-1, keepdims=True) + EPS)
    return (y * g) * (1.0 + sc) + sh


def _nm_matmul_kernel(x_ref, g_ref, sh_ref, sc_ref, w_ref, o_ref):
    h = _norm_mod(x_ref[0], g_ref[...], sh_ref[0], sc_ref[0]).astype(BF16)
    o_ref[0] = _dot(h, w_ref[...]).astype(o_ref.dtype)


def nm_matmul(x, g, sh, sc, w, out_dtype=BF16):
    b, l, d = x.shape
    n = w.shape[1]
    tm = _row_tile(l)
    return pl.pallas_call(
        _nm_matmul_kernel,
        grid=(b, l // tm),
        in_specs=[pl.BlockSpec((1, tm, d), lambda i, j: (i, j, 0)),
                  pl.BlockSpec((1, d), lambda i, j: (0, 0)),
                  pl.BlockSpec((1, 1, d), lambda i, j: (i, 0, 0)),
                  pl.BlockSpec((1, 1, d), lambda i, j: (i, 0, 0)),
                  pl.BlockSpec((d, n), lambda i, j: (0, 0))],
        out_specs=pl.BlockSpec((1, tm, n), lambda i, j: (i, j, 0)),
        out_shape=jax.ShapeDtypeStruct((b, l, n), out_dtype),
        compiler_params=_cparams("arbitrary", "arbitrary"),
        name="nm_matmul",
    )(x, g.reshape(1, d), sh, sc, w)


def _proj_res_kernel(a_ref, w_ref, lat_ref, gt_ref, o_ref):
    o_ref[0] = lat_ref[0] + gt_ref[0] * _dot(a_ref[0], w_ref[...])


def proj_residual(a, w, lat, gate):
    b, l, k = a.shape
    d = w.shape[1]
    tm = _row_tile(l)
    return pl.pallas_call(
        _proj_res_kernel,
        grid=(b, l // tm),
        in_specs=[pl.BlockSpec((1, tm, k), lambda i, j: (i, j, 0)),
                  pl.BlockSpec((k, d), lambda i, j: (0, 0)),
                  pl.BlockSpec((1, tm, d), lambda i, j: (i, j, 0)),
                  pl.BlockSpec((1, 1, d), lambda i, j: (i, 0, 0))],
        out_specs=pl.BlockSpec((1, tm, d), lambda i, j: (i, j, 0)),
        out_shape=jax.ShapeDtypeStruct((b, l, d), F32),
        compiler_params=_cparams("arbitrary", "arbitrary"),
        name="proj_residual",
    )(a, w, lat, gate)


def _sc_out_kernel(pb_ref, pc_ref, pv_ref, cp_ref, vp_ref, cn_ref, vn_ref, cw_ref, w_ref, lat_ref, gt_ref,
                   o_ref, u_ref, *, tm):
    j = pl.program_id(1)
    nj = pl.num_programs(1)
    d = u_ref.shape[1]
    u_ref[pl.ds(SUBLANES, tm), :] = pc_ref[0].astype(F32) * pv_ref[0].astype(F32)
    prev = cp_ref[0, BF16_ROWS - 1:BF16_ROWS, :].astype(F32) * vp_ref[0, BF16_ROWS - 1:BF16_ROWS, :].astype(F32)
    nxt = cn_ref[0, 0:1, :].astype(F32) * vn_ref[0, 0:1, :].astype(F32)
    prev = jnp.where(j == 0, 0.0, prev)
    nxt = jnp.where(j == nj - 1, 0.0, nxt)
    u_ref[pl.ds(0, SUBLANES), :] = jnp.broadcast_to(prev, (SUBLANES, d))
    u_ref[pl.ds(SUBLANES + tm, SUBLANES), :] = jnp.broadcast_to(nxt, (SUBLANES, d))
    cw = cw_ref[...]
    y = (u_ref[pl.ds(SUBLANES - 1, tm), :] * cw[0:1] + u_ref[pl.ds(SUBLANES, tm), :] * cw[1:2]
         + u_ref[pl.ds(SUBLANES + 1, tm), :] * cw[2:3])
    m = (pb_ref[0].astype(F32) * y).astype(BF16)
    o_ref[0] = lat_ref[0] + gt_ref[0] * _dot(m, w_ref[...])


def sc_out(p, conv_w, w_out, lat, gate):
    b, l, d3 = p.shape
    d = d3 // 3
    tm = _row_tile(l)
    hb = tm // BF16_ROWS
    nhb = l // BF16_ROWS

    def prev_map(col):
        return lambda i, j: (i, jnp.maximum(j * hb - 1, 0), col)

    def next_map(col):
        return lambda i, j: (i, jnp.minimum((j + 1) * hb, nhb - 1), col)

    return pl.pallas_call(
        functools.partial(_sc_out_kernel, tm=tm),
        grid=(b, l // tm),
        in_specs=[pl.BlockSpec((1, tm, d), lambda i, j: (i, j, 0)),
                  pl.BlockSpec((1, tm, d), lambda i, j: (i, j, 1)),
                  pl.BlockSpec((1, tm, d), lambda i, j: (i, j, 2)),
                  pl.BlockSpec((1, BF16_ROWS, d), prev_map(1)),
                  pl.BlockSpec((1, BF16_ROWS, d), prev_map(2)),
                  pl.BlockSpec((1, BF16_ROWS, d), next_map(1)),
                  pl.BlockSpec((1, BF16_ROWS, d), next_map(2)),
                  pl.BlockSpec((CONV_W, d), lambda i, j: (0, 0)),
                  pl.BlockSpec((d, d), lambda i, j: (0, 0)),
                  pl.BlockSpec((1, tm, d), lambda i, j: (i, j, 0)),
                  pl.BlockSpec((1, 1, d), lambda i, j: (i, 0, 0))],
        out_specs=pl.BlockSpec((1, tm, d), lambda i, j: (i, j, 0)),
        out_shape=jax.ShapeDtypeStruct((b, l, d), F32),
        scratch_shapes=[pltpu.VMEM((tm + 2 * SUBLANES, d), F32)],
        compiler_params=_cparams("arbitrary", "arbitrary"),
        name="sc_out",
    )(p, p, p, p, p, p, p, conv_w, w_out, lat, gate)


def _hy_filter_kernel(z_ref, w1_ref, b1_ref, w2_ref, b2_ref, fr_ref, w3f_ref, w3b_ref, t_ref, dl_ref,
                      of_ref, ob_ref):
    fr = fr_ref[...]
    a = jnp.sin(fr * (jnp.dot(z_ref[...], w1_ref[...], preferred_element_type=F32, precision=HIGHEST)
                      + b1_ref[...]))
    a = jnp.sin(fr * (jnp.dot(a, w2_ref[...], preferred_element_type=F32, precision=HIGHEST) + b2_ref[...]))
    decay = jnp.exp(-t_ref[...] * dl_ref[...])
    hf = jnp.dot(a, w3f_ref[...], preferred_element_type=F32, precision=HIGHEST) * decay
    hb = jnp.dot(a, w3b_ref[...], preferred_element_type=F32, precision=HIGHEST) * decay
    row = lax.broadcasted_iota(I32, hb.shape, 0)
    hb = jnp.where(row == 0, 0.0, hb)
    inv = 1.0 / (jnp.sum(jnp.abs(hf), axis=0, keepdims=True) + jnp.sum(jnp.abs(hb), axis=0, keepdims=True))
    of_ref[0] = hf * inv
    ob_ref[0] = hb * inv


def hy_filter_taps(l, d, f_w1, f_b1, f_w2, f_b2, f_w3, sin_freq):
    hid = f_w1.shape[1]
    hp = LANES
    t = np.linspace(0.0, 1.0, l)[:, None]
    bands = np.linspace(1e-4, HY_BANDS - 1, HY_BANDS)[None, :]
    ang = (2.0 * math.pi / l) * np.arange(l)[:, None] * bands
    z = np.zeros((l, hp), np.float64)
    z[:, :HY_EMB] = np.concatenate([t, np.cos(ang), -np.sin(ang)], axis=-1)
    deltas = np.abs(np.linspace(math.log(HY_TARGET) / HY_SLOW_DECAY, math.log(HY_TARGET) / HY_FAST_DECAY, d))[None, :]

    def pad2(a, r, c):
        return jnp.zeros((r, c), F32).at[:a.shape[0], :a.shape[1]].set(a.astype(F32))

    w1 = pad2(f_w1, hp, hp)
    w2 = pad2(f_w2, hp, hp)
    w3 = pad2(f_w3, hp, f_w3.shape[1])
    b1 = pad2(f_b1[None], 1, hp)
    b2 = pad2(f_b2[None], 1, hp)
    fr = pad2(sin_freq[None], 1, hp)
    tc = min(d, 2 * LANES)
    nd = d // tc
    full = lambda o, j: (0, 0)
    return pl.pallas_call(
        _hy_filter_kernel,
        grid=(HY_ORDER, nd),
        in_specs=[pl.BlockSpec((l, hp), full), pl.BlockSpec((hp, hp), full), pl.BlockSpec((1, hp), full),
                  pl.BlockSpec((hp, hp), full), pl.BlockSpec((1, hp), full), pl.BlockSpec((1, hp), full),
                  pl.BlockSpec((hp, tc), lambda o, j: (0, o * nd + j)),
                  pl.BlockSpec((hp, tc), lambda o, j: (0, (HY_ORDER + o) * nd + j)),
                  pl.BlockSpec((l, 1), full),
                  pl.BlockSpec((1, tc), lambda o, j: (0, j))],
        out_specs=[pl.BlockSpec((1, l, tc), lambda o, j: (o, 0, j)),
                   pl.BlockSpec((1, l, tc), lambda o, j: (o, 0, j))],
        out_shape=[jax.ShapeDtypeStruct((HY_ORDER, l, d), F32)] * 2,
        compiler_params=_cparams("arbitrary", "arbitrary"),
        name="hy_filter_taps",
    )(jnp.asarray(z, F32), w1, b1, w2, b2, fr, w3, w3, jnp.asarray(t, F32), jnp.asarray(deltas, F32))


def _dft_consts(l):
    n = 2 * l
    if n <= 4 * FFT_N2:
        n1, n2 = 1, n
    else:
        n1, n2 = n // FFT_N2, FFT_N2
    if n1 == 1:
        k = np.arange(n)[:, None]
        t = np.arange(l)[None, :]
        ang = 2.0 * np.pi * (k * t % n) / n
        c, s = np.cos(ang), np.sin(ang)
        fwd = np.block([[c, s], [-s, c]])
        inv = np.block([[c.T, -s.T], [s.T, c.T]]) / n
        return n1, n2, dict(fwd=jnp.asarray(fwd, BF16), inv=jnp.asarray(inv, BF16))
    n1h = n1 // 2
    k1 = np.arange(n1)[None, :, None]
    m1 = np.arange(n1h)[None, None, :]
    r2 = np.arange(n2)[:, None, None]
    ang = 2.0 * np.pi * ((k1 * (n2 * m1 + r2)) % n) / n
    c, s = np.cos(ang), np.sin(ang)
    m1f = np.concatenate([np.concatenate([c, s], 2), np.concatenate([-s, c], 2)], 1)
    ct, st = np.swapaxes(c, 1, 2) / n, np.swapaxes(s, 1, 2) / n
    m1i = np.concatenate([np.concatenate([ct, -st], 2), np.concatenate([st, ct], 2)], 1)
    a2 = 2.0 * np.pi * ((np.arange(n2)[:, None] * np.arange(n2)[None, :]) % n2) / n2
    c2, s2 = np.cos(a2), np.sin(a2)
    f2 = np.block([[c2, s2], [-s2, c2]])
    f2i = np.block([[c2, -s2], [s2, c2]])
    return n1, n2, dict(m1f=jnp.asarray(m1f, BF16), m1i=jnp.asarray(m1i, BF16),
                        f2=jnp.asarray(f2, BF16), f2i=jnp.asarray(f2i, BF16))


def _fwd_stage1(x_re, x_im, a_re, a_im, m1f_ref, n1, n2):
    n1h = n1 // 2

    def body(r, carry):
        re = x_re[pl.ds(r, n1h, stride=n2), :]
        im = x_im[pl.ds(r, n1h, stride=n2), :]
        y = _dot(m1f_ref[r], jnp.concatenate([re, im], axis=0).astype(BF16))
        a_re[pl.ds(pl.multiple_of(r * n1, n1), n1), :] = y[:n1]
        a_im[pl.ds(pl.multiple_of(r * n1, n1), n1), :] = y[n1:]
        return carry

    lax.fori_loop(0, n2, body, 0)


def _inv_stage1(a_re, a_im, y_re, y_im, m1i_ref, n1, n2):
    n1h = n1 // 2

    def body(r, carry):
        vr = a_re[pl.ds(pl.multiple_of(r * n1, n1), n1), :]
        vi = a_im[pl.ds(pl.multiple_of(r * n1, n1), n1), :]
        y = _dot(m1i_ref[r], jnp.concatenate([vr, vi], axis=0).astype(BF16))
        y_re[pl.ds(r, n1h, stride=n2), :] = y[:n1h]
        y_im[pl.ds(r, n1h, stride=n2), :] = y[n1h:]
        return carry

    lax.fori_loop(0, n2, body, 0)


def _load_chunk(a_re, a_im, k1, n1, n2):
    ur = a_re[pl.ds(k1, n2, stride=n1), :]
    ui = a_im[pl.ds(k1, n2, stride=n1), :]
    return jnp.concatenate([ur, ui], axis=0).astype(BF16)


def _hy_spec_kernel(*refs, l, n1, n2, dense):
    if dense:
        tf_ref, tb_ref, sk_ref, fwd_ref, o_ref = refs
        n = n2
        fw = fwd_ref[:, 0:l]
        sf = _dot(fw, tf_ref[0].astype(BF16))
        sb = _dot(fw, tb_ref[0].astype(BF16))
        o_ref[0, 0] = (sf[:n] + sb[:n] + sk_ref[0]).astype(o_ref.dtype)
        o_ref[0, 1] = (sf[n:] - sb[n:]).astype(o_ref.dtype)
        return
    tf_ref, tb_ref, sk_ref, m1f_ref, f2_ref, o_ref, x_re, x_im, a_re, a_im, h_re, h_im = refs
    x_im[...] = jnp.zeros_like(x_im)
    for run, t_ref in enumerate((tf_ref, tb_ref)):
        x_re[...] = t_ref[0]
        _fwd_stage1(x_re, x_im, a_re, a_im, m1f_ref, n1, n2)

        def body(k1, carry, run=run):
            x = _dot(f2_ref[...], _load_chunk(a_re, a_im, k1, n1, n2))
            rows = pl.ds(pl.multiple_of(k1 * n2, n2), n2)
            if run == 0:
                h_re[rows, :] = x[:n2]
                h_im[rows, :] = x[n2:]
            else:
                h_re[rows, :] = h_re[rows, :] + x[:n2]
                h_im[rows, :] = h_im[rows, :] - x[n2:]
            return carry

        lax.fori_loop(0, n1, body, 0)
    o_ref[0, 0] = (h_re[...] + sk_ref[0]).astype(o_ref.dtype)
    o_ref[0, 1] = h_im[...].astype(o_ref.dtype)


def hy_spectrum(taps_f, taps_b, skip, consts, n1, n2):
    o, l, d = taps_f.shape
    n = 2 * l
    dt = HY_DT
    dense = n1 == 1
    tap_spec = pl.BlockSpec((1, l, dt), lambda i, j: (i, 0, j))
    in_specs = [tap_spec, tap_spec, pl.BlockSpec((1, 1, dt), lambda i, j: (i, 0, j))]
    if dense:
        mats = [consts["fwd"]]
        scratch = []
    else:
        mats = [consts["m1f"], consts["f2"]]
        scratch = [pltpu.VMEM((l, dt), F32)] * 2 + [pltpu.VMEM((n, dt), F32)] * 4
    in_specs += [pl.BlockSpec(m.shape, lambda i, j, nd=m.ndim: (0,) * nd) for m in mats]
    return pl.pallas_call(
        functools.partial(_hy_spec_kernel, l=l, n1=n1, n2=n2, dense=dense),
        grid=(o, d // dt),
        in_specs=in_specs,
        out_specs=pl.BlockSpec((1, 2, n, dt), lambda i, j: (i, 0, 0, j)),
        out_shape=jax.ShapeDtypeStruct((o, 2, n, d), BF16),
        scratch_shapes=scratch,
        compiler_params=_cparams("arbitrary", "arbitrary"),
        name="hy_spectrum",
    )(taps_f, taps_b, skip.reshape(o, 1, d), *mats)


def _conv3_rows(src, cw, pad_ref, l):
    c = pad_ref.shape[1]
    pad_ref[pl.ds(0, SUBLANES), :] = jnp.zeros((SUBLANES, c), F32)
    pad_ref[pl.ds(SUBLANES + l, SUBLANES), :] = jnp.zeros((SUBLANES, c), F32)
    pad_ref[pl.ds(SUBLANES, l), :] = src
    return (pad_ref[pl.ds(SUBLANES - 1, l), :] * cw[0:1] + pad_ref[pl.ds(SUBLANES, l), :] * cw[1:2]
            + pad_ref[pl.ds(SUBLANES + 1, l), :] * cw[2:3])


def _hy_conv_kernel(*refs, l, n1, n2, dense, conv_z):
    if dense:
        z_ref, g_ref, cwz_ref, cwg_ref, h_ref, fwd_ref, inv_ref, o_ref, x_re, x_im, pad_ref = refs
    else:
        (z_ref, g_ref, cwz_ref, cwg_ref, h_ref, m1f_ref, m1i_ref, f2_ref, f2i_ref, o_ref,
         x_re, x_im, a_re, a_im, pad_ref) = refs
    for e, x in enumerate((x_re, x_im)):
        z = z_ref[e, 0].astype(F32)
        x[...] = _conv3_rows(z, cwz_ref[...], pad_ref, l) if conv_z else z
    if dense:
        n = n2
        s = _dot(fwd_ref[...], jnp.concatenate([x_re[...], x_im[...]], axis=0).astype(BF16))
        hr = h_ref[0, 0].astype(F32)
        hi = h_ref[0, 1].astype(F32)
        zr = s[:n] * hr - s[n:] * hi
        zi = s[:n] * hi + s[n:] * hr
        y = _dot(inv_ref[...], jnp.concatenate([zr, zi], axis=0).astype(BF16))
        x_re[...] = y[:l]
        x_im[...] = y[l:]
    else:
        _fwd_stage1(x_re, x_im, a_re, a_im, m1f_ref, n1, n2)

        def body(k1, carry):
            s = _dot(f2_ref[...], _load_chunk(a_re, a_im, k1, n1, n2))
            rows = pl.ds(pl.multiple_of(k1 * n2, n2), n2)
            hr = h_ref[0, 0, rows, :].astype(F32)
            hi = h_ref[0, 1, rows, :].astype(F32)
            zr = s[:n2] * hr - s[n2:] * hi
            zi = s[:n2] * hi + s[n2:] * hr
            v = _dot(f2i_ref[...], jnp.concatenate([zr, zi], axis=0).astype(BF16))
            a_re[pl.ds(k1, n2, stride=n1), :] = v[:n2]
            a_im[pl.ds(k1, n2, stride=n1), :] = v[n2:]
            return carry

        lax.fori_loop(0, n1, body, 0)
        _inv_stage1(a_re, a_im, x_re, x_im, m1i_ref, n1, n2)
    for e, x in enumerate((x_re, x_im)):
        g = _conv3_rows(g_ref[e, 0].astype(F32), cwg_ref[...], pad_ref, l)
        o_ref[e, 0] = (g * x[...]).astype(o_ref.dtype)


def hy_conv(zsrc, zcol, p4, gcol, conv_w, spec, order, consts, n1, n2, conv_z):
    _, pairs, l, _ = zsrc.shape
    d = spec.shape[3]
    n = 2 * l
    dt = HY_DT
    nd = d // dt
    dense = n1 == 1
    in_specs = [pl.BlockSpec((2, 1, l, dt), lambda j, i: (0, i, 0, zcol * nd + j)),
                pl.BlockSpec((2, 1, l, dt), lambda j, i: (0, i, 0, gcol * nd + j)),
                pl.BlockSpec((CONV_W, dt), lambda j, i: (0, j)),
                pl.BlockSpec((CONV_W, dt), lambda j, i: (0, gcol * nd + j)),
                pl.BlockSpec((1, 2, n, dt), lambda j, i: (order, 0, 0, j))]
    if dense:
        mats = [consts["fwd"], consts["inv"]]
        scratch = [pltpu.VMEM((l, dt), F32)] * 2
    else:
        mats = [consts["m1f"], consts["m1i"], consts["f2"], consts["f2i"]]
        scratch = [pltpu.VMEM((l, dt), F32)] * 2 + [pltpu.VMEM((n, dt), F32)] * 2
    scratch = scratch + [pltpu.VMEM((l + 2 * SUBLANES, dt), F32)]
    in_specs += [pl.BlockSpec(m.shape, lambda j, i, nd_=m.ndim: (0,) * nd_) for m in mats]
    return pl.pallas_call(
        functools.partial(_hy_conv_kernel, l=l, n1=n1, n2=n2, dense=dense, conv_z=conv_z),
        grid=(nd, pairs),
        in_specs=in_specs,
        out_specs=pl.BlockSpec((2, 1, l, dt), lambda j, i: (0, i, 0, j)),
        out_shape=jax.ShapeDtypeStruct((2, pairs, l, d), BF16),
        scratch_shapes=scratch,
        compiler_params=_cparams("arbitrary", "arbitrary"),
        name="hy_conv",
    )(zsrc, p4, conv_w, conv_w, spec, *mats)


def hyena_mixer(p, conv_w, f_w1, f_b1, f_w2, f_b2, f_w3, sin_freq, skip):
    b, l, d3 = p.shape
    d = d3 // 3
    assert b % 2 == 0 and d % HY_DT == 0
    n1, n2, consts = _dft_consts(l)
    taps_f, taps_b = hy_filter_taps(l, d, f_w1, f_b1, f_w2, f_b2, f_w3, sin_freq)
    spec = hy_spectrum(taps_f, taps_b, skip, consts, n1, n2)
    p4 = p.reshape(2, b // 2, l, d3)
    z = hy_conv(p4, 0, p4, 1, conv_w, spec, 0, consts, n1, n2, True)
    z = hy_conv(z, 0, p4, 2, conv_w, spec, 1, consts, n1, n2, False)
    return z.reshape(b, l, d)


def _head_norm(x, g):
    return x * lax.rsqrt(jnp.mean(x * x, axis=-1, keepdims=True) + EPS) * g


def _rope(x, cos, sin_a, sin_b):
    return (x * cos + pltpu.roll(x, HEAD_DIM // 4, 1) * sin_a
            + pltpu.roll(x, HEAD_DIM - HEAD_DIM // 4, 1) * sin_b)


def _qkv_kernel(x_ref, g_ref, sh_ref, sc_ref, w_ref, qg_ref, kg_ref, cos_ref, sa_ref, sb_ref,
                *out_refs, nq, rope):
    q_ref = out_refs[0] if nq else None
    k_ref, v_ref = out_refs[-2:]
    h = _norm_mod(x_ref[0], g_ref[...], sh_ref[0], sc_ref[0]).astype(BF16)
    acc = _dot(h, w_ref[...])
    hd = HEAD_DIM
    for i in range(nq + N_KV_HEADS):
        t = acc[:, i * hd:(i + 1) * hd]
        t = _head_norm(t, qg_ref[...] if i < nq else kg_ref[...])
        if rope:
            t = _rope(t, cos_ref[...], sa_ref[...], sb_ref[...])
        if i < nq:
            q_ref[0, :, i * hd:(i + 1) * hd] = (t * (hd ** -0.5)).astype(BF16)
        else:
            k_ref[0, :, (i - nq) * hd:(i - nq + 1) * hd] = t.astype(BF16)
    v_ref[0] = acc[:, (nq + N_KV_HEADS) * hd:].astype(BF16)


def qkv_project(x, g, sh, sc, w, q_g, k_g, rope_tabs, nq, rope):
    b, l, d = x.shape
    n = w.shape[1]
    hd = HEAD_DIM
    kvd = N_KV_HEADS * hd
    tm = _row_tile(l)
    cos, sa, sb = rope_tabs
    tab = pl.BlockSpec((tm, hd), lambda i, j: (j, 0))
    vec = pl.BlockSpec((1, 1, d), lambda i, j: (i, 0, 0))
    widths = ([nq * hd] if nq else []) + [kvd, kvd]
    return pl.pallas_call(
        functools.partial(_qkv_kernel, nq=nq, rope=rope),
        grid=(b, l // tm),
        in_specs=[pl.BlockSpec((1, tm, d), lambda i, j: (i, j, 0)),
                  pl.BlockSpec((1, d), lambda i, j: (0, 0)), vec, vec,
                  pl.BlockSpec((d, n), lambda i, j: (0, 0)),
                  pl.BlockSpec((1, hd), lambda i, j: (0, 0)),
                  pl.BlockSpec((1, hd), lambda i, j: (0, 0)),
                  tab, tab, tab],
        out_specs=[pl.BlockSpec((1, tm, wd), lambda i, j: (i, j, 0)) for wd in widths],
        out_shape=[jax.ShapeDtypeStruct((b, l, wd), BF16) for wd in widths],
        compiler_params=_cparams("arbitrary", "arbitrary"),
        name="qkv_project",
    )(x, g.reshape(1, d), sh, sc, w, q_g.reshape(1, hd), k_g.reshape(1, hd), cos, sa, sb)


def _rope_tables(l):
    rows = l // GRID_W
    row = np.repeat(np.arange(rows, dtype=np.float64), GRID_W)
    col = np.tile(np.arange(GRID_W, dtype=np.float64), rows)
    axis = HEAD_DIM // 2
    inv = ROPE_THETA ** (-np.arange(0, axis, 2, dtype=np.float64) / axis)

    def axis_angles(pos):
        a = pos[:, None] * inv[None, :]
        return np.concatenate([a, a], axis=-1)

    ang = np.concatenate([axis_angles(row), axis_angles(col)], axis=-1)
    cos, sin = np.cos(ang), np.sin(ang)
    lane = np.arange(HEAD_DIM)[None, :] % axis
    upper = lane >= axis // 2
    sin_a = np.where(upper, sin, 0.0)
    sin_b = np.where(upper, 0.0, -sin)
    return tuple(jnp.asarray(t, F32) for t in (cos, sin_a, sin_b))


def _attn_kernel(q_ref, k_ref, v_ref, o_ref, *, groups):
    k = k_ref[0]
    v = v_ref[0]
    hd = HEAD_DIM
    for g in range(groups):
        q = q_ref[0, :, g * hd:(g + 1) * hd]
        s = lax.dot_general(q, k, (((1,), (1,)), ((), ())), preferred_element_type=F32)
        m = jnp.max(s, axis=-1, keepdims=True)
        p = jnp.exp(s - m)
        den = jnp.sum(p, axis=-1, keepdims=True)
        o = _dot(p.astype(BF16), v) / den
        o_ref[0, :, g * hd:(g + 1) * hd] = o.astype(o_ref.dtype)


def attention(q, k, v):
    b, l, qd = q.shape
    s = k.shape[1]
    hd = HEAD_DIM
    groups = qd // hd // N_KV_HEADS
    tq = _row_tile(l, 256)
    return pl.pallas_call(
        functools.partial(_attn_kernel, groups=groups),
        grid=(b, N_KV_HEADS, l // tq),
        in_specs=[pl.BlockSpec((1, tq, groups * hd), lambda i, h, j: (i, j, h)),
                  pl.BlockSpec((1, s, hd), lambda i, h, j: (i, 0, h)),
                  pl.BlockSpec((1, s, hd), lambda i, h, j: (i, 0, h))],
        out_specs=pl.BlockSpec((1, tq, groups * hd), lambda i, h, j: (i, j, h)),
        out_shape=jax.ShapeDtypeStruct((b, l, qd), BF16),
        compiler_params=_cparams("arbitrary", "arbitrary", "arbitrary"),
        name="attention",
    )(q, k, v)


def _router_kernel(x_ref, g_ref, sh_ref, sc_ref, wr_ref, h_ref, a_ref):
    h = _norm_mod(x_ref[0], g_ref[...], sh_ref[0], sc_ref[0])
    h_ref[0] = h.astype(BF16)
    logits = lax.dot_general(wr_ref[...], h, (((1,), (1,)), ((), ())), preferred_element_type=F32,
                             precision=HIGHEST)
    e = jnp.exp(logits - jnp.max(logits, axis=0, keepdims=True))
    a_ref[0] = e / jnp.sum(e, axis=0, keepdims=True)


def moe_router(x, g, sh, sc, w_router_t):
    b, n, d = x.shape
    e = w_router_t.shape[0]
    tm = _row_tile(n)
    vec = pl.BlockSpec((1, 1, d), lambda i, j: (i, 0, 0))
    return pl.pallas_call(
        _router_kernel,
        grid=(b, n // tm),
        in_specs=[pl.BlockSpec((1, tm, d), lambda i, j: (i, j, 0)),
                  pl.BlockSpec((1, d), lambda i, j: (0, 0)), vec, vec,
                  pl.BlockSpec((e, d), lambda i, j: (0, 0))],
        out_specs=[pl.BlockSpec((1, tm, d), lambda i, j: (i, j, 0)),
                   pl.BlockSpec((1, e, tm), lambda i, j: (i, 0, j))],
        out_shape=[jax.ShapeDtypeStruct((b, n, d), BF16), jax.ShapeDtypeStruct((b, e, n), F32)],
        compiler_params=_cparams("arbitrary", "arbitrary"),
        name="moe_router",
    )(x, g.reshape(1, d), sh, sc, w_router_t)


def _cumsum_lanes(m, tri_ref):
    e, n = m.shape
    carry = jnp.zeros((e, 1), F32)
    outs = []
    for c in range(n // LANES):
        blk = m[:, c * LANES:(c + 1) * LANES]
        cs = _dot(blk.astype(BF16), tri_ref[...]) + carry
        outs.append(cs)
        carry = cs[:, LANES - 1:LANES]
    return jnp.concatenate(outs, axis=1)


def _select_kernel(a_ref, tri_ref, slot_ref, gate_ref, slot_t_ref, gate_t_ref, *, cap):
    aff = a_ref[0]
    bits = pltpu.bitcast(aff, I32)
    e = aff.shape[0]

    def body(i, thr):
        cand = thr | (1 << (30 - i))
        cnt = jnp.sum((bits >= cand).astype(F32), axis=1, keepdims=True)
        return jnp.where(cnt >= cap, cand, thr)

    thr = lax.fori_loop(0, 31, body, jnp.zeros((e, 1), I32))
    gt = bits > thr
    eq = bits == thr
    n_gt = jnp.sum(gt.astype(F32), axis=1, keepdims=True)
    eq_rank = _cumsum_lanes(eq.astype(F32), tri_ref)
    sel = gt | (eq & (eq_rank <= cap - n_gt))
    self_ = sel.astype(F32)
    pos = _cumsum_lanes(self_, tri_ref) - 1.0
    slot = jnp.where(sel, pos, -1.0)
    gate = jnp.where(sel, aff, 0.0)
    slot_ref[0] = slot
    gate_ref[0] = gate
    slot_t_ref[0] = slot.T
    gate_t_ref[0] = gate.T


def moe_select(aff, cap):
    b, e, n = aff.shape
    tri = jnp.asarray(np.triu(np.ones((LANES, LANES), np.float32)), BF16)
    row = pl.BlockSpec((1, e, n), lambda i: (i, 0, 0))
    col = pl.BlockSpec((1, n, e), lambda i: (i, 0, 0))
    return pl.pallas_call(
        functools.partial(_select_kernel, cap=cap),
        grid=(b,),
        in_specs=[row, pl.BlockSpec((LANES, LANES), lambda i: (0, 0))],
        out_specs=[row, row, col, col],
        out_shape=[jax.ShapeDtypeStruct((b, e, n), F32), jax.ShapeDtypeStruct((b, e, n), F32),
                   jax.ShapeDtypeStruct((b, n, e), F32), jax.ShapeDtypeStruct((b, n, e), F32)],
        compiler_params=_cparams("arbitrary"),
        name="moe_select",
    )(aff, tri)


def _gather_kernel(slot_ref, h_ref, o_ref, *, cap, tk):
    n = h_ref.shape[1]
    jcol = lax.broadcasted_iota(I32, (cap, tk), 0).astype(F32)
    acc = jnp.zeros((cap, h_ref.shape[2]), F32)
    for c in range(n // tk):
        srow = slot_ref[0, 0, :, c * tk:(c + 1) * tk]
        onehot = jnp.where(jcol == srow, 1.0, 0.0).astype(BF16)
        acc = acc + _dot(onehot, h_ref[0, c * tk:(c + 1) * tk, :])
    o_ref[0, 0] = acc.astype(o_ref.dtype)


def moe_gather(slot, h, cap):
    b, e, n = slot.shape
    d = h.shape[2]
    tk = min(n, 512)
    return pl.pallas_call(
        functools.partial(_gather_kernel, cap=cap, tk=tk),
        grid=(b, e),
        in_specs=[pl.BlockSpec((1, 1, 1, n), lambda i, j: (i, j, 0, 0)),
                  pl.BlockSpec((1, n, d), lambda i, j: (i, 0, 0))],
        out_specs=pl.BlockSpec((1, 1, cap, d), lambda i, j: (i, j, 0, 0)),
        out_shape=jax.ShapeDtypeStruct((b, e, cap, d), BF16),
        compiler_params=_cparams("arbitrary", "arbitrary"),
        name="moe_gather",
    )(slot.reshape(b, e, 1, n), h)


def _ffn_kernel(x_ref, wg_ref, wu_ref, wd_ref, o_ref):
    x = x_ref[0, 0]
    a = _dot(x, wg_ref[0])
    u = _dot(x, wu_ref[0])
    hid = (a / (1.0 + jnp.exp(-a))) * u
    o_ref[0, 0] = _dot(hid.astype(BF16), wd_ref[0]).astype(o_ref.dtype)


def moe_ffn(xs, w_gate, w_up, w_down):
    b, e, cap, d = xs.shape
    f = w_gate.shape[2]
    return pl.pallas_call(
        _ffn_kernel,
        grid=(e, b),
        in_specs=[pl.BlockSpec((1, 1, cap, d), lambda j, i: (i, j, 0, 0)),
                  pl.BlockSpec((1, d, f), lambda j, i: (j, 0, 0)),
                  pl.BlockSpec((1, d, f), lambda j, i: (j, 0, 0)),
                  pl.BlockSpec((1, f, d), lambda j, i: (j, 0, 0))],
        out_specs=pl.BlockSpec((1, 1, cap, d), lambda j, i: (i, j, 0, 0)),
        out_shape=jax.ShapeDtypeStruct((b, e, cap, d), BF16),
        compiler_params=_cparams("arbitrary", "arbitrary"),
        name="moe_ffn",
    )(xs, w_gate, w_up, w_down)


def _combine_kernel(slot_ref, gate_ref, ys_ref, lat_ref, gt_ref, o_ref, *, cap, ne):
    tn = lat_ref.shape[1]
    jrow = lax.broadcasted_iota(I32, (tn, cap), 1).astype(F32)
    slot = slot_ref[0]
    gate = gate_ref[0]
    acc = jnp.zeros((tn, lat_ref.shape[2]), F32)
    for e in range(ne):
        onehot = jnp.where(jrow == slot[:, e:e + 1], 1.0, 0.0).astype(BF16)
        acc = acc + gate[:, e:e + 1] * _dot(onehot, ys_ref[0, e])
    o_ref[0] = lat_ref[0] + gt_ref[0] * acc


def moe_combine(slot_t, gate_t, ys, lat, gate_vec, cap):
    b, n, e = slot_t.shape
    d = lat.shape[2]
    tn = _row_tile(n, 256)
    return pl.pallas_call(
        functools.partial(_combine_kernel, cap=cap, ne=e),
        grid=(b, n // tn),
        in_specs=[pl.BlockSpec((1, tn, e), lambda i, j: (i, j, 0)),
                  pl.BlockSpec((1, tn, e), lambda i, j: (i, j, 0)),
                  pl.BlockSpec((1, e, cap, d), lambda i, j: (i, 0, 0, 0), pipeline_mode=pl.Buffered(1)),
                  pl.BlockSpec((1, tn, d), lambda i, j: (i, j, 0)),
                  pl.BlockSpec((1, 1, d), lambda i, j: (i, 0, 0))],
        out_specs=pl.BlockSpec((1, tn, d), lambda i, j: (i, j, 0)),
        out_shape=jax.ShapeDtypeStruct((b, n, d), F32),
        compiler_params=_cparams("arbitrary", "arbitrary"),
        name="moe_combine",
    )(slot_t, gate_t, ys, lat, gate_vec)


def expert_choice_moe(lat, g, sh, sc, gate_vec, w_router, w_gate, w_up, w_down):
    b, n, d = lat.shape
    e = w_router.shape[1]
    cap = EC_CAPACITY_FACTOR * n // e
    h, aff = moe_router(lat, g, sh, sc, w_router.T)
    slot, _, slot_t, gate_t = moe_select(aff, cap)
    xs = moe_gather(slot, h, cap)
    ys = moe_ffn(xs, w_gate, w_up, w_down)
    return moe_combine(slot_t, gate_t, ys, lat, gate_vec, cap)


def kernel(x, c, ctx, c_ctx, ada_w, ada_b, norm_g, sc_w_in, sc_conv, sc_w_out, hy_w_in, hy_conv_w, hy_f_w1,
           hy_f_b1, hy_f_w2, hy_f_b2, hy_f_w3, hy_sin_freq, hy_skip, hy_w_out, at_w_qkv, at_q_g, at_k_g, at_w_o,
           moe_router_w, moe_w_gate, moe_w_up, moe_w_down):
    b, l, d = x.shape
    depth = ada_w.shape[0]
    assert b + 1 <= MOD_ROWS
    qd = at_w_o.shape[1]
    nq = qd // HEAD_DIM
    attn_layers = list(range(MIX_ATTN, depth, N_MIXERS))
    last_attn = max(attn_layers, default=-1)

    cond = jnp.zeros((MOD_ROWS, d), F32).at[:b].set(c).at[b].set(c_ctx)
    mods = ada_modulation(cond, ada_w, ada_b)

    lat, cx = x, ctx
    for i in range(depth):
        kind = i % N_MIXERS
        j = i // N_MIXERS
        need_ctx = i <= last_attn
        upd_ctx = i < last_attn
        m_l = mods[i, :b].reshape(b, 1, 6, d)
        sh1, sc1, g1, sh2, sc2, g2 = (m_l[:, :, k] for k in range(6))
        if need_ctx:
            m_c = jnp.broadcast_to(mods[i, b].reshape(1, 1, 6, d), (b, 1, 6, d))
            csh1, csc1, cg1, csh2, csc2, cg2 = (m_c[:, :, k] for k in range(6))
        streams = [(lat, sh1, sc1, g1)]
        if upd_ctx:
            streams.append((cx, csh1, csc1, cg1))

        outs = []
        if kind == MIX_SHORTCONV:
            w_in, w_out = sc_w_in[j].astype(BF16), sc_w_out[j].astype(BF16)
            for s, sh, sc, gt in streams:
                p = nm_matmul(s, norm_g[i, 0], sh, sc, w_in)
                outs.append(sc_out(p, sc_conv[j], w_out, s, gt))
        elif kind == MIX_HYENA:
            w_in, w_out = hy_w_in[j].astype(BF16), hy_w_out[j].astype(BF16)
            for s, sh, sc, gt in streams:
                p = nm_matmul(s, norm_g[i, 0], sh, sc, w_in)
                z = hyena_mixer(p, hy_conv_w[j], hy_f_w1[j], hy_f_b1[j], hy_f_w2[j], hy_f_b2[j], hy_f_w3[j],
                                hy_sin_freq[j], hy_skip[j])
                outs.append(proj_residual(z, w_out, s, gt))
        else:
            w_qkv, w_o = at_w_qkv[j].astype(BF16), at_w_o[j].astype(BF16)
            q_l, k_l, v_l = qkv_project(lat, norm_g[i, 0], sh1, sc1, w_qkv, at_q_g[j], at_k_g[j],
                                        _rope_tables(l), nq, True)
            lc = cx.shape[1]
            dummy = tuple(jnp.zeros((lc, HEAD_DIM), F32) for _ in range(3))
            if upd_ctx:
                q_c, k_c, v_c = qkv_project(cx, norm_g[i, 0], csh1, csc1, w_qkv, at_q_g[j], at_k_g[j],
                                            dummy, nq, False)
            else:
                k_c, v_c = qkv_project(cx, norm_g[i, 0], csh1, csc1, w_qkv[:, qd:], at_q_g[j], at_k_g[j],
                                       dummy, 0, False)
            o_l = attention(q_l, jnp.concatenate([k_c, k_l], axis=1), jnp.concatenate([v_c, v_l], axis=1))
            outs.append(proj_residual(o_l, w_o, lat, g1))
            if upd_ctx:
                outs.append(proj_residual(attention(q_c, k_c, v_c), w_o, cx, cg1))

        wg, wu, wd = (w[i].astype(BF16) for w in (moe_w_gate, moe_w_up, moe_w_down))
        lat = expert_choice_moe(outs[0], norm_g[i, 1], sh2, sc2, g2, moe_router_w[i], wg, wu, wd)
        if upd_ctx:
            cx = expert_choice_moe(outs[1], norm_g[i, 1], csh2, csc2, cg2, moe_router_w[i], wg, wu, wd)
    return lat
```

```python
import functools
import math

import jax
import jax.numpy as jnp
import numpy as np
from jax import lax
from jax.experimental import pallas as pl
from jax.experimental.pallas import tpu as pltpu

F32 = jnp.float32
BF16 = jnp.bfloat16
I32 = jnp.int32
HIGHEST = lax.Precision.HIGHEST

N_MIXERS = 3
MIX_SHORTCONV, MIX_HYENA, MIX_ATTN = 0, 1, 2
CONV_W = 3
EPS = 1e-6
HY_ORDER = 2
HY_EMB = 33
HY_BANDS = (HY_EMB - 1) // 2
HY_FAST_DECAY = 0.3
HY_SLOW_DECAY = 1.5
HY_TARGET = 1e-2
HEAD_DIM = 128
N_KV_HEADS = 2
GRID_W = 64
ROPE_THETA = 10000.0
EC_CAPACITY_FACTOR = 2

LANES = 128
SUBLANES = 8
BF16_ROWS = 16
VMEM_LIMIT_BYTES = 56 * 1024 * 1024
MOD_ROWS = 16
FFT_N2 = 128
HY_DT = LANES
DFT_UNROLL = 2


def _cparams(*sem):
    return pltpu.CompilerParams(dimension_semantics=sem, vmem_limit_bytes=VMEM_LIMIT_BYTES)


def _row_tile(n, cap=512):
    t = min(n, cap)
    assert n % t == 0
    return t


def _dot(a, b):
    return jnp.dot(a, b, preferred_element_type=F32)


def _ada_kernel(s_ref, w_ref, b_ref, o_ref):
    s = s_ref[...]
    s = s / (1.0 + jnp.exp(-s))
    o_ref[0] = jnp.dot(s, w_ref[0], preferred_element_type=F32, precision=HIGHEST) + b_ref[0]


def ada_modulation(cond, ada_w, ada_b):
    depth, d, n = ada_w.shape
    tn = 1536 if n % 1536 == 0 else n
    return pl.pallas_call(
        _ada_kernel,
        grid=(depth, n // tn),
        in_specs=[pl.BlockSpec((MOD_ROWS, d), lambda i, j: (0, 0)),
                  pl.BlockSpec((1, d, tn), lambda i, j: (i, 0, j)),
                  pl.BlockSpec((1, 1, tn), lambda i, j: (i, 0, j))],
        out_specs=pl.BlockSpec((1, MOD_ROWS, tn), lambda i, j: (i, 0, j)),
        out_shape=jax.ShapeDtypeStruct((depth, MOD_ROWS, n), F32),
        compiler_params=_cparams("arbitrary", "arbitrary"),
        name="ada_modulation",
    )(cond, ada_w, ada_b.reshape(depth, 1, n))


def _norm_mod(x, g, sh, sc):
    y = x * lax.rsqrt(jnp.mean(x * x, axis=-1, keepdims=True) + EPS)
    return (y * g) * (1.0 + sc) + sh


def _nm_matmul_kernel(x_ref, g_ref, sh_ref, sc_ref, w_ref, o_ref):
    h = _norm_mod(x_ref[0], g_ref[...], sh_ref[0], sc_ref[0]).astype(BF16)
    o_ref[0] = _dot(h, w_ref[...]).astype(o_ref.dtype)


def nm_matmul(x, g, sh, sc, w, out_dtype=BF16):
    b, l, d = x.shape
    n = w.shape[1]
    tm = _row_tile(l)
    return pl.pallas_call(
        _nm_matmul_kernel,
        grid=(b, l // tm),
        in_specs=[pl.BlockSpec((1, tm, d), lambda i, j: (i, j, 0)),
                  pl.BlockSpec((1, d), lambda i, j: (0, 0)),
                  pl.BlockSpec((1, 1, d), lambda i, j: (i, 0, 0)),
                  pl.BlockSpec((1, 1, d), lambda i, j: (i, 0, 0)),
                  pl.BlockSpec((d, n), lambda i, j: (0, 0))],
        out_specs=pl.BlockSpec((1, tm, n), lambda i, j: (i, j, 0)),
        out_shape=jax.ShapeDtypeStruct((b, l, n), out_dtype),
        compiler_params=_cparams("arbitrary", "arbitrary"),
        name="nm_matmul",
    )(x, g.reshape(1, d), sh, sc, w)


def _proj_res_kernel(a_ref, w_ref, lat_ref, gt_ref, o_ref):
    o_ref[0] = lat_ref[0] + gt_ref[0] * _dot(a_ref[0], w_ref[...])


def proj_residual(a, w, lat, gate):
    b, l, k = a.shape
    d = w.shape[1]
    tm = _row_tile(l)
    return pl.pallas_call(
        _proj_res_kernel,
        grid=(b, l // tm),
        in_specs=[pl.BlockSpec((1, tm, k), lambda i, j: (i, j, 0)),
                  pl.BlockSpec((k, d), lambda i, j: (0, 0)),
                  pl.BlockSpec((1, tm, d), lambda i, j: (i, j, 0)),
                  pl.BlockSpec((1, 1, d), lambda i, j: (i, 0, 0))],
        out_specs=pl.BlockSpec((1, tm, d), lambda i, j: (i, j, 0)),
        out_shape=jax.ShapeDtypeStruct((b, l, d), F32),
        compiler_params=_cparams("arbitrary", "arbitrary"),
        name="proj_residual",
    )(a, w, lat, gate)


def _sc_out_kernel(pb_ref, pc_ref, pv_ref, cp_ref, vp_ref, cn_ref, vn_ref, cw_ref, w_ref, lat_ref, gt_ref,
                   o_ref, u_ref, *, tm):
    j = pl.program_id(1)
    nj = pl.num_programs(1)
    d = u_ref.shape[1]
    u_ref[pl.ds(SUBLANES, tm), :] = pc_ref[0].astype(F32) * pv_ref[0].astype(F32)
    prev = cp_ref[0, BF16_ROWS - 1:BF16_ROWS, :].astype(F32) * vp_ref[0, BF16_ROWS - 1:BF16_ROWS, :].astype(F32)
    nxt = cn_ref[0, 0:1, :].astype(F32) * vn_ref[0, 0:1, :].astype(F32)
    prev = jnp.where(j == 0, 0.0, prev)
    nxt = jnp.where(j == nj - 1, 0.0, nxt)
    u_ref[pl.ds(0, SUBLANES), :] = jnp.broadcast_to(prev, (SUBLANES, d))
    u_ref[pl.ds(SUBLANES + tm, SUBLANES), :] = jnp.broadcast_to(nxt, (SUBLANES, d))
    cw = cw_ref[...]
    y = (u_ref[pl.ds(SUBLANES - 1, tm), :] * cw[0:1] + u_ref[pl.ds(SUBLANES, tm), :] * cw[1:2]
         + u_ref[pl.ds(SUBLANES + 1, tm), :] * cw[2:3])
    m = (pb_ref[0].astype(F32) * y).astype(BF16)
    o_ref[0] = lat_ref[0] + gt_ref[0] * _dot(m, w_ref[...])


def sc_out(p, conv_w, w_out, lat, gate):
    b, l, d3 = p.shape
    d = d3 // 3
    tm = _row_tile(l)
    hb = tm // BF16_ROWS
    nhb = l // BF16_ROWS

    def prev_map(col):
        return lambda i, j: (i, jnp.maximum(j * hb - 1, 0), col)

    def next_map(col):
        return lambda i, j: (i, jnp.minimum((j + 1) * hb, nhb - 1), col)

    return pl.pallas_call(
        functools.partial(_sc_out_kernel, tm=tm),
        grid=(b, l // tm),
        in_specs=[pl.BlockSpec((1, tm, d), lambda i, j: (i, j, 0)),
                  pl.BlockSpec((1, tm, d), lambda i, j: (i, j, 1)),
                  pl.BlockSpec((1, tm, d), lambda i, j: (i, j, 2)),
                  pl.BlockSpec((1, BF16_ROWS, d), prev_map(1)),
                  pl.BlockSpec((1, BF16_ROWS, d), prev_map(2)),
                  pl.BlockSpec((1, BF16_ROWS, d), next_map(1)),
                  pl.BlockSpec((1, BF16_ROWS, d), next_map(2)),
                  pl.BlockSpec((CONV_W, d), lambda i, j: (0, 0)),
                  pl.BlockSpec((d, d), lambda i, j: (0, 0)),
                  pl.BlockSpec((1, tm, d), lambda i, j: (i, j, 0)),
                  pl.BlockSpec((1, 1, d), lambda i, j: (i, 0, 0))],
        out_specs=pl.BlockSpec((1, tm, d), lambda i, j: (i, j, 0)),
        out_shape=jax.ShapeDtypeStruct((b, l, d), F32),
        scratch_shapes=[pltpu.VMEM((tm + 2 * SUBLANES, d), F32)],
        compiler_params=_cparams("arbitrary", "arbitrary"),
        name="sc_out",
    )(p, p, p, p, p, p, p, conv_w, w_out, lat, gate)


def _hy_filter_kernel(z_ref, w1_ref, b1_ref, w2_ref, b2_ref, fr_ref, w3f_ref, w3b_ref, t_ref, dl_ref,
                      of_ref, ob_ref):
    fr = fr_ref[...]
    a = jnp.sin(fr * (jnp.dot(z_ref[...], w1_ref[...], preferred_element_type=F32, precision=HIGHEST)
                      + b1_ref[...]))
    a = jnp.sin(fr * (jnp.dot(a, w2_ref[...], preferred_element_type=F32, precision=HIGHEST) + b2_ref[...]))
    decay = jnp.exp(-t_ref[...] * dl_ref[...])
    hf = jnp.dot(a, w3f_ref[...], preferred_element_type=F32, precision=HIGHEST) * decay
    hb = jnp.dot(a, w3b_ref[...], preferred_element_type=F32, precision=HIGHEST) * decay
    row = lax.broadcasted_iota(I32, hb.shape, 0)
    hb = jnp.where(row == 0, 0.0, hb)
    inv = 1.0 / (jnp.sum(jnp.abs(hf), axis=0, keepdims=True) + jnp.sum(jnp.abs(hb), axis=0, keepdims=True))
    of_ref[0] = hf * inv
    ob_ref[0] = hb * inv


def hy_filter_taps(l, d, f_w1, f_b1, f_w2, f_b2, f_w3, sin_freq):
    hid = f_w1.shape[1]
    hp = LANES
    t = np.linspace(0.0, 1.0, l)[:, None]
    bands = np.linspace(1e-4, HY_BANDS - 1, HY_BANDS)[None, :]
    ang = (2.0 * math.pi / l) * np.arange(l)[:, None] * bands
    z = np.zeros((l, hp), np.float64)
    z[:, :HY_EMB] = np.concatenate([t, np.cos(ang), -np.sin(ang)], axis=-1)
    deltas = np.abs(np.linspace(math.log(HY_TARGET) / HY_SLOW_DECAY, math.log(HY_TARGET) / HY_FAST_DECAY, d))[None, :]

    def pad2(a, r, c):
        return jnp.zeros((r, c), F32).at[:a.shape[0], :a.shape[1]].set(a.astype(F32))

    w1 = pad2(f_w1, hp, hp)
    w2 = pad2(f_w2, hp, hp)
    w3 = pad2(f_w3, hp, f_w3.shape[1])
    b1 = pad2(f_b1[None], 1, hp)
    b2 = pad2(f_b2[None], 1, hp)
    fr = pad2(sin_freq[None], 1, hp)
    tc = min(d, 2 * LANES)
    nd = d // tc
    full = lambda o, j: (0, 0)
    return pl.pallas_call(
        _hy_filter_kernel,
        grid=(HY_ORDER, nd),
        in_specs=[pl.BlockSpec((l, hp), full), pl.BlockSpec((hp, hp), full), pl.BlockSpec((1, hp), full),
                  pl.BlockSpec((hp, hp), full), pl.BlockSpec((1, hp), full), pl.BlockSpec((1, hp), full),
                  pl.BlockSpec((hp, tc), lambda o, j: (0, o * nd + j)),
                  pl.BlockSpec((hp, tc), lambda o, j: (0, (HY_ORDER + o) * nd + j)),
                  pl.BlockSpec((l, 1), full),
                  pl.BlockSpec((1, tc), lambda o, j: (0, j))],
        out_specs=[pl.BlockSpec((1, l, tc), lambda o, j: (o, 0, j)),
                   pl.BlockSpec((1, l, tc), lambda o, j: (o, 0, j))],
        out_shape=[jax.ShapeDtypeStruct((HY_ORDER, l, d), F32)] * 2,
        compiler_params=_cparams("arbitrary", "arbitrary"),
        name="hy_filter_taps",
    )(jnp.asarray(z, F32), w1, b1, w2, b2, fr, w3, w3, jnp.asarray(t, F32), jnp.asarray(deltas, F32))


def _dft_consts(l):
    n = 2 * l
    if n <= 4 * FFT_N2:
        n1, n2 = 1, n
    else:
        n1, n2 = n // FFT_N2, FFT_N2
    if n1 == 1:
        k = np.arange(n)[:, None]
        t = np.arange(l)[None, :]
        ang = 2.0 * np.pi * (k * t % n) / n
        c, s = np.cos(ang), np.sin(ang)
        fwd = np.block([[c, s], [-s, c]])
        inv = np.block([[c.T, -s.T], [s.T, c.T]]) / n
        return n1, n2, dict(fwd=jnp.asarray(fwd, BF16), inv=jnp.asarray(inv, BF16))
    n1h = n1 // 2
    k1 = np.arange(n1)[None, :, None]
    m1 = np.arange(n1h)[None, None, :]
    r2 = np.arange(n2)[:, None, None]
    ang = 2.0 * np.pi * ((k1 * (n2 * m1 + r2)) % n) / n
    c, s = np.cos(ang), np.sin(ang)
    m1f = np.concatenate([np.concatenate([c, s], 2), np.concatenate([-s, c], 2)], 1)
    ct, st = np.swapaxes(c, 1, 2) / n, np.swapaxes(s, 1, 2) / n
    m1i = np.concatenate([np.concatenate([ct, -st], 2), np.concatenate([st, ct], 2)], 1)
    a2 = 2.0 * np.pi * ((np.arange(n2)[:, None] * np.arange(n2)[None, :]) % n2) / n2
    c2, s2 = np.cos(a2), np.sin(a2)
    f2 = np.block([[c2, s2], [-s2, c2]])
    f2i = np.block([[c2, -s2], [s2, c2]])

    def il(m):
        return np.stack([np.arange(m), m + np.arange(m)], 1).reshape(-1)

    m1f = m1f[:, il(n1), :][:, :, il(n1h)]
    m1i = m1i[:, il(n1h), :][:, :, il(n1)]
    f2 = f2[:, il(n2)]
    f2i = f2i[il(n2), :]
    half = n2 // 2
    m1f = np.concatenate([m1f[:half], m1f[half:]], 2)
    m1i = np.concatenate([m1i[:half], m1i[half:]], 2)
    return n1, n2, dict(m1f=jnp.asarray(m1f, BF16), m1i=jnp.asarray(m1i, BF16),
                        f2=jnp.asarray(f2, BF16), f2i=jnp.asarray(f2i, BF16))


def _blockdiag(a, b):
    z = jnp.zeros_like(a)
    return jnp.concatenate([jnp.concatenate([a, z], axis=1), jnp.concatenate([z, b], axis=1)], axis=0)


def _halves(y):
    return y[:, :LANES], y[:, LANES:]


HI_HALF = -65536


def _pack_pair(re, im):
    lo = (pltpu.bitcast(re.astype(BF16).astype(F32), I32) >> 16) & 0xFFFF
    if im is None:
        return lo
    return lo | (pltpu.bitcast(im.astype(BF16).astype(F32), I32) & HI_HALF)


def _unpack_pair(w):
    return pltpu.bitcast(w << 16, F32), pltpu.bitcast(w & HI_HALF, F32)


def _as_rows(w):
    return pltpu.bitcast(w, BF16)


def _as_words(y):
    return pltpu.bitcast(y.astype(BF16), I32)


def _fwd_stage1(x_pk, a_pk, m1f_ref, n1, n2):
    n1h, half = n1 // 2, n2 // 2

    def column(r):
        return _as_rows(x_pk[pl.ds(r, n1h, stride=n2), :])

    def body(r, carry):
        w = _as_words(_dot(m1f_ref[r], _blockdiag(column(r), column(r + half))))
        for rr, ww in zip((r, r + half), _halves(w)):
            a_pk[pl.ds(pl.multiple_of(rr * n1, n1), n1), :] = ww
        return carry

    lax.fori_loop(0, half, body, 0, unroll=DFT_UNROLL)


def _inv_stage1(b_pk, x_pk, m1i_ref, n1, n2):
    n1h, half = n1 // 2, n2 // 2

    def column(r):
        return _as_rows(b_pk[pl.ds(pl.multiple_of(r * n1, n1), n1), :])

    def body(r, carry):
        w = _as_words(_dot(m1i_ref[r], _blockdiag(column(r), column(r + half))))
        for rr, ww in zip((r, r + half), _halves(w)):
            x_pk[pl.ds(rr, n1h, stride=n2), :] = ww
        return carry

    lax.fori_loop(0, half, body, 0, unroll=DFT_UNROLL)


def _load_chunk(a_pk, k1, n1, n2):
    ra = pl.ds(k1, n2, stride=n1)
    rb = pl.ds(k1 + n1 // 2, n2, stride=n1)
    return jnp.concatenate([_as_rows(a_pk[ra, :]), _as_rows(a_pk[rb, :])], axis=1)


def _hy_spec_kernel(*refs, l, n1, n2, dense):
    if dense:
        tf_ref, tb_ref, sk_ref, fwd_ref, o_ref = refs
        n = n2
        fw = fwd_ref[:, 0:l]
        sf = _dot(fw, tf_ref[0].astype(BF16))
        sb = _dot(fw, tb_ref[0].astype(BF16))
        o_ref[0, 0] = (sf[:n] + sb[:n] + sk_ref[0]).astype(o_ref.dtype)
        o_ref[0, 1] = (sf[n:] - sb[n:]).astype(o_ref.dtype)
        return
    tf_ref, tb_ref, sk_ref, m1f_ref, f2_ref, o_ref, x_pk, a_pk, h_re, h_im = refs
    for run, t_ref in enumerate((tf_ref, tb_ref)):
        x_pk[...] = _pack_pair(t_ref[0], None)
        _fwd_stage1(x_pk, a_pk, m1f_ref, n1, n2)

        def body(k1, carry, run=run):
            x = _dot(f2_ref[...], _load_chunk(a_pk, k1, n1, n2))
            for kk, xx in zip((k1, k1 + n1 // 2), _halves(x)):
                rows = pl.ds(pl.multiple_of(kk * n2, n2), n2)
                if run == 0:
                    h_re[rows, :] = xx[:n2]
                    h_im[rows, :] = xx[n2:]
                else:
                    h_re[rows, :] = h_re[rows, :] + xx[:n2]
                    h_im[rows, :] = h_im[rows, :] - xx[n2:]
            return carry

        lax.fori_loop(0, n1 // 2, body, 0, unroll=DFT_UNROLL)
    o_ref[0, 0] = (h_re[...] + sk_ref[0]).astype(o_ref.dtype)
    o_ref[0, 1] = h_im[...].astype(o_ref.dtype)


def hy_spectrum(taps_f, taps_b, skip, consts, n1, n2):
    o, l, d = taps_f.shape
    n = 2 * l
    dt = HY_DT
    dense = n1 == 1
    tap_spec = pl.BlockSpec((1, l, dt), lambda i, j: (i, 0, j))
    in_specs = [tap_spec, tap_spec, pl.BlockSpec((1, 1, dt), lambda i, j: (i, 0, j))]
    if dense:
        mats = [consts["fwd"]]
        scratch = []
    else:
        mats = [consts["m1f"], consts["f2"]]
        scratch = [pltpu.VMEM((l, dt), I32), pltpu.VMEM((n, dt), I32)] + [pltpu.VMEM((n, dt), F32)] * 2
    in_specs += [pl.BlockSpec(m.shape, lambda i, j, nd=m.ndim: (0,) * nd) for m in mats]
    return pl.pallas_call(
        functools.partial(_hy_spec_kernel, l=l, n1=n1, n2=n2, dense=dense),
        grid=(o, d // dt),
        in_specs=in_specs,
        out_specs=pl.BlockSpec((1, 2, n, dt), lambda i, j: (i, 0, 0, j)),
        out_shape=jax.ShapeDtypeStruct((o, 2, n, d), BF16),
        scratch_shapes=scratch,
        compiler_params=_cparams("arbitrary", "arbitrary"),
        name="hy_spectrum",
    )(taps_f, taps_b, skip.reshape(o, 1, d), *mats)


def _conv3_rows(src, cw, pad_ref, l):
    c = pad_ref.shape[1]
    pad_ref[pl.ds(0, SUBLANES), :] = jnp.zeros((SUBLANES, c), F32)
    pad_ref[pl.ds(SUBLANES + l, SUBLANES), :] = jnp.zeros((SUBLANES, c), F32)
    pad_ref[pl.ds(SUBLANES, l), :] = src
    return (pad_ref[pl.ds(SUBLANES - 1, l), :] * cw[0:1] + pad_ref[pl.ds(SUBLANES, l), :] * cw[1:2]
            + pad_ref[pl.ds(SUBLANES + 1, l), :] * cw[2:3])


def _hy_conv_kernel(*refs, l, n1, n2, dense, conv_z):
    if dense:
        z_ref, g_ref, cwz_ref, cwg_ref, h_ref, fwd_ref, inv_ref, o_ref, pad_ref = refs
    else:
        (z_ref, g_ref, cwz_ref, cwg_ref, h_ref, m1f_ref, m1i_ref, f2_ref, f2i_ref, o_ref,
         x_pk, a_pk, b_pk, pad_ref) = refs
    zs = []
    for e in range(2):
        z = z_ref[e, 0].astype(F32)
        zs.append(_conv3_rows(z, cwz_ref[...], pad_ref, l) if conv_z else z)
    if dense:
        n = n2
        s = _dot(fwd_ref[...], jnp.concatenate(zs, axis=0).astype(BF16))
        hr = h_ref[0, 0].astype(F32)
        hi = h_ref[0, 1].astype(F32)
        zr = s[:n] * hr - s[n:] * hi
        zi = s[:n] * hi + s[n:] * hr
        y = _dot(inv_ref[...], jnp.concatenate([zr, zi], axis=0).astype(BF16))
        ys = (y[:l], y[l:])
    else:
        x_pk[...] = _pack_pair(zs[0], zs[1])
        _fwd_stage1(x_pk, a_pk, m1f_ref, n1, n2)

        def body(k1, carry):
            s = _dot(f2_ref[...], _load_chunk(a_pk, k1, n1, n2))
            kk = (k1, k1 + n1 // 2)
            rows = [pl.ds(pl.multiple_of(k * n2, n2), n2) for k in kk]
            hr = jnp.concatenate([h_ref[0, 0, r, :] for r in rows], axis=1).astype(F32)
            hi = jnp.concatenate([h_ref[0, 1, r, :] for r in rows], axis=1).astype(F32)
            zr = s[:n2] * hr - s[n2:] * hi
            zi = s[:n2] * hi + s[n2:] * hr
            w = _as_words(_dot(f2i_ref[...], jnp.concatenate([zr, zi], axis=0).astype(BF16)))
            for k, ww in zip(kk, _halves(w)):
                b_pk[pl.ds(k, n2, stride=n1), :] = ww
            return carry

        lax.fori_loop(0, n1 // 2, body, 0, unroll=DFT_UNROLL)
        _inv_stage1(b_pk, x_pk, m1i_ref, n1, n2)
        ys = _unpack_pair(x_pk[...])
    for e in range(2):
        g = _conv3_rows(g_ref[e, 0].astype(F32), cwg_ref[...], pad_ref, l)
        o_ref[e, 0] = (g * ys[e]).astype(o_ref.dtype)


def hy_conv(zsrc, zcol, p4, gcol, conv_w, spec, order, consts, n1, n2, conv_z):
    _, pairs, l, _ = zsrc.shape
    d = spec.shape[3]
    n = 2 * l
    dt = HY_DT
    nd = d // dt
    dense = n1 == 1
    once = pl.Buffered(1)
    in_specs = [pl.BlockSpec((2, 1, l, dt), lambda j, i: (0, i, 0, zcol * nd + j)),
                pl.BlockSpec((2, 1, l, dt), lambda j, i: (0, i, 0, gcol * nd + j)),
                pl.BlockSpec((CONV_W, dt), lambda j, i: (0, j)),
                pl.BlockSpec((CONV_W, dt), lambda j, i: (0, gcol * nd + j)),
                pl.BlockSpec((1, 2, n, dt), lambda j, i: (order, 0, 0, j), pipeline_mode=once)]
    if dense:
        mats = [consts["fwd"], consts["inv"]]
        scratch = []
    else:
        mats = [consts["m1f"], consts["m1i"], consts["f2"], consts["f2i"]]
        scratch = [pltpu.VMEM((l, dt), I32)] + [pltpu.VMEM((n, dt), I32)] * 2
    scratch = scratch + [pltpu.VMEM((l + 2 * SUBLANES, dt), F32)]
    in_specs += [pl.BlockSpec(m.shape, lambda j, i, nd_=m.ndim: (0,) * nd_, pipeline_mode=once) for m in mats]
    return pl.pallas_call(
        functools.partial(_hy_conv_kernel, l=l, n1=n1, n2=n2, dense=dense, conv_z=conv_z),
        grid=(nd, pairs),
        in_specs=in_specs,
        out_specs=pl.BlockSpec((2, 1, l, dt), lambda j, i: (0, i, 0, j)),
        out_shape=jax.ShapeDtypeStruct((2, pairs, l, d), BF16),
        scratch_shapes=scratch,
        compiler_params=_cparams("arbitrary", "arbitrary"),
        name="hy_conv",
    )(zsrc, p4, conv_w, conv_w, spec, *mats)


def hyena_mixer(p, conv_w, f_w1, f_b1, f_w2, f_b2, f_w3, sin_freq, skip):
    b, l, d3 = p.shape
    d = d3 // 3
    assert b % 2 == 0 and d % HY_DT == 0
    n1, n2, consts = _dft_consts(l)
    taps_f, taps_b = hy_filter_taps(l, d, f_w1, f_b1, f_w2, f_b2, f_w3, sin_freq)
    spec = hy_spectrum(taps_f, taps_b, skip, consts, n1, n2)
    p4 = p.reshape(2, b // 2, l, d3)
    z = hy_conv(p4, 0, p4, 1, conv_w, spec, 0, consts, n1, n2, True)
    z = hy_conv(z, 0, p4, 2, conv_w, spec, 1, consts, n1, n2, False)
    return z.reshape(b, l, d)


def _head_norm(x, g):
    return x * lax.rsqrt(jnp.mean(x * x, axis=-1, keepdims=True) + EPS) * g


def _rope(x, cos, sin_a, sin_b):
    return (x * cos + pltpu.roll(x, HEAD_DIM // 4, 1) * sin_a
            + pltpu.roll(x, HEAD_DIM - HEAD_DIM // 4, 1) * sin_b)


def _qkv_kernel(x_ref, g_ref, sh_ref, sc_ref, w_ref, qg_ref, kg_ref, cos_ref, sa_ref, sb_ref,
                *out_refs, nq, rope):
    q_ref = out_refs[0] if nq else None
    k_ref, v_ref = out_refs[-2:]
    h = _norm_mod(x_ref[0], g_ref[...], sh_ref[0], sc_ref[0]).astype(BF16)
    acc = _dot(h, w_ref[...])
    hd = HEAD_DIM
    for i in range(nq + N_KV_HEADS):
        t = acc[:, i * hd:(i + 1) * hd]
        t = _head_norm(t, qg_ref[...] if i < nq else kg_ref[...])
        if rope:
            t = _rope(t, cos_ref[...], sa_ref[...], sb_ref[...])
        if i < nq:
            q_ref[0, :, i * hd:(i + 1) * hd] = (t * (hd ** -0.5)).astype(BF16)
        else:
            k_ref[0, :, (i - nq) * hd:(i - nq + 1) * hd] = t.astype(BF16)
    v_ref[0] = acc[:, (nq + N_KV_HEADS) * hd:].astype(BF16)


def qkv_project(x, g, sh, sc, w, q_g, k_g, rope_tabs, nq, rope):
    b, l, d = x.shape
    n = w.shape[1]
    hd = HEAD_DIM
    kvd = N_KV_HEADS * hd
    tm = _row_tile(l)
    cos, sa, sb = rope_tabs
    tab = pl.BlockSpec((tm, hd), lambda i, j: (j, 0))
    vec = pl.BlockSpec((1, 1, d), lambda i, j: (i, 0, 0))
    widths = ([nq * hd] if nq else []) + [kvd, kvd]
    return pl.pallas_call(
        functools.partial(_qkv_kernel, nq=nq, rope=rope),
        grid=(b, l // tm),
        in_specs=[pl.BlockSpec((1, tm, d), lambda i, j: (i, j, 0)),
                  pl.BlockSpec((1, d), lambda i, j: (0, 0)), vec, vec,
                  pl.BlockSpec((d, n), lambda i, j: (0, 0)),
                  pl.BlockSpec((1, hd), lambda i, j: (0, 0)),
                  pl.BlockSpec((1, hd), lambda i, j: (0, 0)),
                  tab, tab, tab],
        out_specs=[pl.BlockSpec((1, tm, wd), lambda i, j: (i, j, 0)) for wd in widths],
        out_shape=[jax.ShapeDtypeStruct((b, l, wd), BF16) for wd in widths],
        compiler_params=_cparams("arbitrary", "arbitrary"),
        name="qkv_project",
    )(x, g.reshape(1, d), sh, sc, w, q_g.reshape(1, hd), k_g.reshape(1, hd), cos, sa, sb)


def _rope_tables(l):
    rows = l // GRID_W
    row = np.repeat(np.arange(rows, dtype=np.float64), GRID_W)
    col = np.tile(np.arange(GRID_W, dtype=np.float64), rows)
    axis = HEAD_DIM // 2
    inv = ROPE_THETA ** (-np.arange(0, axis, 2, dtype=np.float64) / axis)

    def axis_angles(pos):
        a = pos[:, None] * inv[None, :]
        return np.concatenate([a, a], axis=-1)

    ang = np.concatenate([axis_angles(row), axis_angles(col)], axis=-1)
    cos, sin = np.cos(ang), np.sin(ang)
    lane = np.arange(HEAD_DIM)[None, :] % axis
    upper = lane >= axis // 2
    sin_a = np.where(upper, sin, 0.0)
    sin_b = np.where(upper, 0.0, -sin)
    return tuple(jnp.asarray(t, F32) for t in (cos, sin_a, sin_b))


def _attn_kernel(q_ref, k_ref, v_ref, o_ref, *, groups):
    k = k_ref[0]
    v = v_ref[0]
    hd = HEAD_DIM
    for g in range(groups):
        q = q_ref[0, :, g * hd:(g + 1) * hd]
        s = lax.dot_general(q, k, (((1,), (1,)), ((), ())), preferred_element_type=F32)
        m = jnp.max(s, axis=-1, keepdims=True)
        p = jnp.exp(s - m)
        den = jnp.sum(p, axis=-1, keepdims=True)
        o = _dot(p.astype(BF16), v) / den
        o_ref[0, :, g * hd:(g + 1) * hd] = o.astype(o_ref.dtype)


def attention(q, k, v):
    b, l, qd = q.shape
    s = k.shape[1]
    hd = HEAD_DIM
    groups = qd // hd // N_KV_HEADS
    tq = _row_tile(l, 256)
    return pl.pallas_call(
        functools.partial(_attn_kernel, groups=groups),
        grid=(b, N_KV_HEADS, l // tq),
        in_specs=[pl.BlockSpec((1, tq, groups * hd), lambda i, h, j: (i, j, h)),
                  pl.BlockSpec((1, s, hd), lambda i, h, j: (i, 0, h)),
                  pl.BlockSpec((1, s, hd), lambda i, h, j: (i, 0, h))],
        out_specs=pl.BlockSpec((1, tq, groups * hd), lambda i, h, j: (i, j, h)),
        out_shape=jax.ShapeDtypeStruct((b, l, qd), BF16),
        compiler_params=_cparams("arbitrary", "arbitrary", "arbitrary"),
        name="attention",
    )(q, k, v)


def _router_kernel(x_ref, g_ref, sh_ref, sc_ref, wr_ref, h_ref, a_ref):
    h = _norm_mod(x_ref[0], g_ref[...], sh_ref[0], sc_ref[0])
    h_ref[0] = h.astype(BF16)
    logits = lax.dot_general(wr_ref[...], h, (((1,), (1,)), ((), ())), preferred_element_type=F32,
                             precision=HIGHEST)
    e = jnp.exp(logits - jnp.max(logits, axis=0, keepdims=True))
    a_ref[0] = e / jnp.sum(e, axis=0, keepdims=True)


def moe_router(x, g, sh, sc, w_router_t):
    b, n, d = x.shape
    e = w_router_t.shape[0]
    tm = _row_tile(n)
    vec = pl.BlockSpec((1, 1, d), lambda i, j: (i, 0, 0))
    return pl.pallas_call(
        _router_kernel,
        grid=(b, n // tm),
        in_specs=[pl.BlockSpec((1, tm, d), lambda i, j: (i, j, 0)),
                  pl.BlockSpec((1, d), lambda i, j: (0, 0)), vec, vec,
                  pl.BlockSpec((e, d), lambda i, j: (0, 0))],
        out_specs=[pl.BlockSpec((1, tm, d), lambda i, j: (i, j, 0)),
                   pl.BlockSpec((1, e, tm), lambda i, j: (i, 0, j))],
        out_shape=[jax.ShapeDtypeStruct((b, n, d), BF16), jax.ShapeDtypeStruct((b, e, n), F32)],
        compiler_params=_cparams("arbitrary", "arbitrary"),
        name="moe_router",
    )(x, g.reshape(1, d), sh, sc, w_router_t)


def _cumsum_lanes(m, tri_ref):
    e, n = m.shape
    carry = jnp.zeros((e, 1), F32)
    outs = []
    for c in range(n // LANES):
        blk = m[:, c * LANES:(c + 1) * LANES]
        cs = _dot(blk.astype(BF16), tri_ref[...]) + carry
        outs.append(cs)
        carry = cs[:, LANES - 1:LANES]
    return jnp.concatenate(outs, axis=1)


def _select_kernel(a_ref, tri_ref, slot_ref, gate_ref, slot_t_ref, gate_t_ref, *, cap):
    aff = a_ref[0]
    bits = pltpu.bitcast(aff, I32)
    e = aff.shape[0]

    def body(i, thr):
        cand = thr | (1 << (30 - i))
        cnt = jnp.sum((bits >= cand).astype(F32), axis=1, keepdims=True)
        return jnp.where(cnt >= cap, cand, thr)

    thr = lax.fori_loop(0, 31, body, jnp.zeros((e, 1), I32))
    gt = bits > thr
    eq = bits == thr
    n_gt = jnp.sum(gt.astype(F32), axis=1, keepdims=True)
    eq_rank = _cumsum_lanes(eq.astype(F32), tri_ref)
    sel = gt | (eq & (eq_rank <= cap - n_gt))
    self_ = sel.astype(F32)
    pos = _cumsum_lanes(self_, tri_ref) - 1.0
    slot = jnp.where(sel, pos, -1.0)
    gate = jnp.where(sel, aff, 0.0)
    slot_ref[0] = slot
    gate_ref[0] = gate
    slot_t_ref[0] = slot.T
    gate_t_ref[0] = gate.T


def moe_select(aff, cap):
    b, e, n = aff.shape
    tri = jnp.asarray(np.triu(np.ones((LANES, LANES), np.float32)), BF16)
    row = pl.BlockSpec((1, e, n), lambda i: (i, 0, 0))
    col = pl.BlockSpec((1, n, e), lambda i: (i, 0, 0))
    return pl.pallas_call(
        functools.partial(_select_kernel, cap=cap),
        grid=(b,),
        in_specs=[row, pl.BlockSpec((LANES, LANES), lambda i: (0, 0))],
        out_specs=[row, row, col, col],
        out_shape=[jax.ShapeDtypeStruct((b, e, n), F32), jax.ShapeDtypeStruct((b, e, n), F32),
                   jax.ShapeDtypeStruct((b, n, e), F32), jax.ShapeDtypeStruct((b, n, e), F32)],
        compiler_params=_cparams("arbitrary"),
        name="moe_select",
    )(aff, tri)


def _gather_kernel(slot_ref, h_ref, o_ref, *, cap, tk):
    n = h_ref.shape[1]
    jcol = lax.broadcasted_iota(I32, (cap, tk), 0).astype(F32)
    acc = jnp.zeros((cap, h_ref.shape[2]), F32)
    for c in range(n // tk):
        srow = slot_ref[0, 0, :, c * tk:(c + 1) * tk]
        onehot = jnp.where(jcol == srow, 1.0, 0.0).astype(BF16)
        acc = acc + _dot(onehot, h_ref[0, c * tk:(c + 1) * tk, :])
    o_ref[0, 0] = acc.astype(o_ref.dtype)


def moe_gather(slot, h, cap):
    b, e, n = slot.shape
    d = h.shape[2]
    tk = min(n, 512)
    return pl.pallas_call(
        functools.partial(_gather_kernel, cap=cap, tk=tk),
        grid=(b, e),
        in_specs=[pl.BlockSpec((1, 1, 1, n), lambda i, j: (i, j, 0, 0)),
                  pl.BlockSpec((1, n, d), lambda i, j: (i, 0, 0))],
        out_specs=pl.BlockSpec((1, 1, cap, d), lambda i, j: (i, j, 0, 0)),
        out_shape=jax.ShapeDtypeStruct((b, e, cap, d), BF16),
        compiler_params=_cparams("arbitrary", "arbitrary"),
        name="moe_gather",
    )(slot.reshape(b, e, 1, n), h)


def _ffn_kernel(x_ref, wg_ref, wu_ref, wd_ref, o_ref):
    x = x_ref[0, 0]
    a = _dot(x, wg_ref[0])
    u = _dot(x, wu_ref[0])
    hid = (a / (1.0 + jnp.exp(-a))) * u
    o_ref[0, 0] = _dot(hid.astype(BF16), wd_ref[0]).astype(o_ref.dtype)


def moe_ffn(xs, w_gate, w_up, w_down):
    b, e, cap, d = xs.shape
    f = w_gate.shape[2]
    return pl.pallas_call(
        _ffn_kernel,
        grid=(e, b),
        in_specs=[pl.BlockSpec((1, 1, cap, d), lambda j, i: (i, j, 0, 0)),
                  pl.BlockSpec((1, d, f), lambda j, i: (j, 0, 0)),
                  pl.BlockSpec((1, d, f), lambda j, i: (j, 0, 0)),
                  pl.BlockSpec((1, f, d), lambda j, i: (j, 0, 0))],
        out_specs=pl.BlockSpec((1, 1, cap, d), lambda j, i: (i, j, 0, 0)),
        out_shape=jax.ShapeDtypeStruct((b, e, cap, d), BF16),
        compiler_params=_cparams("arbitrary", "arbitrary"),
        name="moe_ffn",
    )(xs, w_gate, w_up, w_down)


def _combine_kernel(slot_ref, gate_ref, ys_ref, lat_ref, gt_ref, o_ref, *, cap, ne):
    tn = lat_ref.shape[1]
    jrow = lax.broadcasted_iota(I32, (tn, cap), 1).astype(F32)
    slot = slot_ref[0]
    gate = gate_ref[0]
    acc = jnp.zeros((tn, lat_ref.shape[2]), F32)
    for e in range(ne):
        onehot = jnp.where(jrow == slot[:, e:e + 1], 1.0, 0.0).astype(BF16)
        acc = acc + gate[:, e:e + 1] * _dot(onehot, ys_ref[0, e])
    o_ref[0] = lat_ref[0] + gt_ref[0] * acc


def moe_combine(slot_t, gate_t, ys, lat, gate_vec, cap):
    b, n, e = slot_t.shape
    d = lat.shape[2]
    tn = _row_tile(n, 256)
    return pl.pallas_call(
        functools.partial(_combine_kernel, cap=cap, ne=e),
        grid=(b, n // tn),
        in_specs=[pl.BlockSpec((1, tn, e), lambda i, j: (i, j, 0)),
                  pl.BlockSpec((1, tn, e), lambda i, j: (i, j, 0)),
                  pl.BlockSpec((1, e, cap, d), lambda i, j: (i, 0, 0, 0), pipeline_mode=pl.Buffered(1)),
                  pl.BlockSpec((1, tn, d), lambda i, j: (i, j, 0)),
                  pl.BlockSpec((1, 1, d), lambda i, j: (i, 0, 0))],
        out_specs=pl.BlockSpec((1, tn, d), lambda i, j: (i, j, 0)),
        out_shape=jax.ShapeDtypeStruct((b, n, d), F32),
        compiler_params=_cparams("arbitrary", "arbitrary"),
        name="moe_combine",
    )(slot_t, gate_t, ys, lat, gate_vec)


def expert_choice_moe(lat, g, sh, sc, gate_vec, w_router, w_gate, w_up, w_down):
    b, n, d = lat.shape
    e = w_router.shape[1]
    cap = EC_CAPACITY_FACTOR * n // e
    h, aff = moe_router(lat, g, sh, sc, w_router.T)
    slot, _, slot_t, gate_t = moe_select(aff, cap)
    xs = moe_gather(slot, h, cap)
    ys = moe_ffn(xs, w_gate, w_up, w_down)
    return moe_combine(slot_t, gate_t, ys, lat, gate_vec, cap)


def kernel(x, c, ctx, c_ctx, ada_w, ada_b, norm_g, sc_w_in, sc_conv, sc_w_out, hy_w_in, hy_conv_w, hy_f_w1,
           hy_f_b1, hy_f_w2, hy_f_b2, hy_f_w3, hy_sin_freq, hy_skip, hy_w_out, at_w_qkv, at_q_g, at_k_g, at_w_o,
           moe_router_w, moe_w_gate, moe_w_up, moe_w_down):
    b, l, d = x.shape
    depth = ada_w.shape[0]
    assert b + 1 <= MOD_ROWS
    qd = at_w_o.shape[1]
    nq = qd // HEAD_DIM
    attn_layers = list(range(MIX_ATTN, depth, N_MIXERS))
    last_attn = max(attn_layers, default=-1)

    cond = jnp.zeros((MOD_ROWS, d), F32).at[:b].set(c).at[b].set(c_ctx)
    mods = ada_modulation(cond, ada_w, ada_b)

    lat, cx = x, ctx
    for i in range(depth):
        kind = i % N_MIXERS
        j = i // N_MIXERS
        need_ctx = i <= last_attn
        upd_ctx = i < last_attn
        m_l = mods[i, :b].reshape(b, 1, 6, d)
        sh1, sc1, g1, sh2, sc2, g2 = (m_l[:, :, k] for k in range(6))
        if need_ctx:
            m_c = jnp.broadcast_to(mods[i, b].reshape(1, 1, 6, d), (b, 1, 6, d))
            csh1, csc1, cg1, csh2, csc2, cg2 = (m_c[:, :, k] for k in range(6))
        streams = [(lat, sh1, sc1, g1)]
        if upd_ctx:
            streams.append((cx, csh1, csc1, cg1))

        outs = []
        if kind == MIX_SHORTCONV:
            w_in, w_out = sc_w_in[j].astype(BF16), sc_w_out[j].astype(BF16)
            for s, sh, sc, gt in streams:
                p = nm_matmul(s, norm_g[i, 0], sh, sc, w_in)
                outs.append(sc_out(p, sc_conv[j], w_out, s, gt))
        elif kind == MIX_HYENA:
            w_in, w_out = hy_w_in[j].astype(BF16), hy_w_out[j].astype(BF16)
            for s, sh, sc, gt in streams:
                p = nm_matmul(s, norm_g[i, 0], sh, sc, w_in)
                z = hyena_mixer(p, hy_conv_w[j], hy_f_w1[j], hy_f_b1[j], hy_f_w2[j], hy_f_b2[j], hy_f_w3[j],
                                hy_sin_freq[j], hy_skip[j])
                outs.append(proj_residual(z, w_out, s, gt))
        else:
            w_qkv, w_o = at_w_qkv[j].astype(BF16), at_w_o[j].astype(BF16)
            q_l, k_l, v_l = qkv_project(lat, norm_g[i, 0], sh1, sc1, w_qkv, at_q_g[j], at_k_g[j],
                                        _rope_tables(l), nq, True)
            lc = cx.shape[1]
            dummy = tuple(jnp.zeros((lc, HEAD_DIM), F32) for _ in range(3))
            if upd_ctx:
                q_c, k_c, v_c = qkv_project(cx, norm_g[i, 0], csh1, csc1, w_qkv, at_q_g[j], at_k_g[j],
                                            dummy, nq, False)
            else:
                k_c, v_c = qkv_project(cx, norm_g[i, 0], csh1, csc1, w_qkv[:, qd:], at_q_g[j], at_k_g[j],
                                       dummy, 0, False)
            o_l = attention(q_l, jnp.concatenate([k_c, k_l], axis=1), jnp.concatenate([v_c, v_l], axis=1))
            outs.append(proj_residual(o_l, w_o, lat, g1))
            if upd_ctx:
                outs.append(proj_residual(attention(q_c, k_c, v_c), w_o, cx, cg1))

        wg, wu, wd = (w[i].astype(BF16) for w in (moe_w_gate, moe_w_up, moe_w_down))
        lat = expert_choice_moe(outs[0], norm_g[i, 1], sh2, sc2, g2, moe_router_w[i], wg, wu, wd)
        if upd_ctx:
            cx = expert_choice_moe(outs[1], norm_g[i, 1], csh2, csc2, cg2, moe_router_w[i], wg, wu, wd)
    return lat
```

```python
import functools
import math

import jax
import jax.numpy as jnp
import numpy as np
from jax import lax
from jax.experimental import pallas as pl
from jax.experimental.pallas import tpu as pltpu

F32 = jnp.float32
BF16 = jnp.bfloat16
I32 = jnp.int32
HIGHEST = lax.Precision.HIGHEST

N_MIXERS = 3
MIX_SHORTCONV, MIX_HYENA, MIX_ATTN = 0, 1, 2
CONV_W = 3
EPS = 1e-6
HY_ORDER = 2
HY_EMB = 33
HY_BANDS = (HY_EMB - 1) // 2
HY_FAST_DECAY = 0.3
HY_SLOW_DECAY = 1.5
HY_TARGET = 1e-2
HEAD_DIM = 128
N_KV_HEADS = 2
GRID_W = 64
ROPE_THETA = 10000.0
EC_CAPACITY_FACTOR = 2

LANES = 128
SUBLANES = 8
BF16_ROWS = 16
VMEM_LIMIT_BYTES = 56 * 1024 * 1024
MOD_ROWS = 16
FFT_N2 = 128
HY_DT = LANES
DFT_UNROLL = 2


def _cparams(*sem):
    return pltpu.CompilerParams(dimension_semantics=sem, vmem_limit_bytes=VMEM_LIMIT_BYTES)


def _row_tile(n, cap=512):
    t = min(n, cap)
    assert n % t == 0
    return t


def _dot(a, b):
    return jnp.dot(a, b, preferred_element_type=F32)


def _ada_kernel(s_ref, w_ref, b_ref, o_ref):
    s = s_ref[...]
    s = s / (1.0 + jnp.exp(-s))
    o_ref[0] = jnp.dot(s, w_ref[0], preferred_element_type=F32, precision=HIGHEST) + b_ref[0]


def ada_modulation(cond, ada_w, ada_b):
    depth, d, n = ada_w.shape
    tn = 1536 if n % 1536 == 0 else n
    return pl.pallas_call(
        _ada_kernel,
        grid=(depth, n // tn),
        in_specs=[pl.BlockSpec((MOD_ROWS, d), lambda i, j: (0, 0)),
                  pl.BlockSpec((1, d, tn), lambda i, j: (i, 0, j)),
                  pl.BlockSpec((1, 1, tn), lambda i, j: (i, 0, j))],
        out_specs=pl.BlockSpec((1, MOD_ROWS, tn), lambda i, j: (i, 0, j)),
        out_shape=jax.ShapeDtypeStruct((depth, MOD_ROWS, n), F32),
        compiler_params=_cparams("arbitrary", "arbitrary"),
        name="ada_modulation",
    )(cond, ada_w, ada_b.reshape(depth, 1, n))


def _norm_mod(x, g, sh, sc):
    y = x * lax.rsqrt(jnp.mean(x * x, axis=-1, keepdims=True) + EPS)
    return (y * g) * (1.0 + sc) + sh


def _nm_matmul_kernel(x_ref, g_ref, sh_ref, sc_ref, w_ref, o_ref):
    h = _norm_mod(x_ref[0], g_ref[...], sh_ref[0], sc_ref[0]).astype(BF16)
    o_ref[0] = _dot(h, w_ref[...]).astype(o_ref.dtype)


def nm_matmul(x, g, sh, sc, w, out_dtype=BF16):
    b, l, d = x.shape
    n = w.shape[1]
    tm = _row_tile(l)
    return pl.pallas_call(
        _nm_matmul_kernel,
        grid=(b, l // tm),
        in_specs=[pl.BlockSpec((1, tm, d), lambda i, j: (i, j, 0)),
                  pl.BlockSpec((1, d), lambda i, j: (0, 0)),
                  pl.BlockSpec((1, 1, d), lambda i, j: (i, 0, 0)),
                  pl.BlockSpec((1, 1, d), lambda i, j: (i, 0, 0)),
                  pl.BlockSpec((d, n), lambda i, j: (0, 0))],
        out_specs=pl.BlockSpec((1, tm, n), lambda i, j: (i, j, 0)),
        out_shape=jax.ShapeDtypeStruct((b, l, n), out_dtype),
        compiler_params=_cparams("arbitrary", "arbitrary"),
        name="nm_matmul",
    )(x, g.reshape(1, d), sh, sc, w)


def _proj_res_kernel(a_ref, w_ref, lat_ref, gt_ref, o_ref):
    o_ref[0] = lat_ref[0] + gt_ref[0] * _dot(a_ref[0], w_ref[...])


def proj_residual(a, w, lat, gate):
    b, l, k = a.shape
    d = w.shape[1]
    tm = _row_tile(l)
    return pl.pallas_call(
        _proj_res_kernel,
        grid=(b, l // tm),
        in_specs=[pl.BlockSpec((1, tm, k), lambda i, j: (i, j, 0)),
                  pl.BlockSpec((k, d), lambda i, j: (0, 0)),
                  pl.BlockSpec((1, tm, d), lambda i, j: (i, j, 0)),
                  pl.BlockSpec((1, 1, d), lambda i, j: (i, 0, 0))],
        out_specs=pl.BlockSpec((1, tm, d), lambda i, j: (i, j, 0)),
        out_shape=jax.ShapeDtypeStruct((b, l, d), F32),
        compiler_params=_cparams("arbitrary", "arbitrary"),
        name="proj_residual",
    )(a, w, lat, gate)


def _sc_out_kernel(pb_ref, pc_ref, pv_ref, cp_ref, vp_ref, cn_ref, vn_ref, cw_ref, w_ref, lat_ref, gt_ref,
                   o_ref, u_ref, *, tm):
    j = pl.program_id(1)
    nj = pl.num_programs(1)
    d = u_ref.shape[1]
    u_ref[pl.ds(SUBLANES, tm), :] = pc_ref[0].astype(F32) * pv_ref[0].astype(F32)
    prev = cp_ref[0, BF16_ROWS - 1:BF16_ROWS, :].astype(F32) * vp_ref[0, BF16_ROWS - 1:BF16_ROWS, :].astype(F32)
    nxt = cn_ref[0, 0:1, :].astype(F32) * vn_ref[0, 0:1, :].astype(F32)
    prev = jnp.where(j == 0, 0.0, prev)
    nxt = jnp.where(j == nj - 1, 0.0, nxt)
    u_ref[pl.ds(0, SUBLANES), :] = jnp.broadcast_to(prev, (SUBLANES, d))
    u_ref[pl.ds(SUBLANES + tm, SUBLANES), :] = jnp.broadcast_to(nxt, (SUBLANES, d))
    cw = cw_ref[...]
    y = (u_ref[pl.ds(SUBLANES - 1, tm), :] * cw[0:1] + u_ref[pl.ds(SUBLANES, tm), :] * cw[1:2]
         + u_ref[pl.ds(SUBLANES + 1, tm), :] * cw[2:3])
    m = (pb_ref[0].astype(F32) * y).astype(BF16)
    o_ref[0] = lat_ref[0] + gt_ref[0] * _dot(m, w_ref[...])


def sc_out(p, conv_w, w_out, lat, gate):
    b, l, d3 = p.shape
    d = d3 // 3
    tm = _row_tile(l)
    hb = tm // BF16_ROWS
    nhb = l // BF16_ROWS

    def prev_map(col):
        return lambda i, j: (i, jnp.maximum(j * hb - 1, 0), col)

    def next_map(col):
        return lambda i, j: (i, jnp.minimum((j + 1) * hb, nhb - 1), col)

    return pl.pallas_call(
        functools.partial(_sc_out_kernel, tm=tm),
        grid=(b, l // tm),
        in_specs=[pl.BlockSpec((1, tm, d), lambda i, j: (i, j, 0)),
                  pl.BlockSpec((1, tm, d), lambda i, j: (i, j, 1)),
                  pl.BlockSpec((1, tm, d), lambda i, j: (i, j, 2)),
                  pl.BlockSpec((1, BF16_ROWS, d), prev_map(1)),
                  pl.BlockSpec((1, BF16_ROWS, d), prev_map(2)),
                  pl.BlockSpec((1, BF16_ROWS, d), next_map(1)),
                  pl.BlockSpec((1, BF16_ROWS, d), next_map(2)),
                  pl.BlockSpec((CONV_W, d), lambda i, j: (0, 0)),
                  pl.BlockSpec((d, d), lambda i, j: (0, 0)),
                  pl.BlockSpec((1, tm, d), lambda i, j: (i, j, 0)),
                  pl.BlockSpec((1, 1, d), lambda i, j: (i, 0, 0))],
        out_specs=pl.BlockSpec((1, tm, d), lambda i, j: (i, j, 0)),
        out_shape=jax.ShapeDtypeStruct((b, l, d), F32),
        scratch_shapes=[pltpu.VMEM((tm + 2 * SUBLANES, d), F32)],
        compiler_params=_cparams("arbitrary", "arbitrary"),
        name="sc_out",
    )(p, p, p, p, p, p, p, conv_w, w_out, lat, gate)


def _hy_filter_kernel(z_ref, w1_ref, b1_ref, w2_ref, b2_ref, fr_ref, w3f_ref, w3b_ref, t_ref, dl_ref,
                      of_ref, ob_ref):
    fr = fr_ref[...]
    a = jnp.sin(fr * (jnp.dot(z_ref[...], w1_ref[...], preferred_element_type=F32, precision=HIGHEST)
                      + b1_ref[...]))
    a = jnp.sin(fr * (jnp.dot(a, w2_ref[...], preferred_element_type=F32, precision=HIGHEST) + b2_ref[...]))
    decay = jnp.exp(-t_ref[...] * dl_ref[...])
    hf = jnp.dot(a, w3f_ref[...], preferred_element_type=F32, precision=HIGHEST) * decay
    hb = jnp.dot(a, w3b_ref[...], preferred_element_type=F32, precision=HIGHEST) * decay
    row = lax.broadcasted_iota(I32, hb.shape, 0)
    hb = jnp.where(row == 0, 0.0, hb)
    inv = 1.0 / (jnp.sum(jnp.abs(hf), axis=0, keepdims=True) + jnp.sum(jnp.abs(hb), axis=0, keepdims=True))
    of_ref[0] = hf * inv
    ob_ref[0] = hb * inv


def hy_filter_taps(l, d, f_w1, f_b1, f_w2, f_b2, f_w3, sin_freq):
    hid = f_w1.shape[1]
    hp = LANES
    t = np.linspace(0.0, 1.0, l)[:, None]
    bands = np.linspace(1e-4, HY_BANDS - 1, HY_BANDS)[None, :]
    ang = (2.0 * math.pi / l) * np.arange(l)[:, None] * bands
    z = np.zeros((l, hp), np.float64)
    z[:, :HY_EMB] = np.concatenate([t, np.cos(ang), -np.sin(ang)], axis=-1)
    deltas = np.abs(np.linspace(math.log(HY_TARGET) / HY_SLOW_DECAY, math.log(HY_TARGET) / HY_FAST_DECAY, d))[None, :]

    def pad2(a, r, c):
        return jnp.zeros((r, c), F32).at[:a.shape[0], :a.shape[1]].set(a.astype(F32))

    w1 = pad2(f_w1, hp, hp)
    w2 = pad2(f_w2, hp, hp)
    w3 = pad2(f_w3, hp, f_w3.shape[1])
    b1 = pad2(f_b1[None], 1, hp)
    b2 = pad2(f_b2[None], 1, hp)
    fr = pad2(sin_freq[None], 1, hp)
    tc = min(d, 2 * LANES)
    nd = d // tc
    full = lambda o, j: (0, 0)
    return pl.pallas_call(
        _hy_filter_kernel,
        grid=(HY_ORDER, nd),
        in_specs=[pl.BlockSpec((l, hp), full), pl.BlockSpec((hp, hp), full), pl.BlockSpec((1, hp), full),
                  pl.BlockSpec((hp, hp), full), pl.BlockSpec((1, hp), full), pl.BlockSpec((1, hp), full),
                  pl.BlockSpec((hp, tc), lambda o, j: (0, o * nd + j)),
                  pl.BlockSpec((hp, tc), lambda o, j: (0, (HY_ORDER + o) * nd + j)),
                  pl.BlockSpec((l, 1), full),
                  pl.BlockSpec((1, tc), lambda o, j: (0, j))],
        out_specs=[pl.BlockSpec((1, l, tc), lambda o, j: (o, 0, j)),
                   pl.BlockSpec((1, l, tc), lambda o, j: (o, 0, j))],
        out_shape=[jax.ShapeDtypeStruct((HY_ORDER, l, d), F32)] * 2,
        compiler_params=_cparams("arbitrary", "arbitrary"),
        name="hy_filter_taps",
    )(jnp.asarray(z, F32), w1, b1, w2, b2, fr, w3, w3, jnp.asarray(t, F32), jnp.asarray(deltas, F32))


def _dft_consts(l):
    n = 2 * l
    if n <= 4 * FFT_N2:
        n1, n2 = 1, n
    else:
        n1, n2 = n // FFT_N2, FFT_N2
    if n1 == 1:
        k = np.arange(n)[:, None]
        t = np.arange(l)[None, :]
        ang = 2.0 * np.pi * (k * t % n) / n
        c, s = np.cos(ang), np.sin(ang)
        fwd = np.block([[c, s], [-s, c]])
        inv = np.block([[c.T, -s.T], [s.T, c.T]]) / n
        return n1, n2, dict(fwd=jnp.asarray(fwd, BF16), inv=jnp.asarray(inv, BF16))
    n1h = n1 // 2
    k1 = np.arange(n1)[None, :, None]
    m1 = np.arange(n1h)[None, None, :]
    r2 = np.arange(n2)[:, None, None]
    ang = 2.0 * np.pi * ((k1 * (n2 * m1 + r2)) % n) / n
    c, s = np.cos(ang), np.sin(ang)
    m1f = np.concatenate([np.concatenate([c, s], 2), np.concatenate([-s, c], 2)], 1)
    ct, st = np.swapaxes(c, 1, 2) / n, np.swapaxes(s, 1, 2) / n
    m1i = np.concatenate([np.concatenate([ct, -st], 2), np.concatenate([st, ct], 2)], 1)
    a2 = 2.0 * np.pi * ((np.arange(n2)[:, None] * np.arange(n2)[None, :]) % n2) / n2
    c2, s2 = np.cos(a2), np.sin(a2)
    f2 = np.block([[c2, s2], [-s2, c2]])
    f2i = np.block([[c2, -s2], [s2, c2]])

    def il(m):
        return np.stack([np.arange(m), m + np.arange(m)], 1).reshape(-1)

    m1f = m1f[:, il(n1), :][:, :, il(n1h)]
    m1i = m1i[:, il(n1h), :][:, :, il(n1)]
    f2 = f2[:, il(n2)]
    f2i = f2i[il(n2), :]
    half = n2 // 2
    m1f = np.concatenate([m1f[:half], m1f[half:]], 2)
    m1i = np.concatenate([m1i[:half], m1i[half:]], 2)
    return n1, n2, dict(m1f=jnp.asarray(m1f, BF16), m1i=jnp.asarray(m1i, BF16),
                        f2=jnp.asarray(f2, BF16), f2i=jnp.asarray(f2i, BF16))


def _blockdiag(a, b):
    z = jnp.zeros_like(a)
    return jnp.concatenate([jnp.concatenate([a, z], axis=1), jnp.concatenate([z, b], axis=1)], axis=0)


def _halves(y):
    return y[:, :LANES], y[:, LANES:]


HI_HALF = -65536


def _pack_pair(re, im):
    lo = (pltpu.bitcast(re.astype(BF16).astype(F32), I32) >> 16) & 0xFFFF
    if im is None:
        return lo
    return lo | (pltpu.bitcast(im.astype(BF16).astype(F32), I32) & HI_HALF)


def _unpack_pair(w):
    return pltpu.bitcast(w << 16, F32), pltpu.bitcast(w & HI_HALF, F32)


def _as_rows(w):
    return pltpu.bitcast(w, BF16)


def _as_words(y):
    return pltpu.bitcast(y.astype(BF16), I32)


def _fwd_stage1(x_pk, a_pk, m1f_ref, n1, n2):
    n1h, half = n1 // 2, n2 // 2

    def column(r):
        return _as_rows(x_pk[pl.ds(r, n1h, stride=n2), :])

    def body(r, carry):
        w = _as_words(_dot(m1f_ref[r], _blockdiag(column(r), column(r + half))))
        for rr, ww in zip((r, r + half), _halves(w)):
            a_pk[pl.ds(pl.multiple_of(rr * n1, n1), n1), :] = ww
        return carry

    lax.fori_loop(0, half, body, 0, unroll=DFT_UNROLL)


def _inv_stage1(b_pk, x_pk, m1i_ref, n1, n2):
    n1h, half = n1 // 2, n2 // 2

    def column(r):
        return _as_rows(b_pk[pl.ds(pl.multiple_of(r * n1, n1), n1), :])

    def body(r, carry):
        w = _as_words(_dot(m1i_ref[r], _blockdiag(column(r), column(r + half))))
        for rr, ww in zip((r, r + half), _halves(w)):
            x_pk[pl.ds(rr, n1h, stride=n2), :] = ww
        return carry

    lax.fori_loop(0, half, body, 0, unroll=DFT_UNROLL)


def _load_chunk(a_pk, k1, n1, n2):
    ra = pl.ds(k1, n2, stride=n1)
    rb = pl.ds(k1 + n1 // 2, n2, stride=n1)
    return jnp.concatenate([_as_rows(a_pk[ra, :]), _as_rows(a_pk[rb, :])], axis=1)


def _hy_spec_kernel(*refs, l, n1, n2, dense):
    if dense:
        tf_ref, tb_ref, sk_ref, fwd_ref, o_ref = refs
        n = n2
        fw = fwd_ref[:, 0:l]
        sf = _dot(fw, tf_ref[0].astype(BF16))
        sb = _dot(fw, tb_ref[0].astype(BF16))
        o_ref[0, 0] = (sf[:n] + sb[:n] + sk_ref[0]).astype(o_ref.dtype)
        o_ref[0, 1] = (sf[n:] - sb[n:]).astype(o_ref.dtype)
        return
    tf_ref, tb_ref, sk_ref, m1f_ref, f2_ref, o_ref, x_pk, a_pk, h_re, h_im = refs
    for run, t_ref in enumerate((tf_ref, tb_ref)):
        x_pk[...] = _pack_pair(t_ref[0], None)
        _fwd_stage1(x_pk, a_pk, m1f_ref, n1, n2)

        def body(k1, carry, run=run):
            x = _dot(f2_ref[...], _load_chunk(a_pk, k1, n1, n2))
            for kk, xx in zip((k1, k1 + n1 // 2), _halves(x)):
                rows = pl.ds(pl.multiple_of(kk * n2, n2), n2)
                if run == 0:
                    h_re[rows, :] = xx[:n2]
                    h_im[rows, :] = xx[n2:]
                else:
                    h_re[rows, :] = h_re[rows, :] + xx[:n2]
                    h_im[rows, :] = h_im[rows, :] - xx[n2:]
            return carry

        lax.fori_loop(0, n1 // 2, body, 0, unroll=DFT_UNROLL)
    o_ref[0, 0] = (h_re[...] + sk_ref[0]).astype(o_ref.dtype)
    o_ref[0, 1] = h_im[...].astype(o_ref.dtype)


def hy_spectrum(taps_f, taps_b, skip, consts, n1, n2):
    o, l, d = taps_f.shape
    n = 2 * l
    dt = HY_DT
    dense = n1 == 1
    tap_spec = pl.BlockSpec((1, l, dt), lambda i, j: (i, 0, j))
    in_specs = [tap_spec, tap_spec, pl.BlockSpec((1, 1, dt), lambda i, j: (i, 0, j))]
    if dense:
        mats = [consts["fwd"]]
        scratch = []
    else:
        mats = [consts["m1f"], consts["f2"]]
        scratch = [pltpu.VMEM((l, dt), I32), pltpu.VMEM((n, dt), I32)] + [pltpu.VMEM((n, dt), F32)] * 2
    in_specs += [pl.BlockSpec(m.shape, lambda i, j, nd=m.ndim: (0,) * nd) for m in mats]
    return pl.pallas_call(
        functools.partial(_hy_spec_kernel, l=l, n1=n1, n2=n2, dense=dense),
        grid=(o, d // dt),
        in_specs=in_specs,
        out_specs=pl.BlockSpec((1, 2, n, dt), lambda i, j: (i, 0, 0, j)),
        out_shape=jax.ShapeDtypeStruct((o, 2, n, d), BF16),
        scratch_shapes=scratch,
        compiler_params=_cparams("arbitrary", "arbitrary"),
        name="hy_spectrum",
    )(taps_f, taps_b, skip.reshape(o, 1, d), *mats)


def _conv3_rows(src, cw, pad_ref, l):
    c = pad_ref.shape[1]
    pad_ref[pl.ds(0, SUBLANES), :] = jnp.zeros((SUBLANES, c), F32)
    pad_ref[pl.ds(SUBLANES + l, SUBLANES), :] = jnp.zeros((SUBLANES, c), F32)
    pad_ref[pl.ds(SUBLANES, l), :] = src
    return (pad_ref[pl.ds(SUBLANES - 1, l), :] * cw[0:1] + pad_ref[pl.ds(SUBLANES, l), :] * cw[1:2]
            + pad_ref[pl.ds(SUBLANES + 1, l), :] * cw[2:3])


def _hy_conv_kernel(*refs, l, n1, n2, dense, conv_z):
    if dense:
        z_ref, g_ref, cwz_ref, cwg_ref, h_ref, fwd_ref, inv_ref, o_ref, pad_ref = refs
    else:
        (z_ref, g_ref, cwz_ref, cwg_ref, h_ref, m1f_ref, m1i_ref, f2_ref, f2i_ref, o_ref,
         x_pk, a_pk, b_pk, pad_ref) = refs
    zs = []
    for e in range(2):
        z = z_ref[e, 0].astype(F32)
        zs.append(_conv3_rows(z, cwz_ref[...], pad_ref, l) if conv_z else z)
    if dense:
        n = n2
        s = _dot(fwd_ref[...], jnp.concatenate(zs, axis=0).astype(BF16))
        hr = h_ref[0, 0].astype(F32)
        hi = h_ref[0, 1].astype(F32)
        zr = s[:n] * hr - s[n:] * hi
        zi = s[:n] * hi + s[n:] * hr
        y = _dot(inv_ref[...], jnp.concatenate([zr, zi], axis=0).astype(BF16))
        ys = (y[:l], y[l:])
    else:
        x_pk[...] = _pack_pair(zs[0], zs[1])
        _fwd_stage1(x_pk, a_pk, m1f_ref, n1, n2)

        def body(k1, carry):
            s = _dot(f2_ref[...], _load_chunk(a_pk, k1, n1, n2))
            kk = (k1, k1 + n1 // 2)
            rows = [pl.ds(pl.multiple_of(k * n2, n2), n2) for k in kk]
            hr = jnp.concatenate([h_ref[0, 0, r, :] for r in rows], axis=1).astype(F32)
            hi = jnp.concatenate([h_ref[0, 1, r, :] for r in rows], axis=1).astype(F32)
            zr = s[:n2] * hr - s[n2:] * hi
            zi = s[:n2] * hi + s[n2:] * hr
            w = _as_words(_dot(f2i_ref[...], jnp.concatenate([zr, zi], axis=0).astype(BF16)))
            for k, ww in zip(kk, _halves(w)):
                b_pk[pl.ds(k, n2, stride=n1), :] = ww
            return carry

        lax.fori_loop(0, n1 // 2, body, 0, unroll=DFT_UNROLL)
        _inv_stage1(b_pk, x_pk, m1i_ref, n1, n2)
        ys = _unpack_pair(x_pk[...])
    for e in range(2):
        g = _conv3_rows(g_ref[e, 0].astype(F32), cwg_ref[...], pad_ref, l)
        o_ref[e, 0] = (g * ys[e]).astype(o_ref.dtype)


def hy_conv(zsrc, zcol, p4, gcol, conv_w, spec, order, consts, n1, n2, conv_z):
    _, pairs, l, _ = zsrc.shape
    d = spec.shape[3]
    n = 2 * l
    dt = HY_DT
    nd = d // dt
    dense = n1 == 1
    once = pl.Buffered(1)
    in_specs = [pl.BlockSpec((2, 1, l, dt), lambda j, i: (0, i, 0, zcol * nd + j)),
                pl.BlockSpec((2, 1, l, dt), lambda j, i: (0, i, 0, gcol * nd + j)),
                pl.BlockSpec((CONV_W, dt), lambda j, i: (0, j)),
                pl.BlockSpec((CONV_W, dt), lambda j, i: (0, gcol * nd + j)),
                pl.BlockSpec((1, 2, n, dt), lambda j, i: (order, 0, 0, j), pipeline_mode=once)]
    if dense:
        mats = [consts["fwd"], consts["inv"]]
        scratch = []
    else:
        mats = [consts["m1f"], consts["m1i"], consts["f2"], consts["f2i"]]
        scratch = [pltpu.VMEM((l, dt), I32)] + [pltpu.VMEM((n, dt), I32)] * 2
    scratch = scratch + [pltpu.VMEM((l + 2 * SUBLANES, dt), F32)]
    in_specs += [pl.BlockSpec(m.shape, lambda j, i, nd_=m.ndim: (0,) * nd_, pipeline_mode=once) for m in mats]
    return pl.pallas_call(
        functools.partial(_hy_conv_kernel, l=l, n1=n1, n2=n2, dense=dense, conv_z=conv_z),
        grid=(nd, pairs),
        in_specs=in_specs,
        out_specs=pl.BlockSpec((2, 1, l, dt), lambda j, i: (0, i, 0, j)),
        out_shape=jax.ShapeDtypeStruct((2, pairs, l, d), BF16),
        scratch_shapes=scratch,
        compiler_params=_cparams("arbitrary", "arbitrary"),
        name="hy_conv",
    )(zsrc, p4, conv_w, conv_w, spec, *mats)


def hyena_mixer(p, conv_w, f_w1, f_b1, f_w2, f_b2, f_w3, sin_freq, skip):
    b, l, d3 = p.shape
    d = d3 // 3
    assert b % 2 == 0 and d % HY_DT == 0
    n1, n2, consts = _dft_consts(l)
    taps_f, taps_b = hy_filter_taps(l, d, f_w1, f_b1, f_w2, f_b2, f_w3, sin_freq)
    spec = hy_spectrum(taps_f, taps_b, skip, consts, n1, n2)
    p4 = p.reshape(2, b // 2, l, d3)
    z = hy_conv(p4, 0, p4, 1, conv_w, spec, 0, consts, n1, n2, True)
    z = hy_conv(z, 0, p4, 2, conv_w, spec, 1, consts, n1, n2, False)
    return z.reshape(b, l, d)


def _head_norm(x, g):
    return x * lax.rsqrt(jnp.mean(x * x, axis=-1, keepdims=True) + EPS) * g


def _rope(x, cos, sin_a, sin_b):
    return (x * cos + pltpu.roll(x, HEAD_DIM // 4, 1) * sin_a
            + pltpu.roll(x, HEAD_DIM - HEAD_DIM // 4, 1) * sin_b)


def _qkv_kernel(x_ref, g_ref, sh_ref, sc_ref, w_ref, qg_ref, kg_ref, cos_ref, sa_ref, sb_ref,
                *out_refs, nq, rope):
    q_ref = out_refs[0] if nq else None
    k_ref, v_ref = out_refs[-2:]
    h = _norm_mod(x_ref[0], g_ref[...], sh_ref[0], sc_ref[0]).astype(BF16)
    acc = _dot(h, w_ref[...])
    hd = HEAD_DIM
    for i in range(nq + N_KV_HEADS):
        t = acc[:, i * hd:(i + 1) * hd]
        t = _head_norm(t, qg_ref[...] if i < nq else kg_ref[...])
        if rope:
            t = _rope(t, cos_ref[...], sa_ref[...], sb_ref[...])
        if i < nq:
            q_ref[0, :, i * hd:(i + 1) * hd] = (t * (hd ** -0.5)).astype(BF16)
        else:
            k_ref[0, :, (i - nq) * hd:(i - nq + 1) * hd] = t.astype(BF16)
    v_ref[0] = acc[:, (nq + N_KV_HEADS) * hd:].astype(BF16)


def qkv_project(x, g, sh, sc, w, q_g, k_g, rope_tabs, nq, rope):
    b, l, d = x.shape
    n = w.shape[1]
    hd = HEAD_DIM
    kvd = N_KV_HEADS * hd
    tm = _row_tile(l)
    cos, sa, sb = rope_tabs
    tab = pl.BlockSpec((tm, hd), lambda i, j: (j, 0))
    vec = pl.BlockSpec((1, 1, d), lambda i, j: (i, 0, 0))
    widths = ([nq * hd] if nq else []) + [kvd, kvd]
    return pl.pallas_call(
        functools.partial(_qkv_kernel, nq=nq, rope=rope),
        grid=(b, l // tm),
        in_specs=[pl.BlockSpec((1, tm, d), lambda i, j: (i, j, 0)),
                  pl.BlockSpec((1, d), lambda i, j: (0, 0)), vec, vec,
                  pl.BlockSpec((d, n), lambda i, j: (0, 0)),
                  pl.BlockSpec((1, hd), lambda i, j: (0, 0)),
                  pl.BlockSpec((1, hd), lambda i, j: (0, 0)),
                  tab, tab, tab],
        out_specs=[pl.BlockSpec((1, tm, wd), lambda i, j: (i, j, 0)) for wd in widths],
        out_shape=[jax.ShapeDtypeStruct((b, l, wd), BF16) for wd in widths],
        compiler_params=_cparams("arbitrary", "arbitrary"),
        name="qkv_project",
    )(x, g.reshape(1, d), sh, sc, w, q_g.reshape(1, hd), k_g.reshape(1, hd), cos, sa, sb)


def _rope_tables(l):
    rows = l // GRID_W
    row = np.repeat(np.arange(rows, dtype=np.float64), GRID_W)
    col = np.tile(np.arange(GRID_W, dtype=np.float64), rows)
    axis = HEAD_DIM // 2
    inv = ROPE_THETA ** (-np.arange(0, axis, 2, dtype=np.float64) / axis)

    def axis_angles(pos):
        a = pos[:, None] * inv[None, :]
        return np.concatenate([a, a], axis=-1)

    ang = np.concatenate([axis_angles(row), axis_angles(col)], axis=-1)
    cos, sin = np.cos(ang), np.sin(ang)
    lane = np.arange(HEAD_DIM)[None, :] % axis
    upper = lane >= axis // 2
    sin_a = np.where(upper, sin, 0.0)
    sin_b = np.where(upper, 0.0, -sin)
    return tuple(jnp.asarray(t, F32) for t in (cos, sin_a, sin_b))


def _attn_kernel(q_ref, k_ref, v_ref, o_ref, *, groups):
    k = k_ref[0]
    v = v_ref[0]
    hd = HEAD_DIM
    for g in range(groups):
        q = q_ref[0, :, g * hd:(g + 1) * hd]
        s = lax.dot_general(q, k, (((1,), (1,)), ((), ())), preferred_element_type=F32)
        m = jnp.max(s, axis=-1, keepdims=True)
        p = jnp.exp(s - m)
        den = jnp.sum(p, axis=-1, keepdims=True)
        o = _dot(p.astype(BF16), v) / den
        o_ref[0, :, g * hd:(g + 1) * hd] = o.astype(o_ref.dtype)


def attention(q, k, v):
    b, l, qd = q.shape
    s = k.shape[1]
    hd = HEAD_DIM
    groups = qd // hd // N_KV_HEADS
    tq = _row_tile(l, 256)
    return pl.pallas_call(
        functools.partial(_attn_kernel, groups=groups),
        grid=(b, N_KV_HEADS, l // tq),
        in_specs=[pl.BlockSpec((1, tq, groups * hd), lambda i, h, j: (i, j, h)),
                  pl.BlockSpec((1, s, hd), lambda i, h, j: (i, 0, h)),
                  pl.BlockSpec((1, s, hd), lambda i, h, j: (i, 0, h))],
        out_specs=pl.BlockSpec((1, tq, groups * hd), lambda i, h, j: (i, j, h)),
        out_shape=jax.ShapeDtypeStruct((b, l, qd), BF16),
        compiler_params=_cparams("arbitrary", "arbitrary", "arbitrary"),
        name="attention",
    )(q, k, v)


def _router_kernel(x_ref, g_ref, sh_ref, sc_ref, wr_ref, h_ref, a_ref):
    h = _norm_mod(x_ref[0], g_ref[...], sh_ref[0], sc_ref[0])
    h_ref[0] = h.astype(BF16)
    logits = lax.dot_general(wr_ref[...], h, (((1,), (1,)), ((), ())), preferred_element_type=F32,
                             precision=HIGHEST)
    e = jnp.exp(logits - jnp.max(logits, axis=0, keepdims=True))
    a_ref[0] = e / jnp.sum(e, axis=0, keepdims=True)


def moe_router(x, g, sh, sc, w_router_t):
    b, n, d = x.shape
    e = w_router_t.shape[0]
    tm = _row_tile(n)
    vec = pl.BlockSpec((1, 1, d), lambda i, j: (i, 0, 0))
    return pl.pallas_call(
        _router_kernel,
        grid=(b, n // tm),
        in_specs=[pl.BlockSpec((1, tm, d), lambda i, j: (i, j, 0)),
                  pl.BlockSpec((1, d), lambda i, j: (0, 0)), vec, vec,
                  pl.BlockSpec((e, d), lambda i, j: (0, 0))],
        out_specs=[pl.BlockSpec((1, tm, d), lambda i, j: (i, j, 0)),
                   pl.BlockSpec((1, e, tm), lambda i, j: (i, 0, j))],
        out_shape=[jax.ShapeDtypeStruct((b, n, d), BF16), jax.ShapeDtypeStruct((b, e, n), F32)],
        compiler_params=_cparams("arbitrary", "arbitrary"),
        name="moe_router",
    )(x, g.reshape(1, d), sh, sc, w_router_t)


def _cumsum_lanes(m, tri_ref):
    e, n = m.shape
    carry = jnp.zeros((e, 1), F32)
    outs = []
    for c in range(n // LANES):
        blk = m[:, c * LANES:(c + 1) * LANES]
        cs = _dot(blk.astype(BF16), tri_ref[...]) + carry
        outs.append(cs)
        carry = cs[:, LANES - 1:LANES]
    return jnp.concatenate(outs, axis=1)


def _select_kernel(a_ref, tri_ref, ind_ref, slot_ref, before_ref, slot_t_ref, gate_t_ref, *, cap):
    aff = a_ref[0]
    bits = pltpu.bitcast(aff, I32)
    e = aff.shape[0]

    def body(i, thr):
        cand = thr | (1 << (30 - i))
        cnt = jnp.sum((bits >= cand).astype(F32), axis=1, keepdims=True)
        return jnp.where(cnt >= cap, cand, thr)

    thr = lax.fori_loop(0, 31, body, jnp.zeros((e, 1), I32))
    gt = bits > thr
    eq = bits == thr
    n_gt = jnp.sum(gt.astype(F32), axis=1, keepdims=True)
    eq_rank = _cumsum_lanes(eq.astype(F32), tri_ref)
    sel = gt | (eq & (eq_rank <= cap - n_gt))
    self_ = sel.astype(F32)
    pos = _cumsum_lanes(self_, tri_ref) - 1.0
    slot = jnp.where(sel, pos, -1.0)
    gate = jnp.where(sel, aff, 0.0)
    slot_ref[0] = slot
    before_ref[0] = _dot(self_.astype(BF16), ind_ref[...])
    slot_t_ref[0] = slot.T
    gate_t_ref[0] = gate.T


def moe_select(aff, cap):
    b, e, n = aff.shape
    assert n // LANES <= LANES
    tri = jnp.asarray(np.triu(np.ones((LANES, LANES), np.float32)), BF16)
    ind = jnp.asarray(np.arange(n)[:, None] < LANES * np.arange(LANES)[None, :], BF16)
    row = pl.BlockSpec((1, e, n), lambda i: (i, 0, 0))
    col = pl.BlockSpec((1, n, e), lambda i: (i, 0, 0))
    return pl.pallas_call(
        functools.partial(_select_kernel, cap=cap),
        grid=(b,),
        in_specs=[row, pl.BlockSpec((LANES, LANES), lambda i: (0, 0)), pl.BlockSpec((n, LANES), lambda i: (0, 0))],
        out_specs=[row, pl.BlockSpec((1, e, LANES), lambda i: (i, 0, 0)), col, col],
        out_shape=[jax.ShapeDtypeStruct((b, e, n), F32), jax.ShapeDtypeStruct((b, e, LANES), F32),
                   jax.ShapeDtypeStruct((b, n, e), F32), jax.ShapeDtypeStruct((b, n, e), F32)],
        compiler_params=_cparams("arbitrary"),
        name="moe_select",
    )(aff, tri, ind)


def _window_starts(before, n, tile, width, cap, align):
    first = before[:, :, 0:n // LANES:tile // LANES].astype(I32)
    return jnp.clip(first // align * align, 0, cap - width).reshape(-1)


def _gather_kernel(ws_ref, slot_ref, h_ref, o_ref, acc_ref, *, tk, width):
    n = h_ref.shape[1]
    nt = n // tk
    base = (pl.program_id(0) * pl.num_programs(1) + pl.program_id(1)) * nt
    jcol = lax.broadcasted_iota(I32, (width, tk), 0)
    acc_ref[...] = jnp.zeros_like(acc_ref)
    for c in range(nt):
        ws = pl.multiple_of(ws_ref[base + c], SUBLANES)
        srow = slot_ref[0, 0, :, c * tk:(c + 1) * tk].astype(I32)
        onehot = jnp.where(jcol + ws == srow, 1.0, 0.0).astype(BF16)
        rows = pl.ds(ws, width)
        acc_ref[rows, :] = acc_ref[rows, :] + _dot(onehot, h_ref[0, c * tk:(c + 1) * tk, :])
    o_ref[0, 0] = acc_ref[...].astype(o_ref.dtype)


def moe_gather(slot, before, h, cap):
    b, e, n = slot.shape
    d = h.shape[2]
    tk = min(n, 2 * LANES)
    width = min(cap, tk + SUBLANES)
    ws = _window_starts(before, n, tk, width, cap, SUBLANES)
    grid_spec = pltpu.PrefetchScalarGridSpec(
        num_scalar_prefetch=1,
        grid=(b, e),
        in_specs=[pl.BlockSpec((1, 1, 1, n), lambda i, j, ws_ref: (i, j, 0, 0)),
                  pl.BlockSpec((1, n, d), lambda i, j, ws_ref: (i, 0, 0))],
        out_specs=pl.BlockSpec((1, 1, cap, d), lambda i, j, ws_ref: (i, j, 0, 0)),
        scratch_shapes=[pltpu.VMEM((cap, d), F32)])
    return pl.pallas_call(
        functools.partial(_gather_kernel, tk=tk, width=width),
        grid_spec=grid_spec,
        out_shape=jax.ShapeDtypeStruct((b, e, cap, d), BF16),
        compiler_params=_cparams("arbitrary", "arbitrary"),
        name="moe_gather",
    )(ws, slot.reshape(b, e, 1, n), h)


def _ffn_kernel(x_ref, wg_ref, wu_ref, wd_ref, o_ref, wgb_ref, wub_ref, wdb_ref):
    @pl.when(pl.program_id(1) == 0)
    def _():
        wgb_ref[...] = wg_ref[0, 0].astype(BF16)
        wub_ref[...] = wu_ref[0, 0].astype(BF16)
        wdb_ref[...] = wd_ref[0, 0].astype(BF16)

    x = x_ref[0, 0]
    a = _dot(x, wgb_ref[...])
    u = _dot(x, wub_ref[...])
    hid = (a / (1.0 + jnp.exp(-a))) * u
    o_ref[0, 0] = _dot(hid.astype(BF16), wdb_ref[...]).astype(o_ref.dtype)


def moe_ffn(xs, w_gate, w_up, w_down, layer):
    b, e, cap, d = xs.shape
    f = w_gate.shape[3]
    return pl.pallas_call(
        _ffn_kernel,
        grid=(e, b),
        in_specs=[pl.BlockSpec((1, 1, cap, d), lambda j, i: (i, j, 0, 0)),
                  pl.BlockSpec((1, 1, d, f), lambda j, i: (layer, j, 0, 0)),
                  pl.BlockSpec((1, 1, d, f), lambda j, i: (layer, j, 0, 0)),
                  pl.BlockSpec((1, 1, f, d), lambda j, i: (layer, j, 0, 0))],
        out_specs=pl.BlockSpec((1, 1, cap, d), lambda j, i: (i, j, 0, 0)),
        out_shape=jax.ShapeDtypeStruct((b, e, cap, d), BF16),
        scratch_shapes=[pltpu.VMEM((d, f), BF16), pltpu.VMEM((d, f), BF16), pltpu.VMEM((f, d), BF16)],
        compiler_params=_cparams("arbitrary", "arbitrary"),
        name="moe_ffn",
    )(xs, w_gate, w_up, w_down)


def _combine_kernel(ws_ref, slot_ref, gate_ref, ys_ref, lat_ref, gt_ref, o_ref, *, width, ne):
    tn = lat_ref.shape[1]
    nt = pl.num_programs(1)
    jrow = lax.broadcasted_iota(I32, (tn, width), 1)
    slot = slot_ref[0].astype(I32)
    gate = gate_ref[0]
    acc = jnp.zeros((tn, lat_ref.shape[2]), F32)
    for e in range(ne):
        ws = pl.multiple_of(ws_ref[(pl.program_id(0) * ne + e) * nt + pl.program_id(1)], BF16_ROWS)
        onehot = jnp.where(jrow + ws == slot[:, e:e + 1], 1.0, 0.0).astype(BF16)
        acc = acc + gate[:, e:e + 1] * _dot(onehot, ys_ref[0, e, pl.ds(ws, width), :])
    o_ref[0] = lat_ref[0] + gt_ref[0] * acc


def moe_combine(slot_t, gate_t, before, ys, lat, gate_vec, cap):
    b, n, e = slot_t.shape
    d = lat.shape[2]
    tn = LANES
    width = min(cap, 2 * LANES)
    assert width == cap or width >= tn + BF16_ROWS
    ws = _window_starts(before, n, tn, width, cap, BF16_ROWS)
    grid_spec = pltpu.PrefetchScalarGridSpec(
        num_scalar_prefetch=1,
        grid=(b, n // tn),
        in_specs=[pl.BlockSpec((1, tn, e), lambda i, j, ws_ref: (i, j, 0)),
                  pl.BlockSpec((1, tn, e), lambda i, j, ws_ref: (i, j, 0)),
                  pl.BlockSpec((1, e, cap, d), lambda i, j, ws_ref: (i, 0, 0, 0), pipeline_mode=pl.Buffered(1)),
                  pl.BlockSpec((1, tn, d), lambda i, j, ws_ref: (i, j, 0)),
                  pl.BlockSpec((1, 1, d), lambda i, j, ws_ref: (i, 0, 0))],
        out_specs=pl.BlockSpec((1, tn, d), lambda i, j, ws_ref: (i, j, 0)))
    return pl.pallas_call(
        functools.partial(_combine_kernel, width=width, ne=e),
        grid_spec=grid_spec,
        out_shape=jax.ShapeDtypeStruct((b, n, d), F32),
        compiler_params=_cparams("arbitrary", "arbitrary"),
        name="moe_combine",
    )(ws, slot_t, gate_t, ys, lat, gate_vec)


def expert_choice_moe(lat, g, sh, sc, gate_vec, w_router, w_gate, w_up, w_down, layer):
    b, n, d = lat.shape
    e = w_router.shape[1]
    cap = EC_CAPACITY_FACTOR * n // e
    h, aff = moe_router(lat, g, sh, sc, w_router.T)
    slot, before, slot_t, gate_t = moe_select(aff, cap)
    xs = moe_gather(slot, before, h, cap)
    ys = moe_ffn(xs, w_gate, w_up, w_down, layer)
    return moe_combine(slot_t, gate_t, before, ys, lat, gate_vec, cap)


def kernel(x, c, ctx, c_ctx, ada_w, ada_b, norm_g, sc_w_in, sc_conv, sc_w_out, hy_w_in, hy_conv_w, hy_f_w1,
           hy_f_b1, hy_f_w2, hy_f_b2, hy_f_w3, hy_sin_freq, hy_skip, hy_w_out, at_w_qkv, at_q_g, at_k_g, at_w_o,
           moe_router_w, moe_w_gate, moe_w_up, moe_w_down):
    b, l, d = x.shape
    depth = ada_w.shape[0]
    assert b + 1 <= MOD_ROWS
    qd = at_w_o.shape[1]
    nq = qd // HEAD_DIM
    attn_layers = list(range(MIX_ATTN, depth, N_MIXERS))
    last_attn = max(attn_layers, default=-1)

    cond = jnp.zeros((MOD_ROWS, d), F32).at[:b].set(c).at[b].set(c_ctx)
    mods = ada_modulation(cond, ada_w, ada_b)

    lat, cx = x, ctx
    for i in range(depth):
        kind = i % N_MIXERS
        j = i // N_MIXERS
        need_ctx = i <= last_attn
        upd_ctx = i < last_attn
        m_l = mods[i, :b].reshape(b, 1, 6, d)
        sh1, sc1, g1, sh2, sc2, g2 = (m_l[:, :, k] for k in range(6))
        if need_ctx:
            m_c = jnp.broadcast_to(mods[i, b].reshape(1, 1, 6, d), (b, 1, 6, d))
            csh1, csc1, cg1, csh2, csc2, cg2 = (m_c[:, :, k] for k in range(6))
        streams = [(lat, sh1, sc1, g1)]
        if upd_ctx:
            streams.append((cx, csh1, csc1, cg1))

        outs = []
        if kind == MIX_SHORTCONV:
            w_in, w_out = sc_w_in[j].astype(BF16), sc_w_out[j].astype(BF16)
            for s, sh, sc, gt in streams:
                p = nm_matmul(s, norm_g[i, 0], sh, sc, w_in)
                outs.append(sc_out(p, sc_conv[j], w_out, s, gt))
        elif kind == MIX_HYENA:
            w_in, w_out = hy_w_in[j].astype(BF16), hy_w_out[j].astype(BF16)
            for s, sh, sc, gt in streams:
                p = nm_matmul(s, norm_g[i, 0], sh, sc, w_in)
                z = hyena_mixer(p, hy_conv_w[j], hy_f_w1[j], hy_f_b1[j], hy_f_w2[j], hy_f_b2[j], hy_f_w3[j],
                                hy_sin_freq[j], hy_skip[j])
                outs.append(proj_residual(z, w_out, s, gt))
        else:
            w_qkv, w_o = at_w_qkv[j].astype(BF16), at_w_o[j].astype(BF16)
            q_l, k_l, v_l = qkv_project(lat, norm_g[i, 0], sh1, sc1, w_qkv, at_q_g[j], at_k_g[j],
                                        _rope_tables(l), nq, True)
            lc = cx.shape[1]
            dummy = tuple(jnp.zeros((lc, HEAD_DIM), F32) for _ in range(3))
            if upd_ctx:
                q_c, k_c, v_c = qkv_project(cx, norm_g[i, 0], csh1, csc1, w_qkv, at_q_g[j], at_k_g[j],
                                            dummy, nq, False)
            else:
                k_c, v_c = qkv_project(cx, norm_g[i, 0], csh1, csc1, w_qkv[:, qd:], at_q_g[j], at_k_g[j],
                                       dummy, 0, False)
            o_l = attention(q_l, jnp.concatenate([k_c, k_l], axis=1), jnp.concatenate([v_c, v_l], axis=1))
            outs.append(proj_residual(o_l, w_o, lat, g1))
            if upd_ctx:
                outs.append(proj_residual(attention(q_c, k_c, v_c), w_o, cx, cg1))

        experts = (moe_w_gate, moe_w_up, moe_w_down, i)
        lat = expert_choice_moe(outs[0], norm_g[i, 1], sh2, sc2, g2, moe_router_w[i], *experts)
        if upd_ctx:
            cx = expert_choice_moe(outs[1], norm_g[i, 1], csh2, csc2, cg2, moe_router_w[i], *experts)
    return lat
```

```python
import functools
import math

import jax
import jax.numpy as jnp
import numpy as np
from jax import lax
from jax.experimental import pallas as pl
from jax.experimental.pallas import tpu as pltpu

F32 = jnp.float32
BF16 = jnp.bfloat16
I32 = jnp.int32
HIGHEST = lax.Precision.HIGHEST

N_MIXERS = 3
MIX_SHORTCONV, MIX_HYENA, MIX_ATTN = 0, 1, 2
CONV_W = 3
EPS = 1e-6
HY_ORDER = 2
HY_EMB = 33
HY_BANDS = (HY_EMB - 1) // 2
HY_FAST_DECAY = 0.3
HY_SLOW_DECAY = 1.5
HY_TARGET = 1e-2
HEAD_DIM = 128
N_KV_HEADS = 2
GRID_W = 64
ROPE_THETA = 10000.0
EC_CAPACITY_FACTOR = 2

LANES = 128
SUBLANES = 8
BF16_ROWS = 16
VMEM_LIMIT_BYTES = 56 * 1024 * 1024
MOD_ROWS = 16
FFT_N2 = 128
HY_DT = LANES
DFT_UNROLL = 4
DFT_UNROLL_STAGE1 = 8


def _cparams(*sem):
    return pltpu.CompilerParams(dimension_semantics=sem, vmem_limit_bytes=VMEM_LIMIT_BYTES)


def _row_tile(n, cap=512):
    t = min(n, cap)
    assert n % t == 0
    return t


def _dot(a, b):
    return jnp.dot(a, b, preferred_element_type=F32)


def _ada_kernel(s_ref, w_ref, b_ref, o_ref):
    s = s_ref[...]
    s = s / (1.0 + jnp.exp(-s))
    o_ref[0] = jnp.dot(s, w_ref[0], preferred_element_type=F32, precision=HIGHEST) + b_ref[0]


def ada_modulation(cond, ada_w, ada_b):
    depth, d, n = ada_w.shape
    tn = 1536 if n % 1536 == 0 else n
    return pl.pallas_call(
        _ada_kernel,
        grid=(depth, n // tn),
        in_specs=[pl.BlockSpec((MOD_ROWS, d), lambda i, j: (0, 0)),
                  pl.BlockSpec((1, d, tn), lambda i, j: (i, 0, j)),
                  pl.BlockSpec((1, 1, tn), lambda i, j: (i, 0, j))],
        out_specs=pl.BlockSpec((1, MOD_ROWS, tn), lambda i, j: (i, 0, j)),
        out_shape=jax.ShapeDtypeStruct((depth, MOD_ROWS, n), F32),
        compiler_params=_cparams("arbitrary", "arbitrary"),
        name="ada_modulation",
    )(cond, ada_w, ada_b.reshape(depth, 1, n))


def _norm_mod(x, g, sh, sc):
    y = x * lax.rsqrt(jnp.mean(x * x, axis=-1, keepdims=True) + EPS)
    return (y * g) * (1.0 + sc) + sh


def _nm_matmul_kernel(x_ref, g_ref, sh_ref, sc_ref, w_ref, o_ref):
    h = _norm_mod(x_ref[0], g_ref[...], sh_ref[0], sc_ref[0]).astype(BF16)
    o_ref[0] = _dot(h, w_ref[...]).astype(o_ref.dtype)


def nm_matmul(x, g, sh, sc, w, out_dtype=BF16):
    b, l, d = x.shape
    n = w.shape[1]
    tm = _row_tile(l)
    return pl.pallas_call(
        _nm_matmul_kernel,
        grid=(b, l // tm),
        in_specs=[pl.BlockSpec((1, tm, d), lambda i, j: (i, j, 0)),
                  pl.BlockSpec((1, d), lambda i, j: (0, 0)),
                  pl.BlockSpec((1, 1, d), lambda i, j: (i, 0, 0)),
                  pl.BlockSpec((1, 1, d), lambda i, j: (i, 0, 0)),
                  pl.BlockSpec((d, n), lambda i, j: (0, 0))],
        out_specs=pl.BlockSpec((1, tm, n), lambda i, j: (i, j, 0)),
        out_shape=jax.ShapeDtypeStruct((b, l, n), out_dtype),
        compiler_params=_cparams("arbitrary", "arbitrary"),
        name="nm_matmul",
    )(x, g.reshape(1, d), sh, sc, w)


def _proj_res_kernel(a_ref, w_ref, lat_ref, gt_ref, o_ref):
    o_ref[0] = lat_ref[0] + gt_ref[0] * _dot(a_ref[0], w_ref[...])


def proj_residual(a, w, lat, gate):
    b, l, k = a.shape
    d = w.shape[1]
    tm = _row_tile(l)
    return pl.pallas_call(
        _proj_res_kernel,
        grid=(b, l // tm),
        in_specs=[pl.BlockSpec((1, tm, k), lambda i, j: (i, j, 0)),
                  pl.BlockSpec((k, d), lambda i, j: (0, 0)),
                  pl.BlockSpec((1, tm, d), lambda i, j: (i, j, 0)),
                  pl.BlockSpec((1, 1, d), lambda i, j: (i, 0, 0))],
        out_specs=pl.BlockSpec((1, tm, d), lambda i, j: (i, j, 0)),
        out_shape=jax.ShapeDtypeStruct((b, l, d), F32),
        compiler_params=_cparams("arbitrary", "arbitrary"),
        name="proj_residual",
    )(a, w, lat, gate)


def _sc_out_kernel(pb_ref, pc_ref, pv_ref, cp_ref, vp_ref, cn_ref, vn_ref, cw_ref, w_ref, lat_ref, gt_ref,
                   o_ref, u_ref, *, tm):
    j = pl.program_id(1)
    nj = pl.num_programs(1)
    d = u_ref.shape[1]
    u_ref[pl.ds(SUBLANES, tm), :] = pc_ref[0].astype(F32) * pv_ref[0].astype(F32)
    prev = cp_ref[0, BF16_ROWS - 1:BF16_ROWS, :].astype(F32) * vp_ref[0, BF16_ROWS - 1:BF16_ROWS, :].astype(F32)
    nxt = cn_ref[0, 0:1, :].astype(F32) * vn_ref[0, 0:1, :].astype(F32)
    prev = jnp.where(j == 0, 0.0, prev)
    nxt = jnp.where(j == nj - 1, 0.0, nxt)
    u_ref[pl.ds(0, SUBLANES), :] = jnp.broadcast_to(prev, (SUBLANES, d))
    u_ref[pl.ds(SUBLANES + tm, SUBLANES), :] = jnp.broadcast_to(nxt, (SUBLANES, d))
    cw = cw_ref[...]
    y = (u_ref[pl.ds(SUBLANES - 1, tm), :] * cw[0:1] + u_ref[pl.ds(SUBLANES, tm), :] * cw[1:2]
         + u_ref[pl.ds(SUBLANES + 1, tm), :] * cw[2:3])
    m = (pb_ref[0].astype(F32) * y).astype(BF16)
    o_ref[0] = lat_ref[0] + gt_ref[0] * _dot(m, w_ref[...])


def sc_out(p, conv_w, w_out, lat, gate):
    b, l, d3 = p.shape
    d = d3 // 3
    tm = _row_tile(l)
    hb = tm // BF16_ROWS
    nhb = l // BF16_ROWS

    def prev_map(col):
        return lambda i, j: (i, jnp.maximum(j * hb - 1, 0), col)

    def next_map(col):
        return lambda i, j: (i, jnp.minimum((j + 1) * hb, nhb - 1), col)

    return pl.pallas_call(
        functools.partial(_sc_out_kernel, tm=tm),
        grid=(b, l // tm),
        in_specs=[pl.BlockSpec((1, tm, d), lambda i, j: (i, j, 0)),
                  pl.BlockSpec((1, tm, d), lambda i, j: (i, j, 1)),
                  pl.BlockSpec((1, tm, d), lambda i, j: (i, j, 2)),
                  pl.BlockSpec((1, BF16_ROWS, d), prev_map(1)),
                  pl.BlockSpec((1, BF16_ROWS, d), prev_map(2)),
                  pl.BlockSpec((1, BF16_ROWS, d), next_map(1)),
                  pl.BlockSpec((1, BF16_ROWS, d), next_map(2)),
                  pl.BlockSpec((CONV_W, d), lambda i, j: (0, 0)),
                  pl.BlockSpec((d, d), lambda i, j: (0, 0)),
                  pl.BlockSpec((1, tm, d), lambda i, j: (i, j, 0)),
                  pl.BlockSpec((1, 1, d), lambda i, j: (i, 0, 0))],
        out_specs=pl.BlockSpec((1, tm, d), lambda i, j: (i, j, 0)),
        out_shape=jax.ShapeDtypeStruct((b, l, d), F32),
        scratch_shapes=[pltpu.VMEM((tm + 2 * SUBLANES, d), F32)],
        compiler_params=_cparams("arbitrary", "arbitrary"),
        name="sc_out",
    )(p, p, p, p, p, p, p, conv_w, w_out, lat, gate)


def _hy_filter_kernel(z_ref, w1_ref, b1_ref, w2_ref, b2_ref, fr_ref, w3f_ref, w3b_ref, t_ref, dl_ref,
                      of_ref, ob_ref, a_ref):
    @pl.when((pl.program_id(0) == 0) & (pl.program_id(1) == 0))
    def _():
        fr = fr_ref[...]
        a1 = jnp.sin(fr * (jnp.dot(z_ref[...], w1_ref[...], preferred_element_type=F32, precision=HIGHEST)
                           + b1_ref[...]))
        a_ref[...] = jnp.sin(fr * (jnp.dot(a1, w2_ref[...], preferred_element_type=F32, precision=HIGHEST)
                                   + b2_ref[...]))

    a = a_ref[...]
    decay = jnp.exp(-t_ref[...] * dl_ref[...])
    hf = jnp.dot(a, w3f_ref[...], preferred_element_type=F32, precision=HIGHEST) * decay
    hb = jnp.dot(a, w3b_ref[...], preferred_element_type=F32, precision=HIGHEST) * decay
    row = lax.broadcasted_iota(I32, hb.shape, 0)
    hb = jnp.where(row == 0, 0.0, hb)
    inv = 1.0 / (jnp.sum(jnp.abs(hf), axis=0, keepdims=True) + jnp.sum(jnp.abs(hb), axis=0, keepdims=True))
    of_ref[0] = hf * inv
    ob_ref[0] = hb * inv


def hy_filter_taps(l, d, f_w1, f_b1, f_w2, f_b2, f_w3, sin_freq):
    hid = f_w1.shape[1]
    hp = LANES
    t = np.linspace(0.0, 1.0, l)[:, None]
    bands = np.linspace(1e-4, HY_BANDS - 1, HY_BANDS)[None, :]
    ang = (2.0 * math.pi / l) * np.arange(l)[:, None] * bands
    z = np.zeros((l, hp), np.float64)
    z[:, :HY_EMB] = np.concatenate([t, np.cos(ang), -np.sin(ang)], axis=-1)
    deltas = np.abs(np.linspace(math.log(HY_TARGET) / HY_SLOW_DECAY, math.log(HY_TARGET) / HY_FAST_DECAY, d))[None, :]

    def pad2(a, r, c):
        return jnp.zeros((r, c), F32).at[:a.shape[0], :a.shape[1]].set(a.astype(F32))

    w1 = pad2(f_w1, hp, hp)
    w2 = pad2(f_w2, hp, hp)
    w3 = pad2(f_w3, hp, f_w3.shape[1])
    b1 = pad2(f_b1[None], 1, hp)
    b2 = pad2(f_b2[None], 1, hp)
    fr = pad2(sin_freq[None], 1, hp)
    tc = min(d, 2 * LANES)
    nd = d // tc
    full = lambda o, j: (0, 0)
    return pl.pallas_call(
        _hy_filter_kernel,
        grid=(HY_ORDER, nd),
        in_specs=[pl.BlockSpec((l, hp), full), pl.BlockSpec((hp, hp), full), pl.BlockSpec((1, hp), full),
                  pl.BlockSpec((hp, hp), full), pl.BlockSpec((1, hp), full), pl.BlockSpec((1, hp), full),
                  pl.BlockSpec((hp, tc), lambda o, j: (0, o * nd + j)),
                  pl.BlockSpec((hp, tc), lambda o, j: (0, (HY_ORDER + o) * nd + j)),
                  pl.BlockSpec((l, 1), full),
                  pl.BlockSpec((1, tc), lambda o, j: (0, j))],
        out_specs=[pl.BlockSpec((1, l, tc), lambda o, j: (o, 0, j)),
                   pl.BlockSpec((1, l, tc), lambda o, j: (o, 0, j))],
        out_shape=[jax.ShapeDtypeStruct((HY_ORDER, l, d), F32)] * 2,
        scratch_shapes=[pltpu.VMEM((l, hp), F32)],
        compiler_params=_cparams("arbitrary", "arbitrary"),
        name="hy_filter_taps",
    )(jnp.asarray(z, F32), w1, b1, w2, b2, fr, w3, w3, jnp.asarray(t, F32), jnp.asarray(deltas, F32))


def _dft_consts(l):
    n = 2 * l
    if n <= 4 * FFT_N2:
        n1, n2 = 1, n
    else:
        n1, n2 = n // FFT_N2, FFT_N2
    if n1 == 1:
        k = np.arange(n)[:, None]
        t = np.arange(l)[None, :]
        ang = 2.0 * np.pi * (k * t % n) / n
        c, s = np.cos(ang), np.sin(ang)
        fwd = np.block([[c, s], [-s, c]])
        inv = np.block([[c.T, -s.T], [s.T, c.T]]) / n
        return n1, n2, dict(fwd=jnp.asarray(fwd, BF16), inv=jnp.asarray(inv, BF16))
    n1h = n1 // 2
    k1 = np.arange(n1)[None, :, None]
    m1 = np.arange(n1h)[None, None, :]
    r2 = np.arange(n2)[:, None, None]
    ang = 2.0 * np.pi * ((k1 * (n2 * m1 + r2)) % n) / n
    c, s = np.cos(ang), np.sin(ang)
    m1f = np.concatenate([np.concatenate([c, s], 2), np.concatenate([-s, c], 2)], 1)
    ct, st = np.swapaxes(c, 1, 2) / n, np.swapaxes(s, 1, 2) / n
    m1i = np.concatenate([np.concatenate([ct, -st], 2), np.concatenate([st, ct], 2)], 1)
    a2 = 2.0 * np.pi * ((np.arange(n2)[:, None] * np.arange(n2)[None, :]) % n2) / n2
    c2, s2 = np.cos(a2), np.sin(a2)
    f2 = np.block([[c2, s2], [-s2, c2]])
    f2i = np.block([[c2, -s2], [s2, c2]])

    def il(m):
        return np.stack([np.arange(m), m + np.arange(m)], 1).reshape(-1)

    m1f = m1f[:, il(n1), :][:, :, il(n1h)]
    m1i = m1i[:, il(n1h), :][:, :, il(n1)]
    f2 = f2[:, il(n2)]
    f2i = f2i[il(n2), :][:, il(n2)]
    half = n2 // 2
    m1f = np.concatenate([m1f[:half], m1f[half:]], 2)
    m1i = np.concatenate([m1i[:half], m1i[half:]], 2)
    return n1, n2, dict(m1f=jnp.asarray(m1f, BF16), m1i=jnp.asarray(m1i, BF16),
                        f2=jnp.asarray(f2, BF16), f2i=jnp.asarray(f2i, BF16))


def _blockdiag(a, b):
    z = jnp.zeros_like(a)
    return jnp.concatenate([jnp.concatenate([a, z], axis=1), jnp.concatenate([z, b], axis=1)], axis=0)


def _halves(y):
    return y[:, :LANES], y[:, LANES:]


HI_HALF = -65536


def _pack_pair(re, im):
    lo = (pltpu.bitcast(re.astype(BF16).astype(F32), I32) >> 16) & 0xFFFF
    if im is None:
        return lo
    return lo | (pltpu.bitcast(im.astype(BF16).astype(F32), I32) & HI_HALF)


def _unpack_pair(w):
    return pltpu.bitcast(w << 16, F32), pltpu.bitcast(w & HI_HALF, F32)


def _as_rows(w):
    return pltpu.bitcast(w, BF16)


def _as_words(y):
    return pltpu.bitcast(y.astype(BF16), I32)


def _fwd_stage1(x_pk, a_pk, m1f_ref, n1, n2):
    n1h, half = n1 // 2, n2 // 2

    def column(r):
        return _as_rows(x_pk[pl.ds(r, n1h, stride=n2), :])

    def body(r, carry):
        w = _as_words(_dot(m1f_ref[r], _blockdiag(column(r), column(r + half))))
        for rr, ww in zip((r, r + half), _halves(w)):
            a_pk[pl.ds(pl.multiple_of(rr * n1, n1), n1), :] = ww
        return carry

    lax.fori_loop(0, half, body, 0, unroll=min(half, DFT_UNROLL_STAGE1))


def _inv_stage1(b_pk, x_pk, m1i_ref, n1, n2):
    n1h, half = n1 // 2, n2 // 2

    def column(r):
        return _as_rows(b_pk[pl.ds(pl.multiple_of(r * n1, n1), n1), :])

    def body(r, carry):
        w = _as_words(_dot(m1i_ref[r], _blockdiag(column(r), column(r + half))))
        for rr, ww in zip((r, r + half), _halves(w)):
            x_pk[pl.ds(rr, n1h, stride=n2), :] = ww
        return carry

    lax.fori_loop(0, half, body, 0, unroll=min(half, DFT_UNROLL_STAGE1))


def _load_chunk(a_pk, k1, n1, n2):
    ra = pl.ds(k1, n2, stride=n1)
    rb = pl.ds(k1 + n1 // 2, n2, stride=n1)
    return jnp.concatenate([_as_rows(a_pk[ra, :]), _as_rows(a_pk[rb, :])], axis=1)


def _hy_spec_kernel(*refs, l, n1, n2, dense):
    if dense:
        tf_ref, tb_ref, sk_ref, fwd_ref, o_ref = refs
        n = n2
        fw = fwd_ref[:, 0:l]
        sf = _dot(fw, tf_ref[0].astype(BF16))
        sb = _dot(fw, tb_ref[0].astype(BF16))
        o_ref[0, 0] = (sf[:n] + sb[:n] + sk_ref[0]).astype(o_ref.dtype)
        o_ref[0, 1] = (sf[n:] - sb[n:]).astype(o_ref.dtype)
        return
    tf_ref, tb_ref, sk_ref, m1f_ref, f2_ref, o_ref, x_pk, a_pk, h_re, h_im = refs
    for run, t_ref in enumerate((tf_ref, tb_ref)):
        x_pk[...] = _pack_pair(t_ref[0], None)
        _fwd_stage1(x_pk, a_pk, m1f_ref, n1, n2)

        def body(k1, carry, run=run):
            x = _dot(f2_ref[...], _load_chunk(a_pk, k1, n1, n2))
            for kk, xx in zip((k1, k1 + n1 // 2), _halves(x)):
                rows = pl.ds(pl.multiple_of(kk * n2, n2), n2)
                if run == 0:
                    h_re[rows, :] = xx[:n2]
                    h_im[rows, :] = xx[n2:]
                else:
                    h_re[rows, :] = h_re[rows, :] + xx[:n2]
                    h_im[rows, :] = h_im[rows, :] - xx[n2:]
            return carry

        lax.fori_loop(0, n1 // 2, body, 0, unroll=DFT_UNROLL)
    o_ref[0, 0] = (h_re[...] + sk_ref[0]).astype(o_ref.dtype)
    o_ref[0, 1] = h_im[...].astype(o_ref.dtype)


def hy_spectrum(taps_f, taps_b, skip, consts, n1, n2):
    o, l, d = taps_f.shape
    n = 2 * l
    dt = HY_DT
    dense = n1 == 1
    tap_spec = pl.BlockSpec((1, l, dt), lambda i, j: (i, 0, j))
    in_specs = [tap_spec, tap_spec, pl.BlockSpec((1, 1, dt), lambda i, j: (i, 0, j))]
    if dense:
        mats = [consts["fwd"]]
        scratch = []
    else:
        mats = [consts["m1f"], consts["f2"]]
        scratch = [pltpu.VMEM((l, dt), I32), pltpu.VMEM((n, dt), I32)] + [pltpu.VMEM((n, dt), F32)] * 2
    in_specs += [pl.BlockSpec(m.shape, lambda i, j, nd=m.ndim: (0,) * nd) for m in mats]
    return pl.pallas_call(
        functools.partial(_hy_spec_kernel, l=l, n1=n1, n2=n2, dense=dense),
        grid=(o, d // dt),
        in_specs=in_specs,
        out_specs=pl.BlockSpec((1, 2, n, dt), lambda i, j: (i, 0, 0, j)),
        out_shape=jax.ShapeDtypeStruct((o, 2, n, d), BF16),
        scratch_shapes=scratch,
        compiler_params=_cparams("arbitrary", "arbitrary"),
        name="hy_spectrum",
    )(taps_f, taps_b, skip.reshape(o, 1, d), *mats)


def _conv3_rows(src, cw, pad_ref, l):
    c = pad_ref.shape[1]
    pad_ref[pl.ds(0, SUBLANES), :] = jnp.zeros((SUBLANES, c), F32)
    pad_ref[pl.ds(SUBLANES + l, SUBLANES), :] = jnp.zeros((SUBLANES, c), F32)
    pad_ref[pl.ds(SUBLANES, l), :] = src
    return (pad_ref[pl.ds(SUBLANES - 1, l), :] * cw[0:1] + pad_ref[pl.ds(SUBLANES, l), :] * cw[1:2]
            + pad_ref[pl.ds(SUBLANES + 1, l), :] * cw[2:3])


def _hy_conv_kernel(*refs, l, n1, n2, dense, conv_z):
    if dense:
        z_ref, g_ref, cwz_ref, cwg_ref, h_ref, fwd_ref, inv_ref, o_ref, pad_ref = refs
    else:
        (z_ref, g_ref, cwz_ref, cwg_ref, h_ref, m1f_ref, m1i_ref, f2_ref, f2i_ref, o_ref,
         x_pk, a_pk, b_pk, pad_ref) = refs
    zs = []
    for e in range(2):
        z = z_ref[e, 0].astype(F32)
        zs.append(_conv3_rows(z, cwz_ref[...], pad_ref, l) if conv_z else z)
    if dense:
        n = n2
        s = _dot(fwd_ref[...], jnp.concatenate(zs, axis=0).astype(BF16))
        hr = h_ref[0, 0].astype(F32)
        hi = h_ref[0, 1].astype(F32)
        zr = s[:n] * hr - s[n:] * hi
        zi = s[:n] * hi + s[n:] * hr
        y = _dot(inv_ref[...], jnp.concatenate([zr, zi], axis=0).astype(BF16))
        ys = (y[:l], y[l:])
    else:
        x_pk[...] = _pack_pair(zs[0], zs[1])
        _fwd_stage1(x_pk, a_pk, m1f_ref, n1, n2)

        nk = n1 // 2

        def spectrum_step(k1, carry):
            s = _dot(f2_ref[...], _load_chunk(a_pk, k1, n1, n2))
            rows = [pl.ds(pl.multiple_of(k * n2, n2), n2) for k in (k1, k1 + nk)]
            hr = jnp.concatenate([h_ref[0, 0, r, :] for r in rows], axis=1).astype(F32)
            hi = jnp.concatenate([h_ref[0, 1, r, :] for r in rows], axis=1).astype(F32)
            zr = s[:n2] * hr - s[n2:] * hi
            zi = s[:n2] * hi + s[n2:] * hr
            for r, zr_half, zi_half in zip(rows, _halves(zr), _halves(zi)):
                b_pk[r, :] = _pack_pair(zr_half, zi_half)
            return carry

        lax.fori_loop(0, nk, spectrum_step, 0, unroll=min(nk, DFT_UNROLL))

        def inverse_step(k1, carry):
            kk = (k1, k1 + nk)
            z = jnp.concatenate([_as_rows(b_pk[pl.ds(pl.multiple_of(k * n2, n2), n2), :]) for k in kk], axis=1)
            w = _as_words(_dot(f2i_ref[...], z))
            for k, ww in zip(kk, _halves(w)):
                a_pk[pl.ds(k, n2, stride=n1), :] = ww
            return carry

        lax.fori_loop(0, nk, inverse_step, 0, unroll=min(nk, DFT_UNROLL))
        _inv_stage1(a_pk, x_pk, m1i_ref, n1, n2)
        ys = _unpack_pair(x_pk[...])
    for e in range(2):
        g = _conv3_rows(g_ref[e, 0].astype(F32), cwg_ref[...], pad_ref, l)
        o_ref[e, 0] = (g * ys[e]).astype(o_ref.dtype)


def hy_conv(zsrc, zcol, p4, gcol, conv_w, spec, order, consts, n1, n2, conv_z):
    _, pairs, l, _ = zsrc.shape
    d = spec.shape[3]
    n = 2 * l
    dt = HY_DT
    nd = d // dt
    dense = n1 == 1
    once = pl.Buffered(1)
    in_specs = [pl.BlockSpec((2, 1, l, dt), lambda j, i: (0, i, 0, zcol * nd + j)),
                pl.BlockSpec((2, 1, l, dt), lambda j, i: (0, i, 0, gcol * nd + j)),
                pl.BlockSpec((CONV_W, dt), lambda j, i: (0, j)),
                pl.BlockSpec((CONV_W, dt), lambda j, i: (0, gcol * nd + j)),
                pl.BlockSpec((1, 2, n, dt), lambda j, i: (order, 0, 0, j), pipeline_mode=once)]
    if dense:
        mats = [consts["fwd"], consts["inv"]]
        scratch = []
    else:
        mats = [consts["m1f"], consts["m1i"], consts["f2"], consts["f2i"]]
        scratch = [pltpu.VMEM((l, dt), I32)] + [pltpu.VMEM((n, dt), I32)] * 2
    scratch = scratch + [pltpu.VMEM((l + 2 * SUBLANES, dt), F32)]
    in_specs += [pl.BlockSpec(m.shape, lambda j, i, nd_=m.ndim: (0,) * nd_, pipeline_mode=once) for m in mats]
    return pl.pallas_call(
        functools.partial(_hy_conv_kernel, l=l, n1=n1, n2=n2, dense=dense, conv_z=conv_z),
        grid=(nd, pairs),
        in_specs=in_specs,
        out_specs=pl.BlockSpec((2, 1, l, dt), lambda j, i: (0, i, 0, j)),
        out_shape=jax.ShapeDtypeStruct((2, pairs, l, d), BF16),
        scratch_shapes=scratch,
        compiler_params=_cparams("arbitrary", "arbitrary"),
        name="hy_conv",
    )(zsrc, p4, conv_w, conv_w, spec, *mats)


def hyena_mixer(p, conv_w, f_w1, f_b1, f_w2, f_b2, f_w3, sin_freq, skip):
    b, l, d3 = p.shape
    d = d3 // 3
    assert b % 2 == 0 and d % HY_DT == 0
    n1, n2, consts = _dft_consts(l)
    taps_f, taps_b = hy_filter_taps(l, d, f_w1, f_b1, f_w2, f_b2, f_w3, sin_freq)
    spec = hy_spectrum(taps_f, taps_b, skip, consts, n1, n2)
    p4 = p.reshape(2, b // 2, l, d3)
    z = hy_conv(p4, 0, p4, 1, conv_w, spec, 0, consts, n1, n2, True)
    z = hy_conv(z, 0, p4, 2, conv_w, spec, 1, consts, n1, n2, False)
    return z.reshape(b, l, d)


def _head_norm(x, g):
    return x * lax.rsqrt(jnp.mean(x * x, axis=-1, keepdims=True) + EPS) * g


def _rope(x, cos, sin_a, sin_b):
    return (x * cos + pltpu.roll(x, HEAD_DIM // 4, 1) * sin_a
            + pltpu.roll(x, HEAD_DIM - HEAD_DIM // 4, 1) * sin_b)


def _qkv_kernel(x_ref, g_ref, sh_ref, sc_ref, w_ref, qg_ref, kg_ref, cos_ref, sa_ref, sb_ref,
                *out_refs, nq, rope):
    q_ref = out_refs[0] if nq else None
    k_ref, v_ref = out_refs[-2:]
    h = _norm_mod(x_ref[0], g_ref[...], sh_ref[0], sc_ref[0]).astype(BF16)
    acc = _dot(h, w_ref[...])
    hd = HEAD_DIM
    for i in range(nq + N_KV_HEADS):
        t = acc[:, i * hd:(i + 1) * hd]
        t = _head_norm(t, qg_ref[...] if i < nq else kg_ref[...])
        if rope:
            t = _rope(t, cos_ref[...], sa_ref[...], sb_ref[...])
        if i < nq:
            q_ref[0, :, i * hd:(i + 1) * hd] = (t * (hd ** -0.5)).astype(BF16)
        else:
            k_ref[0, :, (i - nq) * hd:(i - nq + 1) * hd] = t.astype(BF16)
    v_ref[0] = acc[:, (nq + N_KV_HEADS) * hd:].astype(BF16)


def qkv_project(x, g, sh, sc, w, q_g, k_g, rope_tabs, nq, rope):
    b, l, d = x.shape
    n = w.shape[1]
    hd = HEAD_DIM
    kvd = N_KV_HEADS * hd
    tm = _row_tile(l)
    cos, sa, sb = rope_tabs
    tab = pl.BlockSpec((tm, hd), lambda i, j: (j, 0))
    vec = pl.BlockSpec((1, 1, d), lambda i, j: (i, 0, 0))
    widths = ([nq * hd] if nq else []) + [kvd, kvd]
    return pl.pallas_call(
        functools.partial(_qkv_kernel, nq=nq, rope=rope),
        grid=(b, l // tm),
        in_specs=[pl.BlockSpec((1, tm, d), lambda i, j: (i, j, 0)),
                  pl.BlockSpec((1, d), lambda i, j: (0, 0)), vec, vec,
                  pl.BlockSpec((d, n), lambda i, j: (0, 0)),
                  pl.BlockSpec((1, hd), lambda i, j: (0, 0)),
                  pl.BlockSpec((1, hd), lambda i, j: (0, 0)),
                  tab, tab, tab],
        out_specs=[pl.BlockSpec((1, tm, wd), lambda i, j: (i, j, 0)) for wd in widths],
        out_shape=[jax.ShapeDtypeStruct((b, l, wd), BF16) for wd in widths],
        compiler_params=_cparams("arbitrary", "arbitrary"),
        name="qkv_project",
    )(x, g.reshape(1, d), sh, sc, w, q_g.reshape(1, hd), k_g.reshape(1, hd), cos, sa, sb)


def _rope_tables(l):
    rows = l // GRID_W
    row = np.repeat(np.arange(rows, dtype=np.float64), GRID_W)
    col = np.tile(np.arange(GRID_W, dtype=np.float64), rows)
    axis = HEAD_DIM // 2
    inv = ROPE_THETA ** (-np.arange(0, axis, 2, dtype=np.float64) / axis)

    def axis_angles(pos):
        a = pos[:, None] * inv[None, :]
        return np.concatenate([a, a], axis=-1)

    ang = np.concatenate([axis_angles(row), axis_angles(col)], axis=-1)
    cos, sin = np.cos(ang), np.sin(ang)
    lane = np.arange(HEAD_DIM)[None, :] % axis
    upper = lane >= axis // 2
    sin_a = np.where(upper, sin, 0.0)
    sin_b = np.where(upper, 0.0, -sin)
    return tuple(jnp.asarray(t, F32) for t in (cos, sin_a, sin_b))


def _attn_kernel(q_ref, k_ref, v_ref, o_ref, *, groups):
    k = k_ref[0]
    v = v_ref[0]
    hd = HEAD_DIM
    for g in range(groups):
        q = q_ref[0, :, g * hd:(g + 1) * hd]
        s = lax.dot_general(q, k, (((1,), (1,)), ((), ())), preferred_element_type=F32)
        m = jnp.max(s, axis=-1, keepdims=True)
        p = jnp.exp(s - m)
        den = jnp.sum(p, axis=-1, keepdims=True)
        o = _dot(p.astype(BF16), v) / den
        o_ref[0, :, g * hd:(g + 1) * hd] = o.astype(o_ref.dtype)


def attention(q, k, v):
    b, l, qd = q.shape
    s = k.shape[1]
    hd = HEAD_DIM
    groups = qd // hd // N_KV_HEADS
    tq = _row_tile(l, 256)
    return pl.pallas_call(
        functools.partial(_attn_kernel, groups=groups),
        grid=(b, N_KV_HEADS, l // tq),
        in_specs=[pl.BlockSpec((1, tq, groups * hd), lambda i, h, j: (i, j, h)),
                  pl.BlockSpec((1, s, hd), lambda i, h, j: (i, 0, h)),
                  pl.BlockSpec((1, s, hd), lambda i, h, j: (i, 0, h))],
        out_specs=pl.BlockSpec((1, tq, groups * hd), lambda i, h, j: (i, j, h)),
        out_shape=jax.ShapeDtypeStruct((b, l, qd), BF16),
        compiler_params=_cparams("arbitrary", "arbitrary", "arbitrary"),
        name="attention",
    )(q, k, v)


def _router_kernel(x_ref, g_ref, sh_ref, sc_ref, wr_ref, h_ref, a_ref):
    h = _norm_mod(x_ref[0], g_ref[...], sh_ref[0], sc_ref[0])
    h_ref[0] = h.astype(BF16)
    logits = lax.dot_general(wr_ref[...], h, (((1,), (1,)), ((), ())), preferred_element_type=F32,
                             precision=HIGHEST)
    e = jnp.exp(logits - jnp.max(logits, axis=0, keepdims=True))
    a_ref[0] = e / jnp.sum(e, axis=0, keepdims=True)


def moe_router(x, g, sh, sc, w_router_t):
    b, n, d = x.shape
    e = w_router_t.shape[0]
    tm = _row_tile(n)
    vec = pl.BlockSpec((1, 1, d), lambda i, j: (i, 0, 0))
    return pl.pallas_call(
        _router_kernel,
        grid=(b, n // tm),
        in_specs=[pl.BlockSpec((1, tm, d), lambda i, j: (i, j, 0)),
                  pl.BlockSpec((1, d), lambda i, j: (0, 0)), vec, vec,
                  pl.BlockSpec((e, d), lambda i, j: (0, 0))],
        out_specs=[pl.BlockSpec((1, tm, d), lambda i, j: (i, j, 0)),
                   pl.BlockSpec((1, e, tm), lambda i, j: (i, 0, j))],
        out_shape=[jax.ShapeDtypeStruct((b, n, d), BF16), jax.ShapeDtypeStruct((b, e, n), F32)],
        compiler_params=_cparams("arbitrary", "arbitrary"),
        name="moe_router",
    )(x, g.reshape(1, d), sh, sc, w_router_t)


def _cumsum_lanes(m, tri_ref):
    e, n = m.shape
    carry = jnp.zeros((e, 1), F32)
    outs = []
    for c in range(n // LANES):
        blk = m[:, c * LANES:(c + 1) * LANES]
        cs = _dot(blk.astype(BF16), tri_ref[...]) + carry
        outs.append(cs)
        carry = cs[:, LANES - 1:LANES]
    return jnp.concatenate(outs, axis=1)


def _select_kernel(a_ref, tri_ref, ind_ref, slot_ref, before_ref, slot_t_ref, gate_t_ref, *, cap):
    aff = a_ref[0]
    bits = pltpu.bitcast(aff, I32)
    e = aff.shape[0]

    def body(i, thr):
        cand = thr | (1 << (30 - i))
        cnt = jnp.sum((bits >= cand).astype(F32), axis=1, keepdims=True)
        return jnp.where(cnt >= cap, cand, thr)

    thr = lax.fori_loop(0, 31, body, jnp.zeros((e, 1), I32))
    gt = bits > thr
    eq = bits == thr
    n_gt = jnp.sum(gt.astype(F32), axis=1, keepdims=True)
    eq_rank = _cumsum_lanes(eq.astype(F32), tri_ref)
    sel = gt | (eq & (eq_rank <= cap - n_gt))
    self_ = sel.astype(F32)
    pos = _cumsum_lanes(self_, tri_ref) - 1.0
    slot = jnp.where(sel, pos, -1.0)
    gate = jnp.where(sel, aff, 0.0)
    slot_ref[0] = slot
    before_ref[0] = _dot(self_.astype(BF16), ind_ref[...])
    slot_t_ref[0] = slot.T
    gate_t_ref[0] = gate.T


def moe_select(aff, cap):
    b, e, n = aff.shape
    assert n // LANES <= LANES
    tri = jnp.asarray(np.triu(np.ones((LANES, LANES), np.float32)), BF16)
    ind = jnp.asarray(np.arange(n)[:, None] < LANES * np.arange(LANES)[None, :], BF16)
    row = pl.BlockSpec((1, e, n), lambda i: (i, 0, 0))
    col = pl.BlockSpec((1, n, e), lambda i: (i, 0, 0))
    return pl.pallas_call(
        functools.partial(_select_kernel, cap=cap),
        grid=(b,),
        in_specs=[row, pl.BlockSpec((LANES, LANES), lambda i: (0, 0)), pl.BlockSpec((n, LANES), lambda i: (0, 0))],
        out_specs=[row, pl.BlockSpec((1, e, LANES), lambda i: (i, 0, 0)), col, col],
        out_shape=[jax.ShapeDtypeStruct((b, e, n), F32), jax.ShapeDtypeStruct((b, e, LANES), F32),
                   jax.ShapeDtypeStruct((b, n, e), F32), jax.ShapeDtypeStruct((b, n, e), F32)],
        compiler_params=_cparams("arbitrary"),
        name="moe_select",
    )(aff, tri, ind)


def _window_starts(before, n, tile, width, cap, align):
    first = before[:, :, 0:n // LANES:tile // LANES].astype(I32)
    return jnp.clip(first // align * align, 0, cap - width).reshape(-1)


def _gather_kernel(ws_ref, slot_ref, h_ref, o_ref, acc_ref, *, tk, width):
    n = h_ref.shape[1]
    nt = n // tk
    base = (pl.program_id(0) * pl.num_programs(1) + pl.program_id(1)) * nt
    jcol = lax.broadcasted_iota(I32, (width, tk), 0)
    acc_ref[...] = jnp.zeros_like(acc_ref)
    for c in range(nt):
        ws = pl.multiple_of(ws_ref[base + c], SUBLANES)
        srow = slot_ref[0, 0, :, c * tk:(c + 1) * tk].astype(I32)
        onehot = jnp.where(jcol + ws == srow, 1.0, 0.0).astype(BF16)
        rows = pl.ds(ws, width)
        acc_ref[rows, :] = acc_ref[rows, :] + _dot(onehot, h_ref[0, c * tk:(c + 1) * tk, :])
    o_ref[0, 0] = acc_ref[...].astype(o_ref.dtype)


def moe_gather(slot, before, h, cap):
    b, e, n = slot.shape
    d = h.shape[2]
    tk = min(n, 2 * LANES)
    width = min(cap, tk + SUBLANES)
    ws = _window_starts(before, n, tk, width, cap, SUBLANES)
    grid_spec = pltpu.PrefetchScalarGridSpec(
        num_scalar_prefetch=1,
        grid=(b, e),
        in_specs=[pl.BlockSpec((1, 1, 1, n), lambda i, j, ws_ref: (i, j, 0, 0)),
                  pl.BlockSpec((1, n, d), lambda i, j, ws_ref: (i, 0, 0))],
        out_specs=pl.BlockSpec((1, 1, cap, d), lambda i, j, ws_ref: (i, j, 0, 0)),
        scratch_shapes=[pltpu.VMEM((cap, d), F32)])
    return pl.pallas_call(
        functools.partial(_gather_kernel, tk=tk, width=width),
        grid_spec=grid_spec,
        out_shape=jax.ShapeDtypeStruct((b, e, cap, d), BF16),
        compiler_params=_cparams("arbitrary", "arbitrary"),
        name="moe_gather",
    )(ws, slot.reshape(b, e, 1, n), h)


def _ffn_kernel(*refs, ns):
    x_refs = refs[:ns]
    wg_ref, wu_ref, wd_ref = refs[ns:ns + 3]
    o_refs = refs[ns + 3:2 * ns + 3]
    wgb_ref, wub_ref, wdb_ref = refs[2 * ns + 3:]

    @pl.when(pl.program_id(1) == 0)
    def _():
        wgb_ref[...] = wg_ref[0, 0].astype(BF16)
        wub_ref[...] = wu_ref[0, 0].astype(BF16)
        wdb_ref[...] = wd_ref[0, 0].astype(BF16)

    for x_ref, o_ref in zip(x_refs, o_refs):
        x = x_ref[0, 0]
        a = _dot(x, wgb_ref[...])
        u = _dot(x, wub_ref[...])
        hid = (a / (1.0 + jnp.exp(-a))) * u
        o_ref[0, 0] = _dot(hid.astype(BF16), wdb_ref[...]).astype(o_ref.dtype)


def moe_ffn(xs_list, w_gate, w_up, w_down, layer):
    b, e, _, d = xs_list[0].shape
    f = w_gate.shape[3]
    x_specs = [pl.BlockSpec((1, 1, xs.shape[2], d), lambda j, i: (i, j, 0, 0)) for xs in xs_list]
    return pl.pallas_call(
        functools.partial(_ffn_kernel, ns=len(xs_list)),
        grid=(e, b),
        in_specs=x_specs + [pl.BlockSpec((1, 1, d, f), lambda j, i: (layer, j, 0, 0)),
                            pl.BlockSpec((1, 1, d, f), lambda j, i: (layer, j, 0, 0)),
                            pl.BlockSpec((1, 1, f, d), lambda j, i: (layer, j, 0, 0))],
        out_specs=x_specs,
        out_shape=[jax.ShapeDtypeStruct(xs.shape, BF16) for xs in xs_list],
        scratch_shapes=[pltpu.VMEM((d, f), BF16), pltpu.VMEM((d, f), BF16), pltpu.VMEM((f, d), BF16)],
        compiler_params=_cparams("arbitrary", "arbitrary"),
        name="moe_ffn",
    )(*xs_list, w_gate, w_up, w_down)


def _combine_kernel(ws_ref, slot_ref, gate_ref, ys_ref, lat_ref, gt_ref, o_ref, *, width, ne):
    tn = lat_ref.shape[1]
    nt = pl.num_programs(1)
    jrow = lax.broadcasted_iota(I32, (tn, width), 1)
    slot = slot_ref[0].astype(I32)
    gate = gate_ref[0]
    acc = jnp.zeros((tn, lat_ref.shape[2]), F32)
    for e in range(ne):
        ws = pl.multiple_of(ws_ref[(pl.program_id(0) * ne + e) * nt + pl.program_id(1)], BF16_ROWS)
        onehot = jnp.where(jrow + ws == slot[:, e:e + 1], 1.0, 0.0).astype(BF16)
        acc = acc + gate[:, e:e + 1] * _dot(onehot, ys_ref[0, e, pl.ds(ws, width), :])
    o_ref[0] = lat_ref[0] + gt_ref[0] * acc


def moe_combine(slot_t, gate_t, before, ys, lat, gate_vec, cap):
    b, n, e = slot_t.shape
    d = lat.shape[2]
    tn = LANES
    width = min(cap, 2 * LANES)
    assert width == cap or width >= tn + BF16_ROWS
    ws = _window_starts(before, n, tn, width, cap, BF16_ROWS)
    grid_spec = pltpu.PrefetchScalarGridSpec(
        num_scalar_prefetch=1,
        grid=(b, n // tn),
        in_specs=[pl.BlockSpec((1, tn, e), lambda i, j, ws_ref: (i, j, 0)),
                  pl.BlockSpec((1, tn, e), lambda i, j, ws_ref: (i, j, 0)),
                  pl.BlockSpec((1, e, cap, d), lambda i, j, ws_ref: (i, 0, 0, 0), pipeline_mode=pl.Buffered(1)),
                  pl.BlockSpec((1, tn, d), lambda i, j, ws_ref: (i, j, 0)),
                  pl.BlockSpec((1, 1, d), lambda i, j, ws_ref: (i, 0, 0))],
        out_specs=pl.BlockSpec((1, tn, d), lambda i, j, ws_ref: (i, j, 0)))
    return pl.pallas_call(
        functools.partial(_combine_kernel, width=width, ne=e),
        grid_spec=grid_spec,
        out_shape=jax.ShapeDtypeStruct((b, n, d), F32),
        compiler_params=_cparams("arbitrary", "arbitrary"),
        name="moe_combine",
    )(ws, slot_t, gate_t, ys, lat, gate_vec)


def expert_choice_moe(streams, g, w_router, w_gate, w_up, w_down, layer):
    e = w_router.shape[1]
    routed = []
    for lat, sh, sc, _ in streams:
        cap = EC_CAPACITY_FACTOR * lat.shape[1] // e
        h, aff = moe_router(lat, g, sh, sc, w_router.T)
        slot, before, slot_t, gate_t = moe_select(aff, cap)
        routed.append((moe_gather(slot, before, h, cap), slot_t, gate_t, before, cap))
    ys_list = moe_ffn([r[0] for r in routed], w_gate, w_up, w_down, layer)
    return [moe_combine(slot_t, gate_t, before, ys, lat, gate_vec, cap)
            for (lat, _, _, gate_vec), (_, slot_t, gate_t, before, cap), ys in zip(streams, routed, ys_list)]


def kernel(x, c, ctx, c_ctx, ada_w, ada_b, norm_g, sc_w_in, sc_conv, sc_w_out, hy_w_in, hy_conv_w, hy_f_w1,
           hy_f_b1, hy_f_w2, hy_f_b2, hy_f_w3, hy_sin_freq, hy_skip, hy_w_out, at_w_qkv, at_q_g, at_k_g, at_w_o,
           moe_router_w, moe_w_gate, moe_w_up, moe_w_down):
    b, l, d = x.shape
    depth = ada_w.shape[0]
    assert b + 1 <= MOD_ROWS
    qd = at_w_o.shape[1]
    nq = qd // HEAD_DIM
    attn_layers = list(range(MIX_ATTN, depth, N_MIXERS))
    last_attn = max(attn_layers, default=-1)

    cond = jnp.zeros((MOD_ROWS, d), F32).at[:b].set(c).at[b].set(c_ctx)
    mods = ada_modulation(cond, ada_w, ada_b)

    lat, cx = x, ctx
    for i in range(depth):
        kind = i % N_MIXERS
        j = i // N_MIXERS
        need_ctx = i <= last_attn
        upd_ctx = i < last_attn
        m_l = mods[i, :b].reshape(b, 1, 6, d)
        sh1, sc1, g1, sh2, sc2, g2 = (m_l[:, :, k] for k in range(6))
        if need_ctx:
            m_c = jnp.broadcast_to(mods[i, b].reshape(1, 1, 6, d), (b, 1, 6, d))
            csh1, csc1, cg1, csh2, csc2, cg2 = (m_c[:, :, k] for k in range(6))
        streams = [(lat, sh1, sc1, g1)]
        if upd_ctx:
            streams.append((cx, csh1, csc1, cg1))

        outs = []
        if kind == MIX_SHORTCONV:
            w_in, w_out = sc_w_in[j].astype(BF16), sc_w_out[j].astype(BF16)
            for s, sh, sc, gt in streams:
                p = nm_matmul(s, norm_g[i, 0], sh, sc, w_in)
                outs.append(sc_out(p, sc_conv[j], w_out, s, gt))
        elif kind == MIX_HYENA:
            w_in, w_out = hy_w_in[j].astype(BF16), hy_w_out[j].astype(BF16)
            for s, sh, sc, gt in streams:
                p = nm_matmul(s, norm_g[i, 0], sh, sc, w_in)
                z = hyena_mixer(p, hy_conv_w[j], hy_f_w1[j], hy_f_b1[j], hy_f_w2[j], hy_f_b2[j], hy_f_w3[j],
                                hy_sin_freq[j], hy_skip[j])
                outs.append(proj_residual(z, w_out, s, gt))
        else:
            w_qkv, w_o = at_w_qkv[j].astype(BF16), at_w_o[j].astype(BF16)
            q_l, k_l, v_l = qkv_project(lat, norm_g[i, 0], sh1, sc1, w_qkv, at_q_g[j], at_k_g[j],
                                        _rope_tables(l), nq, True)
            lc = cx.shape[1]
            dummy = tuple(jnp.zeros((lc, HEAD_DIM), F32) for _ in range(3))
            if upd_ctx:
                q_c, k_c, v_c = qkv_project(cx, norm_g[i, 0], csh1, csc1, w_qkv, at_q_g[j], at_k_g[j],
                                            dummy, nq, False)
            else:
                k_c, v_c = qkv_project(cx, norm_g[i, 0], csh1, csc1, w_qkv[:, qd:], at_q_g[j], at_k_g[j],
                                       dummy, 0, False)
            o_l = attention(q_l, jnp.concatenate([k_c, k_l], axis=1), jnp.concatenate([v_c, v_l], axis=1))
            outs.append(proj_residual(o_l, w_o, lat, g1))
            if upd_ctx:
                outs.append(proj_residual(attention(q_c, k_c, v_c), w_o, cx, cg1))

        moe_streams = [(outs[0], sh2, sc2, g2)]
        if upd_ctx:
            moe_streams.append((outs[1], csh2, csc2, cg2))
        moe_out = expert_choice_moe(moe_streams, norm_g[i, 1], moe_router_w[i], moe_w_gate, moe_w_up, moe_w_down, i)
        lat = moe_out[0]
        if upd_ctx:
            cx = moe_out[1]
    return lat
```

```python
import functools
import math

import jax
import jax.numpy as jnp
import numpy as np
from jax import lax
from jax.experimental import pallas as pl
from jax.experimental.pallas import tpu as pltpu

F32 = jnp.float32
BF16 = jnp.bfloat16
I32 = jnp.int32
HIGHEST = lax.Precision.HIGHEST

N_MIXERS = 3
MIX_SHORTCONV, MIX_HYENA, MIX_ATTN = 0, 1, 2
CONV_W = 3
EPS = 1e-6
HY_ORDER = 2
HY_EMB = 33
HY_BANDS = (HY_EMB - 1) // 2
HY_FAST_DECAY = 0.3
HY_SLOW_DECAY = 1.5
HY_TARGET = 1e-2
HEAD_DIM = 128
N_KV_HEADS = 2
GRID_W = 64
ROPE_THETA = 10000.0
EC_CAPACITY_FACTOR = 2
Q_PRESCALE = HEAD_DIM ** -0.5 * math.log2(math.e)

LANES = 128
SUBLANES = 8
BF16_ROWS = 16
VMEM_LIMIT_BYTES = 56 * 1024 * 1024
MOD_ROWS = 16
FFT_N2 = 128
HY_DT = LANES
DFT_UNROLL = 4
DFT_UNROLL_STAGE1 = 8


def _cparams(*sem):
    return pltpu.CompilerParams(dimension_semantics=sem, vmem_limit_bytes=VMEM_LIMIT_BYTES)


def _row_tile(n, cap=512):
    t = min(n, cap)
    assert n % t == 0
    return t


def _dot(a, b):
    return jnp.dot(a, b, preferred_element_type=F32)


def _ada_kernel(s_ref, w_ref, b_ref, o_ref):
    s = s_ref[...]
    s = s / (1.0 + jnp.exp(-s))
    o_ref[0] = jnp.dot(s, w_ref[0], preferred_element_type=F32, precision=HIGHEST) + b_ref[0]


def ada_modulation(cond, ada_w, ada_b):
    depth, d, n = ada_w.shape
    tn = 1536 if n % 1536 == 0 else n
    return pl.pallas_call(
        _ada_kernel,
        grid=(depth, n // tn),
        in_specs=[pl.BlockSpec((MOD_ROWS, d), lambda i, j: (0, 0)),
                  pl.BlockSpec((1, d, tn), lambda i, j: (i, 0, j)),
                  pl.BlockSpec((1, 1, tn), lambda i, j: (i, 0, j))],
        out_specs=pl.BlockSpec((1, MOD_ROWS, tn), lambda i, j: (i, 0, j)),
        out_shape=jax.ShapeDtypeStruct((depth, MOD_ROWS, n), F32),
        compiler_params=_cparams("arbitrary", "arbitrary"),
        name="ada_modulation",
    )(cond, ada_w, ada_b.reshape(depth, 1, n))


def _norm_mod(x, g, sh, sc):
    y = x * lax.rsqrt(jnp.mean(x * x, axis=-1, keepdims=True) + EPS)
    return (y * g) * (1.0 + sc) + sh


def _nm_matmul_kernel(x_ref, g_ref, sh_ref, sc_ref, w_ref, o_ref):
    h = _norm_mod(x_ref[0], g_ref[...], sh_ref[0], sc_ref[0]).astype(BF16)
    o_ref[0] = _dot(h, w_ref[...]).astype(o_ref.dtype)


def nm_matmul(x, g, sh, sc, w, out_dtype=BF16):
    b, l, d = x.shape
    n = w.shape[1]
    tm = _row_tile(l)
    return pl.pallas_call(
        _nm_matmul_kernel,
        grid=(b, l // tm),
        in_specs=[pl.BlockSpec((1, tm, d), lambda i, j: (i, j, 0)),
                  pl.BlockSpec((1, d), lambda i, j: (0, 0)),
                  pl.BlockSpec((1, 1, d), lambda i, j: (i, 0, 0)),
                  pl.BlockSpec((1, 1, d), lambda i, j: (i, 0, 0)),
                  pl.BlockSpec((d, n), lambda i, j: (0, 0))],
        out_specs=pl.BlockSpec((1, tm, n), lambda i, j: (i, j, 0)),
        out_shape=jax.ShapeDtypeStruct((b, l, n), out_dtype),
        compiler_params=_cparams("arbitrary", "arbitrary"),
        name="nm_matmul",
    )(x, g.reshape(1, d), sh, sc, w)


def _proj_res_kernel(a_ref, w_ref, lat_ref, gt_ref, o_ref):
    o_ref[0] = lat_ref[0] + gt_ref[0] * _dot(a_ref[0], w_ref[...])


def proj_residual(a, w, lat, gate):
    b, l, k = a.shape
    d = w.shape[1]
    tm = _row_tile(l)
    return pl.pallas_call(
        _proj_res_kernel,
        grid=(b, l // tm),
        in_specs=[pl.BlockSpec((1, tm, k), lambda i, j: (i, j, 0)),
                  pl.BlockSpec((k, d), lambda i, j: (0, 0)),
                  pl.BlockSpec((1, tm, d), lambda i, j: (i, j, 0)),
                  pl.BlockSpec((1, 1, d), lambda i, j: (i, 0, 0))],
        out_specs=pl.BlockSpec((1, tm, d), lambda i, j: (i, j, 0)),
        out_shape=jax.ShapeDtypeStruct((b, l, d), F32),
        compiler_params=_cparams("arbitrary", "arbitrary"),
        name="proj_residual",
    )(a, w, lat, gate)


def _sc_out_kernel(pb_ref, pc_ref, pv_ref, cp_ref, vp_ref, cn_ref, vn_ref, cw_ref, w_ref, lat_ref, gt_ref,
                   o_ref, u_ref, *, tm):
    j = pl.program_id(1)
    nj = pl.num_programs(1)
    d = u_ref.shape[1]
    u_ref[pl.ds(SUBLANES, tm), :] = pc_ref[0].astype(F32) * pv_ref[0].astype(F32)
    prev = cp_ref[0, BF16_ROWS - 1:BF16_ROWS, :].astype(F32) * vp_ref[0, BF16_ROWS - 1:BF16_ROWS, :].astype(F32)
    nxt = cn_ref[0, 0:1, :].astype(F32) * vn_ref[0, 0:1, :].astype(F32)
    prev = jnp.where(j == 0, 0.0, prev)
    nxt = jnp.where(j == nj - 1, 0.0, nxt)
    u_ref[pl.ds(0, SUBLANES), :] = jnp.broadcast_to(prev, (SUBLANES, d))
    u_ref[pl.ds(SUBLANES + tm, SUBLANES), :] = jnp.broadcast_to(nxt, (SUBLANES, d))
    cw = cw_ref[...]
    y = (u_ref[pl.ds(SUBLANES - 1, tm), :] * cw[0:1] + u_ref[pl.ds(SUBLANES, tm), :] * cw[1:2]
         + u_ref[pl.ds(SUBLANES + 1, tm), :] * cw[2:3])
    m = (pb_ref[0].astype(F32) * y).astype(BF16)
    o_ref[0] = lat_ref[0] + gt_ref[0] * _dot(m, w_ref[...])


def sc_out(p, conv_w, w_out, lat, gate):
    b, l, d3 = p.shape
    d = d3 // 3
    tm = _row_tile(l)
    hb = tm // BF16_ROWS
    nhb = l // BF16_ROWS

    def prev_map(col):
        return lambda i, j: (i, jnp.maximum(j * hb - 1, 0), col)

    def next_map(col):
        return lambda i, j: (i, jnp.minimum((j + 1) * hb, nhb - 1), col)

    return pl.pallas_call(
        functools.partial(_sc_out_kernel, tm=tm),
        grid=(b, l // tm),
        in_specs=[pl.BlockSpec((1, tm, d), lambda i, j: (i, j, 0)),
                  pl.BlockSpec((1, tm, d), lambda i, j: (i, j, 1)),
                  pl.BlockSpec((1, tm, d), lambda i, j: (i, j, 2)),
                  pl.BlockSpec((1, BF16_ROWS, d), prev_map(1)),
                  pl.BlockSpec((1, BF16_ROWS, d), prev_map(2)),
                  pl.BlockSpec((1, BF16_ROWS, d), next_map(1)),
                  pl.BlockSpec((1, BF16_ROWS, d), next_map(2)),
                  pl.BlockSpec((CONV_W, d), lambda i, j: (0, 0)),
                  pl.BlockSpec((d, d), lambda i, j: (0, 0)),
                  pl.BlockSpec((1, tm, d), lambda i, j: (i, j, 0)),
                  pl.BlockSpec((1, 1, d), lambda i, j: (i, 0, 0))],
        out_specs=pl.BlockSpec((1, tm, d), lambda i, j: (i, j, 0)),
        out_shape=jax.ShapeDtypeStruct((b, l, d), F32),
        scratch_shapes=[pltpu.VMEM((tm + 2 * SUBLANES, d), F32)],
        compiler_params=_cparams("arbitrary", "arbitrary"),
        name="sc_out",
    )(p, p, p, p, p, p, p, conv_w, w_out, lat, gate)


def _hy_filter_kernel(z_ref, w1_ref, b1_ref, w2_ref, b2_ref, fr_ref, w3f_ref, w3b_ref, t_ref, dl_ref,
                      of_ref, ob_ref, a_ref):
    @pl.when((pl.program_id(0) == 0) & (pl.program_id(1) == 0))
    def _():
        fr = fr_ref[...]
        a1 = jnp.sin(fr * (jnp.dot(z_ref[...], w1_ref[...], preferred_element_type=F32, precision=HIGHEST)
                           + b1_ref[...]))
        a_ref[...] = jnp.sin(fr * (jnp.dot(a1, w2_ref[...], preferred_element_type=F32, precision=HIGHEST)
                                   + b2_ref[...]))

    a = a_ref[...]
    decay = jnp.exp(-t_ref[...] * dl_ref[...])
    hf = jnp.dot(a, w3f_ref[...], preferred_element_type=F32, precision=HIGHEST) * decay
    hb = jnp.dot(a, w3b_ref[...], preferred_element_type=F32, precision=HIGHEST) * decay
    row = lax.broadcasted_iota(I32, hb.shape, 0)
    hb = jnp.where(row == 0, 0.0, hb)
    inv = 1.0 / (jnp.sum(jnp.abs(hf), axis=0, keepdims=True) + jnp.sum(jnp.abs(hb), axis=0, keepdims=True))
    of_ref[0] = hf * inv
    ob_ref[0] = hb * inv


def hy_filter_taps(l, d, f_w1, f_b1, f_w2, f_b2, f_w3, sin_freq):
    hid = f_w1.shape[1]
    hp = LANES
    t = np.linspace(0.0, 1.0, l)[:, None]
    bands = np.linspace(1e-4, HY_BANDS - 1, HY_BANDS)[None, :]
    ang = (2.0 * math.pi / l) * np.arange(l)[:, None] * bands
    z = np.zeros((l, hp), np.float64)
    z[:, :HY_EMB] = np.concatenate([t, np.cos(ang), -np.sin(ang)], axis=-1)
    deltas = np.abs(np.linspace(math.log(HY_TARGET) / HY_SLOW_DECAY, math.log(HY_TARGET) / HY_FAST_DECAY, d))[None, :]

    def pad2(a, r, c):
        return jnp.zeros((r, c), F32).at[:a.shape[0], :a.shape[1]].set(a.astype(F32))

    w1 = pad2(f_w1, hp, hp)
    w2 = pad2(f_w2, hp, hp)
    w3 = pad2(f_w3, hp, f_w3.shape[1])
    b1 = pad2(f_b1[None], 1, hp)
    b2 = pad2(f_b2[None], 1, hp)
    fr = pad2(sin_freq[None], 1, hp)
    tc = min(d, 2 * LANES)
    nd = d // tc
    full = lambda o, j: (0, 0)
    return pl.pallas_call(
        _hy_filter_kernel,
        grid=(HY_ORDER, nd),
        in_specs=[pl.BlockSpec((l, hp), full), pl.BlockSpec((hp, hp), full), pl.BlockSpec((1, hp), full),
                  pl.BlockSpec((hp, hp), full), pl.BlockSpec((1, hp), full), pl.BlockSpec((1, hp), full),
                  pl.BlockSpec((hp, tc), lambda o, j: (0, o * nd + j)),
                  pl.BlockSpec((hp, tc), lambda o, j: (0, (HY_ORDER + o) * nd + j)),
                  pl.BlockSpec((l, 1), full),
                  pl.BlockSpec((1, tc), lambda o, j: (0, j))],
        out_specs=[pl.BlockSpec((1, l, tc), lambda o, j: (o, 0, j)),
                   pl.BlockSpec((1, l, tc), lambda o, j: (o, 0, j))],
        out_shape=[jax.ShapeDtypeStruct((HY_ORDER, l, d), F32)] * 2,
        scratch_shapes=[pltpu.VMEM((l, hp), F32)],
        compiler_params=_cparams("arbitrary", "arbitrary"),
        name="hy_filter_taps",
    )(jnp.asarray(z, F32), w1, b1, w2, b2, fr, w3, w3, jnp.asarray(t, F32), jnp.asarray(deltas, F32))


def _dft_consts(l):
    n = 2 * l
    if n <= 4 * FFT_N2:
        n1, n2 = 1, n
    else:
        n1, n2 = n // FFT_N2, FFT_N2
    if n1 == 1:
        k = np.arange(n)[:, None]
        t = np.arange(l)[None, :]
        ang = 2.0 * np.pi * (k * t % n) / n
        c, s = np.cos(ang), np.sin(ang)
        fwd = np.block([[c, s], [-s, c]])
        inv = np.block([[c.T, -s.T], [s.T, c.T]]) / n
        return n1, n2, dict(fwd=jnp.asarray(fwd, BF16), inv=jnp.asarray(inv, BF16))
    n1h = n1 // 2
    k1 = np.arange(n1)[None, :, None]
    m1 = np.arange(n1h)[None, None, :]
    r2 = np.arange(n2)[:, None, None]
    ang = 2.0 * np.pi * ((k1 * (n2 * m1 + r2)) % n) / n
    c, s = np.cos(ang), np.sin(ang)
    m1f = np.concatenate([np.concatenate([c, s], 2), np.concatenate([-s, c], 2)], 1)
    ct, st = np.swapaxes(c, 1, 2) / n, np.swapaxes(s, 1, 2) / n
    m1i = np.concatenate([np.concatenate([ct, -st], 2), np.concatenate([st, ct], 2)], 1)
    a2 = 2.0 * np.pi * ((np.arange(n2)[:, None] * np.arange(n2)[None, :]) % n2) / n2
    c2, s2 = np.cos(a2), np.sin(a2)
    f2 = np.block([[c2, s2], [-s2, c2]])
    f2i = np.block([[c2, -s2], [s2, c2]])

    def il(m):
        return np.stack([np.arange(m), m + np.arange(m)], 1).reshape(-1)

    m1f = m1f[:, il(n1), :][:, :, il(n1h)]
    m1i = m1i[:, il(n1h), :][:, :, il(n1)]
    f2 = f2[:, il(n2)]
    f2i = f2i[il(n2), :][:, il(n2)]
    half = n2 // 2
    m1f = np.concatenate([m1f[:half], m1f[half:]], 2)
    m1i = np.concatenate([m1i[:half], m1i[half:]], 2)
    return n1, n2, dict(m1f=jnp.asarray(m1f, BF16), m1i=jnp.asarray(m1i, BF16),
                        f2=jnp.asarray(f2, BF16), f2i=jnp.asarray(f2i, BF16))


def _blockdiag(a, b):
    z = jnp.zeros_like(a)
    return jnp.concatenate([jnp.concatenate([a, z], axis=1), jnp.concatenate([z, b], axis=1)], axis=0)


def _halves(y):
    return y[:, :LANES], y[:, LANES:]


HI_HALF = -65536


def _pack_pair(re, im):
    lo = (pltpu.bitcast(re.astype(BF16).astype(F32), I32) >> 16) & 0xFFFF
    if im is None:
        return lo
    return lo | (pltpu.bitcast(im.astype(BF16).astype(F32), I32) & HI_HALF)


def _unpack_pair(w):
    return pltpu.bitcast(w << 16, F32), pltpu.bitcast(w & HI_HALF, F32)


def _as_rows(w):
    return pltpu.bitcast(w, BF16)


def _as_words(y):
    return pltpu.bitcast(y.astype(BF16), I32)


def _fwd_stage1(x_pk, a_pk, m1f_ref, n1, n2):
    n1h, half = n1 // 2, n2 // 2

    def column(r):
        return _as_rows(x_pk[pl.ds(r, n1h, stride=n2), :])

    def body(r, carry):
        w = _as_words(_dot(m1f_ref[r], _blockdiag(column(r), column(r + half))))
        for rr, ww in zip((r, r + half), _halves(w)):
            a_pk[pl.ds(pl.multiple_of(rr * n1, n1), n1), :] = ww
        return carry

    lax.fori_loop(0, half, body, 0, unroll=min(half, DFT_UNROLL_STAGE1))


def _inv_stage1(b_pk, x_pk, m1i_ref, n1, n2):
    n1h, half = n1 // 2, n2 // 2

    def column(r):
        return _as_rows(b_pk[pl.ds(pl.multiple_of(r * n1, n1), n1), :])

    def body(r, carry):
        w = _as_words(_dot(m1i_ref[r], _blockdiag(column(r), column(r + half))))
        for rr, ww in zip((r, r + half), _halves(w)):
            x_pk[pl.ds(rr, n1h, stride=n2), :] = ww
        return carry

    lax.fori_loop(0, half, body, 0, unroll=min(half, DFT_UNROLL_STAGE1))


def _load_chunk(a_pk, k1, n1, n2):
    ra = pl.ds(k1, n2, stride=n1)
    rb = pl.ds(k1 + n1 // 2, n2, stride=n1)
    return jnp.concatenate([_as_rows(a_pk[ra, :]), _as_rows(a_pk[rb, :])], axis=1)


def _hy_spec_kernel(*refs, l, n1, n2, dense):
    if dense:
        tf_ref, tb_ref, sk_ref, fwd_ref, o_ref = refs
        n = n2
        fw = fwd_ref[:, 0:l]
        sf = _dot(fw, tf_ref[0].astype(BF16))
        sb = _dot(fw, tb_ref[0].astype(BF16))
        o_ref[0, 0] = (sf[:n] + sb[:n] + sk_ref[0]).astype(o_ref.dtype)
        o_ref[0, 1] = (sf[n:] - sb[n:]).astype(o_ref.dtype)
        return
    tf_ref, tb_ref, sk_ref, m1f_ref, f2_ref, o_ref, x_pk, a_pk, h_re, h_im = refs
    for run, t_ref in enumerate((tf_ref, tb_ref)):
        x_pk[...] = _pack_pair(t_ref[0], None)
        _fwd_stage1(x_pk, a_pk, m1f_ref, n1, n2)

        def body(k1, carry, run=run):
            x = _dot(f2_ref[...], _load_chunk(a_pk, k1, n1, n2))
            for kk, xx in zip((k1, k1 + n1 // 2), _halves(x)):
                rows = pl.ds(pl.multiple_of(kk * n2, n2), n2)
                if run == 0:
                    h_re[rows, :] = xx[:n2]
                    h_im[rows, :] = xx[n2:]
                else:
                    h_re[rows, :] = h_re[rows, :] + xx[:n2]
                    h_im[rows, :] = h_im[rows, :] - xx[n2:]
            return carry

        lax.fori_loop(0, n1 // 2, body, 0, unroll=DFT_UNROLL)
    o_ref[0, 0] = (h_re[...] + sk_ref[0]).astype(o_ref.dtype)
    o_ref[0, 1] = h_im[...].astype(o_ref.dtype)


def hy_spectrum(taps_f, taps_b, skip, consts, n1, n2):
    o, l, d = taps_f.shape
    n = 2 * l
    dt = HY_DT
    dense = n1 == 1
    tap_spec = pl.BlockSpec((1, l, dt), lambda i, j: (i, 0, j))
    in_specs = [tap_spec, tap_spec, pl.BlockSpec((1, 1, dt), lambda i, j: (i, 0, j))]
    if dense:
        mats = [consts["fwd"]]
        scratch = []
    else:
        mats = [consts["m1f"], consts["f2"]]
        scratch = [pltpu.VMEM((l, dt), I32), pltpu.VMEM((n, dt), I32)] + [pltpu.VMEM((n, dt), F32)] * 2
    in_specs += [pl.BlockSpec(m.shape, lambda i, j, nd=m.ndim: (0,) * nd) for m in mats]
    return pl.pallas_call(
        functools.partial(_hy_spec_kernel, l=l, n1=n1, n2=n2, dense=dense),
        grid=(o, d // dt),
        in_specs=in_specs,
        out_specs=pl.BlockSpec((1, 2, n, dt), lambda i, j: (i, 0, 0, j)),
        out_shape=jax.ShapeDtypeStruct((o, 2, n, d), BF16),
        scratch_shapes=scratch,
        compiler_params=_cparams("arbitrary", "arbitrary"),
        name="hy_spectrum",
    )(taps_f, taps_b, skip.reshape(o, 1, d), *mats)


def _conv3_rows(src, cw, pad_ref, l):
    c = pad_ref.shape[1]
    pad_ref[pl.ds(0, SUBLANES), :] = jnp.zeros((SUBLANES, c), F32)
    pad_ref[pl.ds(SUBLANES + l, SUBLANES), :] = jnp.zeros((SUBLANES, c), F32)
    pad_ref[pl.ds(SUBLANES, l), :] = src
    return (pad_ref[pl.ds(SUBLANES - 1, l), :] * cw[0:1] + pad_ref[pl.ds(SUBLANES, l), :] * cw[1:2]
            + pad_ref[pl.ds(SUBLANES + 1, l), :] * cw[2:3])


def _hy_conv_kernel(*refs, l, n1, n2, dense, conv_z):
    if dense:
        z_ref, g_ref, cwz_ref, cwg_ref, h_ref, fwd_ref, inv_ref, o_ref, pad_ref = refs
    else:
        (z_ref, g_ref, cwz_ref, cwg_ref, h_ref, m1f_ref, m1i_ref, f2_ref, f2i_ref, o_ref,
         x_pk, a_pk, b_pk, pad_ref) = refs
    zs = []
    for e in range(2):
        z = z_ref[e, 0].astype(F32)
        zs.append(_conv3_rows(z, cwz_ref[...], pad_ref, l) if conv_z else z)
    if dense:
        n = n2
        s = _dot(fwd_ref[...], jnp.concatenate(zs, axis=0).astype(BF16))
        hr = h_ref[0, 0].astype(F32)
        hi = h_ref[0, 1].astype(F32)
        zr = s[:n] * hr - s[n:] * hi
        zi = s[:n] * hi + s[n:] * hr
        y = _dot(inv_ref[...], jnp.concatenate([zr, zi], axis=0).astype(BF16))
        ys = (y[:l], y[l:])
    else:
        x_pk[...] = _pack_pair(zs[0], zs[1])
        _fwd_stage1(x_pk, a_pk, m1f_ref, n1, n2)

        nk = n1 // 2

        def spectrum_step(k1, carry):
            s = _dot(f2_ref[...], _load_chunk(a_pk, k1, n1, n2))
            rows = [pl.ds(pl.multiple_of(k * n2, n2), n2) for k in (k1, k1 + nk)]
            hr = jnp.concatenate([h_ref[0, 0, r, :] for r in rows], axis=1).astype(F32)
            hi = jnp.concatenate([h_ref[0, 1, r, :] for r in rows], axis=1).astype(F32)
            zr = s[:n2] * hr - s[n2:] * hi
            zi = s[:n2] * hi + s[n2:] * hr
            for r, zr_half, zi_half in zip(rows, _halves(zr), _halves(zi)):
                b_pk[r, :] = _pack_pair(zr_half, zi_half)
            return carry

        lax.fori_loop(0, nk, spectrum_step, 0, unroll=min(nk, DFT_UNROLL))

        def inverse_step(k1, carry):
            kk = (k1, k1 + nk)
            z = jnp.concatenate([_as_rows(b_pk[pl.ds(pl.multiple_of(k * n2, n2), n2), :]) for k in kk], axis=1)
            w = _as_words(_dot(f2i_ref[...], z))
            for k, ww in zip(kk, _halves(w)):
                a_pk[pl.ds(k, n2, stride=n1), :] = ww
            return carry

        lax.fori_loop(0, nk, inverse_step, 0, unroll=min(nk, DFT_UNROLL))
        _inv_stage1(a_pk, x_pk, m1i_ref, n1, n2)
        ys = _unpack_pair(x_pk[...])
    for e in range(2):
        g = _conv3_rows(g_ref[e, 0].astype(F32), cwg_ref[...], pad_ref, l)
        o_ref[e, 0] = (g * ys[e]).astype(o_ref.dtype)


def hy_conv(zsrc, zcol, p4, gcol, conv_w, spec, order, consts, n1, n2, conv_z):
    _, pairs, l, _ = zsrc.shape
    d = spec.shape[3]
    n = 2 * l
    dt = HY_DT
    nd = d // dt
    dense = n1 == 1
    once = pl.Buffered(1)
    in_specs = [pl.BlockSpec((2, 1, l, dt), lambda j, i: (0, i, 0, zcol * nd + j)),
                pl.BlockSpec((2, 1, l, dt), lambda j, i: (0, i, 0, gcol * nd + j)),
                pl.BlockSpec((CONV_W, dt), lambda j, i: (0, j)),
                pl.BlockSpec((CONV_W, dt), lambda j, i: (0, gcol * nd + j)),
                pl.BlockSpec((1, 2, n, dt), lambda j, i: (order, 0, 0, j), pipeline_mode=once)]
    if dense:
        mats = [consts["fwd"], consts["inv"]]
        scratch = []
    else:
        mats = [consts["m1f"], consts["m1i"], consts["f2"], consts["f2i"]]
        scratch = [pltpu.VMEM((l, dt), I32)] + [pltpu.VMEM((n, dt), I32)] * 2
    scratch = scratch + [pltpu.VMEM((l + 2 * SUBLANES, dt), F32)]
    in_specs += [pl.BlockSpec(m.shape, lambda j, i, nd_=m.ndim: (0,) * nd_, pipeline_mode=once) for m in mats]
    return pl.pallas_call(
        functools.partial(_hy_conv_kernel, l=l, n1=n1, n2=n2, dense=dense, conv_z=conv_z),
        grid=(nd, pairs),
        in_specs=in_specs,
        out_specs=pl.BlockSpec((2, 1, l, dt), lambda j, i: (0, i, 0, j)),
        out_shape=jax.ShapeDtypeStruct((2, pairs, l, d), BF16),
        scratch_shapes=scratch,
        compiler_params=_cparams("arbitrary", "arbitrary"),
        name="hy_conv",
    )(zsrc, p4, conv_w, conv_w, spec, *mats)


def hyena_mixer(p, conv_w, f_w1, f_b1, f_w2, f_b2, f_w3, sin_freq, skip):
    b, l, d3 = p.shape
    d = d3 // 3
    assert b % 2 == 0 and d % HY_DT == 0
    n1, n2, consts = _dft_consts(l)
    taps_f, taps_b = hy_filter_taps(l, d, f_w1, f_b1, f_w2, f_b2, f_w3, sin_freq)
    spec = hy_spectrum(taps_f, taps_b, skip, consts, n1, n2)
    p4 = p.reshape(2, b // 2, l, d3)
    z = hy_conv(p4, 0, p4, 1, conv_w, spec, 0, consts, n1, n2, True)
    z = hy_conv(z, 0, p4, 2, conv_w, spec, 1, consts, n1, n2, False)
    return z.reshape(b, l, d)


def _head_norm(x, g):
    return x * lax.rsqrt(jnp.mean(x * x, axis=-1, keepdims=True) + EPS) * g


def _rope(x, cos, sin_a, sin_b):
    return (x * cos + pltpu.roll(x, HEAD_DIM // 4, 1) * sin_a
            + pltpu.roll(x, HEAD_DIM - HEAD_DIM // 4, 1) * sin_b)


def _qkv_kernel(x_ref, g_ref, sh_ref, sc_ref, w_ref, qg_ref, kg_ref, cos_ref, sa_ref, sb_ref,
                *out_refs, nq, rope):
    q_ref = out_refs[0] if nq else None
    k_ref, v_ref = out_refs[-2:]
    h = _norm_mod(x_ref[0], g_ref[...], sh_ref[0], sc_ref[0]).astype(BF16)
    acc = _dot(h, w_ref[...])
    hd = HEAD_DIM
    for i in range(nq + N_KV_HEADS):
        t = acc[:, i * hd:(i + 1) * hd]
        t = _head_norm(t, qg_ref[...] if i < nq else kg_ref[...])
        if rope:
            t = _rope(t, cos_ref[...], sa_ref[...], sb_ref[...])
        if i < nq:
            q_ref[0, :, i * hd:(i + 1) * hd] = (t * Q_PRESCALE).astype(BF16)
        else:
            k_ref[0, :, (i - nq) * hd:(i - nq + 1) * hd] = t.astype(BF16)
    v_ref[0] = acc[:, (nq + N_KV_HEADS) * hd:].astype(BF16)


def qkv_project(x, g, sh, sc, w, q_g, k_g, rope_tabs, nq, rope):
    b, l, d = x.shape
    n = w.shape[1]
    hd = HEAD_DIM
    kvd = N_KV_HEADS * hd
    tm = _row_tile(l)
    cos, sa, sb = rope_tabs
    tab = pl.BlockSpec((tm, hd), lambda i, j: (j, 0))
    vec = pl.BlockSpec((1, 1, d), lambda i, j: (i, 0, 0))
    widths = ([nq * hd] if nq else []) + [kvd, kvd]
    return pl.pallas_call(
        functools.partial(_qkv_kernel, nq=nq, rope=rope),
        grid=(b, l // tm),
        in_specs=[pl.BlockSpec((1, tm, d), lambda i, j: (i, j, 0)),
                  pl.BlockSpec((1, d), lambda i, j: (0, 0)), vec, vec,
                  pl.BlockSpec((d, n), lambda i, j: (0, 0)),
                  pl.BlockSpec((1, hd), lambda i, j: (0, 0)),
                  pl.BlockSpec((1, hd), lambda i, j: (0, 0)),
                  tab, tab, tab],
        out_specs=[pl.BlockSpec((1, tm, wd), lambda i, j: (i, j, 0)) for wd in widths],
        out_shape=[jax.ShapeDtypeStruct((b, l, wd), BF16) for wd in widths],
        compiler_params=_cparams("arbitrary", "arbitrary"),
        name="qkv_project",
    )(x, g.reshape(1, d), sh, sc, w, q_g.reshape(1, hd), k_g.reshape(1, hd), cos, sa, sb)


def _rope_tables(l):
    rows = l // GRID_W
    row = np.repeat(np.arange(rows, dtype=np.float64), GRID_W)
    col = np.tile(np.arange(GRID_W, dtype=np.float64), rows)
    axis = HEAD_DIM // 2
    inv = ROPE_THETA ** (-np.arange(0, axis, 2, dtype=np.float64) / axis)

    def axis_angles(pos):
        a = pos[:, None] * inv[None, :]
        return np.concatenate([a, a], axis=-1)

    ang = np.concatenate([axis_angles(row), axis_angles(col)], axis=-1)
    cos, sin = np.cos(ang), np.sin(ang)
    lane = np.arange(HEAD_DIM)[None, :] % axis
    upper = lane >= axis // 2
    sin_a = np.where(upper, sin, 0.0)
    sin_b = np.where(upper, 0.0, -sin)
    return tuple(jnp.asarray(t, F32) for t in (cos, sin_a, sin_b))


def _attn_kernel(q_ref, k_ref, vt_ref, o_ref, *, groups):
    k = k_ref[0]
    vt = vt_ref[0]
    hd = HEAD_DIM
    tq = q_ref.shape[1]
    hp = 2 if groups % 2 == 0 else 1
    def scores(g0):
        q = jnp.concatenate([q_ref[0, :, g * hd:(g + 1) * hd] for g in range(g0, g0 + hp)], axis=0)
        return lax.dot_general(k, q, (((1,), (1,)), ((), ())), preferred_element_type=F32)

    sts = [scores(g0) for g0 in range(0, groups, hp)]
    for g0, st in zip(range(0, groups, hp), sts):
        m = jnp.max(st, axis=0, keepdims=True)
        pt = jnp.exp2(st - m)
        den = jnp.sum(pt, axis=0, keepdims=True)
        ot = _dot(vt, pt.astype(BF16)) / den
        for i in range(hp):
            g = g0 + i
            o_ref[0, :, g * hd:(g + 1) * hd] = ot[:, i * tq:(i + 1) * tq].T.astype(o_ref.dtype)


def attention(q, k, vt):
    b, l, qd = q.shape
    s = k.shape[1]
    hd = HEAD_DIM
    groups = qd // hd // N_KV_HEADS
    tq = _row_tile(l, 256)
    return pl.pallas_call(
        functools.partial(_attn_kernel, groups=groups),
        grid=(b, N_KV_HEADS, l // tq),
        in_specs=[pl.BlockSpec((1, tq, groups * hd), lambda i, h, j: (i, j, h)),
                  pl.BlockSpec((1, s, hd), lambda i, h, j: (i, 0, h)),
                  pl.BlockSpec((1, hd, s), lambda i, h, j: (i, h, 0))],
        out_specs=pl.BlockSpec((1, tq, groups * hd), lambda i, h, j: (i, j, h)),
        out_shape=jax.ShapeDtypeStruct((b, l, qd), BF16),
        compiler_params=_cparams("arbitrary", "arbitrary", "arbitrary"),
        name="attention",
    )(q, k, vt)


def _router_kernel(x_ref, g_ref, sh_ref, sc_ref, wr_ref, h_ref, a_ref):
    h = _norm_mod(x_ref[0], g_ref[...], sh_ref[0], sc_ref[0])
    h_ref[0] = h.astype(BF16)
    logits = lax.dot_general(wr_ref[...], h, (((1,), (1,)), ((), ())), preferred_element_type=F32,
                             precision=HIGHEST)
    e = jnp.exp(logits - jnp.max(logits, axis=0, keepdims=True))
    a_ref[0] = e / jnp.sum(e, axis=0, keepdims=True)


def moe_router(x, g, sh, sc, w_router_t):
    b, n, d = x.shape
    e = w_router_t.shape[0]
    tm = _row_tile(n)
    vec = pl.BlockSpec((1, 1, d), lambda i, j: (i, 0, 0))
    return pl.pallas_call(
        _router_kernel,
        grid=(b, n // tm),
        in_specs=[pl.BlockSpec((1, tm, d), lambda i, j: (i, j, 0)),
                  pl.BlockSpec((1, d), lambda i, j: (0, 0)), vec, vec,
                  pl.BlockSpec((e, d), lambda i, j: (0, 0))],
        out_specs=[pl.BlockSpec((1, tm, d), lambda i, j: (i, j, 0)),
                   pl.BlockSpec((1, e, tm), lambda i, j: (i, 0, j))],
        out_shape=[jax.ShapeDtypeStruct((b, n, d), BF16), jax.ShapeDtypeStruct((b, e, n), F32)],
        compiler_params=_cparams("arbitrary", "arbitrary"),
        name="moe_router",
    )(x, g.reshape(1, d), sh, sc, w_router_t)


def _cumsum_lanes(m, tri_ref):
    e, n = m.shape
    carry = jnp.zeros((e, 1), F32)
    outs = []
    for c in range(n // LANES):
        blk = m[:, c * LANES:(c + 1) * LANES]
        cs = _dot(blk.astype(BF16), tri_ref[...]) + carry
        outs.append(cs)
        carry = cs[:, LANES - 1:LANES]
    return jnp.concatenate(outs, axis=1)


def _select_kernel(a_ref, tri_ref, ind_ref, slot_ref, before_ref, slot_t_ref, gate_t_ref, *, cap):
    aff = a_ref[0]
    bits = pltpu.bitcast(aff, I32)
    e = aff.shape[0]

    def body(i, thr):
        cand = thr | (1 << (30 - i))
        cnt = jnp.sum((bits >= cand).astype(F32), axis=1, keepdims=True)
        return jnp.where(cnt >= cap, cand, thr)

    thr = lax.fori_loop(0, 31, body, jnp.zeros((e, 1), I32))
    gt = bits > thr
    eq = bits == thr
    n_gt = jnp.sum(gt.astype(F32), axis=1, keepdims=True)
    eq_rank = _cumsum_lanes(eq.astype(F32), tri_ref)
    sel = gt | (eq & (eq_rank <= cap - n_gt))
    self_ = sel.astype(F32)
    pos = _cumsum_lanes(self_, tri_ref) - 1.0
    slot = jnp.where(sel, pos, -1.0)
    gate = jnp.where(sel, aff, 0.0)
    slot_ref[0] = slot
    before_ref[0] = _dot(self_.astype(BF16), ind_ref[...])
    slot_t_ref[0] = slot.T
    gate_t_ref[0] = gate.T


def moe_select(aff, cap):
    b, e, n = aff.shape
    assert n // LANES <= LANES
    tri = jnp.asarray(np.triu(np.ones((LANES, LANES), np.float32)), BF16)
    ind = jnp.asarray(np.arange(n)[:, None] < LANES * np.arange(LANES)[None, :], BF16)
    row = pl.BlockSpec((1, e, n), lambda i: (i, 0, 0))
    col = pl.BlockSpec((1, n, e), lambda i: (i, 0, 0))
    return pl.pallas_call(
        functools.partial(_select_kernel, cap=cap),
        grid=(b,),
        in_specs=[row, pl.BlockSpec((LANES, LANES), lambda i: (0, 0)), pl.BlockSpec((n, LANES), lambda i: (0, 0))],
        out_specs=[row, pl.BlockSpec((1, e, LANES), lambda i: (i, 0, 0)), col, col],
        out_shape=[jax.ShapeDtypeStruct((b, e, n), F32), jax.ShapeDtypeStruct((b, e, LANES), F32),
                   jax.ShapeDtypeStruct((b, n, e), F32), jax.ShapeDtypeStruct((b, n, e), F32)],
        compiler_params=_cparams("arbitrary"),
        name="moe_select",
    )(aff, tri, ind)


def _window_starts(before, n, tile, width, cap, align):
    first = before[:, :, 0:n // LANES:tile // LANES].astype(I32)
    return jnp.clip(first // align * align, 0, cap - width).reshape(-1)


def _gather_kernel(ws_ref, slot_ref, h_ref, o_ref, acc_ref, *, tk, width):
    n = h_ref.shape[1]
    nt = n // tk
    base = (pl.program_id(0) * pl.num_programs(1) + pl.program_id(1)) * nt
    jcol = lax.broadcasted_iota(I32, (width, tk), 0)
    acc_ref[...] = jnp.zeros_like(acc_ref)
    for c in range(nt):
        ws = pl.multiple_of(ws_ref[base + c], SUBLANES)
        srow = slot_ref[0, 0, :, c * tk:(c + 1) * tk].astype(I32)
        onehot = jnp.where(jcol + ws == srow, 1.0, 0.0).astype(BF16)
        rows = pl.ds(ws, width)
        acc_ref[rows, :] = acc_ref[rows, :] + _dot(onehot, h_ref[0, c * tk:(c + 1) * tk, :])
    o_ref[0, 0] = acc_ref[...].astype(o_ref.dtype)


def moe_gather(slot, before, h, cap):
    b, e, n = slot.shape
    d = h.shape[2]
    tk = min(n, 2 * LANES)
    width = min(cap, tk + SUBLANES)
    ws = _window_starts(before, n, tk, width, cap, SUBLANES)
    grid_spec = pltpu.PrefetchScalarGridSpec(
        num_scalar_prefetch=1,
        grid=(b, e),
        in_specs=[pl.BlockSpec((1, 1, 1, n), lambda i, j, ws_ref: (i, j, 0, 0)),
                  pl.BlockSpec((1, n, d), lambda i, j, ws_ref: (i, 0, 0))],
        out_specs=pl.BlockSpec((1, 1, cap, d), lambda i, j, ws_ref: (i, j, 0, 0)),
        scratch_shapes=[pltpu.VMEM((cap, d), F32)])
    return pl.pallas_call(
        functools.partial(_gather_kernel, tk=tk, width=width),
        grid_spec=grid_spec,
        out_shape=jax.ShapeDtypeStruct((b, e, cap, d), BF16),
        compiler_params=_cparams("arbitrary", "arbitrary"),
        name="moe_gather",
    )(ws, slot.reshape(b, e, 1, n), h)


def _ffn_kernel(*refs, ns):
    x_refs = refs[:ns]
    wg_ref, wu_ref, wd_ref = refs[ns:ns + 3]
    o_refs = refs[ns + 3:2 * ns + 3]
    wgb_ref, wub_ref, wdb_ref = refs[2 * ns + 3:]

    @pl.when(pl.program_id(1) == 0)
    def _():
        wgb_ref[...] = wg_ref[0, 0].astype(BF16)
        wub_ref[...] = wu_ref[0, 0].astype(BF16)
        wdb_ref[...] = wd_ref[0, 0].astype(BF16)

    x = jnp.concatenate([x_ref[0, 0] for x_ref in x_refs], axis=0)
    a = _dot(x, wgb_ref[...])
    u = _dot(x, wub_ref[...])
    hid = (a / (1.0 + jnp.exp(-a))) * u
    y = _dot(hid.astype(BF16), wdb_ref[...])
    row = 0
    for o_ref in o_refs:
        rows = o_ref.shape[2]
        o_ref[0, 0] = y[row:row + rows].astype(o_ref.dtype)
        row += rows


def moe_ffn(xs_list, w_gate, w_up, w_down, layer):
    b, e, _, d = xs_list[0].shape
    f = w_gate.shape[3]
    x_specs = [pl.BlockSpec((1, 1, xs.shape[2], d), lambda j, i: (i, j, 0, 0)) for xs in xs_list]
    return pl.pallas_call(
        functools.partial(_ffn_kernel, ns=len(xs_list)),
        grid=(e, b),
        in_specs=x_specs + [pl.BlockSpec((1, 1, d, f), lambda j, i: (layer, j, 0, 0)),
                            pl.BlockSpec((1, 1, d, f), lambda j, i: (layer, j, 0, 0)),
                            pl.BlockSpec((1, 1, f, d), lambda j, i: (layer, j, 0, 0))],
        out_specs=x_specs,
        out_shape=[jax.ShapeDtypeStruct(xs.shape, BF16) for xs in xs_list],
        scratch_shapes=[pltpu.VMEM((d, f), BF16), pltpu.VMEM((d, f), BF16), pltpu.VMEM((f, d), BF16)],
        compiler_params=_cparams("arbitrary", "arbitrary"),
        name="moe_ffn",
    )(*xs_list, w_gate, w_up, w_down)


def _combine_kernel(ws_ref, slot_ref, gate_ref, ys_hbm, lat_ref, gt_ref, o_ref, ys_buf, sem, *, width, ne):
    bi, ti = pl.program_id(0), pl.program_id(1)
    nb, nt = pl.num_programs(0), pl.num_programs(1)
    cur = bi % 2

    def fetch(batch, buf):
        return pltpu.make_async_copy(ys_hbm.at[batch], ys_buf.at[buf], sem.at[buf])

    @pl.when(ti == 0)
    def _():
        @pl.when(bi == 0)
        def _():
            fetch(0, 0).start()

        fetch(bi, cur).wait()

        @pl.when(bi + 1 < nb)
        def _():
            fetch(bi + 1, 1 - cur).start()

    tn = lat_ref.shape[1]
    jrow = lax.broadcasted_iota(I32, (tn, width), 1)
    slot = slot_ref[0].astype(I32)
    gate = gate_ref[0]
    acc = jnp.zeros((tn, lat_ref.shape[2]), F32)
    for e in range(ne):
        ws = pl.multiple_of(ws_ref[(bi * ne + e) * nt + ti], BF16_ROWS)
        onehot = jnp.where(jrow + ws == slot[:, e:e + 1], 1.0, 0.0).astype(BF16)
        acc = acc + gate[:, e:e + 1] * _dot(onehot, ys_buf[cur, e, pl.ds(ws, width), :])
    o_ref[0] = lat_ref[0] + gt_ref[0] * acc


def moe_combine(slot_t, gate_t, before, ys, lat, gate_vec, cap):
    b, n, e = slot_t.shape
    d = lat.shape[2]
    tn = LANES
    width = min(cap, 2 * LANES)
    assert width == cap or width >= tn + BF16_ROWS
    ws = _window_starts(before, n, tn, width, cap, BF16_ROWS)
    grid_spec = pltpu.PrefetchScalarGridSpec(
        num_scalar_prefetch=1,
        grid=(b, n // tn),
        in_specs=[pl.BlockSpec((1, tn, e), lambda i, j, ws_ref: (i, j, 0)),
                  pl.BlockSpec((1, tn, e), lambda i, j, ws_ref: (i, j, 0)),
                  pl.BlockSpec(memory_space=pl.ANY),
                  pl.BlockSpec((1, tn, d), lambda i, j, ws_ref: (i, j, 0)),
                  pl.BlockSpec((1, 1, d), lambda i, j, ws_ref: (i, 0, 0))],
        out_specs=pl.BlockSpec((1, tn, d), lambda i, j, ws_ref: (i, j, 0)),
        scratch_shapes=[pltpu.VMEM((2, e, cap, d), BF16), pltpu.SemaphoreType.DMA((2,))])
    return pl.pallas_call(
        functools.partial(_combine_kernel, width=width, ne=e),
        grid_spec=grid_spec,
        out_shape=jax.ShapeDtypeStruct((b, n, d), F32),
        compiler_params=_cparams("arbitrary", "arbitrary"),
        name="moe_combine",
    )(ws, slot_t, gate_t, ys, lat, gate_vec)


def expert_choice_moe(streams, g, w_router, w_gate, w_up, w_down, layer):
    e = w_router.shape[1]
    routed = []
    for lat, sh, sc, _ in streams:
        cap = EC_CAPACITY_FACTOR * lat.shape[1] // e
        h, aff = moe_router(lat, g, sh, sc, w_router.T)
        slot, before, slot_t, gate_t = moe_select(aff, cap)
        routed.append((moe_gather(slot, before, h, cap), slot_t, gate_t, before, cap))
    ys_list = moe_ffn([r[0] for r in routed], w_gate, w_up, w_down, layer)
    return [moe_combine(slot_t, gate_t, before, ys, lat, gate_vec, cap)
            for (lat, _, _, gate_vec), (_, slot_t, gate_t, before, cap), ys in zip(streams, routed, ys_list)]


def kernel(x, c, ctx, c_ctx, ada_w, ada_b, norm_g, sc_w_in, sc_conv, sc_w_out, hy_w_in, hy_conv_w, hy_f_w1,
           hy_f_b1, hy_f_w2, hy_f_b2, hy_f_w3, hy_sin_freq, hy_skip, hy_w_out, at_w_qkv, at_q_g, at_k_g, at_w_o,
           moe_router_w, moe_w_gate, moe_w_up, moe_w_down):
    b, l, d = x.shape
    depth = ada_w.shape[0]
    assert b + 1 <= MOD_ROWS
    qd = at_w_o.shape[1]
    nq = qd // HEAD_DIM
    attn_layers = list(range(MIX_ATTN, depth, N_MIXERS))
    last_attn = max(attn_layers, default=-1)

    cond = jnp.zeros((MOD_ROWS, d), F32).at[:b].set(c).at[b].set(c_ctx)
    mods = ada_modulation(cond, ada_w, ada_b)

    lat, cx = x, ctx
    for i in range(depth):
        kind = i % N_MIXERS
        j = i // N_MIXERS
        need_ctx = i <= last_attn
        upd_ctx = i < last_attn
        m_l = mods[i, :b].reshape(b, 1, 6, d)
        sh1, sc1, g1, sh2, sc2, g2 = (m_l[:, :, k] for k in range(6))
        if need_ctx:
            m_c = jnp.broadcast_to(mods[i, b].reshape(1, 1, 6, d), (b, 1, 6, d))
            csh1, csc1, cg1, csh2, csc2, cg2 = (m_c[:, :, k] for k in range(6))
        streams = [(lat, sh1, sc1, g1)]
        if upd_ctx:
            streams.append((cx, csh1, csc1, cg1))

        outs = []
        if kind == MIX_SHORTCONV:
            w_in, w_out = sc_w_in[j].astype(BF16), sc_w_out[j].astype(BF16)
            for s, sh, sc, gt in streams:
                p = nm_matmul(s, norm_g[i, 0], sh, sc, w_in)
                outs.append(sc_out(p, sc_conv[j], w_out, s, gt))
        elif kind == MIX_HYENA:
            w_in, w_out = hy_w_in[j].astype(BF16), hy_w_out[j].astype(BF16)
            for s, sh, sc, gt in streams:
                p = nm_matmul(s, norm_g[i, 0], sh, sc, w_in)
                z = hyena_mixer(p, hy_conv_w[j], hy_f_w1[j], hy_f_b1[j], hy_f_w2[j], hy_f_b2[j], hy_f_w3[j],
                                hy_sin_freq[j], hy_skip[j])
                outs.append(proj_residual(z, w_out, s, gt))
        else:
            w_qkv, w_o = at_w_qkv[j].astype(BF16), at_w_o[j].astype(BF16)
            q_l, k_l, v_l = qkv_project(lat, norm_g[i, 0], sh1, sc1, w_qkv, at_q_g[j], at_k_g[j],
                                        _rope_tables(l), nq, True)
            lc = cx.shape[1]
            dummy = tuple(jnp.zeros((lc, HEAD_DIM), F32) for _ in range(3))
            if upd_ctx:
                q_c, k_c, v_c = qkv_project(cx, norm_g[i, 0], csh1, csc1, w_qkv, at_q_g[j], at_k_g[j],
                                            dummy, nq, False)
            else:
                k_c, v_c = qkv_project(cx, norm_g[i, 0], csh1, csc1, w_qkv[:, qd:], at_q_g[j], at_k_g[j],
                                       dummy, 0, False)
            v_all = jnp.concatenate([v_c, v_l], axis=1)
            o_l = attention(q_l, jnp.concatenate([k_c, k_l], axis=1), jnp.swapaxes(v_all, 1, 2))
            outs.append(proj_residual(o_l, w_o, lat, g1))
            if upd_ctx:
                outs.append(proj_residual(attention(q_c, k_c, jnp.swapaxes(v_c, 1, 2)), w_o, cx, cg1))

        moe_streams = [(outs[0], sh2, sc2, g2)]
        if upd_ctx:
            moe_streams.append((outs[1], csh2, csc2, cg2))
        moe_out = expert_choice_moe(moe_streams, norm_g[i, 1], moe_router_w[i], moe_w_gate, moe_w_up, moe_w_down, i)
        lat = moe_out[0]
        if upd_ctx:
            cx = moe_out[1]
    return lat
```

```python
import functools
import math

import jax
import jax.numpy as jnp
import numpy as np
from jax import lax
from jax.experimental import pallas as pl
from jax.experimental.pallas import tpu as pltpu

F32 = jnp.float32
BF16 = jnp.bfloat16
I32 = jnp.int32
HIGHEST = lax.Precision.HIGHEST

N_MIXERS = 3
MIX_SHORTCONV, MIX_HYENA, MIX_ATTN = 0, 1, 2
CONV_W = 3
EPS = 1e-6
HY_ORDER = 2
HY_EMB = 33
HY_BANDS = (HY_EMB - 1) // 2
HY_FAST_DECAY = 0.3
HY_SLOW_DECAY = 1.5
HY_TARGET = 1e-2
HEAD_DIM = 128
N_KV_HEADS = 2
GRID_W = 64
ROPE_THETA = 10000.0
EC_CAPACITY_FACTOR = 2
Q_PRESCALE = HEAD_DIM ** -0.5 * math.log2(math.e)

LANES = 128
SUBLANES = 8
BF16_ROWS = 16
VMEM_LIMIT_BYTES = 56 * 1024 * 1024
MOD_ROWS = 16
FFT_N2 = 128
HY_DT = LANES
MOE_TILE = 2 * LANES
MOE_WIN = LANES
MOE_GROUP = SUBLANES
DFT_UNROLL = 4
DFT_UNROLL_STAGE1 = 8


def _cparams(*sem):
    return pltpu.CompilerParams(dimension_semantics=sem, vmem_limit_bytes=VMEM_LIMIT_BYTES)


def _row_tile(n, cap=512):
    t = min(n, cap)
    assert n % t == 0
    return t


def _dot(a, b):
    return jnp.dot(a, b, preferred_element_type=F32)


def _ada_kernel(s_ref, w_ref, b_ref, o_ref):
    s = s_ref[...]
    s = s / (1.0 + jnp.exp(-s))
    o_ref[0] = jnp.dot(s, w_ref[0], preferred_element_type=F32, precision=HIGHEST) + b_ref[0]


def ada_modulation(cond, ada_w, ada_b):
    depth, d, n = ada_w.shape
    tn = 1536 if n % 1536 == 0 else n
    return pl.pallas_call(
        _ada_kernel,
        grid=(depth, n // tn),
        in_specs=[pl.BlockSpec((MOD_ROWS, d), lambda i, j: (0, 0)),
                  pl.BlockSpec((1, d, tn), lambda i, j: (i, 0, j)),
                  pl.BlockSpec((1, 1, tn), lambda i, j: (i, 0, j))],
        out_specs=pl.BlockSpec((1, MOD_ROWS, tn), lambda i, j: (i, 0, j)),
        out_shape=jax.ShapeDtypeStruct((depth, MOD_ROWS, n), F32),
        compiler_params=_cparams("arbitrary", "arbitrary"),
        name="ada_modulation",
    )(cond, ada_w, ada_b.reshape(depth, 1, n))


def _norm_mod(x, g, sh, sc):
    y = x * lax.rsqrt(jnp.mean(x * x, axis=-1, keepdims=True) + EPS)
    return (y * g) * (1.0 + sc) + sh


def _nm_matmul_kernel(x_ref, g_ref, sh_ref, sc_ref, w_ref, o_ref):
    h = _norm_mod(x_ref[0], g_ref[...], sh_ref[0], sc_ref[0]).astype(BF16)
    o_ref[0] = _dot(h, w_ref[...]).astype(o_ref.dtype)


def nm_matmul(x, g, sh, sc, w, out_dtype=BF16):
    b, l, d = x.shape
    n = w.shape[1]
    tm = _row_tile(l)
    return pl.pallas_call(
        _nm_matmul_kernel,
        grid=(b, l // tm),
        in_specs=[pl.BlockSpec((1, tm, d), lambda i, j: (i, j, 0)),
                  pl.BlockSpec((1, d), lambda i, j: (0, 0)),
                  pl.BlockSpec((1, 1, d), lambda i, j: (i, 0, 0)),
                  pl.BlockSpec((1, 1, d), lambda i, j: (i, 0, 0)),
                  pl.BlockSpec((d, n), lambda i, j: (0, 0))],
        out_specs=pl.BlockSpec((1, tm, n), lambda i, j: (i, j, 0)),
        out_shape=jax.ShapeDtypeStruct((b, l, n), out_dtype),
        compiler_params=_cparams("arbitrary", "arbitrary"),
        name="nm_matmul",
    )(x, g.reshape(1, d), sh, sc, w)


def _proj_res_kernel(a_ref, w_ref, lat_ref, gt_ref, o_ref):
    o_ref[0] = lat_ref[0] + gt_ref[0] * _dot(a_ref[0], w_ref[...])


def proj_residual(a, w, lat, gate):
    b, l, k = a.shape
    d = w.shape[1]
    tm = _row_tile(l)
    return pl.pallas_call(
        _proj_res_kernel,
        grid=(b, l // tm),
        in_specs=[pl.BlockSpec((1, tm, k), lambda i, j: (i, j, 0)),
                  pl.BlockSpec((k, d), lambda i, j: (0, 0)),
                  pl.BlockSpec((1, tm, d), lambda i, j: (i, j, 0)),
                  pl.BlockSpec((1, 1, d), lambda i, j: (i, 0, 0))],
        out_specs=pl.BlockSpec((1, tm, d), lambda i, j: (i, j, 0)),
        out_shape=jax.ShapeDtypeStruct((b, l, d), F32),
        compiler_params=_cparams("arbitrary", "arbitrary"),
        name="proj_residual",
    )(a, w, lat, gate)


def _sc_out_kernel(pb_ref, pc_ref, pv_ref, cp_ref, vp_ref, cn_ref, vn_ref, cw_ref, w_ref, lat_ref, gt_ref,
                   o_ref, u_ref, *, tm):
    j = pl.program_id(1)
    nj = pl.num_programs(1)
    d = u_ref.shape[1]
    u_ref[pl.ds(SUBLANES, tm), :] = pc_ref[0].astype(F32) * pv_ref[0].astype(F32)
    prev = cp_ref[0, BF16_ROWS - 1:BF16_ROWS, :].astype(F32) * vp_ref[0, BF16_ROWS - 1:BF16_ROWS, :].astype(F32)
    nxt = cn_ref[0, 0:1, :].astype(F32) * vn_ref[0, 0:1, :].astype(F32)
    prev = jnp.where(j == 0, 0.0, prev)
    nxt = jnp.where(j == nj - 1, 0.0, nxt)
    u_ref[pl.ds(0, SUBLANES), :] = jnp.broadcast_to(prev, (SUBLANES, d))
    u_ref[pl.ds(SUBLANES + tm, SUBLANES), :] = jnp.broadcast_to(nxt, (SUBLANES, d))
    cw = cw_ref[...]
    y = (u_ref[pl.ds(SUBLANES - 1, tm), :] * cw[0:1] + u_ref[pl.ds(SUBLANES, tm), :] * cw[1:2]
         + u_ref[pl.ds(SUBLANES + 1, tm), :] * cw[2:3])
    m = (pb_ref[0].astype(F32) * y).astype(BF16)
    o_ref[0] = lat_ref[0] + gt_ref[0] * _dot(m, w_ref[...])


def sc_out(p, conv_w, w_out, lat, gate):
    b, l, d3 = p.shape
    d = d3 // 3
    tm = _row_tile(l)
    hb = tm // BF16_ROWS
    nhb = l // BF16_ROWS

    def prev_map(col):
        return lambda i, j: (i, jnp.maximum(j * hb - 1, 0), col)

    def next_map(col):
        return lambda i, j: (i, jnp.minimum((j + 1) * hb, nhb - 1), col)

    return pl.pallas_call(
        functools.partial(_sc_out_kernel, tm=tm),
        grid=(b, l // tm),
        in_specs=[pl.BlockSpec((1, tm, d), lambda i, j: (i, j, 0)),
                  pl.BlockSpec((1, tm, d), lambda i, j: (i, j, 1)),
                  pl.BlockSpec((1, tm, d), lambda i, j: (i, j, 2)),
                  pl.BlockSpec((1, BF16_ROWS, d), prev_map(1)),
                  pl.BlockSpec((1, BF16_ROWS, d), prev_map(2)),
                  pl.BlockSpec((1, BF16_ROWS, d), next_map(1)),
                  pl.BlockSpec((1, BF16_ROWS, d), next_map(2)),
                  pl.BlockSpec((CONV_W, d), lambda i, j: (0, 0)),
                  pl.BlockSpec((d, d), lambda i, j: (0, 0)),
                  pl.BlockSpec((1, tm, d), lambda i, j: (i, j, 0)),
                  pl.BlockSpec((1, 1, d), lambda i, j: (i, 0, 0))],
        out_specs=pl.BlockSpec((1, tm, d), lambda i, j: (i, j, 0)),
        out_shape=jax.ShapeDtypeStruct((b, l, d), F32),
        scratch_shapes=[pltpu.VMEM((tm + 2 * SUBLANES, d), F32)],
        compiler_params=_cparams("arbitrary", "arbitrary"),
        name="sc_out",
    )(p, p, p, p, p, p, p, conv_w, w_out, lat, gate)


def _hy_filter_kernel(z_ref, w1_ref, b1_ref, w2_ref, b2_ref, fr_ref, w3f_ref, w3b_ref, t_ref, dl_ref,
                      of_ref, ob_ref, a_ref):
    @pl.when((pl.program_id(0) == 0) & (pl.program_id(1) == 0))
    def _():
        fr = fr_ref[...]
        a1 = jnp.sin(fr * (jnp.dot(z_ref[...], w1_ref[...], preferred_element_type=F32, precision=HIGHEST)
                           + b1_ref[...]))
        a_ref[...] = jnp.sin(fr * (jnp.dot(a1, w2_ref[...], preferred_element_type=F32, precision=HIGHEST)
                                   + b2_ref[...]))

    a = a_ref[...]
    decay = jnp.exp(-t_ref[...] * dl_ref[...])
    hf = jnp.dot(a, w3f_ref[...], preferred_element_type=F32, precision=HIGHEST) * decay
    hb = jnp.dot(a, w3b_ref[...], preferred_element_type=F32, precision=HIGHEST) * decay
    row = lax.broadcasted_iota(I32, hb.shape, 0)
    hb = jnp.where(row == 0, 0.0, hb)
    inv = 1.0 / (jnp.sum(jnp.abs(hf), axis=0, keepdims=True) + jnp.sum(jnp.abs(hb), axis=0, keepdims=True))
    of_ref[0] = hf * inv
    ob_ref[0] = hb * inv


def hy_filter_taps(l, d, f_w1, f_b1, f_w2, f_b2, f_w3, sin_freq):
    hid = f_w1.shape[1]
    hp = LANES
    t = np.linspace(0.0, 1.0, l)[:, None]
    bands = np.linspace(1e-4, HY_BANDS - 1, HY_BANDS)[None, :]
    ang = (2.0 * math.pi / l) * np.arange(l)[:, None] * bands
    z = np.zeros((l, hp), np.float64)
    z[:, :HY_EMB] = np.concatenate([t, np.cos(ang), -np.sin(ang)], axis=-1)
    deltas = np.abs(np.linspace(math.log(HY_TARGET) / HY_SLOW_DECAY, math.log(HY_TARGET) / HY_FAST_DECAY, d))[None, :]

    def pad2(a, r, c):
        return jnp.zeros((r, c), F32).at[:a.shape[0], :a.shape[1]].set(a.astype(F32))

    w1 = pad2(f_w1, hp, hp)
    w2 = pad2(f_w2, hp, hp)
    w3 = pad2(f_w3, hp, f_w3.shape[1])
    b1 = pad2(f_b1[None], 1, hp)
    b2 = pad2(f_b2[None], 1, hp)
    fr = pad2(sin_freq[None], 1, hp)
    tc = min(d, 2 * LANES)
    nd = d // tc
    full = lambda o, j: (0, 0)
    return pl.pallas_call(
        _hy_filter_kernel,
        grid=(HY_ORDER, nd),
        in_specs=[pl.BlockSpec((l, hp), full), pl.BlockSpec((hp, hp), full), pl.BlockSpec((1, hp), full),
                  pl.BlockSpec((hp, hp), full), pl.BlockSpec((1, hp), full), pl.BlockSpec((1, hp), full),
                  pl.BlockSpec((hp, tc), lambda o, j: (0, o * nd + j)),
                  pl.BlockSpec((hp, tc), lambda o, j: (0, (HY_ORDER + o) * nd + j)),
                  pl.BlockSpec((l, 1), full),
                  pl.BlockSpec((1, tc), lambda o, j: (0, j))],
        out_specs=[pl.BlockSpec((1, l, tc), lambda o, j: (o, 0, j)),
                   pl.BlockSpec((1, l, tc), lambda o, j: (o, 0, j))],
        out_shape=[jax.ShapeDtypeStruct((HY_ORDER, l, d), F32)] * 2,
        scratch_shapes=[pltpu.VMEM((l, hp), F32)],
        compiler_params=_cparams("arbitrary", "arbitrary"),
        name="hy_filter_taps",
    )(jnp.asarray(z, F32), w1, b1, w2, b2, fr, w3, w3, jnp.asarray(t, F32), jnp.asarray(deltas, F32))


def _dft_consts(l):
    n = 2 * l
    if n <= 4 * FFT_N2:
        n1, n2 = 1, n
    else:
        n1, n2 = n // FFT_N2, FFT_N2
    if n1 == 1:
        k = np.arange(n)[:, None]
        t = np.arange(l)[None, :]
        ang = 2.0 * np.pi * (k * t % n) / n
        c, s = np.cos(ang), np.sin(ang)
        fwd = np.block([[c, s], [-s, c]])
        inv = np.block([[c.T, -s.T], [s.T, c.T]]) / n
        return n1, n2, dict(fwd=jnp.asarray(fwd, BF16), inv=jnp.asarray(inv, BF16))
    n1h = n1 // 2
    k1 = np.arange(n1)[None, :, None]
    m1 = np.arange(n1h)[None, None, :]
    r2 = np.arange(n2)[:, None, None]
    ang = 2.0 * np.pi * ((k1 * (n2 * m1 + r2)) % n) / n
    c, s = np.cos(ang), np.sin(ang)
    m1f = np.concatenate([np.concatenate([c, s], 2), np.concatenate([-s, c], 2)], 1)
    ct, st = np.swapaxes(c, 1, 2) / n, np.swapaxes(s, 1, 2) / n
    m1i = np.concatenate([np.concatenate([ct, -st], 2), np.concatenate([st, ct], 2)], 1)
    a2 = 2.0 * np.pi * ((np.arange(n2)[:, None] * np.arange(n2)[None, :]) % n2) / n2
    c2, s2 = np.cos(a2), np.sin(a2)
    f2 = np.block([[c2, s2], [-s2, c2]])
    f2i = np.block([[c2, -s2], [s2, c2]])

    def il(m):
        return np.stack([np.arange(m), m + np.arange(m)], 1).reshape(-1)

    m1f = m1f[:, il(n1), :][:, :, il(n1h)]
    m1i = m1i[:, il(n1h), :][:, :, il(n1)]
    f2 = f2[:, il(n2)]
    f2i = f2i[il(n2), :][:, il(n2)]
    half = n2 // 2
    m1f = np.concatenate([m1f[:half], m1f[half:]], 2)
    m1i = np.concatenate([m1i[:half], m1i[half:]], 2)
    return n1, n2, dict(m1f=jnp.asarray(m1f, BF16), m1i=jnp.asarray(m1i, BF16),
                        f2=jnp.asarray(f2, BF16), f2i=jnp.asarray(f2i, BF16))


def _blockdiag(a, b):
    z = jnp.zeros_like(a)
    return jnp.concatenate([jnp.concatenate([a, z], axis=1), jnp.concatenate([z, b], axis=1)], axis=0)


def _halves(y):
    return y[:, :LANES], y[:, LANES:]


HI_HALF = -65536


def _pack_pair(re, im):
    lo = (pltpu.bitcast(re.astype(BF16).astype(F32), I32) >> 16) & 0xFFFF
    if im is None:
        return lo
    return lo | (pltpu.bitcast(im.astype(BF16).astype(F32), I32) & HI_HALF)


def _unpack_pair(w):
    return pltpu.bitcast(w << 16, F32), pltpu.bitcast(w & HI_HALF, F32)


def _as_rows(w):
    return pltpu.bitcast(w, BF16)


def _as_words(y):
    return pltpu.bitcast(y.astype(BF16), I32)


def _fwd_stage1(x_pk, a_pk, m1f_ref, n1, n2):
    n1h, half = n1 // 2, n2 // 2

    def column(r):
        return _as_rows(x_pk[pl.ds(r, n1h, stride=n2), :])

    def body(r, carry):
        w = _as_words(_dot(m1f_ref[r], _blockdiag(column(r), column(r + half))))
        for rr, ww in zip((r, r + half), _halves(w)):
            a_pk[pl.ds(pl.multiple_of(rr * n1, n1), n1), :] = ww
        return carry

    lax.fori_loop(0, half, body, 0, unroll=min(half, DFT_UNROLL_STAGE1))


def _inv_stage1(b_pk, x_pk, m1i_ref, n1, n2):
    n1h, half = n1 // 2, n2 // 2

    def column(r):
        return _as_rows(b_pk[pl.ds(pl.multiple_of(r * n1, n1), n1), :])

    def body(r, carry):
        w = _as_words(_dot(m1i_ref[r], _blockdiag(column(r), column(r + half))))
        for rr, ww in zip((r, r + half), _halves(w)):
            x_pk[pl.ds(rr, n1h, stride=n2), :] = ww
        return carry

    lax.fori_loop(0, half, body, 0, unroll=min(half, DFT_UNROLL_STAGE1))


def _load_chunk(a_pk, k1, n1, n2):
    ra = pl.ds(k1, n2, stride=n1)
    rb = pl.ds(k1 + n1 // 2, n2, stride=n1)
    return jnp.concatenate([_as_rows(a_pk[ra, :]), _as_rows(a_pk[rb, :])], axis=1)


def _hy_spec_kernel(*refs, l, n1, n2, dense):
    if dense:
        tf_ref, tb_ref, sk_ref, fwd_ref, o_ref = refs
        n = n2
        fw = fwd_ref[:, 0:l]
        sf = _dot(fw, tf_ref[0].astype(BF16))
        sb = _dot(fw, tb_ref[0].astype(BF16))
        o_ref[0, 0] = (sf[:n] + sb[:n] + sk_ref[0]).astype(o_ref.dtype)
        o_ref[0, 1] = (sf[n:] - sb[n:]).astype(o_ref.dtype)
        return
    tf_ref, tb_ref, sk_ref, m1f_ref, f2_ref, o_ref, x_pk, a_pk, h_re, h_im = refs
    for run, t_ref in enumerate((tf_ref, tb_ref)):
        x_pk[...] = _pack_pair(t_ref[0], None)
        _fwd_stage1(x_pk, a_pk, m1f_ref, n1, n2)

        def body(k1, carry, run=run):
            x = _dot(f2_ref[...], _load_chunk(a_pk, k1, n1, n2))
            for kk, xx in zip((k1, k1 + n1 // 2), _halves(x)):
                rows = pl.ds(pl.multiple_of(kk * n2, n2), n2)
                if run == 0:
                    h_re[rows, :] = xx[:n2]
                    h_im[rows, :] = xx[n2:]
                else:
                    h_re[rows, :] = h_re[rows, :] + xx[:n2]
                    h_im[rows, :] = h_im[rows, :] - xx[n2:]
            return carry

        lax.fori_loop(0, n1 // 2, body, 0, unroll=DFT_UNROLL)
    o_ref[0, 0] = (h_re[...] + sk_ref[0]).astype(o_ref.dtype)
    o_ref[0, 1] = h_im[...].astype(o_ref.dtype)


def hy_spectrum(taps_f, taps_b, skip, consts, n1, n2):
    o, l, d = taps_f.shape
    n = 2 * l
    dt = HY_DT
    dense = n1 == 1
    tap_spec = pl.BlockSpec((1, l, dt), lambda i, j: (i, 0, j))
    in_specs = [tap_spec, tap_spec, pl.BlockSpec((1, 1, dt), lambda i, j: (i, 0, j))]
    if dense:
        mats = [consts["fwd"]]
        scratch = []
    else:
        mats = [consts["m1f"], consts["f2"]]
        scratch = [pltpu.VMEM((l, dt), I32), pltpu.VMEM((n, dt), I32)] + [pltpu.VMEM((n, dt), F32)] * 2
    in_specs += [pl.BlockSpec(m.shape, lambda i, j, nd=m.ndim: (0,) * nd) for m in mats]
    return pl.pallas_call(
        functools.partial(_hy_spec_kernel, l=l, n1=n1, n2=n2, dense=dense),
        grid=(o, d // dt),
        in_specs=in_specs,
        out_specs=pl.BlockSpec((1, 2, n, dt), lambda i, j: (i, 0, 0, j)),
        out_shape=jax.ShapeDtypeStruct((o, 2, n, d), BF16),
        scratch_shapes=scratch,
        compiler_params=_cparams("arbitrary", "arbitrary"),
        name="hy_spectrum",
    )(taps_f, taps_b, skip.reshape(o, 1, d), *mats)


def _conv3_rows(src, cw, pad_ref, l):
    c = pad_ref.shape[1]
    pad_ref[pl.ds(0, SUBLANES), :] = jnp.zeros((SUBLANES, c), F32)
    pad_ref[pl.ds(SUBLANES + l, SUBLANES), :] = jnp.zeros((SUBLANES, c), F32)
    pad_ref[pl.ds(SUBLANES, l), :] = src
    return (pad_ref[pl.ds(SUBLANES - 1, l), :] * cw[0:1] + pad_ref[pl.ds(SUBLANES, l), :] * cw[1:2]
            + pad_ref[pl.ds(SUBLANES + 1, l), :] * cw[2:3])


def _hy_conv_kernel(*refs, l, n1, n2, dense, conv_z):
    if dense:
        z_ref, g_ref, cwz_ref, cwg_ref, h_ref, fwd_ref, inv_ref, o_ref, pad_ref = refs
    else:
        (z_ref, g_ref, cwz_ref, cwg_ref, h_ref, m1f_ref, m1i_ref, f2_ref, f2i_ref, o_ref,
         x_pk, a_pk, b_pk, pad_ref) = refs
    zs = []
    for e in range(2):
        z = z_ref[e, 0].astype(F32)
        zs.append(_conv3_rows(z, cwz_ref[...], pad_ref, l) if conv_z else z)
    if dense:
        n = n2
        s = _dot(fwd_ref[...], jnp.concatenate(zs, axis=0).astype(BF16))
        hr = h_ref[0, 0].astype(F32)
        hi = h_ref[0, 1].astype(F32)
        zr = s[:n] * hr - s[n:] * hi
        zi = s[:n] * hi + s[n:] * hr
        y = _dot(inv_ref[...], jnp.concatenate([zr, zi], axis=0).astype(BF16))
        ys = (y[:l], y[l:])
    else:
        x_pk[...] = _pack_pair(zs[0], zs[1])
        _fwd_stage1(x_pk, a_pk, m1f_ref, n1, n2)

        nk = n1 // 2

        def spectrum_step(k1, carry):
            s = _dot(f2_ref[...], _load_chunk(a_pk, k1, n1, n2))
            rows = [pl.ds(pl.multiple_of(k * n2, n2), n2) for k in (k1, k1 + nk)]
            hr = jnp.concatenate([h_ref[0, 0, r, :] for r in rows], axis=1).astype(F32)
            hi = jnp.concatenate([h_ref[0, 1, r, :] for r in rows], axis=1).astype(F32)
            zr = s[:n2] * hr - s[n2:] * hi
            zi = s[:n2] * hi + s[n2:] * hr
            for r, zr_half, zi_half in zip(rows, _halves(zr), _halves(zi)):
                b_pk[r, :] = _pack_pair(zr_half, zi_half)
            return carry

        lax.fori_loop(0, nk, spectrum_step, 0, unroll=min(nk, DFT_UNROLL))

        def inverse_step(k1, carry):
            kk = (k1, k1 + nk)
            z = jnp.concatenate([_as_rows(b_pk[pl.ds(pl.multiple_of(k * n2, n2), n2), :]) for k in kk], axis=1)
            w = _as_words(_dot(f2i_ref[...], z))
            for k, ww in zip(kk, _halves(w)):
                a_pk[pl.ds(k, n2, stride=n1), :] = ww
            return carry

        lax.fori_loop(0, nk, inverse_step, 0, unroll=min(nk, DFT_UNROLL))
        _inv_stage1(a_pk, x_pk, m1i_ref, n1, n2)
        ys = _unpack_pair(x_pk[...])
    for e in range(2):
        g = _conv3_rows(g_ref[e, 0].astype(F32), cwg_ref[...], pad_ref, l)
        o_ref[e, 0] = (g * ys[e]).astype(o_ref.dtype)


def hy_conv(zsrc, zcol, p4, gcol, conv_w, spec, order, consts, n1, n2, conv_z):
    _, pairs, l, _ = zsrc.shape
    d = spec.shape[3]
    n = 2 * l
    dt = HY_DT
    nd = d // dt
    dense = n1 == 1
    once = pl.Buffered(1)
    in_specs = [pl.BlockSpec((2, 1, l, dt), lambda j, i: (0, i, 0, zcol * nd + j)),
                pl.BlockSpec((2, 1, l, dt), lambda j, i: (0, i, 0, gcol * nd + j)),
                pl.BlockSpec((CONV_W, dt), lambda j, i: (0, j)),
                pl.BlockSpec((CONV_W, dt), lambda j, i: (0, gcol * nd + j)),
                pl.BlockSpec((1, 2, n, dt), lambda j, i: (order, 0, 0, j), pipeline_mode=once)]
    if dense:
        mats = [consts["fwd"], consts["inv"]]
        scratch = []
    else:
        mats = [consts["m1f"], consts["m1i"], consts["f2"], consts["f2i"]]
        scratch = [pltpu.VMEM((l, dt), I32)] + [pltpu.VMEM((n, dt), I32)] * 2
    scratch = scratch + [pltpu.VMEM((l + 2 * SUBLANES, dt), F32)]
    in_specs += [pl.BlockSpec(m.shape, lambda j, i, nd_=m.ndim: (0,) * nd_, pipeline_mode=once) for m in mats]
    return pl.pallas_call(
        functools.partial(_hy_conv_kernel, l=l, n1=n1, n2=n2, dense=dense, conv_z=conv_z),
        grid=(nd, pairs),
        in_specs=in_specs,
        out_specs=pl.BlockSpec((2, 1, l, dt), lambda j, i: (0, i, 0, j)),
        out_shape=jax.ShapeDtypeStruct((2, pairs, l, d), BF16),
        scratch_shapes=scratch,
        compiler_params=_cparams("arbitrary", "arbitrary"),
        name="hy_conv",
    )(zsrc, p4, conv_w, conv_w, spec, *mats)


def hyena_mixer(p, conv_w, f_w1, f_b1, f_w2, f_b2, f_w3, sin_freq, skip):
    b, l, d3 = p.shape
    d = d3 // 3
    assert b % 2 == 0 and d % HY_DT == 0
    n1, n2, consts = _dft_consts(l)
    taps_f, taps_b = hy_filter_taps(l, d, f_w1, f_b1, f_w2, f_b2, f_w3, sin_freq)
    spec = hy_spectrum(taps_f, taps_b, skip, consts, n1, n2)
    p4 = p.reshape(2, b // 2, l, d3)
    z = hy_conv(p4, 0, p4, 1, conv_w, spec, 0, consts, n1, n2, True)
    z = hy_conv(z, 0, p4, 2, conv_w, spec, 1, consts, n1, n2, False)
    return z.reshape(b, l, d)


def _head_norm(x, g):
    return x * lax.rsqrt(jnp.mean(x * x, axis=-1, keepdims=True) + EPS) * g


def _rope(x, cos, sin_a, sin_b):
    return (x * cos + pltpu.roll(x, HEAD_DIM // 4, 1) * sin_a
            + pltpu.roll(x, HEAD_DIM - HEAD_DIM // 4, 1) * sin_b)


def _qkv_kernel(x_ref, g_ref, sh_ref, sc_ref, w_ref, qg_ref, kg_ref, cos_ref, sa_ref, sb_ref,
                *out_refs, nq, rope):
    q_ref = out_refs[0] if nq else None
    k_ref, v_ref = out_refs[-2:]
    h = _norm_mod(x_ref[0], g_ref[...], sh_ref[0], sc_ref[0]).astype(BF16)
    acc = _dot(h, w_ref[...])
    hd = HEAD_DIM
    for i in range(nq + N_KV_HEADS):
        t = acc[:, i * hd:(i + 1) * hd]
        t = _head_norm(t, qg_ref[...] if i < nq else kg_ref[...])
        if rope:
            t = _rope(t, cos_ref[...], sa_ref[...], sb_ref[...])
        if i < nq:
            q_ref[0, :, i * hd:(i + 1) * hd] = (t * Q_PRESCALE).astype(BF16)
        else:
            k_ref[0, :, (i - nq) * hd:(i - nq + 1) * hd] = t.astype(BF16)
    v_ref[0] = acc[:, (nq + N_KV_HEADS) * hd:].astype(BF16)


def qkv_project(x, g, sh, sc, w, q_g, k_g, rope_tabs, nq, rope):
    b, l, d = x.shape
    n = w.shape[1]
    hd = HEAD_DIM
    kvd = N_KV_HEADS * hd
    tm = _row_tile(l)
    cos, sa, sb = rope_tabs
    tab = pl.BlockSpec((tm, hd), lambda i, j: (j, 0))
    vec = pl.BlockSpec((1, 1, d), lambda i, j: (i, 0, 0))
    widths = ([nq * hd] if nq else []) + [kvd, kvd]
    return pl.pallas_call(
        functools.partial(_qkv_kernel, nq=nq, rope=rope),
        grid=(b, l // tm),
        in_specs=[pl.BlockSpec((1, tm, d), lambda i, j: (i, j, 0)),
                  pl.BlockSpec((1, d), lambda i, j: (0, 0)), vec, vec,
                  pl.BlockSpec((d, n), lambda i, j: (0, 0)),
                  pl.BlockSpec((1, hd), lambda i, j: (0, 0)),
                  pl.BlockSpec((1, hd), lambda i, j: (0, 0)),
                  tab, tab, tab],
        out_specs=[pl.BlockSpec((1, tm, wd), lambda i, j: (i, j, 0)) for wd in widths],
        out_shape=[jax.ShapeDtypeStruct((b, l, wd), BF16) for wd in widths],
        compiler_params=_cparams("arbitrary", "arbitrary"),
        name="qkv_project",
    )(x, g.reshape(1, d), sh, sc, w, q_g.reshape(1, hd), k_g.reshape(1, hd), cos, sa, sb)


def _rope_tables(l):
    rows = l // GRID_W
    row = np.repeat(np.arange(rows, dtype=np.float64), GRID_W)
    col = np.tile(np.arange(GRID_W, dtype=np.float64), rows)
    axis = HEAD_DIM // 2
    inv = ROPE_THETA ** (-np.arange(0, axis, 2, dtype=np.float64) / axis)

    def axis_angles(pos):
        a = pos[:, None] * inv[None, :]
        return np.concatenate([a, a], axis=-1)

    ang = np.concatenate([axis_angles(row), axis_angles(col)], axis=-1)
    cos, sin = np.cos(ang), np.sin(ang)
    lane = np.arange(HEAD_DIM)[None, :] % axis
    upper = lane >= axis // 2
    sin_a = np.where(upper, sin, 0.0)
    sin_b = np.where(upper, 0.0, -sin)
    return tuple(jnp.asarray(t, F32) for t in (cos, sin_a, sin_b))


def _attn_kernel(q_ref, k_ref, vt_ref, o_ref, *, groups):
    k = k_ref[0]
    vt = vt_ref[0]
    hd = HEAD_DIM
    tq = q_ref.shape[1]
    hp = 2 if groups % 2 == 0 else 1
    def scores(g0):
        q = jnp.concatenate([q_ref[0, :, g * hd:(g + 1) * hd] for g in range(g0, g0 + hp)], axis=0)
        return lax.dot_general(k, q, (((1,), (1,)), ((), ())), preferred_element_type=F32)

    sts = [scores(g0) for g0 in range(0, groups, hp)]
    for g0, st in zip(range(0, groups, hp), sts):
        m = jnp.max(st, axis=0, keepdims=True)
        pt = jnp.exp2(st - m)
        den = jnp.sum(pt, axis=0, keepdims=True)
        ot = _dot(vt, pt.astype(BF16)) / den
        for i in range(hp):
            g = g0 + i
            o_ref[0, :, g * hd:(g + 1) * hd] = ot[:, i * tq:(i + 1) * tq].T.astype(o_ref.dtype)


def attention(q, k, vt):
    b, l, qd = q.shape
    s = k.shape[1]
    hd = HEAD_DIM
    groups = qd // hd // N_KV_HEADS
    tq = _row_tile(l, 256)
    return pl.pallas_call(
        functools.partial(_attn_kernel, groups=groups),
        grid=(b, N_KV_HEADS, l // tq),
        in_specs=[pl.BlockSpec((1, tq, groups * hd), lambda i, h, j: (i, j, h)),
                  pl.BlockSpec((1, s, hd), lambda i, h, j: (i, 0, h)),
                  pl.BlockSpec((1, hd, s), lambda i, h, j: (i, h, 0))],
        out_specs=pl.BlockSpec((1, tq, groups * hd), lambda i, h, j: (i, j, h)),
        out_shape=jax.ShapeDtypeStruct((b, l, qd), BF16),
        compiler_params=_cparams("arbitrary", "arbitrary", "arbitrary"),
        name="attention",
    )(q, k, vt)


def _router_kernel(x_ref, g_ref, sh_ref, sc_ref, wr_ref, h_ref, a_ref):
    h = _norm_mod(x_ref[0], g_ref[...], sh_ref[0], sc_ref[0])
    d = h.shape[1]
    logits = lax.dot_general(wr_ref[...], h, (((1,), (1,)), ((), ())), preferred_element_type=F32,
                             precision=HIGHEST)
    ex = jnp.exp(logits - jnp.max(logits, axis=0, keepdims=True))
    aff = ex / jnp.sum(ex, axis=0, keepdims=True)
    a_ref[0] = aff
    ne, tm = aff.shape
    hi = aff.astype(BF16).astype(F32)
    mid = (aff - hi).astype(BF16).astype(F32)
    lo = (aff - hi - mid).astype(BF16).astype(F32)
    pieces = jnp.concatenate([hi, mid, lo, jnp.zeros((LANES - 3 * ne, tm), F32)], axis=0)
    h_ref[0, :, 0:d] = h.astype(BF16)
    h_ref[0, :, d:d + LANES] = pieces.T.astype(BF16)


def moe_router(x, g, sh, sc, w_router_t):
    b, n, d = x.shape
    e = w_router_t.shape[0]
    assert 3 * e <= LANES and e % SUBLANES == 0
    tm = _row_tile(n)
    vec = pl.BlockSpec((1, 1, d), lambda i, j: (i, 0, 0))
    return pl.pallas_call(
        _router_kernel,
        grid=(b, n // tm),
        in_specs=[pl.BlockSpec((1, tm, d), lambda i, j: (i, j, 0)),
                  pl.BlockSpec((1, d), lambda i, j: (0, 0)), vec, vec,
                  pl.BlockSpec((e, d), lambda i, j: (0, 0))],
        out_specs=[pl.BlockSpec((1, tm, d + LANES), lambda i, j: (i, j, 0)),
                   pl.BlockSpec((1, e, tm), lambda i, j: (i, 0, j))],
        out_shape=[jax.ShapeDtypeStruct((b, n, d + LANES), BF16), jax.ShapeDtypeStruct((b, e, n), F32)],
        compiler_params=_cparams("arbitrary", "arbitrary"),
        name="moe_router",
    )(x, g.reshape(1, d), sh, sc, w_router_t)


def _cumsum_lanes(m, tri_ref):
    e, n = m.shape
    carry = jnp.zeros((e, 1), F32)
    outs = []
    for c in range(n // LANES):
        blk = m[:, c * LANES:(c + 1) * LANES]
        cs = _dot(blk.astype(BF16), tri_ref[...]) + carry
        outs.append(cs)
        carry = cs[:, LANES - 1:LANES]
    return jnp.concatenate(outs, axis=1)


def _select_kernel(a_ref, tri_ref, ind_ref, slot_ref, before_ref, slot_t_ref, *, cap):
    aff = a_ref[0]
    bits = pltpu.bitcast(aff, I32)
    e = aff.shape[0]

    def body(i, thr):
        cand = thr | (1 << (30 - i))
        cnt = jnp.sum((bits >= cand).astype(F32), axis=1, keepdims=True)
        return jnp.where(cnt >= cap, cand, thr)

    thr = lax.fori_loop(0, 31, body, jnp.zeros((e, 1), I32))
    gt = bits > thr
    eq = bits == thr
    n_gt = jnp.sum(gt.astype(F32), axis=1, keepdims=True)
    eq_rank = _cumsum_lanes(eq.astype(F32), tri_ref)
    sel = gt | (eq & (eq_rank <= cap - n_gt))
    self_ = sel.astype(F32)
    pos = _cumsum_lanes(self_, tri_ref) - 1.0
    slot = jnp.where(sel, pos, -1.0)
    slot_ref[0] = slot
    before_ref[0] = _dot(self_.astype(BF16), ind_ref[...])
    slot_t_ref[0] = slot.T


def moe_select(aff, cap):
    b, e, n = aff.shape
    assert n // LANES <= LANES
    tri = jnp.asarray(np.triu(np.ones((LANES, LANES), np.float32)), BF16)
    ind = jnp.asarray(np.arange(n)[:, None] < LANES * np.arange(LANES)[None, :], BF16)
    row = pl.BlockSpec((1, e, n), lambda i: (i, 0, 0))
    col = pl.BlockSpec((1, n, e), lambda i: (i, 0, 0))
    return pl.pallas_call(
        functools.partial(_select_kernel, cap=cap),
        grid=(b,),
        in_specs=[row, pl.BlockSpec((LANES, LANES), lambda i: (0, 0)), pl.BlockSpec((n, LANES), lambda i: (0, 0))],
        out_specs=[row, pl.BlockSpec((1, e, LANES), lambda i: (i, 0, 0)), col],
        out_shape=[jax.ShapeDtypeStruct((b, e, n), F32), jax.ShapeDtypeStruct((b, e, LANES), F32),
                   jax.ShapeDtypeStruct((b, n, e), F32)],
        compiler_params=_cparams("arbitrary"),
        name="moe_select",
    )(aff, tri, ind)


class _Windows:
    def __init__(self, before, n, cap):
        self.tile = min(n, MOE_TILE)
        self.nt = n // self.tile
        self.narrow = min(cap, MOE_WIN)
        self.wide = min(cap, self.tile + BF16_ROWS)
        first = before[:, :, 0:n // LANES:self.tile // LANES].astype(I32)
        count = jnp.concatenate([first[:, :, 1:], jnp.full_like(first[:, :, :1], cap)], axis=2) - first
        aligned = first // BF16_ROWS * BF16_ROWS
        start_narrow = jnp.clip(aligned, 0, cap - self.narrow)
        fits = jnp.all(first - start_narrow + count <= self.narrow, axis=1)
        self.args = (start_narrow.reshape(-1), jnp.clip(aligned, 0, cap - self.wide).reshape(-1),
                     fits.astype(I32).reshape(-1))


def _gather_kernel(wn_ref, ww_ref, fits_ref, slot_ref, h_ref, o_ref, *, ne, narrow, wide):
    bi, gi, ki = pl.program_id(0), pl.program_id(1), pl.program_id(2)
    nt = pl.num_programs(2)
    group, tk = slot_ref.shape[1], slot_ref.shape[2]

    @pl.when(ki == 0)
    def _():
        o_ref[...] = jnp.zeros_like(o_ref)

    slot = slot_ref[0].astype(I32)

    def start(ref, j):
        return pl.multiple_of(ref[(bi * ne + gi * group + j) * nt + ki], BF16_ROWS)

    def onehot(j, ws, width):
        jcol = lax.broadcasted_iota(I32, (width, tk), 0)
        return jnp.where(jcol + ws == slot[j:j + 1, :], 1.0, 0.0).astype(BF16)

    def add(j, ws, width, rows_f32):
        rows = pl.ds(ws, width)
        o_ref[0, j, rows, :] = o_ref[0, j, rows, :] + rows_f32.astype(o_ref.dtype)

    @pl.when(fits_ref[bi * nt + ki] != 0)
    def _():
        starts = [start(wn_ref, j) for j in range(group)]
        lhs = jnp.concatenate([onehot(j, starts[j], narrow) for j in range(group)], axis=0)
        res = _dot(lhs, h_ref[0])
        for j in range(group):
            add(j, starts[j], narrow, res[j * narrow:(j + 1) * narrow])

    @pl.when(fits_ref[bi * nt + ki] == 0)
    def _():
        for j in range(group):
            ws = start(ww_ref, j)
            add(j, ws, wide, _dot(onehot(j, ws, wide), h_ref[0]))


def moe_gather(slot, win, h, cap):
    b, e, n = slot.shape
    dx = h.shape[2]
    group = MOE_GROUP
    grid_spec = pltpu.PrefetchScalarGridSpec(
        num_scalar_prefetch=3,
        grid=(b, e // group, win.nt),
        in_specs=[pl.BlockSpec((1, group, win.tile), lambda i, g, k, *_: (i, g, k)),
                  pl.BlockSpec((1, win.tile, dx), lambda i, g, k, *_: (i, k, 0))],
        out_specs=pl.BlockSpec((1, group, cap, dx), lambda i, g, k, *_: (i, g, 0, 0)))
    return pl.pallas_call(
        functools.partial(_gather_kernel, ne=e, narrow=win.narrow, wide=win.wide),
        grid_spec=grid_spec,
        out_shape=jax.ShapeDtypeStruct((b, e, cap, dx), BF16),
        compiler_params=_cparams("arbitrary", "arbitrary", "arbitrary"),
        name="moe_gather",
    )(*win.args, slot, h)


def _ffn_kernel(*refs, ns):
    x_refs = refs[:ns]
    wg_ref, wu_ref, wd_ref = refs[ns:ns + 3]
    o_refs = refs[ns + 3:2 * ns + 3]
    wgb_ref, wub_ref, wdb_ref = refs[2 * ns + 3:]

    @pl.when(pl.program_id(1) == 0)
    def _():
        wgb_ref[...] = wg_ref[0, 0].astype(BF16)
        wub_ref[...] = wu_ref[0, 0].astype(BF16)
        wdb_ref[...] = wd_ref[0, 0].astype(BF16)

    xe = jnp.concatenate([x_ref[0, 0] for x_ref in x_refs], axis=0)
    d = wgb_ref.shape[0]
    x = xe[:, :d]
    ext = xe[:, d:].astype(F32)
    lane = lax.broadcasted_iota(I32, ext.shape, 1)
    ei, ne = pl.program_id(0), pl.num_programs(0)
    mine = (lane == ei) | (lane == ei + ne) | (lane == ei + 2 * ne)
    gate = jnp.sum(jnp.where(mine, ext, 0.0), axis=1, keepdims=True)
    a = _dot(x, wgb_ref[...])
    u = _dot(x, wub_ref[...])
    hid = (a / (1.0 + jnp.exp(-a))) * u
    y = _dot(hid.astype(BF16), wdb_ref[...]) * gate
    row = 0
    for o_ref in o_refs:
        rows = o_ref.shape[2]
        o_ref[0, 0] = y[row:row + rows].astype(o_ref.dtype)
        row += rows


def moe_ffn(xs_list, w_gate, w_up, w_down, layer):
    b, e, _, dx = xs_list[0].shape
    d, f = w_gate.shape[2], w_gate.shape[3]

    def specs(width):
        return [pl.BlockSpec((1, 1, xs.shape[2], width), lambda j, i: (i, j, 0, 0)) for xs in xs_list]

    return pl.pallas_call(
        functools.partial(_ffn_kernel, ns=len(xs_list)),
        grid=(e, b),
        in_specs=specs(dx) + [pl.BlockSpec((1, 1, d, f), lambda j, i: (layer, j, 0, 0)),
                              pl.BlockSpec((1, 1, d, f), lambda j, i: (layer, j, 0, 0)),
                              pl.BlockSpec((1, 1, f, d), lambda j, i: (layer, j, 0, 0))],
        out_specs=specs(d),
        out_shape=[jax.ShapeDtypeStruct(xs.shape[:3] + (d,), BF16) for xs in xs_list],
        scratch_shapes=[pltpu.VMEM((d, f), BF16), pltpu.VMEM((d, f), BF16), pltpu.VMEM((f, d), BF16)],
        compiler_params=_cparams("arbitrary", "arbitrary"),
        name="moe_ffn",
    )(*xs_list, w_gate, w_up, w_down)


def _combine_kernel(wn_ref, ww_ref, fits_ref, slot_ref, ys_hbm, lat_ref, gt_ref, o_ref, ys_buf, sem,
                    *, ne, narrow, wide):
    bi, ti = pl.program_id(0), pl.program_id(1)
    nb, nt = pl.num_programs(0), pl.num_programs(1)
    cur = bi % 2

    def fetch(batch, buf):
        return pltpu.make_async_copy(ys_hbm.at[batch], ys_buf.at[buf], sem.at[buf])

    @pl.when(ti == 0)
    def _():
        @pl.when(bi == 0)
        def _():
            fetch(0, 0).start()

        fetch(bi, cur).wait()

        @pl.when(bi + 1 < nb)
        def _():
            fetch(bi + 1, 1 - cur).start()

    tn = lat_ref.shape[1]
    slot = slot_ref[0].astype(I32)

    def start(ref, e):
        return pl.multiple_of(ref[(bi * ne + e) * nt + ti], BF16_ROWS)

    def onehot(e, ws, width):
        jrow = lax.broadcasted_iota(I32, (tn, width), 1)
        return jnp.where(jrow + ws == slot[:, e:e + 1], 1.0, 0.0).astype(BF16)

    def finish(acc):
        o_ref[0] = lat_ref[0] + gt_ref[0] * acc

    def one_expert_at_a_time(ref, width):
        acc = jnp.zeros((tn, lat_ref.shape[2]), F32)
        for e in range(ne):
            ws = start(ref, e)
            acc = acc + _dot(onehot(e, ws, width), ys_buf[cur, e, pl.ds(ws, width), :])
        finish(acc)

    if narrow % LANES:
        one_expert_at_a_time(ww_ref, wide)
        return

    @pl.when(fits_ref[bi * nt + ti] != 0)
    def _():
        starts = [start(wn_ref, e) for e in range(ne)]
        lhs = jnp.concatenate([onehot(e, starts[e], narrow) for e in range(ne)], axis=1)
        rhs = jnp.concatenate([ys_buf[cur, e, pl.ds(starts[e], narrow), :] for e in range(ne)], axis=0)
        finish(_dot(lhs, rhs))

    @pl.when(fits_ref[bi * nt + ti] == 0)
    def _():
        one_expert_at_a_time(ww_ref, wide)


def moe_combine(slot_t, win, ys, lat, gate_vec, cap):
    b, n, e = slot_t.shape
    d = lat.shape[2]
    tn = win.tile
    grid_spec = pltpu.PrefetchScalarGridSpec(
        num_scalar_prefetch=3,
        grid=(b, win.nt),
        in_specs=[pl.BlockSpec((1, tn, e), lambda i, j, *_: (i, j, 0)),
                  pl.BlockSpec(memory_space=pl.ANY),
                  pl.BlockSpec((1, tn, d), lambda i, j, *_: (i, j, 0)),
                  pl.BlockSpec((1, 1, d), lambda i, j, *_: (i, 0, 0))],
        out_specs=pl.BlockSpec((1, tn, d), lambda i, j, *_: (i, j, 0)),
        scratch_shapes=[pltpu.VMEM((2, e, cap, d), BF16), pltpu.SemaphoreType.DMA((2,))])
    return pl.pallas_call(
        functools.partial(_combine_kernel, ne=e, narrow=win.narrow, wide=win.wide),
        grid_spec=grid_spec,
        out_shape=jax.ShapeDtypeStruct((b, n, d), F32),
        compiler_params=_cparams("arbitrary", "arbitrary"),
        name="moe_combine",
    )(*win.args, slot_t, ys, lat, gate_vec)


def expert_choice_moe(streams, g, w_router, w_gate, w_up, w_down, layer):
    e = w_router.shape[1]
    routed = []
    for lat, sh, sc, _ in streams:
        cap = EC_CAPACITY_FACTOR * lat.shape[1] // e
        h, aff = moe_router(lat, g, sh, sc, w_router.T)
        slot, before, slot_t = moe_select(aff, cap)
        win = _Windows(before, lat.shape[1], cap)
        routed.append((moe_gather(slot, win, h, cap), slot_t, win, cap))
    ys_list = moe_ffn([r[0] for r in routed], w_gate, w_up, w_down, layer)
    return [moe_combine(slot_t, win, ys, lat, gate_vec, cap)
            for (lat, _, _, gate_vec), (_, slot_t, win, cap), ys in zip(streams, routed, ys_list)]


def kernel(x, c, ctx, c_ctx, ada_w, ada_b, norm_g, sc_w_in, sc_conv, sc_w_out, hy_w_in, hy_conv_w, hy_f_w1,
           hy_f_b1, hy_f_w2, hy_f_b2, hy_f_w3, hy_sin_freq, hy_skip, hy_w_out, at_w_qkv, at_q_g, at_k_g, at_w_o,
           moe_router_w, moe_w_gate, moe_w_up, moe_w_down):
    b, l, d = x.shape
    depth = ada_w.shape[0]
    assert b + 1 <= MOD_ROWS
    qd = at_w_o.shape[1]
    nq = qd // HEAD_DIM
    attn_layers = list(range(MIX_ATTN, depth, N_MIXERS))
    last_attn = max(attn_layers, default=-1)

    cond = jnp.zeros((MOD_ROWS, d), F32).at[:b].set(c).at[b].set(c_ctx)
    mods = ada_modulation(cond, ada_w, ada_b)

    lat, cx = x, ctx
    for i in range(depth):
        kind = i % N_MIXERS
        j = i // N_MIXERS
        need_ctx = i <= last_attn
        upd_ctx = i < last_attn
        m_l = mods[i, :b].reshape(b, 1, 6, d)
        sh1, sc1, g1, sh2, sc2, g2 = (m_l[:, :, k] for k in range(6))
        if need_ctx:
            m_c = jnp.broadcast_to(mods[i, b].reshape(1, 1, 6, d), (b, 1, 6, d))
            csh1, csc1, cg1, csh2, csc2, cg2 = (m_c[:, :, k] for k in range(6))
        streams = [(lat, sh1, sc1, g1)]
        if upd_ctx:
            streams.append((cx, csh1, csc1, cg1))

        outs = []
        if kind == MIX_SHORTCONV:
            w_in, w_out = sc_w_in[j].astype(BF16), sc_w_out[j].astype(BF16)
            for s, sh, sc, gt in streams:
                p = nm_matmul(s, norm_g[i, 0], sh, sc, w_in)
                outs.append(sc_out(p, sc_conv[j], w_out, s, gt))
        elif kind == MIX_HYENA:
            w_in, w_out = hy_w_in[j].astype(BF16), hy_w_out[j].astype(BF16)
            for s, sh, sc, gt in streams:
                p = nm_matmul(s, norm_g[i, 0], sh, sc, w_in)
                z = hyena_mixer(p, hy_conv_w[j], hy_f_w1[j], hy_f_b1[j], hy_f_w2[j], hy_f_b2[j], hy_f_w3[j],
                                hy_sin_freq[j], hy_skip[j])
                outs.append(proj_residual(z, w_out, s, gt))
        else:
            w_qkv, w_o = at_w_qkv[j].astype(BF16), at_w_o[j].astype(BF16)
            q_l, k_l, v_l = qkv_project(lat, norm_g[i, 0], sh1, sc1, w_qkv, at_q_g[j], at_k_g[j],
                                        _rope_tables(l), nq, True)
            lc = cx.shape[1]
            dummy = tuple(jnp.zeros((lc, HEAD_DIM), F32) for _ in range(3))
            if upd_ctx:
                q_c, k_c, v_c = qkv_project(cx, norm_g[i, 0], csh1, csc1, w_qkv, at_q_g[j], at_k_g[j],
                                            dummy, nq, False)
            else:
                k_c, v_c = qkv_project(cx, norm_g[i, 0], csh1, csc1, w_qkv[:, qd:], at_q_g[j], at_k_g[j],
                                       dummy, 0, False)
            v_all = jnp.concatenate([v_c, v_l], axis=1)
            o_l = attention(q_l, jnp.concatenate([k_c, k_l], axis=1), jnp.swapaxes(v_all, 1, 2))
            outs.append(proj_residual(o_l, w_o, lat, g1))
            if upd_ctx:
                outs.append(proj_residual(attention(q_c, k_c, jnp.swapaxes(v_c, 1, 2)), w_o, cx, cg1))

        moe_streams = [(outs[0], sh2, sc2, g2)]
        if upd_ctx:
            moe_streams.append((outs[1], csh2, csc2, cg2))
        moe_out = expert_choice_moe(moe_streams, norm_g[i, 1], moe_router_w[i], moe_w_gate, moe_w_up, moe_w_down, i)
        lat = moe_out[0]
        if upd_ctx:
            cx = moe_out[1]
    return lat
```

```python
import functools
import math

import jax
import jax.numpy as jnp
import numpy as np
from jax import lax
from jax.experimental import pallas as pl
from jax.experimental.pallas import tpu as pltpu

F32 = jnp.float32
BF16 = jnp.bfloat16
I32 = jnp.int32
HIGHEST = lax.Precision.HIGHEST

N_MIXERS = 3
MIX_SHORTCONV, MIX_HYENA, MIX_ATTN = 0, 1, 2
CONV_W = 3
EPS = 1e-6
HY_ORDER = 2
HY_EMB = 33
HY_BANDS = (HY_EMB - 1) // 2
HY_FAST_DECAY = 0.3
HY_SLOW_DECAY = 1.5
HY_TARGET = 1e-2
HEAD_DIM = 128
N_KV_HEADS = 2
GRID_W = 64
ROPE_THETA = 10000.0
EC_CAPACITY_FACTOR = 2
Q_PRESCALE = HEAD_DIM ** -0.5 * math.log2(math.e)

LANES = 128
SUBLANES = 8
BF16_ROWS = 16
VMEM_LIMIT_BYTES = 56 * 1024 * 1024
MOD_ROWS = 16
FFT_N2 = 128
HY_DT = LANES
MOE_TILE = 2 * LANES
MOE_WINDOWS = (LANES // 2, LANES)
MOE_GROUP = SUBLANES
FFN_ROWS = 2 * LANES
DFT_UNROLL = 4
DFT_UNROLL_STAGE1 = 8


def _cparams(*sem):
    return pltpu.CompilerParams(dimension_semantics=sem, vmem_limit_bytes=VMEM_LIMIT_BYTES)


def _row_tile(n, cap=512):
    t = min(n, cap)
    assert n % t == 0
    return t


def _dot(a, b):
    return jnp.dot(a, b, preferred_element_type=F32)


def _ada_kernel(s_ref, w_ref, b_ref, o_ref):
    s = s_ref[...]
    s = s / (1.0 + jnp.exp(-s))
    o_ref[0] = jnp.dot(s, w_ref[0], preferred_element_type=F32, precision=HIGHEST) + b_ref[0]


def ada_modulation(cond, ada_w, ada_b):
    depth, d, n = ada_w.shape
    tn = 1536 if n % 1536 == 0 else n
    return pl.pallas_call(
        _ada_kernel,
        grid=(depth, n // tn),
        in_specs=[pl.BlockSpec((MOD_ROWS, d), lambda i, j: (0, 0)),
                  pl.BlockSpec((1, d, tn), lambda i, j: (i, 0, j)),
                  pl.BlockSpec((1, 1, tn), lambda i, j: (i, 0, j))],
        out_specs=pl.BlockSpec((1, MOD_ROWS, tn), lambda i, j: (i, 0, j)),
        out_shape=jax.ShapeDtypeStruct((depth, MOD_ROWS, n), F32),
        compiler_params=_cparams("arbitrary", "arbitrary"),
        name="ada_modulation",
    )(cond, ada_w, ada_b.reshape(depth, 1, n))


def _norm_mod(x, g, sh, sc):
    y = x * lax.rsqrt(jnp.mean(x * x, axis=-1, keepdims=True) + EPS)
    return (y * g) * (1.0 + sc) + sh


def _nm_matmul_kernel(x_ref, g_ref, sh_ref, sc_ref, w_ref, o_ref):
    h = _norm_mod(x_ref[0], g_ref[...], sh_ref[0], sc_ref[0]).astype(BF16)
    o_ref[0] = _dot(h, w_ref[...]).astype(o_ref.dtype)


def nm_matmul(x, g, sh, sc, w, out_dtype=BF16):
    b, l, d = x.shape
    n = w.shape[1]
    tm = _row_tile(l)
    return pl.pallas_call(
        _nm_matmul_kernel,
        grid=(b, l // tm),
        in_specs=[pl.BlockSpec((1, tm, d), lambda i, j: (i, j, 0)),
                  pl.BlockSpec((1, d), lambda i, j: (0, 0)),
                  pl.BlockSpec((1, 1, d), lambda i, j: (i, 0, 0)),
                  pl.BlockSpec((1, 1, d), lambda i, j: (i, 0, 0)),
                  pl.BlockSpec((d, n), lambda i, j: (0, 0))],
        out_specs=pl.BlockSpec((1, tm, n), lambda i, j: (i, j, 0)),
        out_shape=jax.ShapeDtypeStruct((b, l, n), out_dtype),
        compiler_params=_cparams("arbitrary", "arbitrary"),
        name="nm_matmul",
    )(x, g.reshape(1, d), sh, sc, w)


def _proj_res_kernel(a_ref, w_ref, lat_ref, gt_ref, o_ref):
    o_ref[0] = lat_ref[0] + gt_ref[0] * _dot(a_ref[0], w_ref[...])


def proj_residual(a, w, lat, gate):
    b, l, k = a.shape
    d = w.shape[1]
    tm = _row_tile(l)
    return pl.pallas_call(
        _proj_res_kernel,
        grid=(b, l // tm),
        in_specs=[pl.BlockSpec((1, tm, k), lambda i, j: (i, j, 0)),
                  pl.BlockSpec((k, d), lambda i, j: (0, 0)),
                  pl.BlockSpec((1, tm, d), lambda i, j: (i, j, 0)),
                  pl.BlockSpec((1, 1, d), lambda i, j: (i, 0, 0))],
        out_specs=pl.BlockSpec((1, tm, d), lambda i, j: (i, j, 0)),
        out_shape=jax.ShapeDtypeStruct((b, l, d), F32),
        compiler_params=_cparams("arbitrary", "arbitrary"),
        name="proj_residual",
    )(a, w, lat, gate)


def _sc_out_kernel(pb_ref, pc_ref, pv_ref, cp_ref, vp_ref, cn_ref, vn_ref, cw_ref, w_ref, lat_ref, gt_ref,
                   o_ref, u_ref, *, tm):
    j = pl.program_id(1)
    nj = pl.num_programs(1)
    d = u_ref.shape[1]
    u_ref[pl.ds(SUBLANES, tm), :] = pc_ref[0].astype(F32) * pv_ref[0].astype(F32)
    prev = cp_ref[0, BF16_ROWS - 1:BF16_ROWS, :].astype(F32) * vp_ref[0, BF16_ROWS - 1:BF16_ROWS, :].astype(F32)
    nxt = cn_ref[0, 0:1, :].astype(F32) * vn_ref[0, 0:1, :].astype(F32)
    prev = jnp.where(j == 0, 0.0, prev)
    nxt = jnp.where(j == nj - 1, 0.0, nxt)
    u_ref[pl.ds(0, SUBLANES), :] = jnp.broadcast_to(prev, (SUBLANES, d))
    u_ref[pl.ds(SUBLANES + tm, SUBLANES), :] = jnp.broadcast_to(nxt, (SUBLANES, d))
    cw = cw_ref[...]
    y = (u_ref[pl.ds(SUBLANES - 1, tm), :] * cw[0:1] + u_ref[pl.ds(SUBLANES, tm), :] * cw[1:2]
         + u_ref[pl.ds(SUBLANES + 1, tm), :] * cw[2:3])
    m = (pb_ref[0].astype(F32) * y).astype(BF16)
    o_ref[0] = lat_ref[0] + gt_ref[0] * _dot(m, w_ref[...])


def sc_out(p, conv_w, w_out, lat, gate):
    b, l, d3 = p.shape
    d = d3 // 3
    tm = _row_tile(l)
    hb = tm // BF16_ROWS
    nhb = l // BF16_ROWS

    def prev_map(col):
        return lambda i, j: (i, jnp.maximum(j * hb - 1, 0), col)

    def next_map(col):
        return lambda i, j: (i, jnp.minimum((j + 1) * hb, nhb - 1), col)

    return pl.pallas_call(
        functools.partial(_sc_out_kernel, tm=tm),
        grid=(b, l // tm),
        in_specs=[pl.BlockSpec((1, tm, d), lambda i, j: (i, j, 0)),
                  pl.BlockSpec((1, tm, d), lambda i, j: (i, j, 1)),
                  pl.BlockSpec((1, tm, d), lambda i, j: (i, j, 2)),
                  pl.BlockSpec((1, BF16_ROWS, d), prev_map(1)),
                  pl.BlockSpec((1, BF16_ROWS, d), prev_map(2)),
                  pl.BlockSpec((1, BF16_ROWS, d), next_map(1)),
                  pl.BlockSpec((1, BF16_ROWS, d), next_map(2)),
                  pl.BlockSpec((CONV_W, d), lambda i, j: (0, 0)),
                  pl.BlockSpec((d, d), lambda i, j: (0, 0)),
                  pl.BlockSpec((1, tm, d), lambda i, j: (i, j, 0)),
                  pl.BlockSpec((1, 1, d), lambda i, j: (i, 0, 0))],
        out_specs=pl.BlockSpec((1, tm, d), lambda i, j: (i, j, 0)),
        out_shape=jax.ShapeDtypeStruct((b, l, d), F32),
        scratch_shapes=[pltpu.VMEM((tm + 2 * SUBLANES, d), F32)],
        compiler_params=_cparams("arbitrary", "arbitrary"),
        name="sc_out",
    )(p, p, p, p, p, p, p, conv_w, w_out, lat, gate)


def _hy_filter_kernel(z_ref, w1_ref, b1_ref, w2_ref, b2_ref, fr_ref, w3f_ref, w3b_ref, t_ref, dl_ref,
                      of_ref, ob_ref, a_ref):
    @pl.when((pl.program_id(0) == 0) & (pl.program_id(1) == 0))
    def _():
        fr = fr_ref[...]
        a1 = jnp.sin(fr * (jnp.dot(z_ref[...], w1_ref[...], preferred_element_type=F32, precision=HIGHEST)
                           + b1_ref[...]))
        a_ref[...] = jnp.sin(fr * (jnp.dot(a1, w2_ref[...], preferred_element_type=F32, precision=HIGHEST)
                                   + b2_ref[...]))

    a = a_ref[...]
    decay = jnp.exp(-t_ref[...] * dl_ref[...])
    hf = jnp.dot(a, w3f_ref[...], preferred_element_type=F32, precision=HIGHEST) * decay
    hb = jnp.dot(a, w3b_ref[...], preferred_element_type=F32, precision=HIGHEST) * decay
    row = lax.broadcasted_iota(I32, hb.shape, 0)
    hb = jnp.where(row == 0, 0.0, hb)
    inv = 1.0 / (jnp.sum(jnp.abs(hf), axis=0, keepdims=True) + jnp.sum(jnp.abs(hb), axis=0, keepdims=True))
    of_ref[0] = hf * inv
    ob_ref[0] = hb * inv


def hy_filter_taps(l, d, f_w1, f_b1, f_w2, f_b2, f_w3, sin_freq):
    hid = f_w1.shape[1]
    hp = LANES
    t = np.linspace(0.0, 1.0, l)[:, None]
    bands = np.linspace(1e-4, HY_BANDS - 1, HY_BANDS)[None, :]
    ang = (2.0 * math.pi / l) * np.arange(l)[:, None] * bands
    z = np.zeros((l, hp), np.float64)
    z[:, :HY_EMB] = np.concatenate([t, np.cos(ang), -np.sin(ang)], axis=-1)
    deltas = np.abs(np.linspace(math.log(HY_TARGET) / HY_SLOW_DECAY, math.log(HY_TARGET) / HY_FAST_DECAY, d))[None, :]

    def pad2(a, r, c):
        return jnp.zeros((r, c), F32).at[:a.shape[0], :a.shape[1]].set(a.astype(F32))

    w1 = pad2(f_w1, hp, hp)
    w2 = pad2(f_w2, hp, hp)
    w3 = pad2(f_w3, hp, f_w3.shape[1])
    b1 = pad2(f_b1[None], 1, hp)
    b2 = pad2(f_b2[None], 1, hp)
    fr = pad2(sin_freq[None], 1, hp)
    tc = min(d, 2 * LANES)
    nd = d // tc
    full = lambda o, j: (0, 0)
    return pl.pallas_call(
        _hy_filter_kernel,
        grid=(HY_ORDER, nd),
        in_specs=[pl.BlockSpec((l, hp), full), pl.BlockSpec((hp, hp), full), pl.BlockSpec((1, hp), full),
                  pl.BlockSpec((hp, hp), full), pl.BlockSpec((1, hp), full), pl.BlockSpec((1, hp), full),
                  pl.BlockSpec((hp, tc), lambda o, j: (0, o * nd + j)),
                  pl.BlockSpec((hp, tc), lambda o, j: (0, (HY_ORDER + o) * nd + j)),
                  pl.BlockSpec((l, 1), full),
                  pl.BlockSpec((1, tc), lambda o, j: (0, j))],
        out_specs=[pl.BlockSpec((1, l, tc), lambda o, j: (o, 0, j)),
                   pl.BlockSpec((1, l, tc), lambda o, j: (o, 0, j))],
        out_shape=[jax.ShapeDtypeStruct((HY_ORDER, l, d), F32)] * 2,
        scratch_shapes=[pltpu.VMEM((l, hp), F32)],
        compiler_params=_cparams("arbitrary", "arbitrary"),
        name="hy_filter_taps",
    )(jnp.asarray(z, F32), w1, b1, w2, b2, fr, w3, w3, jnp.asarray(t, F32), jnp.asarray(deltas, F32))


def _dft_consts(l):
    n = 2 * l
    if n <= 4 * FFT_N2:
        n1, n2 = 1, n
    else:
        n1, n2 = n // FFT_N2, FFT_N2
    if n1 == 1:
        k = np.arange(n)[:, None]
        t = np.arange(l)[None, :]
        ang = 2.0 * np.pi * (k * t % n) / n
        c, s = np.cos(ang), np.sin(ang)
        fwd = np.block([[c, s], [-s, c]])
        inv = np.block([[c.T, -s.T], [s.T, c.T]]) / n
        return n1, n2, dict(fwd=jnp.asarray(fwd, BF16), inv=jnp.asarray(inv, BF16))
    n1h = n1 // 2
    k1 = np.arange(n1)[None, :, None]
    m1 = np.arange(n1h)[None, None, :]
    r2 = np.arange(n2)[:, None, None]
    ang = 2.0 * np.pi * ((k1 * (n2 * m1 + r2)) % n) / n
    c, s = np.cos(ang), np.sin(ang)
    m1f = np.concatenate([np.concatenate([c, s], 2), np.concatenate([-s, c], 2)], 1)
    ct, st = np.swapaxes(c, 1, 2) / n, np.swapaxes(s, 1, 2) / n
    m1i = np.concatenate([np.concatenate([ct, -st], 2), np.concatenate([st, ct], 2)], 1)
    a2 = 2.0 * np.pi * ((np.arange(n2)[:, None] * np.arange(n2)[None, :]) % n2) / n2
    c2, s2 = np.cos(a2), np.sin(a2)
    f2 = np.block([[c2, s2], [-s2, c2]])
    f2i = np.block([[c2, -s2], [s2, c2]])

    def il(m):
        return np.stack([np.arange(m), m + np.arange(m)], 1).reshape(-1)

    m1f = m1f[:, il(n1), :][:, :, il(n1h)]
    m1i = m1i[:, il(n1h), :][:, :, il(n1)]
    f2 = f2[:, il(n2)]
    f2i = f2i[il(n2), :][:, il(n2)]
    half = n2 // 2
    m1f = np.concatenate([m1f[:half], m1f[half:]], 2)
    m1i = np.concatenate([m1i[:half], m1i[half:]], 2)
    return n1, n2, dict(m1f=jnp.asarray(m1f, BF16), m1i=jnp.asarray(m1i, BF16),
                        f2=jnp.asarray(f2, BF16), f2i=jnp.asarray(f2i, BF16))


def _blockdiag(a, b):
    z = jnp.zeros_like(a)
    return jnp.concatenate([jnp.concatenate([a, z], axis=1), jnp.concatenate([z, b], axis=1)], axis=0)


def _halves(y):
    return y[:, :LANES], y[:, LANES:]


HI_HALF = -65536


def _pack_pair(re, im):
    lo = (pltpu.bitcast(re.astype(BF16).astype(F32), I32) >> 16) & 0xFFFF
    if im is None:
        return lo
    return lo | (pltpu.bitcast(im.astype(BF16).astype(F32), I32) & HI_HALF)


def _unpack_pair(w):
    return pltpu.bitcast(w << 16, F32), pltpu.bitcast(w & HI_HALF, F32)


def _as_rows(w):
    return pltpu.bitcast(w, BF16)


def _as_words(y):
    return pltpu.bitcast(y.astype(BF16), I32)


def _fwd_stage1(x_pk, a_pk, m1f_ref, n1, n2):
    n1h, half = n1 // 2, n2 // 2

    def column(r):
        return _as_rows(x_pk[pl.ds(r, n1h, stride=n2), :])

    def body(r, carry):
        w = _as_words(_dot(m1f_ref[r], _blockdiag(column(r), column(r + half))))
        for rr, ww in zip((r, r + half), _halves(w)):
            a_pk[pl.ds(pl.multiple_of(rr * n1, n1), n1), :] = ww
        return carry

    lax.fori_loop(0, half, body, 0, unroll=min(half, DFT_UNROLL_STAGE1))


def _inv_stage1(b_pk, x_pk, m1i_ref, n1, n2):
    n1h, half = n1 // 2, n2 // 2

    def column(r):
        return _as_rows(b_pk[pl.ds(pl.multiple_of(r * n1, n1), n1), :])

    def body(r, carry):
        w = _as_words(_dot(m1i_ref[r], _blockdiag(column(r), column(r + half))))
        for rr, ww in zip((r, r + half), _halves(w)):
            x_pk[pl.ds(rr, n1h, stride=n2), :] = ww
        return carry

    lax.fori_loop(0, half, body, 0, unroll=min(half, DFT_UNROLL_STAGE1))


def _load_chunk(a_pk, k1, n1, n2):
    ra = pl.ds(k1, n2, stride=n1)
    rb = pl.ds(k1 + n1 // 2, n2, stride=n1)
    return jnp.concatenate([_as_rows(a_pk[ra, :]), _as_rows(a_pk[rb, :])], axis=1)


def _hy_spec_kernel(*refs, l, n1, n2, dense):
    if dense:
        tf_ref, tb_ref, sk_ref, fwd_ref, o_ref = refs
        n = n2
        fw = fwd_ref[:, 0:l]
        sf = _dot(fw, tf_ref[0].astype(BF16))
        sb = _dot(fw, tb_ref[0].astype(BF16))
        o_ref[0, 0] = (sf[:n] + sb[:n] + sk_ref[0]).astype(o_ref.dtype)
        o_ref[0, 1] = (sf[n:] - sb[n:]).astype(o_ref.dtype)
        return
    tf_ref, tb_ref, sk_ref, m1f_ref, f2_ref, o_ref, x_pk, a_pk, h_re, h_im = refs
    for run, t_ref in enumerate((tf_ref, tb_ref)):
        x_pk[...] = _pack_pair(t_ref[0], None)
        _fwd_stage1(x_pk, a_pk, m1f_ref, n1, n2)

        def body(k1, carry, run=run):
            x = _dot(f2_ref[...], _load_chunk(a_pk, k1, n1, n2))
            for kk, xx in zip((k1, k1 + n1 // 2), _halves(x)):
                rows = pl.ds(pl.multiple_of(kk * n2, n2), n2)
                if run == 0:
                    h_re[rows, :] = xx[:n2]
                    h_im[rows, :] = xx[n2:]
                else:
                    h_re[rows, :] = h_re[rows, :] + xx[:n2]
                    h_im[rows, :] = h_im[rows, :] - xx[n2:]
            return carry

        lax.fori_loop(0, n1 // 2, body, 0, unroll=DFT_UNROLL)
    o_ref[0, 0] = (h_re[...] + sk_ref[0]).astype(o_ref.dtype)
    o_ref[0, 1] = h_im[...].astype(o_ref.dtype)


def hy_spectrum(taps_f, taps_b, skip, consts, n1, n2):
    o, l, d = taps_f.shape
    n = 2 * l
    dt = HY_DT
    dense = n1 == 1
    tap_spec = pl.BlockSpec((1, l, dt), lambda i, j: (i, 0, j))
    in_specs = [tap_spec, tap_spec, pl.BlockSpec((1, 1, dt), lambda i, j: (i, 0, j))]
    if dense:
        mats = [consts["fwd"]]
        scratch = []
    else:
        mats = [consts["m1f"], consts["f2"]]
        scratch = [pltpu.VMEM((l, dt), I32), pltpu.VMEM((n, dt), I32)] + [pltpu.VMEM((n, dt), F32)] * 2
    in_specs += [pl.BlockSpec(m.shape, lambda i, j, nd=m.ndim: (0,) * nd) for m in mats]
    return pl.pallas_call(
        functools.partial(_hy_spec_kernel, l=l, n1=n1, n2=n2, dense=dense),
        grid=(o, d // dt),
        in_specs=in_specs,
        out_specs=pl.BlockSpec((1, 2, n, dt), lambda i, j: (i, 0, 0, j)),
        out_shape=jax.ShapeDtypeStruct((o, 2, n, d), BF16),
        scratch_shapes=scratch,
        compiler_params=_cparams("arbitrary", "arbitrary"),
        name="hy_spectrum",
    )(taps_f, taps_b, skip.reshape(o, 1, d), *mats)


def _conv3_rows(src, cw, pad_ref, l):
    c = pad_ref.shape[1]
    pad_ref[pl.ds(0, SUBLANES), :] = jnp.zeros((SUBLANES, c), F32)
    pad_ref[pl.ds(SUBLANES + l, SUBLANES), :] = jnp.zeros((SUBLANES, c), F32)
    pad_ref[pl.ds(SUBLANES, l), :] = src
    return (pad_ref[pl.ds(SUBLANES - 1, l), :] * cw[0:1] + pad_ref[pl.ds(SUBLANES, l), :] * cw[1:2]
            + pad_ref[pl.ds(SUBLANES + 1, l), :] * cw[2:3])


def _hy_conv_kernel(*refs, l, n1, n2, dense, conv_z):
    if dense:
        z_ref, g_ref, cwz_ref, cwg_ref, h_ref, fwd_ref, inv_ref, o_ref, pad_ref = refs
    else:
        (z_ref, g_ref, cwz_ref, cwg_ref, h_ref, m1f_ref, m1i_ref, f2_ref, f2i_ref, o_ref,
         x_pk, a_pk, b_pk, pad_ref) = refs
    zs = []
    for e in range(2):
        z = z_ref[e, 0].astype(F32)
        zs.append(_conv3_rows(z, cwz_ref[...], pad_ref, l) if conv_z else z)
    if dense:
        n = n2
        s = _dot(fwd_ref[...], jnp.concatenate(zs, axis=0).astype(BF16))
        hr = h_ref[0, 0].astype(F32)
        hi = h_ref[0, 1].astype(F32)
        zr = s[:n] * hr - s[n:] * hi
        zi = s[:n] * hi + s[n:] * hr
        y = _dot(inv_ref[...], jnp.concatenate([zr, zi], axis=0).astype(BF16))
        ys = (y[:l], y[l:])
    else:
        x_pk[...] = _pack_pair(zs[0], zs[1])
        _fwd_stage1(x_pk, a_pk, m1f_ref, n1, n2)

        nk = n1 // 2

        def spectrum_step(k1, carry):
            s = _dot(f2_ref[...], _load_chunk(a_pk, k1, n1, n2))
            rows = [pl.ds(pl.multiple_of(k * n2, n2), n2) for k in (k1, k1 + nk)]
            hr = jnp.concatenate([h_ref[0, 0, r, :] for r in rows], axis=1).astype(F32)
            hi = jnp.concatenate([h_ref[0, 1, r, :] for r in rows], axis=1).astype(F32)
            zr = s[:n2] * hr - s[n2:] * hi
            zi = s[:n2] * hi + s[n2:] * hr
            for r, zr_half, zi_half in zip(rows, _halves(zr), _halves(zi)):
                b_pk[r, :] = _pack_pair(zr_half, zi_half)
            return carry

        lax.fori_loop(0, nk, spectrum_step, 0, unroll=min(nk, DFT_UNROLL))

        def inverse_step(k1, carry):
            kk = (k1, k1 + nk)
            z = jnp.concatenate([_as_rows(b_pk[pl.ds(pl.multiple_of(k * n2, n2), n2), :]) for k in kk], axis=1)
            w = _as_words(_dot(f2i_ref[...], z))
            for k, ww in zip(kk, _halves(w)):
                a_pk[pl.ds(k, n2, stride=n1), :] = ww
            return carry

        lax.fori_loop(0, nk, inverse_step, 0, unroll=min(nk, DFT_UNROLL))
        _inv_stage1(a_pk, x_pk, m1i_ref, n1, n2)
        ys = _unpack_pair(x_pk[...])
    for e in range(2):
        g = _conv3_rows(g_ref[e, 0].astype(F32), cwg_ref[...], pad_ref, l)
        o_ref[e, 0] = (g * ys[e]).astype(o_ref.dtype)


def hy_conv(zsrc, zcol, p4, gcol, conv_w, spec, order, consts, n1, n2, conv_z):
    _, pairs, l, _ = zsrc.shape
    d = spec.shape[3]
    n = 2 * l
    dt = HY_DT
    nd = d // dt
    dense = n1 == 1
    once = pl.Buffered(1)
    in_specs = [pl.BlockSpec((2, 1, l, dt), lambda j, i: (0, i, 0, zcol * nd + j)),
                pl.BlockSpec((2, 1, l, dt), lambda j, i: (0, i, 0, gcol * nd + j)),
                pl.BlockSpec((CONV_W, dt), lambda j, i: (0, j)),
                pl.BlockSpec((CONV_W, dt), lambda j, i: (0, gcol * nd + j)),
                pl.BlockSpec((1, 2, n, dt), lambda j, i: (order, 0, 0, j), pipeline_mode=once)]
    if dense:
        mats = [consts["fwd"], consts["inv"]]
        scratch = []
    else:
        mats = [consts["m1f"], consts["m1i"], consts["f2"], consts["f2i"]]
        scratch = [pltpu.VMEM((l, dt), I32)] + [pltpu.VMEM((n, dt), I32)] * 2
    scratch = scratch + [pltpu.VMEM((l + 2 * SUBLANES, dt), F32)]
    in_specs += [pl.BlockSpec(m.shape, lambda j, i, nd_=m.ndim: (0,) * nd_, pipeline_mode=once) for m in mats]
    return pl.pallas_call(
        functools.partial(_hy_conv_kernel, l=l, n1=n1, n2=n2, dense=dense, conv_z=conv_z),
        grid=(nd, pairs),
        in_specs=in_specs,
        out_specs=pl.BlockSpec((2, 1, l, dt), lambda j, i: (0, i, 0, j)),
        out_shape=jax.ShapeDtypeStruct((2, pairs, l, d), BF16),
        scratch_shapes=scratch,
        compiler_params=_cparams("arbitrary", "arbitrary"),
        name="hy_conv",
    )(zsrc, p4, conv_w, conv_w, spec, *mats)


def hyena_mixer(p, conv_w, f_w1, f_b1, f_w2, f_b2, f_w3, sin_freq, skip):
    b, l, d3 = p.shape
    d = d3 // 3
    assert b % 2 == 0 and d % HY_DT == 0
    n1, n2, consts = _dft_consts(l)
    taps_f, taps_b = hy_filter_taps(l, d, f_w1, f_b1, f_w2, f_b2, f_w3, sin_freq)
    spec = hy_spectrum(taps_f, taps_b, skip, consts, n1, n2)
    p4 = p.reshape(2, b // 2, l, d3)
    z = hy_conv(p4, 0, p4, 1, conv_w, spec, 0, consts, n1, n2, True)
    z = hy_conv(z, 0, p4, 2, conv_w, spec, 1, consts, n1, n2, False)
    return z.reshape(b, l, d)


def _head_norm(x, g):
    return x * lax.rsqrt(jnp.mean(x * x, axis=-1, keepdims=True) + EPS) * g


def _rope(x, cos, sin_a, sin_b):
    return (x * cos + pltpu.roll(x, HEAD_DIM // 4, 1) * sin_a
            + pltpu.roll(x, HEAD_DIM - HEAD_DIM // 4, 1) * sin_b)


def _qkv_kernel(x_ref, g_ref, sh_ref, sc_ref, w_ref, qg_ref, kg_ref, cos_ref, sa_ref, sb_ref,
                *out_refs, nq, rope):
    q_ref = out_refs[0] if nq else None
    k_ref, v_ref = out_refs[-2:]
    h = _norm_mod(x_ref[0], g_ref[...], sh_ref[0], sc_ref[0]).astype(BF16)
    acc = _dot(h, w_ref[...])
    hd = HEAD_DIM
    for i in range(nq + N_KV_HEADS):
        t = acc[:, i * hd:(i + 1) * hd]
        t = _head_norm(t, qg_ref[...] if i < nq else kg_ref[...])
        if rope:
            t = _rope(t, cos_ref[...], sa_ref[...], sb_ref[...])
        if i < nq:
            q_ref[0, :, i * hd:(i + 1) * hd] = (t * Q_PRESCALE).astype(BF16)
        else:
            k_ref[0, :, (i - nq) * hd:(i - nq + 1) * hd] = t.astype(BF16)
    v_ref[0] = acc[:, (nq + N_KV_HEADS) * hd:].astype(BF16)


def qkv_project(x, g, sh, sc, w, q_g, k_g, rope_tabs, nq, rope):
    b, l, d = x.shape
    n = w.shape[1]
    hd = HEAD_DIM
    kvd = N_KV_HEADS * hd
    tm = _row_tile(l)
    cos, sa, sb = rope_tabs
    tab = pl.BlockSpec((tm, hd), lambda i, j: (j, 0))
    vec = pl.BlockSpec((1, 1, d), lambda i, j: (i, 0, 0))
    widths = ([nq * hd] if nq else []) + [kvd, kvd]
    return pl.pallas_call(
        functools.partial(_qkv_kernel, nq=nq, rope=rope),
        grid=(b, l // tm),
        in_specs=[pl.BlockSpec((1, tm, d), lambda i, j: (i, j, 0)),
                  pl.BlockSpec((1, d), lambda i, j: (0, 0)), vec, vec,
                  pl.BlockSpec((d, n), lambda i, j: (0, 0)),
                  pl.BlockSpec((1, hd), lambda i, j: (0, 0)),
                  pl.BlockSpec((1, hd), lambda i, j: (0, 0)),
                  tab, tab, tab],
        out_specs=[pl.BlockSpec((1, tm, wd), lambda i, j: (i, j, 0)) for wd in widths],
        out_shape=[jax.ShapeDtypeStruct((b, l, wd), BF16) for wd in widths],
        compiler_params=_cparams("arbitrary", "arbitrary"),
        name="qkv_project",
    )(x, g.reshape(1, d), sh, sc, w, q_g.reshape(1, hd), k_g.reshape(1, hd), cos, sa, sb)


def _rope_tables(l):
    rows = l // GRID_W
    row = np.repeat(np.arange(rows, dtype=np.float64), GRID_W)
    col = np.tile(np.arange(GRID_W, dtype=np.float64), rows)
    axis = HEAD_DIM // 2
    inv = ROPE_THETA ** (-np.arange(0, axis, 2, dtype=np.float64) / axis)

    def axis_angles(pos):
        a = pos[:, None] * inv[None, :]
        return np.concatenate([a, a], axis=-1)

    ang = np.concatenate([axis_angles(row), axis_angles(col)], axis=-1)
    cos, sin = np.cos(ang), np.sin(ang)
    lane = np.arange(HEAD_DIM)[None, :] % axis
    upper = lane >= axis // 2
    sin_a = np.where(upper, sin, 0.0)
    sin_b = np.where(upper, 0.0, -sin)
    return tuple(jnp.asarray(t, F32) for t in (cos, sin_a, sin_b))


def _attn_kernel(q_ref, k_ref, vt_ref, o_ref, *, groups):
    k = k_ref[0]
    vt = vt_ref[0]
    hd = HEAD_DIM
    tq = q_ref.shape[1]
    hp = 2 if groups % 2 == 0 else 1
    def scores(g0):
        q = jnp.concatenate([q_ref[0, :, g * hd:(g + 1) * hd] for g in range(g0, g0 + hp)], axis=0)
        return lax.dot_general(k, q, (((1,), (1,)), ((), ())), preferred_element_type=F32)

    sts = [scores(g0) for g0 in range(0, groups, hp)]
    for g0, st in zip(range(0, groups, hp), sts):
        m = jnp.max(st, axis=0, keepdims=True)
        pt = jnp.exp2(st - m)
        den = jnp.sum(pt, axis=0, keepdims=True)
        ot = _dot(vt, pt.astype(BF16)) / den
        for i in range(hp):
            g = g0 + i
            o_ref[0, :, g * hd:(g + 1) * hd] = ot[:, i * tq:(i + 1) * tq].T.astype(o_ref.dtype)


def attention(q, k, vt):
    b, l, qd = q.shape
    s = k.shape[1]
    hd = HEAD_DIM
    groups = qd // hd // N_KV_HEADS
    tq = _row_tile(l, 256)
    return pl.pallas_call(
        functools.partial(_attn_kernel, groups=groups),
        grid=(b, N_KV_HEADS, l // tq),
        in_specs=[pl.BlockSpec((1, tq, groups * hd), lambda i, h, j: (i, j, h)),
                  pl.BlockSpec((1, s, hd), lambda i, h, j: (i, 0, h)),
                  pl.BlockSpec((1, hd, s), lambda i, h, j: (i, h, 0))],
        out_specs=pl.BlockSpec((1, tq, groups * hd), lambda i, h, j: (i, j, h)),
        out_shape=jax.ShapeDtypeStruct((b, l, qd), BF16),
        compiler_params=_cparams("arbitrary", "arbitrary", "arbitrary"),
        name="attention",
    )(q, k, vt)


def _router_kernel(x_ref, g_ref, sh_ref, sc_ref, wr_ref, h_ref, a_ref):
    h = _norm_mod(x_ref[0], g_ref[...], sh_ref[0], sc_ref[0])
    d = h.shape[1]
    logits = lax.dot_general(wr_ref[...], h, (((1,), (1,)), ((), ())), preferred_element_type=F32,
                             precision=HIGHEST)
    ex = jnp.exp(logits - jnp.max(logits, axis=0, keepdims=True))
    aff = ex / jnp.sum(ex, axis=0, keepdims=True)
    a_ref[0] = aff
    ne, tm = aff.shape
    hi = aff.astype(BF16).astype(F32)
    mid = (aff - hi).astype(BF16).astype(F32)
    lo = (aff - hi - mid).astype(BF16).astype(F32)
    pieces = jnp.concatenate([hi, mid, lo, jnp.zeros((LANES - 3 * ne, tm), F32)], axis=0)
    h_ref[0, :, 0:d] = h.astype(BF16)
    h_ref[0, :, d:d + LANES] = pieces.T.astype(BF16)


def moe_router(x, g, sh, sc, w_router_t):
    b, n, d = x.shape
    e = w_router_t.shape[0]
    assert 3 * e <= LANES and e % SUBLANES == 0
    tm = _row_tile(n)
    vec = pl.BlockSpec((1, 1, d), lambda i, j: (i, 0, 0))
    return pl.pallas_call(
        _router_kernel,
        grid=(b, n // tm),
        in_specs=[pl.BlockSpec((1, tm, d), lambda i, j: (i, j, 0)),
                  pl.BlockSpec((1, d), lambda i, j: (0, 0)), vec, vec,
                  pl.BlockSpec((e, d), lambda i, j: (0, 0))],
        out_specs=[pl.BlockSpec((1, tm, d + LANES), lambda i, j: (i, j, 0)),
                   pl.BlockSpec((1, e, tm), lambda i, j: (i, 0, j))],
        out_shape=[jax.ShapeDtypeStruct((b, n, d + LANES), BF16), jax.ShapeDtypeStruct((b, e, n), F32)],
        compiler_params=_cparams("arbitrary", "arbitrary"),
        name="moe_router",
    )(x, g.reshape(1, d), sh, sc, w_router_t)


def _cumsum_lanes(m, tri_ref):
    e, n = m.shape
    carry = jnp.zeros((e, 1), F32)
    outs = []
    for c in range(n // LANES):
        blk = m[:, c * LANES:(c + 1) * LANES]
        cs = _dot(blk.astype(BF16), tri_ref[...]) + carry
        outs.append(cs)
        carry = cs[:, LANES - 1:LANES]
    return jnp.concatenate(outs, axis=1)


def _select_kernel(a_ref, tri_ref, ind_ref, slot_ref, before_ref, slot_t_ref, *, cap):
    aff = a_ref[0]
    bits = pltpu.bitcast(aff, I32)
    e = aff.shape[0]

    def body(i, thr):
        cand = thr | (1 << (30 - i))
        cnt = jnp.sum((bits >= cand).astype(F32), axis=1, keepdims=True)
        return jnp.where(cnt >= cap, cand, thr)

    thr = lax.fori_loop(0, 31, body, jnp.zeros((e, 1), I32))
    gt = bits > thr
    eq = bits == thr
    n_gt = jnp.sum(gt.astype(F32), axis=1, keepdims=True)
    eq_rank = _cumsum_lanes(eq.astype(F32), tri_ref)
    sel = gt | (eq & (eq_rank <= cap - n_gt))
    self_ = sel.astype(F32)
    pos = _cumsum_lanes(self_, tri_ref) - 1.0
    slot = jnp.where(sel, pos, -1.0)
    slot_ref[0] = slot
    before_ref[0] = _dot(self_.astype(BF16), ind_ref[...])
    slot_t_ref[0] = slot.T


def moe_select(aff, cap):
    b, e, n = aff.shape
    assert n // LANES <= LANES
    tri = jnp.asarray(np.triu(np.ones((LANES, LANES), np.float32)), BF16)
    ind = jnp.asarray(np.arange(n)[:, None] < LANES * np.arange(LANES)[None, :], BF16)
    row = pl.BlockSpec((1, e, n), lambda i: (i, 0, 0))
    col = pl.BlockSpec((1, n, e), lambda i: (i, 0, 0))
    return pl.pallas_call(
        functools.partial(_select_kernel, cap=cap),
        grid=(b,),
        in_specs=[row, pl.BlockSpec((LANES, LANES), lambda i: (0, 0)), pl.BlockSpec((n, LANES), lambda i: (0, 0))],
        out_specs=[row, pl.BlockSpec((1, e, LANES), lambda i: (i, 0, 0)), col],
        out_shape=[jax.ShapeDtypeStruct((b, e, n), F32), jax.ShapeDtypeStruct((b, e, LANES), F32),
                   jax.ShapeDtypeStruct((b, n, e), F32)],
        compiler_params=_cparams("arbitrary"),
        name="moe_select",
    )(aff, tri, ind)


class _Windows:
    def __init__(self, before, n, cap):
        self.tile = min(n, MOE_TILE)
        self.nt = n // self.tile
        self.narrow = sorted({w for w in (min(cap, w) for w in MOE_WINDOWS)
                              if (w % LANES == 0 or LANES % w == 0) and w % BF16_ROWS == 0})
        self.wide = min(cap, self.tile + BF16_ROWS)
        first = before[:, :, 0:n // LANES:self.tile // LANES].astype(I32)
        count = jnp.concatenate([first[:, :, 1:], jnp.full_like(first[:, :, :1], cap)], axis=2) - first
        aligned = first // BF16_ROWS * BF16_ROWS
        starts = [jnp.clip(aligned, 0, cap - w) for w in self.narrow + [self.wide]]
        level = jnp.zeros(first.shape[:1] + first.shape[2:], I32)
        for i in reversed(range(len(self.narrow))):
            fits = jnp.all(first - starts[i] + count <= self.narrow[i], axis=1)
            level = jnp.where(fits, i + 1, level)
        self.args = tuple(s.reshape(-1) for s in starts) + (level.reshape(-1),)


def _gather_kernel(*refs, ne, narrow, wide):
    narrow_refs = refs[:len(narrow)]
    ww_ref, level_ref, slot_ref, h_ref, o_ref = refs[len(narrow):]
    bi, gi, ki = pl.program_id(0), pl.program_id(1), pl.program_id(2)
    nt = pl.num_programs(2)
    group, tk = slot_ref.shape[1], slot_ref.shape[2]
    level = level_ref[bi * nt + ki]

    @pl.when(ki == 0)
    def _():
        o_ref[...] = jnp.zeros_like(o_ref)

    slot = slot_ref[0].astype(I32)

    def start(ref, j):
        return pl.multiple_of(ref[(bi * ne + gi * group + j) * nt + ki], BF16_ROWS)

    def onehot(j, ws, width):
        jcol = lax.broadcasted_iota(I32, (width, tk), 0)
        return jnp.where(jcol + ws == slot[j:j + 1, :], 1.0, 0.0).astype(BF16)

    def add(j, ws, width, rows_f32):
        rows = pl.ds(ws, width)
        o_ref[0, j, rows, :] = o_ref[0, j, rows, :] + rows_f32.astype(o_ref.dtype)

    for i, (wn_ref, width) in enumerate(zip(narrow_refs, narrow)):
        @pl.when(level == i + 1)
        def _(wn_ref=wn_ref, width=width):
            starts = [start(wn_ref, j) for j in range(group)]
            lhs = jnp.concatenate([onehot(j, starts[j], width) for j in range(group)], axis=0)
            res = _dot(lhs, h_ref[0])
            for j in range(group):
                add(j, starts[j], width, res[j * width:(j + 1) * width])

    @pl.when(level == 0)
    def _():
        for j in range(group):
            ws = start(ww_ref, j)
            add(j, ws, wide, _dot(onehot(j, ws, wide), h_ref[0]))


def moe_gather(slot, win, h, cap):
    b, e, n = slot.shape
    dx = h.shape[2]
    group = MOE_GROUP
    grid_spec = pltpu.PrefetchScalarGridSpec(
        num_scalar_prefetch=len(win.args),
        grid=(b, e // group, win.nt),
        in_specs=[pl.BlockSpec((1, group, win.tile), lambda i, g, k, *_: (i, g, k)),
                  pl.BlockSpec((1, win.tile, dx), lambda i, g, k, *_: (i, k, 0))],
        out_specs=pl.BlockSpec((1, group, cap, dx), lambda i, g, k, *_: (i, g, 0, 0)))
    return pl.pallas_call(
        functools.partial(_gather_kernel, ne=e, narrow=win.narrow, wide=win.wide),
        grid_spec=grid_spec,
        out_shape=jax.ShapeDtypeStruct((b, e, cap, dx), BF16),
        compiler_params=_cparams("arbitrary", "arbitrary", "arbitrary"),
        name="moe_gather",
    )(*win.args, slot, h)


def _ffn_kernel(*refs, ns):
    x_refs = refs[:ns]
    wg_ref, wu_ref, wd_ref = refs[ns:ns + 3]
    o_refs = refs[ns + 3:2 * ns + 3]
    wgb_ref, wub_ref, wdb_ref = refs[2 * ns + 3:]

    @pl.when(pl.program_id(1) == 0)
    def _():
        wgb_ref[...] = wg_ref[0, 0].astype(BF16)
        wub_ref[...] = wu_ref[0, 0].astype(BF16)
        wdb_ref[...] = wd_ref[0, 0].astype(BF16)

    xe = jnp.concatenate([x_ref[i, 0] for x_ref in x_refs for i in range(x_ref.shape[0])], axis=0)
    d = wgb_ref.shape[0]
    x = xe[:, :d]
    ext = xe[:, d:].astype(F32)
    lane = lax.broadcasted_iota(I32, ext.shape, 1)
    ei, ne = pl.program_id(0), pl.num_programs(0)
    mine = (lane == ei) | (lane == ei + ne) | (lane == ei + 2 * ne)
    gate = jnp.sum(jnp.where(mine, ext, 0.0), axis=1, keepdims=True)
    rows = x.shape[0]
    bounds = [(r, min(r + FFN_ROWS, rows)) for r in range(0, rows, FFN_ROWS)]
    ups = [(_dot(x[r0:r1], wgb_ref[...]), _dot(x[r0:r1], wub_ref[...])) for r0, r1 in bounds]
    y = jnp.concatenate([_dot(((a / (1.0 + jnp.exp(-a))) * u).astype(BF16), wdb_ref[...]) for a, u in ups],
                        axis=0) * gate
    row = 0
    for o_ref in o_refs:
        for i in range(o_ref.shape[0]):
            rows = o_ref.shape[2]
            o_ref[i, 0] = y[row:row + rows].astype(o_ref.dtype)
            row += rows


def moe_ffn(xs_list, w_gate, w_up, w_down, layer):
    b, e, _, dx = xs_list[0].shape
    d, f = w_gate.shape[2], w_gate.shape[3]
    bb = 2 if b % 2 == 0 else 1

    def specs(width):
        return [pl.BlockSpec((bb, 1, xs.shape[2], width), lambda j, i: (i, j, 0, 0)) for xs in xs_list]

    return pl.pallas_call(
        functools.partial(_ffn_kernel, ns=len(xs_list)),
        grid=(e, b // bb),
        in_specs=specs(dx) + [pl.BlockSpec((1, 1, d, f), lambda j, i: (layer, j, 0, 0)),
                              pl.BlockSpec((1, 1, d, f), lambda j, i: (layer, j, 0, 0)),
                              pl.BlockSpec((1, 1, f, d), lambda j, i: (layer, j, 0, 0))],
        out_specs=specs(d),
        out_shape=[jax.ShapeDtypeStruct(xs.shape[:3] + (d,), BF16) for xs in xs_list],
        scratch_shapes=[pltpu.VMEM((d, f), BF16), pltpu.VMEM((d, f), BF16), pltpu.VMEM((f, d), BF16)],
        compiler_params=_cparams("arbitrary", "arbitrary"),
        name="moe_ffn",
    )(*xs_list, w_gate, w_up, w_down)


def _combine_kernel(*refs, ne, narrow, wide):
    narrow_refs = refs[:len(narrow)]
    ww_ref, level_ref, slot_ref, ys_hbm, lat_ref, gt_ref, o_ref, ys_buf, sem = refs[len(narrow):]
    bi, ti = pl.program_id(0), pl.program_id(1)
    nb, nt = pl.num_programs(0), pl.num_programs(1)
    cur = bi % 2

    def fetch(batch, buf):
        return pltpu.make_async_copy(ys_hbm.at[batch], ys_buf.at[buf], sem.at[buf])

    @pl.when(ti == 0)
    def _():
        @pl.when(bi == 0)
        def _():
            fetch(0, 0).start()

        fetch(bi, cur).wait()

        @pl.when(bi + 1 < nb)
        def _():
            fetch(bi + 1, 1 - cur).start()

    tn = lat_ref.shape[1]
    slot = slot_ref[0].astype(I32)

    def start(ref, e):
        return pl.multiple_of(ref[(bi * ne + e) * nt + ti], BF16_ROWS)

    def onehot(e, ws, width):
        jrow = lax.broadcasted_iota(I32, (tn, width), 1)
        return jnp.where(jrow + ws == slot[:, e:e + 1], 1.0, 0.0).astype(BF16)

    def onehot_lane_group(first, starts, width):
        lane = lax.broadcasted_iota(I32, (tn, LANES), 1)
        hit = None
        for i in range(LANES // width):
            e = first + i
            mine = (lane + (starts[e] - i * width) == slot[:, e:e + 1]) & (lane >= i * width) & (lane < (i + 1) * width)
            hit = mine if hit is None else hit | mine
        return jnp.where(hit, 1.0, 0.0).astype(BF16)

    def finish(acc):
        o_ref[0] = lat_ref[0] + gt_ref[0] * acc

    level = level_ref[bi * nt + ti]
    for i, (wn_ref, width) in enumerate(zip(narrow_refs, narrow)):
        @pl.when(level == i + 1)
        def _(wn_ref=wn_ref, width=width):
            starts = [start(wn_ref, e) for e in range(ne)]
            if width % LANES == 0:
                lhs = jnp.concatenate([onehot(e, starts[e], width) for e in range(ne)], axis=1)
            else:
                lhs = jnp.concatenate([onehot_lane_group(e, starts, width)
                                       for e in range(0, ne, LANES // width)], axis=1)
            rhs = jnp.concatenate([ys_buf[cur, e, pl.ds(starts[e], width), :] for e in range(ne)], axis=0)
            finish(_dot(lhs, rhs))

    @pl.when(level == 0)
    def _():
        acc = jnp.zeros((tn, lat_ref.shape[2]), F32)
        for e in range(ne):
            ws = start(ww_ref, e)
            acc = acc + _dot(onehot(e, ws, wide), ys_buf[cur, e, pl.ds(ws, wide), :])
        finish(acc)


def moe_combine(slot_t, win, ys, lat, gate_vec, cap):
    b, n, e = slot_t.shape
    d = lat.shape[2]
    tn = win.tile
    grid_spec = pltpu.PrefetchScalarGridSpec(
        num_scalar_prefetch=len(win.args),
        grid=(b, win.nt),
        in_specs=[pl.BlockSpec((1, tn, e), lambda i, j, *_: (i, j, 0)),
                  pl.BlockSpec(memory_space=pl.ANY),
                  pl.BlockSpec((1, tn, d), lambda i, j, *_: (i, j, 0)),
                  pl.BlockSpec((1, 1, d), lambda i, j, *_: (i, 0, 0))],
        out_specs=pl.BlockSpec((1, tn, d), lambda i, j, *_: (i, j, 0)),
        scratch_shapes=[pltpu.VMEM((2, e, cap, d), BF16), pltpu.SemaphoreType.DMA((2,))])
    return pl.pallas_call(
        functools.partial(_combine_kernel, ne=e, narrow=win.narrow, wide=win.wide),
        grid_spec=grid_spec,
        out_shape=jax.ShapeDtypeStruct((b, n, d), F32),
        compiler_params=_cparams("arbitrary", "arbitrary"),
        name="moe_combine",
    )(*win.args, slot_t, ys, lat, gate_vec)


def expert_choice_moe(streams, g, w_router, w_gate, w_up, w_down, layer):
    e = w_router.shape[1]
    routed = []
    for lat, sh, sc, _ in streams:
        cap = EC_CAPACITY_FACTOR * lat.shape[1] // e
        h, aff = moe_router(lat, g, sh, sc, w_router.T)
        slot, before, slot_t = moe_select(aff, cap)
        win = _Windows(before, lat.shape[1], cap)
        routed.append((moe_gather(slot, win, h, cap), slot_t, win, cap))
    ys_list = moe_ffn([r[0] for r in routed], w_gate, w_up, w_down, layer)
    return [moe_combine(slot_t, win, ys, lat, gate_vec, cap)
            for (lat, _, _, gate_vec), (_, slot_t, win, cap), ys in zip(streams, routed, ys_list)]


def kernel(x, c, ctx, c_ctx, ada_w, ada_b, norm_g, sc_w_in, sc_conv, sc_w_out, hy_w_in, hy_conv_w, hy_f_w1,
           hy_f_b1, hy_f_w2, hy_f_b2, hy_f_w3, hy_sin_freq, hy_skip, hy_w_out, at_w_qkv, at_q_g, at_k_g, at_w_o,
           moe_router_w, moe_w_gate, moe_w_up, moe_w_down):
    b, l, d = x.shape
    depth = ada_w.shape[0]
    assert b + 1 <= MOD_ROWS
    qd = at_w_o.shape[1]
    nq = qd // HEAD_DIM
    attn_layers = list(range(MIX_ATTN, depth, N_MIXERS))
    last_attn = max(attn_layers, default=-1)

    cond = jnp.zeros((MOD_ROWS, d), F32).at[:b].set(c).at[b].set(c_ctx)
    mods = ada_modulation(cond, ada_w, ada_b)

    lat, cx = x, ctx
    for i in range(depth):
        kind = i % N_MIXERS
        j = i // N_MIXERS
        need_ctx = i <= last_attn
        upd_ctx = i < last_attn
        m_l = mods[i, :b].reshape(b, 1, 6, d)
        sh1, sc1, g1, sh2, sc2, g2 = (m_l[:, :, k] for k in range(6))
        if need_ctx:
            m_c = jnp.broadcast_to(mods[i, b].reshape(1, 1, 6, d), (b, 1, 6, d))
            csh1, csc1, cg1, csh2, csc2, cg2 = (m_c[:, :, k] for k in range(6))
        streams = [(lat, sh1, sc1, g1)]
        if upd_ctx:
            streams.append((cx, csh1, csc1, cg1))

        outs = []
        if kind == MIX_SHORTCONV:
            w_in, w_out = sc_w_in[j].astype(BF16), sc_w_out[j].astype(BF16)
            for s, sh, sc, gt in streams:
                p = nm_matmul(s, norm_g[i, 0], sh, sc, w_in)
                outs.append(sc_out(p, sc_conv[j], w_out, s, gt))
        elif kind == MIX_HYENA:
            w_in, w_out = hy_w_in[j].astype(BF16), hy_w_out[j].astype(BF16)
            for s, sh, sc, gt in streams:
                p = nm_matmul(s, norm_g[i, 0], sh, sc, w_in)
                z = hyena_mixer(p, hy_conv_w[j], hy_f_w1[j], hy_f_b1[j], hy_f_w2[j], hy_f_b2[j], hy_f_w3[j],
                                hy_sin_freq[j], hy_skip[j])
                outs.append(proj_residual(z, w_out, s, gt))
        else:
            w_qkv, w_o = at_w_qkv[j].astype(BF16), at_w_o[j].astype(BF16)
            q_l, k_l, v_l = qkv_project(lat, norm_g[i, 0], sh1, sc1, w_qkv, at_q_g[j], at_k_g[j],
                                        _rope_tables(l), nq, True)
            lc = cx.shape[1]
            dummy = tuple(jnp.zeros((lc, HEAD_DIM), F32) for _ in range(3))
            if upd_ctx:
                q_c, k_c, v_c = qkv_project(cx, norm_g[i, 0], csh1, csc1, w_qkv, at_q_g[j], at_k_g[j],
                                            dummy, nq, False)
            else:
                k_c, v_c = qkv_project(cx, norm_g[i, 0], csh1, csc1, w_qkv[:, qd:], at_q_g[j], at_k_g[j],
                                       dummy, 0, False)
            v_all = jnp.concatenate([v_c, v_l], axis=1)
            o_l = attention(q_l, jnp.concatenate([k_c, k_l], axis=1), jnp.swapaxes(v_all, 1, 2))
            outs.append(proj_residual(o_l, w_o, lat, g1))
            if upd_ctx:
                outs.append(proj_residual(attention(q_c, k_c, jnp.swapaxes(v_c, 1, 2)), w_o, cx, cg1))

        moe_streams = [(outs[0], sh2, sc2, g2)]
        if upd_ctx:
            moe_streams.append((outs[1], csh2, csc2, cg2))
        moe_out = expert_choice_moe(moe_streams, norm_g[i, 1], moe_router_w[i], moe_w_gate, moe_w_up, moe_w_down, i)
        lat = moe_out[0]
        if upd_ctx:
            cx = moe_out[1]
    return lat
```

```python
import functools
import math

import jax
import jax.numpy as jnp
import numpy as np
from jax import lax
from jax.experimental import pallas as pl
from jax.experimental.pallas import tpu as pltpu

F32 = jnp.float32
BF16 = jnp.bfloat16
I32 = jnp.int32
HIGHEST = lax.Precision.HIGHEST

N_MIXERS = 3
MIX_SHORTCONV, MIX_HYENA, MIX_ATTN = 0, 1, 2
CONV_W = 3
EPS = 1e-6
HY_ORDER = 2
HY_EMB = 33
HY_BANDS = (HY_EMB - 1) // 2
HY_FAST_DECAY = 0.3
HY_SLOW_DECAY = 1.5
HY_TARGET = 1e-2
HEAD_DIM = 128
N_KV_HEADS = 2
GRID_W = 64
ROPE_THETA = 10000.0
EC_CAPACITY_FACTOR = 2
Q_PRESCALE = HEAD_DIM ** -0.5 * math.log2(math.e)

LANES = 128
SUBLANES = 8
BF16_ROWS = 16
VMEM_LIMIT_BYTES = 56 * 1024 * 1024
ROW_TILE = 1024
MOD_ROWS = 16
FFT_N2 = 128
HY_DT = LANES
MOE_TILE = 2 * LANES
MOE_WINDOWS = (LANES // 2, LANES)
MOE_GROUP = 2 * SUBLANES
FFN_ROWS = 2 * LANES
DFT_UNROLL = 4
DFT_UNROLL_STAGE1 = 8


def _cparams(*sem):
    return pltpu.CompilerParams(dimension_semantics=sem, vmem_limit_bytes=VMEM_LIMIT_BYTES)


def _row_tile(n, cap=ROW_TILE):
    t = min(n, cap)
    assert n % t == 0
    return t


def _dot(a, b):
    return jnp.dot(a, b, preferred_element_type=F32)


def _ada_kernel(s_ref, w_ref, b_ref, o_ref):
    s = s_ref[...]
    s = s / (1.0 + jnp.exp(-s))
    o_ref[0] = jnp.dot(s, w_ref[0], preferred_element_type=F32, precision=HIGHEST) + b_ref[0]


def ada_modulation(cond, ada_w, ada_b):
    depth, d, n = ada_w.shape
    tn = 1536 if n % 1536 == 0 else n
    return pl.pallas_call(
        _ada_kernel,
        grid=(depth, n // tn),
        in_specs=[pl.BlockSpec((MOD_ROWS, d), lambda i, j: (0, 0)),
                  pl.BlockSpec((1, d, tn), lambda i, j: (i, 0, j)),
                  pl.BlockSpec((1, 1, tn), lambda i, j: (i, 0, j))],
        out_specs=pl.BlockSpec((1, MOD_ROWS, tn), lambda i, j: (i, 0, j)),
        out_shape=jax.ShapeDtypeStruct((depth, MOD_ROWS, n), F32),
        compiler_params=_cparams("arbitrary", "arbitrary"),
        name="ada_modulation",
    )(cond, ada_w, ada_b.reshape(depth, 1, n))


def _norm_mod(x, g, sh, sc):
    y = x * lax.rsqrt(jnp.mean(x * x, axis=-1, keepdims=True) + EPS)
    return (y * g) * (1.0 + sc) + sh


def _nm_matmul_kernel(x_ref, g_ref, sh_ref, sc_ref, w_ref, o_ref):
    h = _norm_mod(x_ref[0], g_ref[...], sh_ref[0], sc_ref[0]).astype(BF16)
    o_ref[0] = _dot(h, w_ref[...]).astype(o_ref.dtype)


def nm_matmul(x, g, sh, sc, w, out_dtype=BF16):
    b, l, d = x.shape
    n = w.shape[1]
    tm = _row_tile(l)
    return pl.pallas_call(
        _nm_matmul_kernel,
        grid=(b, l // tm),
        in_specs=[pl.BlockSpec((1, tm, d), lambda i, j: (i, j, 0)),
                  pl.BlockSpec((1, d), lambda i, j: (0, 0)),
                  pl.BlockSpec((1, 1, d), lambda i, j: (i, 0, 0)),
                  pl.BlockSpec((1, 1, d), lambda i, j: (i, 0, 0)),
                  pl.BlockSpec((d, n), lambda i, j: (0, 0))],
        out_specs=pl.BlockSpec((1, tm, n), lambda i, j: (i, j, 0)),
        out_shape=jax.ShapeDtypeStruct((b, l, n), out_dtype),
        compiler_params=_cparams("arbitrary", "arbitrary"),
        name="nm_matmul",
    )(x, g.reshape(1, d), sh, sc, w)


def _proj_res_kernel(a_ref, w_ref, lat_ref, gt_ref, o_ref):
    o_ref[0] = lat_ref[0] + gt_ref[0] * _dot(a_ref[0], w_ref[...])


def proj_residual(a, w, lat, gate):
    b, l, k = a.shape
    d = w.shape[1]
    tm = _row_tile(l)
    return pl.pallas_call(
        _proj_res_kernel,
        grid=(b, l // tm),
        in_specs=[pl.BlockSpec((1, tm, k), lambda i, j: (i, j, 0)),
                  pl.BlockSpec((k, d), lambda i, j: (0, 0)),
                  pl.BlockSpec((1, tm, d), lambda i, j: (i, j, 0)),
                  pl.BlockSpec((1, 1, d), lambda i, j: (i, 0, 0))],
        out_specs=pl.BlockSpec((1, tm, d), lambda i, j: (i, j, 0)),
        out_shape=jax.ShapeDtypeStruct((b, l, d), F32),
        compiler_params=_cparams("arbitrary", "arbitrary"),
        name="proj_residual",
    )(a, w, lat, gate)


def _sc_out_kernel(pb_ref, pc_ref, pv_ref, cp_ref, vp_ref, cn_ref, vn_ref, cw_ref, w_ref, lat_ref, gt_ref,
                   o_ref, u_ref, *, tm):
    j = pl.program_id(1)
    nj = pl.num_programs(1)
    d = u_ref.shape[1]
    u_ref[pl.ds(SUBLANES, tm), :] = pc_ref[0].astype(F32) * pv_ref[0].astype(F32)
    prev = cp_ref[0, BF16_ROWS - 1:BF16_ROWS, :].astype(F32) * vp_ref[0, BF16_ROWS - 1:BF16_ROWS, :].astype(F32)
    nxt = cn_ref[0, 0:1, :].astype(F32) * vn_ref[0, 0:1, :].astype(F32)
    prev = jnp.where(j == 0, 0.0, prev)
    nxt = jnp.where(j == nj - 1, 0.0, nxt)
    u_ref[pl.ds(0, SUBLANES), :] = jnp.broadcast_to(prev, (SUBLANES, d))
    u_ref[pl.ds(SUBLANES + tm, SUBLANES), :] = jnp.broadcast_to(nxt, (SUBLANES, d))
    cw = cw_ref[...]
    y = (u_ref[pl.ds(SUBLANES - 1, tm), :] * cw[0:1] + u_ref[pl.ds(SUBLANES, tm), :] * cw[1:2]
         + u_ref[pl.ds(SUBLANES + 1, tm), :] * cw[2:3])
    m = (pb_ref[0].astype(F32) * y).astype(BF16)
    o_ref[0] = lat_ref[0] + gt_ref[0] * _dot(m, w_ref[...])


def sc_out(p, conv_w, w_out, lat, gate):
    b, l, d3 = p.shape
    d = d3 // 3
    tm = _row_tile(l)
    hb = tm // BF16_ROWS
    nhb = l // BF16_ROWS

    def prev_map(col):
        return lambda i, j: (i, jnp.maximum(j * hb - 1, 0), col)

    def next_map(col):
        return lambda i, j: (i, jnp.minimum((j + 1) * hb, nhb - 1), col)

    return pl.pallas_call(
        functools.partial(_sc_out_kernel, tm=tm),
        grid=(b, l // tm),
        in_specs=[pl.BlockSpec((1, tm, d), lambda i, j: (i, j, 0)),
                  pl.BlockSpec((1, tm, d), lambda i, j: (i, j, 1)),
                  pl.BlockSpec((1, tm, d), lambda i, j: (i, j, 2)),
                  pl.BlockSpec((1, BF16_ROWS, d), prev_map(1)),
                  pl.BlockSpec((1, BF16_ROWS, d), prev_map(2)),
                  pl.BlockSpec((1, BF16_ROWS, d), next_map(1)),
                  pl.BlockSpec((1, BF16_ROWS, d), next_map(2)),
                  pl.BlockSpec((CONV_W, d), lambda i, j: (0, 0)),
                  pl.BlockSpec((d, d), lambda i, j: (0, 0)),
                  pl.BlockSpec((1, tm, d), lambda i, j: (i, j, 0)),
                  pl.BlockSpec((1, 1, d), lambda i, j: (i, 0, 0))],
        out_specs=pl.BlockSpec((1, tm, d), lambda i, j: (i, j, 0)),
        out_shape=jax.ShapeDtypeStruct((b, l, d), F32),
        scratch_shapes=[pltpu.VMEM((tm + 2 * SUBLANES, d), F32)],
        compiler_params=_cparams("arbitrary", "arbitrary"),
        name="sc_out",
    )(p, p, p, p, p, p, p, conv_w, w_out, lat, gate)


def _hy_filter_kernel(z_ref, w1_ref, b1_ref, w2_ref, b2_ref, fr_ref, w3f_ref, w3b_ref, t_ref, dl_ref,
                      of_ref, ob_ref, a_ref):
    @pl.when((pl.program_id(0) == 0) & (pl.program_id(1) == 0))
    def _():
        fr = fr_ref[...]
        a1 = jnp.sin(fr * (jnp.dot(z_ref[...], w1_ref[...], preferred_element_type=F32, precision=HIGHEST)
                           + b1_ref[...]))
        a_ref[...] = jnp.sin(fr * (jnp.dot(a1, w2_ref[...], preferred_element_type=F32, precision=HIGHEST)
                                   + b2_ref[...]))

    a = a_ref[...]
    decay = jnp.exp(-t_ref[...] * dl_ref[...])
    hf = jnp.dot(a, w3f_ref[...], preferred_element_type=F32, precision=HIGHEST) * decay
    hb = jnp.dot(a, w3b_ref[...], preferred_element_type=F32, precision=HIGHEST) * decay
    row = lax.broadcasted_iota(I32, hb.shape, 0)
    hb = jnp.where(row == 0, 0.0, hb)
    inv = 1.0 / (jnp.sum(jnp.abs(hf), axis=0, keepdims=True) + jnp.sum(jnp.abs(hb), axis=0, keepdims=True))
    of_ref[0] = hf * inv
    ob_ref[0] = hb * inv


def hy_filter_taps(l, d, f_w1, f_b1, f_w2, f_b2, f_w3, sin_freq):
    hid = f_w1.shape[1]
    hp = LANES
    t = np.linspace(0.0, 1.0, l)[:, None]
    bands = np.linspace(1e-4, HY_BANDS - 1, HY_BANDS)[None, :]
    ang = (2.0 * math.pi / l) * np.arange(l)[:, None] * bands
    z = np.zeros((l, hp), np.float64)
    z[:, :HY_EMB] = np.concatenate([t, np.cos(ang), -np.sin(ang)], axis=-1)
    deltas = np.abs(np.linspace(math.log(HY_TARGET) / HY_SLOW_DECAY, math.log(HY_TARGET) / HY_FAST_DECAY, d))[None, :]

    def pad2(a, r, c):
        return jnp.zeros((r, c), F32).at[:a.shape[0], :a.shape[1]].set(a.astype(F32))

    w1 = pad2(f_w1, hp, hp)
    w2 = pad2(f_w2, hp, hp)
    w3 = pad2(f_w3, hp, f_w3.shape[1])
    b1 = pad2(f_b1[None], 1, hp)
    b2 = pad2(f_b2[None], 1, hp)
    fr = pad2(sin_freq[None], 1, hp)
    tc = min(d, 2 * LANES)
    nd = d // tc
    full = lambda o, j: (0, 0)
    return pl.pallas_call(
        _hy_filter_kernel,
        grid=(HY_ORDER, nd),
        in_specs=[pl.BlockSpec((l, hp), full), pl.BlockSpec((hp, hp), full), pl.BlockSpec((1, hp), full),
                  pl.BlockSpec((hp, hp), full), pl.BlockSpec((1, hp), full), pl.BlockSpec((1, hp), full),
                  pl.BlockSpec((hp, tc), lambda o, j: (0, o * nd + j)),
                  pl.BlockSpec((hp, tc), lambda o, j: (0, (HY_ORDER + o) * nd + j)),
                  pl.BlockSpec((l, 1), full),
                  pl.BlockSpec((1, tc), lambda o, j: (0, j))],
        out_specs=[pl.BlockSpec((1, l, tc), lambda o, j: (o, 0, j)),
                   pl.BlockSpec((1, l, tc), lambda o, j: (o, 0, j))],
        out_shape=[jax.ShapeDtypeStruct((HY_ORDER, l, d), F32)] * 2,
        scratch_shapes=[pltpu.VMEM((l, hp), F32)],
        compiler_params=_cparams("arbitrary", "arbitrary"),
        name="hy_filter_taps",
    )(jnp.asarray(z, F32), w1, b1, w2, b2, fr, w3, w3, jnp.asarray(t, F32), jnp.asarray(deltas, F32))


def _dft_consts(l):
    n = 2 * l
    if n <= 4 * FFT_N2:
        n1, n2 = 1, n
    else:
        n1, n2 = n // FFT_N2, FFT_N2
    if n1 == 1:
        k = np.arange(n)[:, None]
        t = np.arange(l)[None, :]
        ang = 2.0 * np.pi * (k * t % n) / n
        c, s = np.cos(ang), np.sin(ang)
        fwd = np.block([[c, s], [-s, c]])
        inv = np.block([[c.T, -s.T], [s.T, c.T]]) / n
        return n1, n2, dict(fwd=jnp.asarray(fwd, BF16), inv=jnp.asarray(inv, BF16))
    n1h = n1 // 2
    k1 = np.arange(n1)[None, :, None]
    m1 = np.arange(n1h)[None, None, :]
    r2 = np.arange(n2)[:, None, None]
    ang = 2.0 * np.pi * ((k1 * (n2 * m1 + r2)) % n) / n
    c, s = np.cos(ang), np.sin(ang)
    m1f = np.concatenate([np.concatenate([c, s], 2), np.concatenate([-s, c], 2)], 1)
    ct, st = np.swapaxes(c, 1, 2) / n, np.swapaxes(s, 1, 2) / n
    m1i = np.concatenate([np.concatenate([ct, -st], 2), np.concatenate([st, ct], 2)], 1)
    a2 = 2.0 * np.pi * ((np.arange(n2)[:, None] * np.arange(n2)[None, :]) % n2) / n2
    c2, s2 = np.cos(a2), np.sin(a2)
    f2 = np.block([[c2, s2], [-s2, c2]])
    f2i = np.block([[c2, -s2], [s2, c2]])

    def il(m):
        return np.stack([np.arange(m), m + np.arange(m)], 1).reshape(-1)

    m1f = m1f[:, il(n1), :][:, :, il(n1h)]
    m1i = m1i[:, il(n1h), :][:, :, il(n1)]
    f2 = f2[:, il(n2)]
    f2i = f2i[il(n2), :][:, il(n2)]
    half = n2 // 2
    m1f = np.concatenate([m1f[:half], m1f[half:]], 2)
    m1i = np.concatenate([m1i[:half], m1i[half:]], 2)
    return n1, n2, dict(m1f=jnp.asarray(m1f, BF16), m1i=jnp.asarray(m1i, BF16),
                        f2=jnp.asarray(f2, BF16), f2i=jnp.asarray(f2i, BF16))


def _blockdiag(a, b):
    z = jnp.zeros_like(a)
    return jnp.concatenate([jnp.concatenate([a, z], axis=1), jnp.concatenate([z, b], axis=1)], axis=0)


def _halves(y):
    return y[:, :LANES], y[:, LANES:]


HI_HALF = -65536


def _pack_pair(re, im):
    lo = (pltpu.bitcast(re.astype(BF16).astype(F32), I32) >> 16) & 0xFFFF
    if im is None:
        return lo
    return lo | (pltpu.bitcast(im.astype(BF16).astype(F32), I32) & HI_HALF)


def _unpack_pair(w):
    return pltpu.bitcast(w << 16, F32), pltpu.bitcast(w & HI_HALF, F32)


def _as_rows(w):
    return pltpu.bitcast(w, BF16)


def _as_words(y):
    return pltpu.bitcast(y.astype(BF16), I32)


def _fwd_stage1(x_pk, a_pk, m1f_ref, n1, n2):
    n1h, half = n1 // 2, n2 // 2

    def column(r):
        return _as_rows(x_pk[pl.ds(r, n1h, stride=n2), :])

    def body(r, carry):
        w = _as_words(_dot(m1f_ref[r], _blockdiag(column(r), column(r + half))))
        for rr, ww in zip((r, r + half), _halves(w)):
            a_pk[pl.ds(pl.multiple_of(rr * n1, n1), n1), :] = ww
        return carry

    lax.fori_loop(0, half, body, 0, unroll=min(half, DFT_UNROLL_STAGE1))


def _inv_stage1(b_pk, x_pk, m1i_ref, n1, n2):
    n1h, half = n1 // 2, n2 // 2

    def column(r):
        return _as_rows(b_pk[pl.ds(pl.multiple_of(r * n1, n1), n1), :])

    def body(r, carry):
        w = _as_words(_dot(m1i_ref[r], _blockdiag(column(r), column(r + half))))
        for rr, ww in zip((r, r + half), _halves(w)):
            x_pk[pl.ds(rr, n1h, stride=n2), :] = ww
        return carry

    lax.fori_loop(0, half, body, 0, unroll=min(half, DFT_UNROLL_STAGE1))


def _load_chunk(a_pk, k1, n1, n2):
    ra = pl.ds(k1, n2, stride=n1)
    rb = pl.ds(k1 + n1 // 2, n2, stride=n1)
    return jnp.concatenate([_as_rows(a_pk[ra, :]), _as_rows(a_pk[rb, :])], axis=1)


def _hy_spec_kernel(*refs, l, n1, n2, dense):
    if dense:
        tf_ref, tb_ref, sk_ref, fwd_ref, o_ref = refs
        n = n2
        fw = fwd_ref[:, 0:l]
        sf = _dot(fw, tf_ref[0].astype(BF16))
        sb = _dot(fw, tb_ref[0].astype(BF16))
        o_ref[0, 0] = (sf[:n] + sb[:n] + sk_ref[0]).astype(o_ref.dtype)
        o_ref[0, 1] = (sf[n:] - sb[n:]).astype(o_ref.dtype)
        return
    tf_ref, tb_ref, sk_ref, m1f_ref, f2_ref, o_ref, x_pk, a_pk, h_re, h_im = refs
    for run, t_ref in enumerate((tf_ref, tb_ref)):
        x_pk[...] = _pack_pair(t_ref[0], None)
        _fwd_stage1(x_pk, a_pk, m1f_ref, n1, n2)

        def body(k1, carry, run=run):
            x = _dot(f2_ref[...], _load_chunk(a_pk, k1, n1, n2))
            for kk, xx in zip((k1, k1 + n1 // 2), _halves(x)):
                rows = pl.ds(pl.multiple_of(kk * n2, n2), n2)
                if run == 0:
                    h_re[rows, :] = xx[:n2]
                    h_im[rows, :] = xx[n2:]
                else:
                    h_re[rows, :] = h_re[rows, :] + xx[:n2]
                    h_im[rows, :] = h_im[rows, :] - xx[n2:]
            return carry

        lax.fori_loop(0, n1 // 2, body, 0, unroll=DFT_UNROLL)
    o_ref[0, 0] = (h_re[...] + sk_ref[0]).astype(o_ref.dtype)
    o_ref[0, 1] = h_im[...].astype(o_ref.dtype)


def hy_spectrum(taps_f, taps_b, skip, consts, n1, n2):
    o, l, d = taps_f.shape
    n = 2 * l
    dt = HY_DT
    dense = n1 == 1
    tap_spec = pl.BlockSpec((1, l, dt), lambda i, j: (i, 0, j))
    in_specs = [tap_spec, tap_spec, pl.BlockSpec((1, 1, dt), lambda i, j: (i, 0, j))]
    if dense:
        mats = [consts["fwd"]]
        scratch = []
    else:
        mats = [consts["m1f"], consts["f2"]]
        scratch = [pltpu.VMEM((l, dt), I32), pltpu.VMEM((n, dt), I32)] + [pltpu.VMEM((n, dt), F32)] * 2
    in_specs += [pl.BlockSpec(m.shape, lambda i, j, nd=m.ndim: (0,) * nd) for m in mats]
    return pl.pallas_call(
        functools.partial(_hy_spec_kernel, l=l, n1=n1, n2=n2, dense=dense),
        grid=(o, d // dt),
        in_specs=in_specs,
        out_specs=pl.BlockSpec((1, 2, n, dt), lambda i, j: (i, 0, 0, j)),
        out_shape=jax.ShapeDtypeStruct((o, 2, n, d), BF16),
        scratch_shapes=scratch,
        compiler_params=_cparams("arbitrary", "arbitrary"),
        name="hy_spectrum",
    )(taps_f, taps_b, skip.reshape(o, 1, d), *mats)


def _conv3_rows(src, cw, pad_ref, l):
    c = pad_ref.shape[1]
    pad_ref[pl.ds(0, SUBLANES), :] = jnp.zeros((SUBLANES, c), F32)
    pad_ref[pl.ds(SUBLANES + l, SUBLANES), :] = jnp.zeros((SUBLANES, c), F32)
    pad_ref[pl.ds(SUBLANES, l), :] = src
    return (pad_ref[pl.ds(SUBLANES - 1, l), :] * cw[0:1] + pad_ref[pl.ds(SUBLANES, l), :] * cw[1:2]
            + pad_ref[pl.ds(SUBLANES + 1, l), :] * cw[2:3])


def _hy_conv_kernel(*refs, l, n1, n2, dense, conv_z):
    if dense:
        z_ref, g_ref, cwz_ref, cwg_ref, h_ref, fwd_ref, inv_ref, o_ref, pad_ref = refs
    else:
        (z_ref, g_ref, cwz_ref, cwg_ref, h_ref, m1f_ref, m1i_ref, f2_ref, f2i_ref, o_ref,
         x_pk, a_pk, b_pk, pad_ref) = refs
    zs = []
    for e in range(2):
        z = z_ref[e, 0].astype(F32)
        zs.append(_conv3_rows(z, cwz_ref[...], pad_ref, l) if conv_z else z)
    if dense:
        n = n2
        s = _dot(fwd_ref[...], jnp.concatenate(zs, axis=0).astype(BF16))
        hr = h_ref[0, 0].astype(F32)
        hi = h_ref[0, 1].astype(F32)
        zr = s[:n] * hr - s[n:] * hi
        zi = s[:n] * hi + s[n:] * hr
        y = _dot(inv_ref[...], jnp.concatenate([zr, zi], axis=0).astype(BF16))
        ys = (y[:l], y[l:])
    else:
        x_pk[...] = _pack_pair(zs[0], zs[1])
        _fwd_stage1(x_pk, a_pk, m1f_ref, n1, n2)

        nk = n1 // 2

        def spectrum_step(k1, carry):
            s = _dot(f2_ref[...], _load_chunk(a_pk, k1, n1, n2))
            rows = [pl.ds(pl.multiple_of(k * n2, n2), n2) for k in (k1, k1 + nk)]
            hr = jnp.concatenate([h_ref[0, 0, r, :] for r in rows], axis=1).astype(F32)
            hi = jnp.concatenate([h_ref[0, 1, r, :] for r in rows], axis=1).astype(F32)
            zr = s[:n2] * hr - s[n2:] * hi
            zi = s[:n2] * hi + s[n2:] * hr
            for r, zr_half, zi_half in zip(rows, _halves(zr), _halves(zi)):
                b_pk[r, :] = _pack_pair(zr_half, zi_half)
            return carry

        lax.fori_loop(0, nk, spectrum_step, 0, unroll=min(nk, DFT_UNROLL))

        def inverse_step(k1, carry):
            kk = (k1, k1 + nk)
            z = jnp.concatenate([_as_rows(b_pk[pl.ds(pl.multiple_of(k * n2, n2), n2), :]) for k in kk], axis=1)
            w = _as_words(_dot(f2i_ref[...], z))
            for k, ww in zip(kk, _halves(w)):
                a_pk[pl.ds(k, n2, stride=n1), :] = ww
            return carry

        lax.fori_loop(0, nk, inverse_step, 0, unroll=min(nk, DFT_UNROLL))
        _inv_stage1(a_pk, x_pk, m1i_ref, n1, n2)
        ys = _unpack_pair(x_pk[...])
    for e in range(2):
        g = _conv3_rows(g_ref[e, 0].astype(F32), cwg_ref[...], pad_ref, l)
        o_ref[e, 0] = (g * ys[e]).astype(o_ref.dtype)


def hy_conv(zsrc, zcol, p4, gcol, conv_w, spec, order, consts, n1, n2, conv_z):
    _, pairs, l, _ = zsrc.shape
    d = spec.shape[3]
    n = 2 * l
    dt = HY_DT
    nd = d // dt
    dense = n1 == 1
    once = pl.Buffered(1)
    in_specs = [pl.BlockSpec((2, 1, l, dt), lambda j, i: (0, i, 0, zcol * nd + j)),
                pl.BlockSpec((2, 1, l, dt), lambda j, i: (0, i, 0, gcol * nd + j)),
                pl.BlockSpec((CONV_W, dt), lambda j, i: (0, j)),
                pl.BlockSpec((CONV_W, dt), lambda j, i: (0, gcol * nd + j)),
                pl.BlockSpec((1, 2, n, dt), lambda j, i: (order, 0, 0, j), pipeline_mode=once)]
    if dense:
        mats = [consts["fwd"], consts["inv"]]
        scratch = []
    else:
        mats = [consts["m1f"], consts["m1i"], consts["f2"], consts["f2i"]]
        scratch = [pltpu.VMEM((l, dt), I32)] + [pltpu.VMEM((n, dt), I32)] * 2
    scratch = scratch + [pltpu.VMEM((l + 2 * SUBLANES, dt), F32)]
    in_specs += [pl.BlockSpec(m.shape, lambda j, i, nd_=m.ndim: (0,) * nd_, pipeline_mode=once) for m in mats]
    return pl.pallas_call(
        functools.partial(_hy_conv_kernel, l=l, n1=n1, n2=n2, dense=dense, conv_z=conv_z),
        grid=(nd, pairs),
        in_specs=in_specs,
        out_specs=pl.BlockSpec((2, 1, l, dt), lambda j, i: (0, i, 0, j)),
        out_shape=jax.ShapeDtypeStruct((2, pairs, l, d), BF16),
        scratch_shapes=scratch,
        compiler_params=_cparams("arbitrary", "arbitrary"),
        name="hy_conv",
    )(zsrc, p4, conv_w, conv_w, spec, *mats)


def hyena_mixer(p, conv_w, f_w1, f_b1, f_w2, f_b2, f_w3, sin_freq, skip):
    b, l, d3 = p.shape
    d = d3 // 3
    assert b % 2 == 0 and d % HY_DT == 0
    n1, n2, consts = _dft_consts(l)
    taps_f, taps_b = hy_filter_taps(l, d, f_w1, f_b1, f_w2, f_b2, f_w3, sin_freq)
    spec = hy_spectrum(taps_f, taps_b, skip, consts, n1, n2)
    p4 = p.reshape(2, b // 2, l, d3)
    z = hy_conv(p4, 0, p4, 1, conv_w, spec, 0, consts, n1, n2, True)
    z = hy_conv(z, 0, p4, 2, conv_w, spec, 1, consts, n1, n2, False)
    return z.reshape(b, l, d)


def _head_norm(x, g):
    sq = x * x
    hi = sq.astype(BF16)
    lo = (sq - hi.astype(F32)).astype(BF16)
    ones = jnp.ones((HEAD_DIM, HEAD_DIM), BF16)
    total = _dot(hi, ones) + _dot(lo, ones)
    return x * lax.rsqrt(total * (1.0 / HEAD_DIM) + EPS) * g


def _rope(x, cos, sin_a, sin_b):
    return (x * cos + pltpu.roll(x, HEAD_DIM // 4, 1) * sin_a
            + pltpu.roll(x, HEAD_DIM - HEAD_DIM // 4, 1) * sin_b)


def _qkv_kernel(x_ref, g_ref, sh_ref, sc_ref, w_ref, qg_ref, kg_ref, cos_ref, sa_ref, sb_ref,
                *out_refs, nq, rope):
    q_ref = out_refs[0] if nq else None
    k_ref, v_ref = out_refs[-2:]
    h = _norm_mod(x_ref[0], g_ref[...], sh_ref[0], sc_ref[0]).astype(BF16)
    acc = _dot(h, w_ref[...])
    hd = HEAD_DIM
    for i in range(nq + N_KV_HEADS):
        t = acc[:, i * hd:(i + 1) * hd]
        t = _head_norm(t, qg_ref[...] if i < nq else kg_ref[...])
        if rope:
            t = _rope(t, cos_ref[...], sa_ref[...], sb_ref[...])
        if i < nq:
            q_ref[0, :, i * hd:(i + 1) * hd] = (t * Q_PRESCALE).astype(BF16)
        else:
            k_ref[0, :, (i - nq) * hd:(i - nq + 1) * hd] = t.astype(BF16)
    v_ref[0] = acc[:, (nq + N_KV_HEADS) * hd:].astype(BF16)


def qkv_project(x, g, sh, sc, w, q_g, k_g, rope_tabs, nq, rope):
    b, l, d = x.shape
    n = w.shape[1]
    hd = HEAD_DIM
    kvd = N_KV_HEADS * hd
    tm = _row_tile(l)
    cos, sa, sb = rope_tabs
    tab = pl.BlockSpec((tm, hd), lambda i, j: (j, 0))
    vec = pl.BlockSpec((1, 1, d), lambda i, j: (i, 0, 0))
    widths = ([nq * hd] if nq else []) + [kvd, kvd]
    return pl.pallas_call(
        functools.partial(_qkv_kernel, nq=nq, rope=rope),
        grid=(b, l // tm),
        in_specs=[pl.BlockSpec((1, tm, d), lambda i, j: (i, j, 0)),
                  pl.BlockSpec((1, d), lambda i, j: (0, 0)), vec, vec,
                  pl.BlockSpec((d, n), lambda i, j: (0, 0)),
                  pl.BlockSpec((1, hd), lambda i, j: (0, 0)),
                  pl.BlockSpec((1, hd), lambda i, j: (0, 0)),
                  tab, tab, tab],
        out_specs=[pl.BlockSpec((1, tm, wd), lambda i, j: (i, j, 0)) for wd in widths],
        out_shape=[jax.ShapeDtypeStruct((b, l, wd), BF16) for wd in widths],
        compiler_params=_cparams("arbitrary", "arbitrary"),
        name="qkv_project",
    )(x, g.reshape(1, d), sh, sc, w, q_g.reshape(1, hd), k_g.reshape(1, hd), cos, sa, sb)


def _rope_tables(l):
    rows = l // GRID_W
    row = np.repeat(np.arange(rows, dtype=np.float64), GRID_W)
    col = np.tile(np.arange(GRID_W, dtype=np.float64), rows)
    axis = HEAD_DIM // 2
    inv = ROPE_THETA ** (-np.arange(0, axis, 2, dtype=np.float64) / axis)

    def axis_angles(pos):
        a = pos[:, None] * inv[None, :]
        return np.concatenate([a, a], axis=-1)

    ang = np.concatenate([axis_angles(row), axis_angles(col)], axis=-1)
    cos, sin = np.cos(ang), np.sin(ang)
    lane = np.arange(HEAD_DIM)[None, :] % axis
    upper = lane >= axis // 2
    sin_a = np.where(upper, sin, 0.0)
    sin_b = np.where(upper, 0.0, -sin)
    return tuple(jnp.asarray(t, F32) for t in (cos, sin_a, sin_b))


def _attn_kernel(q_ref, k_ref, vt_ref, o_ref, *, groups):
    k = k_ref[0]
    vt = vt_ref[0]
    hd = HEAD_DIM
    tq = q_ref.shape[1]
    hp = 2 if groups % 2 == 0 else 1
    def scores(g0):
        q = jnp.concatenate([q_ref[0, :, g * hd:(g + 1) * hd] for g in range(g0, g0 + hp)], axis=0)
        return lax.dot_general(k, q, (((1,), (1,)), ((), ())), preferred_element_type=F32)

    sts = [scores(g0) for g0 in range(0, groups, hp)]
    for g0, st in zip(range(0, groups, hp), sts):
        m = jnp.max(st, axis=0, keepdims=True)
        pt = jnp.exp2(st - m)
        den = jnp.sum(pt, axis=0, keepdims=True)
        ot = _dot(vt, pt.astype(BF16)) / den
        for i in range(hp):
            g = g0 + i
            o_ref[0, :, g * hd:(g + 1) * hd] = ot[:, i * tq:(i + 1) * tq].T.astype(o_ref.dtype)


def attention(q, k, vt):
    b, l, qd = q.shape
    s = k.shape[1]
    hd = HEAD_DIM
    groups = qd // hd // N_KV_HEADS
    tq = _row_tile(l, 256)
    return pl.pallas_call(
        functools.partial(_attn_kernel, groups=groups),
        grid=(b, N_KV_HEADS, l // tq),
        in_specs=[pl.BlockSpec((1, tq, groups * hd), lambda i, h, j: (i, j, h)),
                  pl.BlockSpec((1, s, hd), lambda i, h, j: (i, 0, h)),
                  pl.BlockSpec((1, hd, s), lambda i, h, j: (i, h, 0))],
        out_specs=pl.BlockSpec((1, tq, groups * hd), lambda i, h, j: (i, j, h)),
        out_shape=jax.ShapeDtypeStruct((b, l, qd), BF16),
        compiler_params=_cparams("arbitrary", "arbitrary", "arbitrary"),
        name="attention",
    )(q, k, vt)


def _router_kernel(x_ref, g_ref, sh_ref, sc_ref, wr_ref, h_ref, a_ref):
    h = _norm_mod(x_ref[0], g_ref[...], sh_ref[0], sc_ref[0])
    d = h.shape[1]
    logits = lax.dot_general(wr_ref[...], h, (((1,), (1,)), ((), ())), preferred_element_type=F32,
                             precision=HIGHEST)
    ex = jnp.exp(logits - jnp.max(logits, axis=0, keepdims=True))
    aff = ex / jnp.sum(ex, axis=0, keepdims=True)
    a_ref[0] = aff
    ne, tm = aff.shape
    hi = aff.astype(BF16).astype(F32)
    mid = (aff - hi).astype(BF16).astype(F32)
    lo = (aff - hi - mid).astype(BF16).astype(F32)
    pieces = jnp.concatenate([hi, mid, lo, jnp.zeros((LANES - 3 * ne, tm), F32)], axis=0)
    h_ref[0, :, 0:d] = h.astype(BF16)
    h_ref[0, :, d:d + LANES] = pieces.T.astype(BF16)


def moe_router(x, g, sh, sc, w_router_t):
    b, n, d = x.shape
    e = w_router_t.shape[0]
    assert 3 * e <= LANES and e % SUBLANES == 0
    tm = _row_tile(n)
    vec = pl.BlockSpec((1, 1, d), lambda i, j: (i, 0, 0))
    return pl.pallas_call(
        _router_kernel,
        grid=(b, n // tm),
        in_specs=[pl.BlockSpec((1, tm, d), lambda i, j: (i, j, 0)),
                  pl.BlockSpec((1, d), lambda i, j: (0, 0)), vec, vec,
                  pl.BlockSpec((e, d), lambda i, j: (0, 0))],
        out_specs=[pl.BlockSpec((1, tm, d + LANES), lambda i, j: (i, j, 0)),
                   pl.BlockSpec((1, e, tm), lambda i, j: (i, 0, j))],
        out_shape=[jax.ShapeDtypeStruct((b, n, d + LANES), BF16), jax.ShapeDtypeStruct((b, e, n), F32)],
        compiler_params=_cparams("arbitrary", "arbitrary"),
        name="moe_router",
    )(x, g.reshape(1, d), sh, sc, w_router_t)


def _cumsum_lanes(m, tri_ref):
    e, n = m.shape
    carry = jnp.zeros((e, 1), F32)
    outs = []
    for c in range(n // LANES):
        blk = m[:, c * LANES:(c + 1) * LANES]
        cs = _dot(blk.astype(BF16), tri_ref[...]) + carry
        outs.append(cs)
        carry = cs[:, LANES - 1:LANES]
    return jnp.concatenate(outs, axis=1)


def _select_kernel(a_ref, tri_ref, ind_ref, slot_ref, before_ref, slot_t_ref, *, cap):
    aff = a_ref[0]
    bits = pltpu.bitcast(aff, I32)
    e = aff.shape[0]

    def body(i, thr):
        cand = thr | (1 << (30 - i))
        cnt = jnp.sum((bits >= cand).astype(F32), axis=1, keepdims=True)
        return jnp.where(cnt >= cap, cand, thr)

    thr = lax.fori_loop(0, 31, body, jnp.zeros((e, 1), I32))
    gt = bits > thr
    eq = bits == thr
    n_gt = jnp.sum(gt.astype(F32), axis=1, keepdims=True)
    eq_rank = _cumsum_lanes(eq.astype(F32), tri_ref)
    sel = gt | (eq & (eq_rank <= cap - n_gt))
    self_ = sel.astype(F32)
    pos = _cumsum_lanes(self_, tri_ref) - 1.0
    slot = jnp.where(sel, pos, -1.0)
    slot_ref[0] = slot
    before_ref[0] = _dot(self_.astype(BF16), ind_ref[...])
    slot_t_ref[0] = slot.T


def moe_select(aff, cap):
    b, e, n = aff.shape
    assert n // LANES <= LANES
    tri = jnp.asarray(np.triu(np.ones((LANES, LANES), np.float32)), BF16)
    ind = jnp.asarray(np.arange(n)[:, None] < LANES * np.arange(LANES)[None, :], BF16)
    row = pl.BlockSpec((1, e, n), lambda i: (i, 0, 0))
    col = pl.BlockSpec((1, n, e), lambda i: (i, 0, 0))
    return pl.pallas_call(
        functools.partial(_select_kernel, cap=cap),
        grid=(b,),
        in_specs=[row, pl.BlockSpec((LANES, LANES), lambda i: (0, 0)), pl.BlockSpec((n, LANES), lambda i: (0, 0))],
        out_specs=[row, pl.BlockSpec((1, e, LANES), lambda i: (i, 0, 0)), col],
        out_shape=[jax.ShapeDtypeStruct((b, e, n), F32), jax.ShapeDtypeStruct((b, e, LANES), F32),
                   jax.ShapeDtypeStruct((b, n, e), F32)],
        compiler_params=_cparams("arbitrary"),
        name="moe_select",
    )(aff, tri, ind)


class _Windows:
    def __init__(self, before, n, cap):
        self.tile = min(n, MOE_TILE)
        self.nt = n // self.tile
        self.narrow = sorted({w for w in (min(cap, w) for w in MOE_WINDOWS)
                              if (w % LANES == 0 or LANES % w == 0) and w % BF16_ROWS == 0})
        self.wide = min(cap, self.tile + BF16_ROWS)
        first = before[:, :, 0:n // LANES:self.tile // LANES].astype(I32)
        count = jnp.concatenate([first[:, :, 1:], jnp.full_like(first[:, :, :1], cap)], axis=2) - first
        aligned = first // BF16_ROWS * BF16_ROWS
        starts = [jnp.clip(aligned, 0, cap - w) for w in self.narrow + [self.wide]]
        level = jnp.zeros(first.shape[:1] + first.shape[2:], I32)
        for i in reversed(range(len(self.narrow))):
            fits = jnp.all(first - starts[i] + count <= self.narrow[i], axis=1)
            level = jnp.where(fits, i + 1, level)
        self.args = tuple(s.reshape(-1) for s in starts) + (level.reshape(-1),)


def _gather_kernel(*refs, ne, narrow, wide):
    narrow_refs = refs[:len(narrow)]
    ww_ref, level_ref, slot_ref, h_ref, o_ref = refs[len(narrow):]
    bi, gi, ki = pl.program_id(0), pl.program_id(1), pl.program_id(2)
    nt = pl.num_programs(2)
    group, tk = slot_ref.shape[1], slot_ref.shape[2]
    level = level_ref[bi * nt + ki]

    @pl.when(ki == 0)
    def _():
        o_ref[...] = jnp.zeros_like(o_ref)

    slot = slot_ref[0].astype(I32)

    def start(ref, j):
        return pl.multiple_of(ref[(bi * ne + gi * group + j) * nt + ki], BF16_ROWS)

    def onehot(j, ws, width):
        jcol = lax.broadcasted_iota(I32, (width, tk), 0)
        return jnp.where(jcol + ws == slot[j:j + 1, :], 1.0, 0.0).astype(BF16)

    def add(j, ws, width, rows_f32):
        rows = pl.ds(ws, width)
        o_ref[0, j, rows, :] = o_ref[0, j, rows, :] + rows_f32.astype(o_ref.dtype)

    for i, (wn_ref, width) in enumerate(zip(narrow_refs, narrow)):
        @pl.when(level == i + 1)
        def _(wn_ref=wn_ref, width=width):
            starts = [start(wn_ref, j) for j in range(group)]
            lhs = jnp.concatenate([onehot(j, starts[j], width) for j in range(group)], axis=0)
            res = _dot(lhs, h_ref[0])
            for j in range(group):
                add(j, starts[j], width, res[j * width:(j + 1) * width])

    @pl.when(level == 0)
    def _():
        for j in range(group):
            ws = start(ww_ref, j)
            add(j, ws, wide, _dot(onehot(j, ws, wide), h_ref[0]))


def moe_gather(slot, win, h, cap):
    b, e, n = slot.shape
    dx = h.shape[2]
    group = math.gcd(e, MOE_GROUP)
    grid_spec = pltpu.PrefetchScalarGridSpec(
        num_scalar_prefetch=len(win.args),
        grid=(b, e // group, win.nt),
        in_specs=[pl.BlockSpec((1, group, win.tile), lambda i, g, k, *_: (i, g, k)),
                  pl.BlockSpec((1, win.tile, dx), lambda i, g, k, *_: (i, k, 0))],
        out_specs=pl.BlockSpec((1, group, cap, dx), lambda i, g, k, *_: (i, g, 0, 0)))
    return pl.pallas_call(
        functools.partial(_gather_kernel, ne=e, narrow=win.narrow, wide=win.wide),
        grid_spec=grid_spec,
        out_shape=jax.ShapeDtypeStruct((b, e, cap, dx), BF16),
        compiler_params=_cparams("arbitrary", "arbitrary", "arbitrary"),
        name="moe_gather",
    )(*win.args, slot, h)


def _ffn_kernel(*refs, ns):
    x_refs = refs[:ns]
    wg_ref, wu_ref, wd_ref = refs[ns:ns + 3]
    o_refs = refs[ns + 3:2 * ns + 3]
    wgb_ref, wub_ref, wdb_ref = refs[2 * ns + 3:]

    @pl.when(pl.program_id(1) == 0)
    def _():
        wgb_ref[...] = wg_ref[0, 0].astype(BF16)
        wub_ref[...] = wu_ref[0, 0].astype(BF16)
        wdb_ref[...] = wd_ref[0, 0].astype(BF16)

    xe = jnp.concatenate([x_ref[i, 0] for x_ref in x_refs for i in range(x_ref.shape[0])], axis=0)
    d = wgb_ref.shape[0]
    x = xe[:, :d]
    ext = xe[:, d:].astype(F32)
    lane = lax.broadcasted_iota(I32, ext.shape, 1)
    ei, ne = pl.program_id(0), pl.num_programs(0)
    mine = (lane == ei) | (lane == ei + ne) | (lane == ei + 2 * ne)
    gate = jnp.sum(jnp.where(mine, ext, 0.0), axis=1, keepdims=True)
    rows = x.shape[0]
    bounds = [(r, min(r + FFN_ROWS, rows)) for r in range(0, rows, FFN_ROWS)]
    ups = [(_dot(x[r0:r1], wgb_ref[...]), _dot(x[r0:r1], wub_ref[...])) for r0, r1 in bounds]
    y = jnp.concatenate([_dot(((a / (1.0 + jnp.exp(-a))) * u).astype(BF16), wdb_ref[...]) for a, u in ups],
                        axis=0) * gate
    row = 0
    for o_ref in o_refs:
        for i in range(o_ref.shape[0]):
            rows = o_ref.shape[2]
            o_ref[i, 0] = y[row:row + rows].astype(o_ref.dtype)
            row += rows


def moe_ffn(xs_list, w_gate, w_up, w_down, layer):
    b, e, _, dx = xs_list[0].shape
    d, f = w_gate.shape[2], w_gate.shape[3]
    bb = 2 if b % 2 == 0 else 1

    def specs(width):
        return [pl.BlockSpec((bb, 1, xs.shape[2], width), lambda j, i: (i, j, 0, 0)) for xs in xs_list]

    return pl.pallas_call(
        functools.partial(_ffn_kernel, ns=len(xs_list)),
        grid=(e, b // bb),
        in_specs=specs(dx) + [pl.BlockSpec((1, 1, d, f), lambda j, i: (layer, j, 0, 0)),
                              pl.BlockSpec((1, 1, d, f), lambda j, i: (layer, j, 0, 0)),
                              pl.BlockSpec((1, 1, f, d), lambda j, i: (layer, j, 0, 0))],
        out_specs=specs(d),
        out_shape=[jax.ShapeDtypeStruct(xs.shape[:3] + (d,), BF16) for xs in xs_list],
        scratch_shapes=[pltpu.VMEM((d, f), BF16), pltpu.VMEM((d, f), BF16), pltpu.VMEM((f, d), BF16)],
        compiler_params=_cparams("arbitrary", "arbitrary"),
        name="moe_ffn",
    )(*xs_list, w_gate, w_up, w_down)


def _combine_kernel(*refs, ne, narrow, wide):
    narrow_refs = refs[:len(narrow)]
    ww_ref, level_ref, slot_ref, ys_hbm, lat_ref, gt_ref, o_ref, ys_buf, sem = refs[len(narrow):]
    bi, ti = pl.program_id(0), pl.program_id(1)
    nb, nt = pl.num_programs(0), pl.num_programs(1)
    cur = bi % 2

    def fetch(batch, buf):
        return pltpu.make_async_copy(ys_hbm.at[batch], ys_buf.at[buf], sem.at[buf])

    @pl.when(ti == 0)
    def _():
        @pl.when(bi == 0)
        def _():
            fetch(0, 0).start()

        fetch(bi, cur).wait()

        @pl.when(bi + 1 < nb)
        def _():
            fetch(bi + 1, 1 - cur).start()

    tn = lat_ref.shape[1]
    slot = slot_ref[0].astype(I32)

    def start(ref, e):
        return pl.multiple_of(ref[(bi * ne + e) * nt + ti], BF16_ROWS)

    def onehot(e, ws, width):
        jrow = lax.broadcasted_iota(I32, (tn, width), 1)
        return jnp.where(jrow + ws == slot[:, e:e + 1], 1.0, 0.0).astype(BF16)

    def onehot_lane_group(first, starts, width):
        lane = lax.broadcasted_iota(I32, (tn, LANES), 1)
        hit = None
        for i in range(LANES // width):
            e = first + i
            mine = (lane + (starts[e] - i * width) == slot[:, e:e + 1]) & (lane >= i * width) & (lane < (i + 1) * width)
            hit = mine if hit is None else hit | mine
        return jnp.where(hit, 1.0, 0.0).astype(BF16)

    def finish(acc):
        o_ref[0] = lat_ref[0] + gt_ref[0] * acc

    level = level_ref[bi * nt + ti]
    for i, (wn_ref, width) in enumerate(zip(narrow_refs, narrow)):
        @pl.when(level == i + 1)
        def _(wn_ref=wn_ref, width=width):
            starts = [start(wn_ref, e) for e in range(ne)]
            if width % LANES == 0:
                lhs = jnp.concatenate([onehot(e, starts[e], width) for e in range(ne)], axis=1)
            else:
                lhs = jnp.concatenate([onehot_lane_group(e, starts, width)
                                       for e in range(0, ne, LANES // width)], axis=1)
            rhs = jnp.concatenate([ys_buf[cur, e, pl.ds(starts[e], width), :] for e in range(ne)], axis=0)
            finish(_dot(lhs, rhs))

    @pl.when(level == 0)
    def _():
        acc = jnp.zeros((tn, lat_ref.shape[2]), F32)
        for e in range(ne):
            ws = start(ww_ref, e)
            acc = acc + _dot(onehot(e, ws, wide), ys_buf[cur, e, pl.ds(ws, wide), :])
        finish(acc)


def moe_combine(slot_t, win, ys, lat, gate_vec, cap):
    b, n, e = slot_t.shape
    d = lat.shape[2]
    tn = win.tile
    grid_spec = pltpu.PrefetchScalarGridSpec(
        num_scalar_prefetch=len(win.args),
        grid=(b, win.nt),
        in_specs=[pl.BlockSpec((1, tn, e), lambda i, j, *_: (i, j, 0)),
                  pl.BlockSpec(memory_space=pl.ANY),
                  pl.BlockSpec((1, tn, d), lambda i, j, *_: (i, j, 0)),
                  pl.BlockSpec((1, 1, d), lambda i, j, *_: (i, 0, 0))],
        out_specs=pl.BlockSpec((1, tn, d), lambda i, j, *_: (i, j, 0)),
        scratch_shapes=[pltpu.VMEM((2, e, cap, d), BF16), pltpu.SemaphoreType.DMA((2,))])
    return pl.pallas_call(
        functools.partial(_combine_kernel, ne=e, narrow=win.narrow, wide=win.wide),
        grid_spec=grid_spec,
        out_shape=jax.ShapeDtypeStruct((b, n, d), F32),
        compiler_params=_cparams("arbitrary", "arbitrary"),
        name="moe_combine",
    )(*win.args, slot_t, ys, lat, gate_vec)


def expert_choice_moe(streams, g, w_router, w_gate, w_up, w_down, layer):
    e = w_router.shape[1]
    routed = []
    for lat, sh, sc, _ in streams:
        cap = EC_CAPACITY_FACTOR * lat.shape[1] // e
        h, aff = moe_router(lat, g, sh, sc, w_router.T)
        slot, before, slot_t = moe_select(aff, cap)
        win = _Windows(before, lat.shape[1], cap)
        routed.append((moe_gather(slot, win, h, cap), slot_t, win, cap))
    ys_list = moe_ffn([r[0] for r in routed], w_gate, w_up, w_down, layer)
    return [moe_combine(slot_t, win, ys, lat, gate_vec, cap)
            for (lat, _, _, gate_vec), (_, slot_t, win, cap), ys in zip(streams, routed, ys_list)]


def kernel(x, c, ctx, c_ctx, ada_w, ada_b, norm_g, sc_w_in, sc_conv, sc_w_out, hy_w_in, hy_conv_w, hy_f_w1,
           hy_f_b1, hy_f_w2, hy_f_b2, hy_f_w3, hy_sin_freq, hy_skip, hy_w_out, at_w_qkv, at_q_g, at_k_g, at_w_o,
           moe_router_w, moe_w_gate, moe_w_up, moe_w_down):
    b, l, d = x.shape
    depth = ada_w.shape[0]
    assert b + 1 <= MOD_ROWS
    qd = at_w_o.shape[1]
    nq = qd // HEAD_DIM
    attn_layers = list(range(MIX_ATTN, depth, N_MIXERS))
    last_attn = max(attn_layers, default=-1)

    cond = jnp.zeros((MOD_ROWS, d), F32).at[:b].set(c).at[b].set(c_ctx)
    mods = ada_modulation(cond, ada_w, ada_b)

    lat, cx = x, ctx
    for i in range(depth):
        kind = i % N_MIXERS
        j = i // N_MIXERS
        need_ctx = i <= last_attn
        upd_ctx = i < last_attn
        m_l = mods[i, :b].reshape(b, 1, 6, d)
        sh1, sc1, g1, sh2, sc2, g2 = (m_l[:, :, k] for k in range(6))
        if need_ctx:
            m_c = jnp.broadcast_to(mods[i, b].reshape(1, 1, 6, d), (b, 1, 6, d))
            csh1, csc1, cg1, csh2, csc2, cg2 = (m_c[:, :, k] for k in range(6))
        streams = [(lat, sh1, sc1, g1)]
        if upd_ctx:
            streams.append((cx, csh1, csc1, cg1))

        outs = []
        if kind == MIX_SHORTCONV:
            w_in, w_out = sc_w_in[j].astype(BF16), sc_w_out[j].astype(BF16)
            for s, sh, sc, gt in streams:
                p = nm_matmul(s, norm_g[i, 0], sh, sc, w_in)
                outs.append(sc_out(p, sc_conv[j], w_out, s, gt))
        elif kind == MIX_HYENA:
            w_in, w_out = hy_w_in[j].astype(BF16), hy_w_out[j].astype(BF16)
            for s, sh, sc, gt in streams:
                p = nm_matmul(s, norm_g[i, 0], sh, sc, w_in)
                z = hyena_mixer(p, hy_conv_w[j], hy_f_w1[j], hy_f_b1[j], hy_f_w2[j], hy_f_b2[j], hy_f_w3[j],
                                hy_sin_freq[j], hy_skip[j])
                outs.append(proj_residual(z, w_out, s, gt))
        else:
            w_qkv, w_o = at_w_qkv[j].astype(BF16), at_w_o[j].astype(BF16)
            q_l, k_l, v_l = qkv_project(lat, norm_g[i, 0], sh1, sc1, w_qkv, at_q_g[j], at_k_g[j],
                                        _rope_tables(l), nq, True)
            lc = cx.shape[1]
            dummy = tuple(jnp.zeros((lc, HEAD_DIM), F32) for _ in range(3))
            if upd_ctx:
                q_c, k_c, v_c = qkv_project(cx, norm_g[i, 0], csh1, csc1, w_qkv, at_q_g[j], at_k_g[j],
                                            dummy, nq, False)
            else:
                k_c, v_c = qkv_project(cx, norm_g[i, 0], csh1, csc1, w_qkv[:, qd:], at_q_g[j], at_k_g[j],
                                       dummy, 0, False)
            v_all = jnp.concatenate([v_c, v_l], axis=1)
            o_l = attention(q_l, jnp.concatenate([k_c, k_l], axis=1), jnp.swapaxes(v_all, 1, 2))
            outs.append(proj_residual(o_l, w_o, lat, g1))
            if upd_ctx:
                outs.append(proj_residual(attention(q_c, k_c, jnp.swapaxes(v_c, 1, 2)), w_o, cx, cg1))

        moe_streams = [(outs[0], sh2, sc2, g2)]
        if upd_ctx:
            moe_streams.append((outs[1], csh2, csc2, cg2))
        moe_out = expert_choice_moe(moe_streams, norm_g[i, 1], moe_router_w[i], moe_w_gate, moe_w_up, moe_w_down, i)
        lat = moe_out[0]
        if upd_ctx:
            cx = moe_out[1]
    return lat
```

```python
import functools
import math

import jax
import jax.numpy as jnp
import numpy as np
from jax import lax
from jax.experimental import pallas as pl
from jax.experimental.pallas import tpu as pltpu

F32 = jnp.float32
BF16 = jnp.bfloat16
I32 = jnp.int32
HIGHEST = lax.Precision.HIGHEST

N_MIXERS = 3
MIX_SHORTCONV, MIX_HYENA, MIX_ATTN = 0, 1, 2
CONV_W = 3
EPS = 1e-6
HY_ORDER = 2
HY_EMB = 33
HY_BANDS = (HY_EMB - 1) // 2
HY_FAST_DECAY = 0.3
HY_SLOW_DECAY = 1.5
HY_TARGET = 1e-2
HEAD_DIM = 128
N_KV_HEADS = 2
GRID_W = 64
ROPE_THETA = 10000.0
EC_CAPACITY_FACTOR = 2
Q_PRESCALE = HEAD_DIM ** -0.5 * math.log2(math.e)

LANES = 128
SUBLANES = 8
BF16_ROWS = 16
VMEM_LIMIT_BYTES = 56 * 1024 * 1024
ROW_TILE = 1024
MOD_ROWS = 16
FFT_N2 = 128
HY_DT = LANES
MOE_TILE = 2 * LANES
MOE_WINDOWS = (LANES // 2, LANES)
MOE_GROUP = 2 * SUBLANES
FFN_ROWS = 2 * LANES
DFT_UNROLL = 4
DFT_UNROLL_STAGE1 = 8


def _cparams(*sem):
    return pltpu.CompilerParams(dimension_semantics=sem, vmem_limit_bytes=VMEM_LIMIT_BYTES)


def _row_tile(n, cap=ROW_TILE):
    t = min(n, cap)
    assert n % t == 0
    return t


def _dot(a, b):
    return jnp.dot(a, b, preferred_element_type=F32)


def _ada_kernel(s_ref, w_ref, b_ref, o_ref):
    s = s_ref[...]
    s = s / (1.0 + jnp.exp(-s))
    o_ref[0] = jnp.dot(s, w_ref[0], preferred_element_type=F32, precision=HIGHEST) + b_ref[0]


def ada_modulation(cond, ada_w, ada_b):
    depth, d, n = ada_w.shape
    tn = 1536 if n % 1536 == 0 else n
    return pl.pallas_call(
        _ada_kernel,
        grid=(depth, n // tn),
        in_specs=[pl.BlockSpec((MOD_ROWS, d), lambda i, j: (0, 0)),
                  pl.BlockSpec((1, d, tn), lambda i, j: (i, 0, j)),
                  pl.BlockSpec((1, 1, tn), lambda i, j: (i, 0, j))],
        out_specs=pl.BlockSpec((1, MOD_ROWS, tn), lambda i, j: (i, 0, j)),
        out_shape=jax.ShapeDtypeStruct((depth, MOD_ROWS, n), F32),
        compiler_params=_cparams("arbitrary", "arbitrary"),
        name="ada_modulation",
    )(cond, ada_w, ada_b.reshape(depth, 1, n))


def _norm_mod(x, g, sh, sc):
    y = x * lax.rsqrt(jnp.mean(x * x, axis=-1, keepdims=True) + EPS)
    return (y * g) * (1.0 + sc) + sh


def _nm_matmul_kernel(x_ref, g_ref, sh_ref, sc_ref, w_ref, o_ref):
    h = _norm_mod(x_ref[0], g_ref[...], sh_ref[0], sc_ref[0]).astype(BF16)
    o_ref[0] = _dot(h, w_ref[...]).astype(o_ref.dtype)


def nm_matmul(x, g, sh, sc, w, out_dtype=BF16):
    b, l, d = x.shape
    n = w.shape[1]
    tm = _row_tile(l)
    return pl.pallas_call(
        _nm_matmul_kernel,
        grid=(b, l // tm),
        in_specs=[pl.BlockSpec((1, tm, d), lambda i, j: (i, j, 0)),
                  pl.BlockSpec((1, d), lambda i, j: (0, 0)),
                  pl.BlockSpec((1, 1, d), lambda i, j: (i, 0, 0)),
                  pl.BlockSpec((1, 1, d), lambda i, j: (i, 0, 0)),
                  pl.BlockSpec((d, n), lambda i, j: (0, 0))],
        out_specs=pl.BlockSpec((1, tm, n), lambda i, j: (i, j, 0)),
        out_shape=jax.ShapeDtypeStruct((b, l, n), out_dtype),
        compiler_params=_cparams("arbitrary", "arbitrary"),
        name="nm_matmul",
    )(x, g.reshape(1, d), sh, sc, w)


def _proj_res_kernel(a_ref, w_ref, lat_ref, gt_ref, o_ref):
    o_ref[0] = lat_ref[0] + gt_ref[0] * _dot(a_ref[0], w_ref[...])


def proj_residual(a, w, lat, gate):
    b, l, k = a.shape
    d = w.shape[1]
    tm = _row_tile(l)
    return pl.pallas_call(
        _proj_res_kernel,
        grid=(b, l // tm),
        in_specs=[pl.BlockSpec((1, tm, k), lambda i, j: (i, j, 0)),
                  pl.BlockSpec((k, d), lambda i, j: (0, 0)),
                  pl.BlockSpec((1, tm, d), lambda i, j: (i, j, 0)),
                  pl.BlockSpec((1, 1, d), lambda i, j: (i, 0, 0))],
        out_specs=pl.BlockSpec((1, tm, d), lambda i, j: (i, j, 0)),
        out_shape=jax.ShapeDtypeStruct((b, l, d), F32),
        compiler_params=_cparams("arbitrary", "arbitrary"),
        name="proj_residual",
    )(a, w, lat, gate)


def _sc_out_kernel(pb_ref, pc_ref, pv_ref, cp_ref, vp_ref, cn_ref, vn_ref, cw_ref, w_ref, lat_ref, gt_ref,
                   o_ref, u_ref, *, tm):
    j = pl.program_id(1)
    nj = pl.num_programs(1)
    d = u_ref.shape[1]
    u_ref[pl.ds(SUBLANES, tm), :] = pc_ref[0].astype(F32) * pv_ref[0].astype(F32)
    prev = cp_ref[0, BF16_ROWS - 1:BF16_ROWS, :].astype(F32) * vp_ref[0, BF16_ROWS - 1:BF16_ROWS, :].astype(F32)
    nxt = cn_ref[0, 0:1, :].astype(F32) * vn_ref[0, 0:1, :].astype(F32)
    prev = jnp.where(j == 0, 0.0, prev)
    nxt = jnp.where(j == nj - 1, 0.0, nxt)
    u_ref[pl.ds(0, SUBLANES), :] = jnp.broadcast_to(prev, (SUBLANES, d))
    u_ref[pl.ds(SUBLANES + tm, SUBLANES), :] = jnp.broadcast_to(nxt, (SUBLANES, d))
    cw = cw_ref[...]
    y = (u_ref[pl.ds(SUBLANES - 1, tm), :] * cw[0:1] + u_ref[pl.ds(SUBLANES, tm), :] * cw[1:2]
         + u_ref[pl.ds(SUBLANES + 1, tm), :] * cw[2:3])
    m = (pb_ref[0].astype(F32) * y).astype(BF16)
    o_ref[0] = lat_ref[0] + gt_ref[0] * _dot(m, w_ref[...])


def sc_out(p, conv_w, w_out, lat, gate):
    b, l, d3 = p.shape
    d = d3 // 3
    tm = _row_tile(l)
    hb = tm // BF16_ROWS
    nhb = l // BF16_ROWS

    def prev_map(col):
        return lambda i, j: (i, jnp.maximum(j * hb - 1, 0), col)

    def next_map(col):
        return lambda i, j: (i, jnp.minimum((j + 1) * hb, nhb - 1), col)

    return pl.pallas_call(
        functools.partial(_sc_out_kernel, tm=tm),
        grid=(b, l // tm),
        in_specs=[pl.BlockSpec((1, tm, d), lambda i, j: (i, j, 0)),
                  pl.BlockSpec((1, tm, d), lambda i, j: (i, j, 1)),
                  pl.BlockSpec((1, tm, d), lambda i, j: (i, j, 2)),
                  pl.BlockSpec((1, BF16_ROWS, d), prev_map(1)),
                  pl.BlockSpec((1, BF16_ROWS, d), prev_map(2)),
                  pl.BlockSpec((1, BF16_ROWS, d), next_map(1)),
                  pl.BlockSpec((1, BF16_ROWS, d), next_map(2)),
                  pl.BlockSpec((CONV_W, d), lambda i, j: (0, 0)),
                  pl.BlockSpec((d, d), lambda i, j: (0, 0)),
                  pl.BlockSpec((1, tm, d), lambda i, j: (i, j, 0)),
                  pl.BlockSpec((1, 1, d), lambda i, j: (i, 0, 0))],
        out_specs=pl.BlockSpec((1, tm, d), lambda i, j: (i, j, 0)),
        out_shape=jax.ShapeDtypeStruct((b, l, d), F32),
        scratch_shapes=[pltpu.VMEM((tm + 2 * SUBLANES, d), F32)],
        compiler_params=_cparams("arbitrary", "arbitrary"),
        name="sc_out",
    )(p, p, p, p, p, p, p, conv_w, w_out, lat, gate)


def _hy_filter_kernel(z_ref, w1_ref, b1_ref, w2_ref, b2_ref, fr_ref, w3f_ref, w3b_ref, t_ref, dl_ref,
                      of_ref, ob_ref, a_ref):
    @pl.when((pl.program_id(0) == 0) & (pl.program_id(1) == 0))
    def _():
        fr = fr_ref[...]
        a1 = jnp.sin(fr * (jnp.dot(z_ref[...], w1_ref[...], preferred_element_type=F32, precision=HIGHEST)
                           + b1_ref[...]))
        a_ref[...] = jnp.sin(fr * (jnp.dot(a1, w2_ref[...], preferred_element_type=F32, precision=HIGHEST)
                                   + b2_ref[...]))

    a = a_ref[...]
    decay = jnp.exp(-t_ref[...] * dl_ref[...])
    hf = jnp.dot(a, w3f_ref[...], preferred_element_type=F32, precision=HIGHEST) * decay
    hb = jnp.dot(a, w3b_ref[...], preferred_element_type=F32, precision=HIGHEST) * decay
    row = lax.broadcasted_iota(I32, hb.shape, 0)
    hb = jnp.where(row == 0, 0.0, hb)
    inv = 1.0 / (jnp.sum(jnp.abs(hf), axis=0, keepdims=True) + jnp.sum(jnp.abs(hb), axis=0, keepdims=True))
    of_ref[0] = hf * inv
    ob_ref[0] = hb * inv


def hy_filter_taps(l, d, f_w1, f_b1, f_w2, f_b2, f_w3, sin_freq):
    hid = f_w1.shape[1]
    hp = LANES
    t = np.linspace(0.0, 1.0, l)[:, None]
    bands = np.linspace(1e-4, HY_BANDS - 1, HY_BANDS)[None, :]
    ang = (2.0 * math.pi / l) * np.arange(l)[:, None] * bands
    z = np.zeros((l, hp), np.float64)
    z[:, :HY_EMB] = np.concatenate([t, np.cos(ang), -np.sin(ang)], axis=-1)
    deltas = np.abs(np.linspace(math.log(HY_TARGET) / HY_SLOW_DECAY, math.log(HY_TARGET) / HY_FAST_DECAY, d))[None, :]

    def pad2(a, r, c):
        return jnp.zeros((r, c), F32).at[:a.shape[0], :a.shape[1]].set(a.astype(F32))

    w1 = pad2(f_w1, hp, hp)
    w2 = pad2(f_w2, hp, hp)
    w3 = pad2(f_w3, hp, f_w3.shape[1])
    b1 = pad2(f_b1[None], 1, hp)
    b2 = pad2(f_b2[None], 1, hp)
    fr = pad2(sin_freq[None], 1, hp)
    tc = min(d, 2 * LANES)
    nd = d // tc
    full = lambda o, j: (0, 0)
    return pl.pallas_call(
        _hy_filter_kernel,
        grid=(HY_ORDER, nd),
        in_specs=[pl.BlockSpec((l, hp), full), pl.BlockSpec((hp, hp), full), pl.BlockSpec((1, hp), full),
                  pl.BlockSpec((hp, hp), full), pl.BlockSpec((1, hp), full), pl.BlockSpec((1, hp), full),
                  pl.BlockSpec((hp, tc), lambda o, j: (0, o * nd + j)),
                  pl.BlockSpec((hp, tc), lambda o, j: (0, (HY_ORDER + o) * nd + j)),
                  pl.BlockSpec((l, 1), full),
                  pl.BlockSpec((1, tc), lambda o, j: (0, j))],
        out_specs=[pl.BlockSpec((1, l, tc), lambda o, j: (o, 0, j)),
                   pl.BlockSpec((1, l, tc), lambda o, j: (o, 0, j))],
        out_shape=[jax.ShapeDtypeStruct((HY_ORDER, l, d), F32)] * 2,
        scratch_shapes=[pltpu.VMEM((l, hp), F32)],
        compiler_params=_cparams("arbitrary", "arbitrary"),
        name="hy_filter_taps",
    )(jnp.asarray(z, F32), w1, b1, w2, b2, fr, w3, w3, jnp.asarray(t, F32), jnp.asarray(deltas, F32))


def _dft_consts(l):
    n = 2 * l
    if n <= 4 * FFT_N2:
        n1, n2 = 1, n
    else:
        n1, n2 = n // FFT_N2, FFT_N2
    if n1 == 1:
        k = np.arange(n)[:, None]
        t = np.arange(l)[None, :]
        ang = 2.0 * np.pi * (k * t % n) / n
        c, s = np.cos(ang), np.sin(ang)
        fwd = np.block([[c, s], [-s, c]])
        inv = np.block([[c.T, -s.T], [s.T, c.T]]) / n
        return n1, n2, dict(fwd=jnp.asarray(fwd, BF16), inv=jnp.asarray(inv, BF16))
    n1h = n1 // 2
    k1 = np.arange(n1)[None, :, None]
    m1 = np.arange(n1h)[None, None, :]
    r2 = np.arange(n2)[:, None, None]
    ang = 2.0 * np.pi * ((k1 * (n2 * m1 + r2)) % n) / n
    c, s = np.cos(ang), np.sin(ang)
    m1f = np.concatenate([np.concatenate([c, s], 2), np.concatenate([-s, c], 2)], 1)
    ct, st = np.swapaxes(c, 1, 2) / n, np.swapaxes(s, 1, 2) / n
    m1i = np.concatenate([np.concatenate([ct, -st], 2), np.concatenate([st, ct], 2)], 1)
    a2 = 2.0 * np.pi * ((np.arange(n2)[:, None] * np.arange(n2)[None, :]) % n2) / n2
    c2, s2 = np.cos(a2), np.sin(a2)
    f2 = np.block([[c2, s2], [-s2, c2]])
    f2i = np.block([[c2, -s2], [s2, c2]])

    def il(m):
        return np.stack([np.arange(m), m + np.arange(m)], 1).reshape(-1)

    m1f = m1f[:, il(n1), :][:, :, il(n1h)]
    m1i = m1i[:, il(n1h), :][:, :, il(n1)]
    f2 = f2[:, il(n2)]
    f2i = f2i[il(n2), :][:, il(n2)]
    half = n2 // 2
    m1f = np.concatenate([m1f[:half], m1f[half:]], 2)
    m1i = np.concatenate([m1i[:half], m1i[half:]], 2)
    return n1, n2, dict(m1f=jnp.asarray(m1f, BF16), m1i=jnp.asarray(m1i, BF16),
                        f2=jnp.asarray(f2, BF16), f2i=jnp.asarray(f2i, BF16))


def _blockdiag(a, b):
    z = jnp.zeros_like(a)
    return jnp.concatenate([jnp.concatenate([a, z], axis=1), jnp.concatenate([z, b], axis=1)], axis=0)


def _halves(y):
    return y[:, :LANES], y[:, LANES:]


HI_HALF = -65536


def _pack_pair(re, im):
    lo = (pltpu.bitcast(re.astype(BF16).astype(F32), I32) >> 16) & 0xFFFF
    if im is None:
        return lo
    return lo | (pltpu.bitcast(im.astype(BF16).astype(F32), I32) & HI_HALF)


def _unpack_pair(w):
    return pltpu.bitcast(w << 16, F32), pltpu.bitcast(w & HI_HALF, F32)


def _as_rows(w):
    return pltpu.bitcast(w, BF16)


def _as_words(y):
    return pltpu.bitcast(y.astype(BF16), I32)


def _fwd_stage1(x_pk, a_pk, m1f_ref, n1, n2):
    n1h, half = n1 // 2, n2 // 2

    def column(r):
        return _as_rows(x_pk[pl.ds(r, n1h, stride=n2), :])

    def body(r, carry):
        w = _as_words(_dot(m1f_ref[r], _blockdiag(column(r), column(r + half))))
        for rr, ww in zip((r, r + half), _halves(w)):
            a_pk[pl.ds(pl.multiple_of(rr * n1, n1), n1), :] = ww
        return carry

    lax.fori_loop(0, half, body, 0, unroll=min(half, DFT_UNROLL_STAGE1))


def _inv_stage1(b_pk, x_pk, m1i_ref, n1, n2):
    n1h, half = n1 // 2, n2 // 2

    def column(r):
        return _as_rows(b_pk[pl.ds(pl.multiple_of(r * n1, n1), n1), :])

    def body(r, carry):
        w = _as_words(_dot(m1i_ref[r], _blockdiag(column(r), column(r + half))))
        for rr, ww in zip((r, r + half), _halves(w)):
            x_pk[pl.ds(rr, n1h, stride=n2), :] = ww
        return carry

    lax.fori_loop(0, half, body, 0, unroll=min(half, DFT_UNROLL_STAGE1))


def _load_chunk(a_pk, k1, n1, n2):
    ra = pl.ds(k1, n2, stride=n1)
    rb = pl.ds(k1 + n1 // 2, n2, stride=n1)
    return jnp.concatenate([_as_rows(a_pk[ra, :]), _as_rows(a_pk[rb, :])], axis=1)


def _hy_spec_kernel(*refs, l, n1, n2, dense):
    if dense:
        tf_ref, tb_ref, sk_ref, fwd_ref, o_ref = refs
        n = n2
        fw = fwd_ref[:, 0:l]
        sf = _dot(fw, tf_ref[0].astype(BF16))
        sb = _dot(fw, tb_ref[0].astype(BF16))
        o_ref[0, 0] = (sf[:n] + sb[:n] + sk_ref[0]).astype(o_ref.dtype)
        o_ref[0, 1] = (sf[n:] - sb[n:]).astype(o_ref.dtype)
        return
    tf_ref, tb_ref, sk_ref, m1f_ref, f2_ref, o_ref, x_pk, a_pk, h_re, h_im = refs
    for run, t_ref in enumerate((tf_ref, tb_ref)):
        x_pk[...] = _pack_pair(t_ref[0], None)
        _fwd_stage1(x_pk, a_pk, m1f_ref, n1, n2)

        def body(k1, carry, run=run):
            x = _dot(f2_ref[...], _load_chunk(a_pk, k1, n1, n2))
            for kk, xx in zip((k1, k1 + n1 // 2), _halves(x)):
                rows = pl.ds(pl.multiple_of(kk * n2, n2), n2)
                if run == 0:
                    h_re[rows, :] = xx[:n2]
                    h_im[rows, :] = xx[n2:]
                else:
                    h_re[rows, :] = h_re[rows, :] + xx[:n2]
                    h_im[rows, :] = h_im[rows, :] - xx[n2:]
            return carry

        lax.fori_loop(0, n1 // 2, body, 0, unroll=DFT_UNROLL)
    o_ref[0, 0] = (h_re[...] + sk_ref[0]).astype(o_ref.dtype)
    o_ref[0, 1] = h_im[...].astype(o_ref.dtype)


def hy_spectrum(taps_f, taps_b, skip, consts, n1, n2):
    o, l, d = taps_f.shape
    n = 2 * l
    dt = HY_DT
    dense = n1 == 1
    tap_spec = pl.BlockSpec((1, l, dt), lambda i, j: (i, 0, j))
    in_specs = [tap_spec, tap_spec, pl.BlockSpec((1, 1, dt), lambda i, j: (i, 0, j))]
    if dense:
        mats = [consts["fwd"]]
        scratch = []
    else:
        mats = [consts["m1f"], consts["f2"]]
        scratch = [pltpu.VMEM((l, dt), I32), pltpu.VMEM((n, dt), I32)] + [pltpu.VMEM((n, dt), F32)] * 2
    in_specs += [pl.BlockSpec(m.shape, lambda i, j, nd=m.ndim: (0,) * nd) for m in mats]
    return pl.pallas_call(
        functools.partial(_hy_spec_kernel, l=l, n1=n1, n2=n2, dense=dense),
        grid=(o, d // dt),
        in_specs=in_specs,
        out_specs=pl.BlockSpec((1, 2, n, dt), lambda i, j: (i, 0, 0, j)),
        out_shape=jax.ShapeDtypeStruct((o, 2, n, d), BF16),
        scratch_shapes=scratch,
        compiler_params=_cparams("arbitrary", "arbitrary"),
        name="hy_spectrum",
    )(taps_f, taps_b, skip.reshape(o, 1, d), *mats)


def _conv3_rows(src, cw, pad_ref, l):
    c = pad_ref.shape[1]
    pad_ref[pl.ds(0, SUBLANES), :] = jnp.zeros((SUBLANES, c), F32)
    pad_ref[pl.ds(SUBLANES + l, SUBLANES), :] = jnp.zeros((SUBLANES, c), F32)
    pad_ref[pl.ds(SUBLANES, l), :] = src
    return (pad_ref[pl.ds(SUBLANES - 1, l), :] * cw[0:1] + pad_ref[pl.ds(SUBLANES, l), :] * cw[1:2]
            + pad_ref[pl.ds(SUBLANES + 1, l), :] * cw[2:3])


def _hy_conv_kernel(*refs, l, n1, n2, dense, conv_z):
    if dense:
        z_ref, g_ref, cwz_ref, cwg_ref, h_ref, fwd_ref, inv_ref, o_ref, pad_ref = refs
    else:
        (z_ref, g_ref, cwz_ref, cwg_ref, h_ref, m1f_ref, m1i_ref, f2_ref, f2i_ref, o_ref,
         x_pk, a_pk, b_pk, pad_ref) = refs
    zs = []
    for e in range(2):
        z = z_ref[e, 0].astype(F32)
        zs.append(_conv3_rows(z, cwz_ref[...], pad_ref, l) if conv_z else z)
    if dense:
        n = n2
        s = _dot(fwd_ref[...], jnp.concatenate(zs, axis=0).astype(BF16))
        hr = h_ref[0, 0].astype(F32)
        hi = h_ref[0, 1].astype(F32)
        zr = s[:n] * hr - s[n:] * hi
        zi = s[:n] * hi + s[n:] * hr
        y = _dot(inv_ref[...], jnp.concatenate([zr, zi], axis=0).astype(BF16))
        ys = (y[:l], y[l:])
    else:
        x_pk[...] = _pack_pair(zs[0], zs[1])
        _fwd_stage1(x_pk, a_pk, m1f_ref, n1, n2)

        nk = n1 // 2

        def spectrum_step(k1, carry):
            s = _dot(f2_ref[...], _load_chunk(a_pk, k1, n1, n2))
            rows = [pl.ds(pl.multiple_of(k * n2, n2), n2) for k in (k1, k1 + nk)]
            hr = jnp.concatenate([h_ref[0, 0, r, :] for r in rows], axis=1).astype(F32)
            hi = jnp.concatenate([h_ref[0, 1, r, :] for r in rows], axis=1).astype(F32)
            zr = s[:n2] * hr - s[n2:] * hi
            zi = s[:n2] * hi + s[n2:] * hr
            for r, zr_half, zi_half in zip(rows, _halves(zr), _halves(zi)):
                b_pk[r, :] = _pack_pair(zr_half, zi_half)
            return carry

        lax.fori_loop(0, nk, spectrum_step, 0, unroll=min(nk, DFT_UNROLL))

        def inverse_step(k1, carry):
            kk = (k1, k1 + nk)
            z = jnp.concatenate([_as_rows(b_pk[pl.ds(pl.multiple_of(k * n2, n2), n2), :]) for k in kk], axis=1)
            w = _as_words(_dot(f2i_ref[...], z))
            for k, ww in zip(kk, _halves(w)):
                a_pk[pl.ds(k, n2, stride=n1), :] = ww
            return carry

        lax.fori_loop(0, nk, inverse_step, 0, unroll=min(nk, DFT_UNROLL))
        _inv_stage1(a_pk, x_pk, m1i_ref, n1, n2)
        ys = _unpack_pair(x_pk[...])
    for e in range(2):
        g = _conv3_rows(g_ref[e, 0].astype(F32), cwg_ref[...], pad_ref, l)
        o_ref[e, 0] = (g * ys[e]).astype(o_ref.dtype)


def hy_conv(zsrc, zcol, p4, gcol, conv_w, spec, order, consts, n1, n2, conv_z):
    _, pairs, l, _ = zsrc.shape
    d = spec.shape[3]
    n = 2 * l
    dt = HY_DT
    nd = d // dt
    dense = n1 == 1
    once = pl.Buffered(1)
    in_specs = [pl.BlockSpec((2, 1, l, dt), lambda j, i: (0, i, 0, zcol * nd + j)),
                pl.BlockSpec((2, 1, l, dt), lambda j, i: (0, i, 0, gcol * nd + j)),
                pl.BlockSpec((CONV_W, dt), lambda j, i: (0, j)),
                pl.BlockSpec((CONV_W, dt), lambda j, i: (0, gcol * nd + j)),
                pl.BlockSpec((1, 2, n, dt), lambda j, i: (order, 0, 0, j), pipeline_mode=once)]
    if dense:
        mats = [consts["fwd"], consts["inv"]]
        scratch = []
    else:
        mats = [consts["m1f"], consts["m1i"], consts["f2"], consts["f2i"]]
        scratch = [pltpu.VMEM((l, dt), I32)] + [pltpu.VMEM((n, dt), I32)] * 2
    scratch = scratch + [pltpu.VMEM((l + 2 * SUBLANES, dt), F32)]
    in_specs += [pl.BlockSpec(m.shape, lambda j, i, nd_=m.ndim: (0,) * nd_, pipeline_mode=once) for m in mats]
    return pl.pallas_call(
        functools.partial(_hy_conv_kernel, l=l, n1=n1, n2=n2, dense=dense, conv_z=conv_z),
        grid=(nd, pairs),
        in_specs=in_specs,
        out_specs=pl.BlockSpec((2, 1, l, dt), lambda j, i: (0, i, 0, j)),
        out_shape=jax.ShapeDtypeStruct((2, pairs, l, d), BF16),
        scratch_shapes=scratch,
        compiler_params=_cparams("arbitrary", "arbitrary"),
        name="hy_conv",
    )(zsrc, p4, conv_w, conv_w, spec, *mats)


def hyena_mixer(p, conv_w, f_w1, f_b1, f_w2, f_b2, f_w3, sin_freq, skip):
    b, l, d3 = p.shape
    d = d3 // 3
    assert b % 2 == 0 and d % HY_DT == 0
    n1, n2, consts = _dft_consts(l)
    taps_f, taps_b = hy_filter_taps(l, d, f_w1, f_b1, f_w2, f_b2, f_w3, sin_freq)
    spec = hy_spectrum(taps_f, taps_b, skip, consts, n1, n2)
    p4 = p.reshape(2, b // 2, l, d3)
    z = hy_conv(p4, 0, p4, 1, conv_w, spec, 0, consts, n1, n2, True)
    z = hy_conv(z, 0, p4, 2, conv_w, spec, 1, consts, n1, n2, False)
    return z.reshape(b, l, d)


def _head_norm(x, g):
    sq = x * x
    hi = sq.astype(BF16)
    lo = (sq - hi.astype(F32)).astype(BF16)
    ones = jnp.ones((HEAD_DIM, HEAD_DIM), BF16)
    total = _dot(hi, ones) + _dot(lo, ones)
    return x * lax.rsqrt(total * (1.0 / HEAD_DIM) + EPS) * g


def _rope(x, cos, sin_a, sin_b):
    return (x * cos + pltpu.roll(x, HEAD_DIM // 4, 1) * sin_a
            + pltpu.roll(x, HEAD_DIM - HEAD_DIM // 4, 1) * sin_b)


def _qkv_kernel(x_ref, g_ref, sh_ref, sc_ref, w_ref, qg_ref, kg_ref, cos_ref, sa_ref, sb_ref,
                *out_refs, nq, rope):
    q_ref = out_refs[0] if nq else None
    k_ref, v_ref = out_refs[-2:]
    h = _norm_mod(x_ref[0], g_ref[...], sh_ref[0], sc_ref[0]).astype(BF16)
    acc = _dot(h, w_ref[...])
    hd = HEAD_DIM
    for i in range(nq + N_KV_HEADS):
        t = acc[:, i * hd:(i + 1) * hd]
        t = _head_norm(t, qg_ref[...] if i < nq else kg_ref[...])
        if rope:
            t = _rope(t, cos_ref[...], sa_ref[...], sb_ref[...])
        if i < nq:
            q_ref[0, :, i * hd:(i + 1) * hd] = (t * Q_PRESCALE).astype(BF16)
        else:
            k_ref[0, :, (i - nq) * hd:(i - nq + 1) * hd] = t.astype(BF16)
    v_ref[0] = acc[:, (nq + N_KV_HEADS) * hd:].astype(BF16)


def qkv_project(x, g, sh, sc, w, q_g, k_g, rope_tabs, nq, rope):
    b, l, d = x.shape
    n = w.shape[1]
    hd = HEAD_DIM
    kvd = N_KV_HEADS * hd
    tm = _row_tile(l)
    cos, sa, sb = rope_tabs
    tab = pl.BlockSpec((tm, hd), lambda i, j: (j, 0))
    vec = pl.BlockSpec((1, 1, d), lambda i, j: (i, 0, 0))
    widths = ([nq * hd] if nq else []) + [kvd, kvd]
    return pl.pallas_call(
        functools.partial(_qkv_kernel, nq=nq, rope=rope),
        grid=(b, l // tm),
        in_specs=[pl.BlockSpec((1, tm, d), lambda i, j: (i, j, 0)),
                  pl.BlockSpec((1, d), lambda i, j: (0, 0)), vec, vec,
                  pl.BlockSpec((d, n), lambda i, j: (0, 0)),
                  pl.BlockSpec((1, hd), lambda i, j: (0, 0)),
                  pl.BlockSpec((1, hd), lambda i, j: (0, 0)),
                  tab, tab, tab],
        out_specs=[pl.BlockSpec((1, tm, wd), lambda i, j: (i, j, 0)) for wd in widths],
        out_shape=[jax.ShapeDtypeStruct((b, l, wd), BF16) for wd in widths],
        compiler_params=_cparams("arbitrary", "arbitrary"),
        name="qkv_project",
    )(x, g.reshape(1, d), sh, sc, w, q_g.reshape(1, hd), k_g.reshape(1, hd), cos, sa, sb)


def _rope_tables(l):
    rows = l // GRID_W
    row = np.repeat(np.arange(rows, dtype=np.float64), GRID_W)
    col = np.tile(np.arange(GRID_W, dtype=np.float64), rows)
    axis = HEAD_DIM // 2
    inv = ROPE_THETA ** (-np.arange(0, axis, 2, dtype=np.float64) / axis)

    def axis_angles(pos):
        a = pos[:, None] * inv[None, :]
        return np.concatenate([a, a], axis=-1)

    ang = np.concatenate([axis_angles(row), axis_angles(col)], axis=-1)
    cos, sin = np.cos(ang), np.sin(ang)
    lane = np.arange(HEAD_DIM)[None, :] % axis
    upper = lane >= axis // 2
    sin_a = np.where(upper, sin, 0.0)
    sin_b = np.where(upper, 0.0, -sin)
    return tuple(jnp.asarray(t, F32) for t in (cos, sin_a, sin_b))


def _attn_kernel(q_ref, k_ref, vt_ref, o_ref, *, groups):
    k = k_ref[0]
    vt = vt_ref[0]
    hd = HEAD_DIM
    tq = q_ref.shape[1]
    hp = 2 if groups % 2 == 0 else 1
    def scores(g0):
        q = jnp.concatenate([q_ref[0, :, g * hd:(g + 1) * hd] for g in range(g0, g0 + hp)], axis=0)
        return lax.dot_general(k, q, (((1,), (1,)), ((), ())), preferred_element_type=F32)

    sts = [scores(g0) for g0 in range(0, groups, hp)]
    for g0, st in zip(range(0, groups, hp), sts):
        m = jnp.max(st, axis=0, keepdims=True)
        pt = jnp.exp2(st - m)
        den = jnp.sum(pt, axis=0, keepdims=True)
        ot = _dot(vt, pt.astype(BF16)) / den
        for i in range(hp):
            g = g0 + i
            o_ref[0, :, g * hd:(g + 1) * hd] = ot[:, i * tq:(i + 1) * tq].T.astype(o_ref.dtype)


def attention(q, k, v):
    b, l, qd = q.shape
    s = k.shape[1]
    hd = HEAD_DIM
    groups = qd // hd // N_KV_HEADS
    tq = _row_tile(l, 256)
    vt = jnp.swapaxes(v, 1, 2)
    return pl.pallas_call(
        functools.partial(_attn_kernel, groups=groups),
        grid=(b, N_KV_HEADS, l // tq),
        in_specs=[pl.BlockSpec((1, tq, groups * hd), lambda i, h, j: (i, j, h)),
                  pl.BlockSpec((1, s, hd), lambda i, h, j: (i, 0, h)),
                  pl.BlockSpec((1, hd, s), lambda i, h, j: (i, h, 0))],
        out_specs=pl.BlockSpec((1, tq, groups * hd), lambda i, h, j: (i, j, h)),
        out_shape=jax.ShapeDtypeStruct((b, l, qd), BF16),
        compiler_params=_cparams("arbitrary", "arbitrary", "arbitrary"),
        name="attention",
    )(q, k, vt)


def _router_kernel(x_ref, g_ref, sh_ref, sc_ref, wr_ref, h_ref, a_ref):
    h = _norm_mod(x_ref[0], g_ref[...], sh_ref[0], sc_ref[0])
    d = h.shape[1]
    logits = lax.dot_general(wr_ref[...], h, (((1,), (1,)), ((), ())), preferred_element_type=F32,
                             precision=HIGHEST)
    ex = jnp.exp(logits - jnp.max(logits, axis=0, keepdims=True))
    aff = ex / jnp.sum(ex, axis=0, keepdims=True)
    a_ref[0] = aff
    ne, tm = aff.shape
    hi = aff.astype(BF16).astype(F32)
    mid = (aff - hi).astype(BF16).astype(F32)
    lo = (aff - hi - mid).astype(BF16).astype(F32)
    pieces = jnp.concatenate([hi, mid, lo, jnp.zeros((LANES - 3 * ne, tm), F32)], axis=0)
    h_ref[0, :, 0:d] = h.astype(BF16)
    h_ref[0, :, d:d + LANES] = pieces.T.astype(BF16)


def moe_router(x, g, sh, sc, w_router_t):
    b, n, d = x.shape
    e = w_router_t.shape[0]
    assert 3 * e <= LANES and e % SUBLANES == 0
    tm = _row_tile(n)
    vec = pl.BlockSpec((1, 1, d), lambda i, j: (i, 0, 0))
    return pl.pallas_call(
        _router_kernel,
        grid=(b, n // tm),
        in_specs=[pl.BlockSpec((1, tm, d), lambda i, j: (i, j, 0)),
                  pl.BlockSpec((1, d), lambda i, j: (0, 0)), vec, vec,
                  pl.BlockSpec((e, d), lambda i, j: (0, 0))],
        out_specs=[pl.BlockSpec((1, tm, d + LANES), lambda i, j: (i, j, 0)),
                   pl.BlockSpec((1, e, tm), lambda i, j: (i, 0, j))],
        out_shape=[jax.ShapeDtypeStruct((b, n, d + LANES), BF16), jax.ShapeDtypeStruct((b, e, n), F32)],
        compiler_params=_cparams("arbitrary", "arbitrary"),
        name="moe_router",
    )(x, g.reshape(1, d), sh, sc, w_router_t)


def _cumsum_lanes(m, tri_ref, ind_ref):
    e, n = m.shape
    mb = m.astype(BF16)
    before = _dot(mb, ind_ref[...])
    outs = []
    for c in range(n // LANES):
        outs.append(_dot(mb[:, c * LANES:(c + 1) * LANES], tri_ref[...]) + before[:, c:c + 1])
    return jnp.concatenate(outs, axis=1), before


def _select_kernel(a_ref, tri_ref, ind_ref, slot_ref, before_ref, slot_t_ref, *, cap):
    aff = a_ref[0]
    bits = pltpu.bitcast(aff, I32)
    e = aff.shape[0]

    def body(i, thr):
        cand = thr | (1 << (30 - i))
        cnt = jnp.sum((bits >= cand).astype(F32), axis=1, keepdims=True)
        return jnp.where(cnt >= cap, cand, thr)

    thr = lax.fori_loop(0, 31, body, jnp.zeros((e, 1), I32))
    gt = bits > thr
    eq = bits == thr
    n_gt = jnp.sum(gt.astype(F32), axis=1, keepdims=True)
    eq_rank, _ = _cumsum_lanes(eq.astype(F32), tri_ref, ind_ref)
    sel = gt | (eq & (eq_rank <= cap - n_gt))
    pos, before = _cumsum_lanes(sel.astype(F32), tri_ref, ind_ref)
    slot = jnp.where(sel, pos - 1.0, -1.0)
    slot_ref[0] = slot
    before_ref[0] = before
    slot_t_ref[0] = slot.T


def moe_select(aff, cap):
    b, e, n = aff.shape
    assert n // LANES <= LANES
    tri = jnp.asarray(np.triu(np.ones((LANES, LANES), np.float32)), BF16)
    ind = jnp.asarray(np.arange(n)[:, None] < LANES * np.arange(LANES)[None, :], BF16)
    row = pl.BlockSpec((1, e, n), lambda i: (i, 0, 0))
    col = pl.BlockSpec((1, n, e), lambda i: (i, 0, 0))
    return pl.pallas_call(
        functools.partial(_select_kernel, cap=cap),
        grid=(b,),
        in_specs=[row, pl.BlockSpec((LANES, LANES), lambda i: (0, 0)), pl.BlockSpec((n, LANES), lambda i: (0, 0))],
        out_specs=[row, pl.BlockSpec((1, e, LANES), lambda i: (i, 0, 0)), col],
        out_shape=[jax.ShapeDtypeStruct((b, e, n), F32), jax.ShapeDtypeStruct((b, e, LANES), F32),
                   jax.ShapeDtypeStruct((b, n, e), F32)],
        compiler_params=_cparams("arbitrary"),
        name="moe_select",
    )(aff, tri, ind)


class _Windows:
    def __init__(self, before, n, cap):
        self.tile = min(n, MOE_TILE)
        self.nt = n // self.tile
        self.narrow = sorted({w for w in (min(cap, w) for w in MOE_WINDOWS)
                              if (w % LANES == 0 or LANES % w == 0) and w % BF16_ROWS == 0})
        self.wide = min(cap, self.tile + BF16_ROWS)
        first = before[:, :, 0:n // LANES:self.tile // LANES].astype(I32)
        count = jnp.concatenate([first[:, :, 1:], jnp.full_like(first[:, :, :1], cap)], axis=2) - first
        aligned = first // BF16_ROWS * BF16_ROWS
        starts = [jnp.clip(aligned, 0, cap - w) for w in self.narrow + [self.wide]]
        level = jnp.zeros(first.shape[:1] + first.shape[2:], I32)
        for i in reversed(range(len(self.narrow))):
            fits = jnp.all(first - starts[i] + count <= self.narrow[i], axis=1)
            level = jnp.where(fits, i + 1, level)
        self.args = tuple(s.reshape(-1) for s in starts) + (level.reshape(-1),)


def _gather_kernel(*refs, ne, narrow, wide):
    narrow_refs = refs[:len(narrow)]
    ww_ref, level_ref, slot_ref, h_ref, o_ref = refs[len(narrow):]
    bi, gi, ki = pl.program_id(0), pl.program_id(1), pl.program_id(2)
    nt = pl.num_programs(2)
    group, tk = slot_ref.shape[1], slot_ref.shape[2]
    level = level_ref[bi * nt + ki]

    @pl.when(ki == 0)
    def _():
        o_ref[...] = jnp.zeros_like(o_ref)

    slot = slot_ref[0].astype(I32)

    def start(ref, j):
        return pl.multiple_of(ref[(bi * ne + gi * group + j) * nt + ki], BF16_ROWS)

    def onehot(j, ws, width):
        jcol = lax.broadcasted_iota(I32, (width, tk), 0)
        return jnp.where(jcol + ws == slot[j:j + 1, :], 1.0, 0.0).astype(BF16)

    def add(j, ws, width, rows_f32):
        rows = pl.ds(ws, width)
        o_ref[0, j, rows, :] = o_ref[0, j, rows, :] + rows_f32.astype(o_ref.dtype)

    for i, (wn_ref, width) in enumerate(zip(narrow_refs, narrow)):
        @pl.when(level == i + 1)
        def _(wn_ref=wn_ref, width=width):
            starts = [start(wn_ref, j) for j in range(group)]
            lhs = jnp.concatenate([onehot(j, starts[j], width) for j in range(group)], axis=0)
            res = _dot(lhs, h_ref[0])
            for j in range(group):
                add(j, starts[j], width, res[j * width:(j + 1) * width])

    @pl.when(level == 0)
    def _():
        for j in range(group):
            ws = start(ww_ref, j)
            add(j, ws, wide, _dot(onehot(j, ws, wide), h_ref[0]))


def moe_gather(slot, win, h, cap):
    b, e, n = slot.shape
    dx = h.shape[2]
    group = math.gcd(e, MOE_GROUP)
    grid_spec = pltpu.PrefetchScalarGridSpec(
        num_scalar_prefetch=len(win.args),
        grid=(b, e // group, win.nt),
        in_specs=[pl.BlockSpec((1, group, win.tile), lambda i, g, k, *_: (i, g, k)),
                  pl.BlockSpec((1, win.tile, dx), lambda i, g, k, *_: (i, k, 0))],
        out_specs=pl.BlockSpec((1, group, cap, dx), lambda i, g, k, *_: (i, g, 0, 0)))
    return pl.pallas_call(
        functools.partial(_gather_kernel, ne=e, narrow=win.narrow, wide=win.wide),
        grid_spec=grid_spec,
        out_shape=jax.ShapeDtypeStruct((b, e, cap, dx), BF16),
        compiler_params=_cparams("arbitrary", "arbitrary", "arbitrary"),
        name="moe_gather",
    )(*win.args, slot, h)


def _ffn_kernel(*refs, ns):
    x_refs = refs[:ns]
    wg_ref, wu_ref, wd_ref = refs[ns:ns + 3]
    o_refs = refs[ns + 3:2 * ns + 3]
    wgb_ref, wub_ref, wdb_ref = refs[2 * ns + 3:]

    @pl.when(pl.program_id(1) == 0)
    def _():
        wgb_ref[...] = wg_ref[0, 0].astype(BF16)
        wub_ref[...] = wu_ref[0, 0].astype(BF16)
        wdb_ref[...] = wd_ref[0, 0].astype(BF16)

    xe = jnp.concatenate([x_ref[i, 0] for x_ref in x_refs for i in range(x_ref.shape[0])], axis=0)
    d = wgb_ref.shape[0]
    x = xe[:, :d]
    ext = xe[:, d:].astype(F32)
    lane = lax.broadcasted_iota(I32, ext.shape, 1)
    ei, ne = pl.program_id(0), pl.num_programs(0)
    mine = (lane == ei) | (lane == ei + ne) | (lane == ei + 2 * ne)
    gate = jnp.sum(jnp.where(mine, ext, 0.0), axis=1, keepdims=True)
    rows = x.shape[0]
    bounds = [(r, min(r + FFN_ROWS, rows)) for r in range(0, rows, FFN_ROWS)]
    ups = [(_dot(x[r0:r1], wgb_ref[...]), _dot(x[r0:r1], wub_ref[...])) for r0, r1 in bounds]
    y = jnp.concatenate([_dot(((a / (1.0 + jnp.exp(-a))) * u).astype(BF16), wdb_ref[...]) for a, u in ups],
                        axis=0) * gate
    row = 0
    for o_ref in o_refs:
        for i in range(o_ref.shape[0]):
            rows = o_ref.shape[2]
            o_ref[i, 0] = y[row:row + rows].astype(o_ref.dtype)
            row += rows


def moe_ffn(xs_list, w_gate, w_up, w_down, layer):
    b, e, _, dx = xs_list[0].shape
    d, f = w_gate.shape[2], w_gate.shape[3]
    bb = 2 if b % 2 == 0 else 1

    def specs(width):
        return [pl.BlockSpec((bb, 1, xs.shape[2], width), lambda j, i: (i, j, 0, 0)) for xs in xs_list]

    return pl.pallas_call(
        functools.partial(_ffn_kernel, ns=len(xs_list)),
        grid=(e, b // bb),
        in_specs=specs(dx) + [pl.BlockSpec((1, 1, d, f), lambda j, i: (layer, j, 0, 0)),
                              pl.BlockSpec((1, 1, d, f), lambda j, i: (layer, j, 0, 0)),
                              pl.BlockSpec((1, 1, f, d), lambda j, i: (layer, j, 0, 0))],
        out_specs=specs(d),
        out_shape=[jax.ShapeDtypeStruct(xs.shape[:3] + (d,), BF16) for xs in xs_list],
        scratch_shapes=[pltpu.VMEM((d, f), BF16), pltpu.VMEM((d, f), BF16), pltpu.VMEM((f, d), BF16)],
        compiler_params=_cparams("arbitrary", "arbitrary"),
        name="moe_ffn",
    )(*xs_list, w_gate, w_up, w_down)


def _combine_kernel(*refs, ne, narrow, wide):
    narrow_refs = refs[:len(narrow)]
    ww_ref, level_ref, slot_ref, ys_hbm, lat_ref, gt_ref, o_ref, ys_buf, sem = refs[len(narrow):]
    bi, ti = pl.program_id(0), pl.program_id(1)
    nb, nt = pl.num_programs(0), pl.num_programs(1)
    cur = bi % 2

    def fetch(batch, buf):
        return pltpu.make_async_copy(ys_hbm.at[batch], ys_buf.at[buf], sem.at[buf])

    @pl.when(ti == 0)
    def _():
        @pl.when(bi == 0)
        def _():
            fetch(0, 0).start()

        fetch(bi, cur).wait()

        @pl.when(bi + 1 < nb)
        def _():
            fetch(bi + 1, 1 - cur).start()

    tn = lat_ref.shape[1]
    slot = slot_ref[0].astype(I32)

    def start(ref, e):
        return pl.multiple_of(ref[(bi * ne + e) * nt + ti], BF16_ROWS)

    def onehot(e, ws, width):
        jrow = lax.broadcasted_iota(I32, (tn, width), 1)
        return jnp.where(jrow + ws == slot[:, e:e + 1], 1.0, 0.0).astype(BF16)

    def onehot_lane_group(first, starts, width):
        lane = lax.broadcasted_iota(I32, (tn, LANES), 1)
        target = slot[:, first:first + 1] - starts[first]
        for i in range(1, LANES // width):
            e = first + i
            target = jnp.where(lane >= i * width, slot[:, e:e + 1] - (starts[e] - i * width), target)
        return jnp.where(target == lane, 1.0, 0.0).astype(BF16)

    def finish(acc):
        o_ref[0] = lat_ref[0] + gt_ref[0] * acc

    level = level_ref[bi * nt + ti]
    for i, (wn_ref, width) in enumerate(zip(narrow_refs, narrow)):
        @pl.when(level == i + 1)
        def _(wn_ref=wn_ref, width=width):
            starts = [start(wn_ref, e) for e in range(ne)]
            if width % LANES == 0:
                lhs = jnp.concatenate([onehot(e, starts[e], width) for e in range(ne)], axis=1)
            else:
                lhs = jnp.concatenate([onehot_lane_group(e, starts, width)
                                       for e in range(0, ne, LANES // width)], axis=1)
            rhs = jnp.concatenate([ys_buf[cur, e, pl.ds(starts[e], width), :] for e in range(ne)], axis=0)
            finish(_dot(lhs, rhs))

    @pl.when(level == 0)
    def _():
        acc = jnp.zeros((tn, lat_ref.shape[2]), F32)
        for e in range(ne):
            ws = start(ww_ref, e)
            acc = acc + _dot(onehot(e, ws, wide), ys_buf[cur, e, pl.ds(ws, wide), :])
        finish(acc)


def moe_combine(slot_t, win, ys, lat, gate_vec, cap):
    b, n, e = slot_t.shape
    d = lat.shape[2]
    tn = win.tile
    grid_spec = pltpu.PrefetchScalarGridSpec(
        num_scalar_prefetch=len(win.args),
        grid=(b, win.nt),
        in_specs=[pl.BlockSpec((1, tn, e), lambda i, j, *_: (i, j, 0)),
                  pl.BlockSpec(memory_space=pl.ANY),
                  pl.BlockSpec((1, tn, d), lambda i, j, *_: (i, j, 0)),
                  pl.BlockSpec((1, 1, d), lambda i, j, *_: (i, 0, 0))],
        out_specs=pl.BlockSpec((1, tn, d), lambda i, j, *_: (i, j, 0)),
        scratch_shapes=[pltpu.VMEM((2, e, cap, d), BF16), pltpu.SemaphoreType.DMA((2,))])
    return pl.pallas_call(
        functools.partial(_combine_kernel, ne=e, narrow=win.narrow, wide=win.wide),
        grid_spec=grid_spec,
        out_shape=jax.ShapeDtypeStruct((b, n, d), F32),
        compiler_params=_cparams("arbitrary", "arbitrary"),
        name="moe_combine",
    )(*win.args, slot_t, ys, lat, gate_vec)


def expert_choice_moe(streams, g, w_router, w_gate, w_up, w_down, layer):
    e = w_router.shape[1]
    routed = []
    for lat, sh, sc, _ in streams:
        cap = EC_CAPACITY_FACTOR * lat.shape[1] // e
        h, aff = moe_router(lat, g, sh, sc, w_router.T)
        slot, before, slot_t = moe_select(aff, cap)
        win = _Windows(before, lat.shape[1], cap)
        routed.append((moe_gather(slot, win, h, cap), slot_t, win, cap))
    ys_list = moe_ffn([r[0] for r in routed], w_gate, w_up, w_down, layer)
    return [moe_combine(slot_t, win, ys, lat, gate_vec, cap)
            for (lat, _, _, gate_vec), (_, slot_t, win, cap), ys in zip(streams, routed, ys_list)]


def kernel(x, c, ctx, c_ctx, ada_w, ada_b, norm_g, sc_w_in, sc_conv, sc_w_out, hy_w_in, hy_conv_w, hy_f_w1,
           hy_f_b1, hy_f_w2, hy_f_b2, hy_f_w3, hy_sin_freq, hy_skip, hy_w_out, at_w_qkv, at_q_g, at_k_g, at_w_o,
           moe_router_w, moe_w_gate, moe_w_up, moe_w_down):
    b, l, d = x.shape
    depth = ada_w.shape[0]
    assert b + 1 <= MOD_ROWS
    qd = at_w_o.shape[1]
    nq = qd // HEAD_DIM
    attn_layers = list(range(MIX_ATTN, depth, N_MIXERS))
    last_attn = max(attn_layers, default=-1)

    cond = jnp.zeros((MOD_ROWS, d), F32).at[:b].set(c).at[b].set(c_ctx)
    mods = ada_modulation(cond, ada_w, ada_b)

    lat, cx = x, ctx
    for i in range(depth):
        kind = i % N_MIXERS
        j = i // N_MIXERS
        need_ctx = i <= last_attn
        upd_ctx = i < last_attn
        m_l = mods[i, :b].reshape(b, 1, 6, d)
        sh1, sc1, g1, sh2, sc2, g2 = (m_l[:, :, k] for k in range(6))
        if need_ctx:
            m_c = jnp.broadcast_to(mods[i, b].reshape(1, 1, 6, d), (b, 1, 6, d))
            csh1, csc1, cg1, csh2, csc2, cg2 = (m_c[:, :, k] for k in range(6))
        streams = [(lat, sh1, sc1, g1)]
        if upd_ctx:
            streams.append((cx, csh1, csc1, cg1))

        outs = []
        if kind == MIX_SHORTCONV:
            w_in, w_out = sc_w_in[j].astype(BF16), sc_w_out[j].astype(BF16)
            for s, sh, sc, gt in streams:
                p = nm_matmul(s, norm_g[i, 0], sh, sc, w_in)
                outs.append(sc_out(p, sc_conv[j], w_out, s, gt))
        elif kind == MIX_HYENA:
            w_in, w_out = hy_w_in[j].astype(BF16), hy_w_out[j].astype(BF16)
            for s, sh, sc, gt in streams:
                p = nm_matmul(s, norm_g[i, 0], sh, sc, w_in)
                z = hyena_mixer(p, hy_conv_w[j], hy_f_w1[j], hy_f_b1[j], hy_f_w2[j], hy_f_b2[j], hy_f_w3[j],
                                hy_sin_freq[j], hy_skip[j])
                outs.append(proj_residual(z, w_out, s, gt))
        else:
            w_qkv, w_o = at_w_qkv[j].astype(BF16), at_w_o[j].astype(BF16)
            q_l, k_l, v_l = qkv_project(lat, norm_g[i, 0], sh1, sc1, w_qkv, at_q_g[j], at_k_g[j],
                                        _rope_tables(l), nq, True)
            lc = cx.shape[1]
            dummy = tuple(jnp.zeros((lc, HEAD_DIM), F32) for _ in range(3))
            if upd_ctx:
                q_c, k_c, v_c = qkv_project(cx, norm_g[i, 0], csh1, csc1, w_qkv, at_q_g[j], at_k_g[j],
                                            dummy, nq, False)
            else:
                k_c, v_c = qkv_project(cx, norm_g[i, 0], csh1, csc1, w_qkv[:, qd:], at_q_g[j], at_k_g[j],
                                       dummy, 0, False)
            v_all = jnp.concatenate([v_c, v_l], axis=1)
            o_l = attention(q_l, jnp.concatenate([k_c, k_l], axis=1), v_all)
            outs.append(proj_residual(o_l, w_o, lat, g1))
            if upd_ctx:
                outs.append(proj_residual(attention(q_c, k_c, v_c), w_o, cx, cg1))

        moe_streams = [(outs[0], sh2, sc2, g2)]
        if upd_ctx:
            moe_streams.append((outs[1], csh2, csc2, cg2))
        moe_out = expert_choice_moe(moe_streams, norm_g[i, 1], moe_router_w[i], moe_w_gate, moe_w_up, moe_w_down, i)
        lat = moe_out[0]
        if upd_ctx:
            cx = moe_out[1]
    return lat
```

```python
import functools
import math

import jax
import jax.numpy as jnp
import numpy as np
from jax import lax
from jax.experimental import pallas as pl
from jax.experimental.pallas import tpu as pltpu

F32 = jnp.float32
BF16 = jnp.bfloat16
I32 = jnp.int32
HIGHEST = lax.Precision.HIGHEST

N_MIXERS = 3
MIX_SHORTCONV, MIX_HYENA, MIX_ATTN = 0, 1, 2
CONV_W = 3
EPS = 1e-6
HY_ORDER = 2
HY_EMB = 33
HY_BANDS = (HY_EMB - 1) // 2
HY_FAST_DECAY = 0.3
HY_SLOW_DECAY = 1.5
HY_TARGET = 1e-2
HEAD_DIM = 128
N_KV_HEADS = 2
GRID_W = 64
ROPE_THETA = 10000.0
EC_CAPACITY_FACTOR = 2
Q_PRESCALE = HEAD_DIM ** -0.5 * math.log2(math.e)

LANES = 128
SUBLANES = 8
BF16_ROWS = 16
VMEM_LIMIT_BYTES = 56 * 1024 * 1024
ROW_TILE = 1024
MOD_ROWS = 16
FFT_N2 = 128
HY_DT = LANES
MOE_TILE = 2 * LANES
MOE_WINDOWS = (LANES // 2, LANES)
MOE_GROUP = 2 * SUBLANES
FFN_ROWS = 2 * LANES
DFT_UNROLL = 8
DFT_UNROLL_STAGE1 = 16


def _cparams(*sem):
    return pltpu.CompilerParams(dimension_semantics=sem, vmem_limit_bytes=VMEM_LIMIT_BYTES)


def _row_tile(n, cap=ROW_TILE):
    t = min(n, cap)
    assert n % t == 0
    return t


def _dot(a, b):
    return jnp.dot(a, b, preferred_element_type=F32)


def _ada_kernel(s_ref, w_ref, b_ref, o_ref):
    s = s_ref[...]
    s = s / (1.0 + jnp.exp(-s))
    o_ref[0] = jnp.dot(s, w_ref[0], preferred_element_type=F32, precision=HIGHEST) + b_ref[0]


def ada_modulation(cond, ada_w, ada_b):
    depth, d, n = ada_w.shape
    tn = 1536 if n % 1536 == 0 else n
    return pl.pallas_call(
        _ada_kernel,
        grid=(depth, n // tn),
        in_specs=[pl.BlockSpec((MOD_ROWS, d), lambda i, j: (0, 0)),
                  pl.BlockSpec((1, d, tn), lambda i, j: (i, 0, j)),
                  pl.BlockSpec((1, 1, tn), lambda i, j: (i, 0, j))],
        out_specs=pl.BlockSpec((1, MOD_ROWS, tn), lambda i, j: (i, 0, j)),
        out_shape=jax.ShapeDtypeStruct((depth, MOD_ROWS, n), F32),
        compiler_params=_cparams("arbitrary", "arbitrary"),
        name="ada_modulation",
    )(cond, ada_w, ada_b.reshape(depth, 1, n))


def _norm_mod(x, g, sh, sc):
    y = x * lax.rsqrt(jnp.mean(x * x, axis=-1, keepdims=True) + EPS)
    return (y * g) * (1.0 + sc) + sh


def _nm_matmul_kernel(x_ref, g_ref, sh_ref, sc_ref, w_ref, o_ref):
    h = _norm_mod(x_ref[0], g_ref[...], sh_ref[0], sc_ref[0]).astype(BF16)
    o_ref[0] = _dot(h, w_ref[...]).astype(o_ref.dtype)


def nm_matmul(x, g, sh, sc, w, out_dtype=BF16):
    b, l, d = x.shape
    n = w.shape[1]
    tm = _row_tile(l)
    return pl.pallas_call(
        _nm_matmul_kernel,
        grid=(b, l // tm),
        in_specs=[pl.BlockSpec((1, tm, d), lambda i, j: (i, j, 0)),
                  pl.BlockSpec((1, d), lambda i, j: (0, 0)),
                  pl.BlockSpec((1, 1, d), lambda i, j: (i, 0, 0)),
                  pl.BlockSpec((1, 1, d), lambda i, j: (i, 0, 0)),
                  pl.BlockSpec((d, n), lambda i, j: (0, 0))],
        out_specs=pl.BlockSpec((1, tm, n), lambda i, j: (i, j, 0)),
        out_shape=jax.ShapeDtypeStruct((b, l, n), out_dtype),
        compiler_params=_cparams("arbitrary", "arbitrary"),
        name="nm_matmul",
    )(x, g.reshape(1, d), sh, sc, w)


def _proj_res_kernel(a_ref, w_ref, lat_ref, gt_ref, o_ref):
    o_ref[0] = lat_ref[0] + gt_ref[0] * _dot(a_ref[0], w_ref[...])


def proj_residual(a, w, lat, gate):
    b, l, k = a.shape
    d = w.shape[1]
    tm = _row_tile(l)
    return pl.pallas_call(
        _proj_res_kernel,
        grid=(b, l // tm),
        in_specs=[pl.BlockSpec((1, tm, k), lambda i, j: (i, j, 0)),
                  pl.BlockSpec((k, d), lambda i, j: (0, 0)),
                  pl.BlockSpec((1, tm, d), lambda i, j: (i, j, 0)),
                  pl.BlockSpec((1, 1, d), lambda i, j: (i, 0, 0))],
        out_specs=pl.BlockSpec((1, tm, d), lambda i, j: (i, j, 0)),
        out_shape=jax.ShapeDtypeStruct((b, l, d), F32),
        compiler_params=_cparams("arbitrary", "arbitrary"),
        name="proj_residual",
    )(a, w, lat, gate)


def _sc_out_kernel(pb_ref, pc_ref, pv_ref, cp_ref, vp_ref, cn_ref, vn_ref, cw_ref, w_ref, lat_ref, gt_ref,
                   o_ref, u_ref, *, tm):
    j = pl.program_id(1)
    nj = pl.num_programs(1)
    d = u_ref.shape[1]
    u_ref[pl.ds(SUBLANES, tm), :] = pc_ref[0].astype(F32) * pv_ref[0].astype(F32)
    prev = cp_ref[0, BF16_ROWS - 1:BF16_ROWS, :].astype(F32) * vp_ref[0, BF16_ROWS - 1:BF16_ROWS, :].astype(F32)
    nxt = cn_ref[0, 0:1, :].astype(F32) * vn_ref[0, 0:1, :].astype(F32)
    prev = jnp.where(j == 0, 0.0, prev)
    nxt = jnp.where(j == nj - 1, 0.0, nxt)
    u_ref[pl.ds(0, SUBLANES), :] = jnp.broadcast_to(prev, (SUBLANES, d))
    u_ref[pl.ds(SUBLANES + tm, SUBLANES), :] = jnp.broadcast_to(nxt, (SUBLANES, d))
    cw = cw_ref[...]
    y = (u_ref[pl.ds(SUBLANES - 1, tm), :] * cw[0:1] + u_ref[pl.ds(SUBLANES, tm), :] * cw[1:2]
         + u_ref[pl.ds(SUBLANES + 1, tm), :] * cw[2:3])
    m = (pb_ref[0].astype(F32) * y).astype(BF16)
    o_ref[0] = lat_ref[0] + gt_ref[0] * _dot(m, w_ref[...])


def sc_out(p, conv_w, w_out, lat, gate):
    b, l, d3 = p.shape
    d = d3 // 3
    tm = _row_tile(l)
    hb = tm // BF16_ROWS
    nhb = l // BF16_ROWS

    def prev_map(col):
        return lambda i, j: (i, jnp.maximum(j * hb - 1, 0), col)

    def next_map(col):
        return lambda i, j: (i, jnp.minimum((j + 1) * hb, nhb - 1), col)

    return pl.pallas_call(
        functools.partial(_sc_out_kernel, tm=tm),
        grid=(b, l // tm),
        in_specs=[pl.BlockSpec((1, tm, d), lambda i, j: (i, j, 0)),
                  pl.BlockSpec((1, tm, d), lambda i, j: (i, j, 1)),
                  pl.BlockSpec((1, tm, d), lambda i, j: (i, j, 2)),
                  pl.BlockSpec((1, BF16_ROWS, d), prev_map(1)),
                  pl.BlockSpec((1, BF16_ROWS, d), prev_map(2)),
                  pl.BlockSpec((1, BF16_ROWS, d), next_map(1)),
                  pl.BlockSpec((1, BF16_ROWS, d), next_map(2)),
                  pl.BlockSpec((CONV_W, d), lambda i, j: (0, 0)),
                  pl.BlockSpec((d, d), lambda i, j: (0, 0)),
                  pl.BlockSpec((1, tm, d), lambda i, j: (i, j, 0)),
                  pl.BlockSpec((1, 1, d), lambda i, j: (i, 0, 0))],
        out_specs=pl.BlockSpec((1, tm, d), lambda i, j: (i, j, 0)),
        out_shape=jax.ShapeDtypeStruct((b, l, d), F32),
        scratch_shapes=[pltpu.VMEM((tm + 2 * SUBLANES, d), F32)],
        compiler_params=_cparams("arbitrary", "arbitrary"),
        name="sc_out",
    )(p, p, p, p, p, p, p, conv_w, w_out, lat, gate)


def _hy_filter_kernel(z_ref, w1_ref, b1_ref, w2_ref, b2_ref, fr_ref, w3f_ref, w3b_ref, t_ref, dl_ref,
                      of_ref, ob_ref, a_ref):
    @pl.when((pl.program_id(0) == 0) & (pl.program_id(1) == 0))
    def _():
        fr = fr_ref[...]
        a1 = jnp.sin(fr * (jnp.dot(z_ref[...], w1_ref[...], preferred_element_type=F32, precision=HIGHEST)
                           + b1_ref[...]))
        a_ref[...] = jnp.sin(fr * (jnp.dot(a1, w2_ref[...], preferred_element_type=F32, precision=HIGHEST)
                                   + b2_ref[...]))

    a = a_ref[...]
    decay = jnp.exp(-t_ref[...] * dl_ref[...])
    hf = jnp.dot(a, w3f_ref[...], preferred_element_type=F32, precision=HIGHEST) * decay
    hb = jnp.dot(a, w3b_ref[...], preferred_element_type=F32, precision=HIGHEST) * decay
    row = lax.broadcasted_iota(I32, hb.shape, 0)
    hb = jnp.where(row == 0, 0.0, hb)
    inv = 1.0 / (jnp.sum(jnp.abs(hf), axis=0, keepdims=True) + jnp.sum(jnp.abs(hb), axis=0, keepdims=True))
    of_ref[0] = hf * inv
    ob_ref[0] = hb * inv


def hy_filter_taps(l, d, f_w1, f_b1, f_w2, f_b2, f_w3, sin_freq):
    hid = f_w1.shape[1]
    hp = LANES
    t = np.linspace(0.0, 1.0, l)[:, None]
    bands = np.linspace(1e-4, HY_BANDS - 1, HY_BANDS)[None, :]
    ang = (2.0 * math.pi / l) * np.arange(l)[:, None] * bands
    z = np.zeros((l, hp), np.float64)
    z[:, :HY_EMB] = np.concatenate([t, np.cos(ang), -np.sin(ang)], axis=-1)
    deltas = np.abs(np.linspace(math.log(HY_TARGET) / HY_SLOW_DECAY, math.log(HY_TARGET) / HY_FAST_DECAY, d))[None, :]

    def pad2(a, r, c):
        return jnp.zeros((r, c), F32).at[:a.shape[0], :a.shape[1]].set(a.astype(F32))

    w1 = pad2(f_w1, hp, hp)
    w2 = pad2(f_w2, hp, hp)
    w3 = pad2(f_w3, hp, f_w3.shape[1])
    b1 = pad2(f_b1[None], 1, hp)
    b2 = pad2(f_b2[None], 1, hp)
    fr = pad2(sin_freq[None], 1, hp)
    tc = min(d, 2 * LANES)
    nd = d // tc
    full = lambda o, j: (0, 0)
    return pl.pallas_call(
        _hy_filter_kernel,
        grid=(HY_ORDER, nd),
        in_specs=[pl.BlockSpec((l, hp), full), pl.BlockSpec((hp, hp), full), pl.BlockSpec((1, hp), full),
                  pl.BlockSpec((hp, hp), full), pl.BlockSpec((1, hp), full), pl.BlockSpec((1, hp), full),
                  pl.BlockSpec((hp, tc), lambda o, j: (0, o * nd + j)),
                  pl.BlockSpec((hp, tc), lambda o, j: (0, (HY_ORDER + o) * nd + j)),
                  pl.BlockSpec((l, 1), full),
                  pl.BlockSpec((1, tc), lambda o, j: (0, j))],
        out_specs=[pl.BlockSpec((1, l, tc), lambda o, j: (o, 0, j)),
                   pl.BlockSpec((1, l, tc), lambda o, j: (o, 0, j))],
        out_shape=[jax.ShapeDtypeStruct((HY_ORDER, l, d), F32)] * 2,
        scratch_shapes=[pltpu.VMEM((l, hp), F32)],
        compiler_params=_cparams("arbitrary", "arbitrary"),
        name="hy_filter_taps",
    )(jnp.asarray(z, F32), w1, b1, w2, b2, fr, w3, w3, jnp.asarray(t, F32), jnp.asarray(deltas, F32))


def _dft_consts(l):
    n = 2 * l
    if n <= 4 * FFT_N2:
        n1, n2 = 1, n
    else:
        n1, n2 = n // FFT_N2, FFT_N2
    if n1 == 1:
        k = np.arange(n)[:, None]
        t = np.arange(l)[None, :]
        ang = 2.0 * np.pi * (k * t % n) / n
        c, s = np.cos(ang), np.sin(ang)
        fwd = np.block([[c, s], [-s, c]])
        inv = np.block([[c.T, -s.T], [s.T, c.T]]) / n
        return n1, n2, dict(fwd=jnp.asarray(fwd, BF16), inv=jnp.asarray(inv, BF16))
    n1h = n1 // 2
    k1 = np.arange(n1)[None, :, None]
    m1 = np.arange(n1h)[None, None, :]
    r2 = np.arange(n2)[:, None, None]
    ang = 2.0 * np.pi * ((k1 * (n2 * m1 + r2)) % n) / n
    c, s = np.cos(ang), np.sin(ang)
    m1f = np.concatenate([np.concatenate([c, s], 2), np.concatenate([-s, c], 2)], 1)
    ct, st = np.swapaxes(c, 1, 2) / n, np.swapaxes(s, 1, 2) / n
    m1i = np.concatenate([np.concatenate([ct, -st], 2), np.concatenate([st, ct], 2)], 1)
    a2 = 2.0 * np.pi * ((np.arange(n2)[:, None] * np.arange(n2)[None, :]) % n2) / n2
    c2, s2 = np.cos(a2), np.sin(a2)
    f2 = np.block([[c2, s2], [-s2, c2]])
    f2i = np.block([[c2, -s2], [s2, c2]])

    def il(m):
        return np.stack([np.arange(m), m + np.arange(m)], 1).reshape(-1)

    m1f = m1f[:, il(n1), :][:, :, il(n1h)]
    m1i = m1i[:, il(n1h), :][:, :, il(n1)]
    f2 = f2[:, il(n2)]
    f2i = f2i[il(n2), :][:, il(n2)]
    half = n2 // 2
    m1f = np.concatenate([m1f[:half], m1f[half:]], 2)
    m1i = np.concatenate([m1i[:half], m1i[half:]], 2)
    return n1, n2, dict(m1f=jnp.asarray(m1f, BF16), m1i=jnp.asarray(m1i, BF16),
                        f2=jnp.asarray(f2, BF16), f2i=jnp.asarray(f2i, BF16))


def _blockdiag(a, b):
    z = jnp.zeros_like(a)
    return jnp.concatenate([jnp.concatenate([a, z], axis=1), jnp.concatenate([z, b], axis=1)], axis=0)


def _halves(y):
    return y[:, :LANES], y[:, LANES:]


HI_HALF = -65536


def _pack_pair(re, im):
    lo = (pltpu.bitcast(re.astype(BF16).astype(F32), I32) >> 16) & 0xFFFF
    if im is None:
        return lo
    return lo | (pltpu.bitcast(im.astype(BF16).astype(F32), I32) & HI_HALF)


def _unpack_pair(w):
    return pltpu.bitcast(w << 16, F32), pltpu.bitcast(w & HI_HALF, F32)


def _as_rows(w):
    return pltpu.bitcast(w, BF16)


def _as_words(y):
    return pltpu.bitcast(y.astype(BF16), I32)


def _fwd_stage1(x_pk, a_pk, m1f_ref, n1, n2):
    n1h, half = n1 // 2, n2 // 2

    def column(r):
        return _as_rows(x_pk[pl.ds(r, n1h, stride=n2), :])

    def body(r, carry):
        w = _as_words(_dot(m1f_ref[r], _blockdiag(column(r), column(r + half))))
        for rr, ww in zip((r, r + half), _halves(w)):
            a_pk[pl.ds(pl.multiple_of(rr * n1, n1), n1), :] = ww
        return carry

    lax.fori_loop(0, half, body, 0, unroll=min(half, DFT_UNROLL_STAGE1))


def _inv_stage1(b_pk, x_pk, m1i_ref, n1, n2):
    n1h, half = n1 // 2, n2 // 2

    def column(r):
        return _as_rows(b_pk[pl.ds(pl.multiple_of(r * n1, n1), n1), :])

    def body(r, carry):
        w = _as_words(_dot(m1i_ref[r], _blockdiag(column(r), column(r + half))))
        for rr, ww in zip((r, r + half), _halves(w)):
            x_pk[pl.ds(rr, n1h, stride=n2), :] = ww
        return carry

    lax.fori_loop(0, half, body, 0, unroll=min(half, DFT_UNROLL_STAGE1))


def _load_chunk(a_pk, k1, n1, n2):
    ra = pl.ds(k1, n2, stride=n1)
    rb = pl.ds(k1 + n1 // 2, n2, stride=n1)
    return jnp.concatenate([_as_rows(a_pk[ra, :]), _as_rows(a_pk[rb, :])], axis=1)


def _hy_spec_kernel(*refs, l, n1, n2, dense):
    if dense:
        tf_ref, tb_ref, sk_ref, fwd_ref, o_ref = refs
        n = n2
        fw = fwd_ref[:, 0:l]
        sf = _dot(fw, tf_ref[0].astype(BF16))
        sb = _dot(fw, tb_ref[0].astype(BF16))
        o_ref[0, 0] = (sf[:n] + sb[:n] + sk_ref[0]).astype(o_ref.dtype)
        o_ref[0, 1] = (sf[n:] - sb[n:]).astype(o_ref.dtype)
        return
    tf_ref, tb_ref, sk_ref, m1f_ref, f2_ref, o_ref, x_pk, a_pk, h_re, h_im = refs
    for run, t_ref in enumerate((tf_ref, tb_ref)):
        x_pk[...] = _pack_pair(t_ref[0], None)
        _fwd_stage1(x_pk, a_pk, m1f_ref, n1, n2)

        def body(k1, carry, run=run):
            x = _dot(f2_ref[...], _load_chunk(a_pk, k1, n1, n2))
            for kk, xx in zip((k1, k1 + n1 // 2), _halves(x)):
                rows = pl.ds(pl.multiple_of(kk * n2, n2), n2)
                if run == 0:
                    h_re[rows, :] = xx[:n2]
                    h_im[rows, :] = xx[n2:]
                else:
                    h_re[rows, :] = h_re[rows, :] + xx[:n2]
                    h_im[rows, :] = h_im[rows, :] - xx[n2:]
            return carry

        lax.fori_loop(0, n1 // 2, body, 0, unroll=DFT_UNROLL)
    o_ref[0, 0] = (h_re[...] + sk_ref[0]).astype(o_ref.dtype)
    o_ref[0, 1] = h_im[...].astype(o_ref.dtype)


def hy_spectrum(taps_f, taps_b, skip, consts, n1, n2):
    o, l, d = taps_f.shape
    n = 2 * l
    dt = HY_DT
    dense = n1 == 1
    tap_spec = pl.BlockSpec((1, l, dt), lambda i, j: (i, 0, j))
    in_specs = [tap_spec, tap_spec, pl.BlockSpec((1, 1, dt), lambda i, j: (i, 0, j))]
    if dense:
        mats = [consts["fwd"]]
        scratch = []
    else:
        mats = [consts["m1f"], consts["f2"]]
        scratch = [pltpu.VMEM((l, dt), I32), pltpu.VMEM((n, dt), I32)] + [pltpu.VMEM((n, dt), F32)] * 2
    in_specs += [pl.BlockSpec(m.shape, lambda i, j, nd=m.ndim: (0,) * nd) for m in mats]
    return pl.pallas_call(
        functools.partial(_hy_spec_kernel, l=l, n1=n1, n2=n2, dense=dense),
        grid=(o, d // dt),
        in_specs=in_specs,
        out_specs=pl.BlockSpec((1, 2, n, dt), lambda i, j: (i, 0, 0, j)),
        out_shape=jax.ShapeDtypeStruct((o, 2, n, d), BF16),
        scratch_shapes=scratch,
        compiler_params=_cparams("arbitrary", "arbitrary"),
        name="hy_spectrum",
    )(taps_f, taps_b, skip.reshape(o, 1, d), *mats)


def _conv3_rows(src, cw, pad_ref, l):
    c = pad_ref.shape[1]
    pad_ref[pl.ds(0, SUBLANES), :] = jnp.zeros((SUBLANES, c), F32)
    pad_ref[pl.ds(SUBLANES + l, SUBLANES), :] = jnp.zeros((SUBLANES, c), F32)
    pad_ref[pl.ds(SUBLANES, l), :] = src
    return (pad_ref[pl.ds(SUBLANES - 1, l), :] * cw[0:1] + pad_ref[pl.ds(SUBLANES, l), :] * cw[1:2]
            + pad_ref[pl.ds(SUBLANES + 1, l), :] * cw[2:3])


def _hy_conv_kernel(*refs, l, n1, n2, dense, conv_z):
    if dense:
        z_ref, g_ref, cwz_ref, cwg_ref, h_ref, fwd_ref, inv_ref, o_ref, pad_ref = refs
    else:
        (z_ref, g_ref, cwz_ref, cwg_ref, h_ref, m1f_ref, m1i_ref, f2_ref, f2i_ref, o_ref,
         x_pk, a_pk, b_pk, pad_ref) = refs
    zs = []
    for e in range(2):
        z = z_ref[e, 0].astype(F32)
        zs.append(_conv3_rows(z, cwz_ref[...], pad_ref, l) if conv_z else z)
    if dense:
        n = n2
        s = _dot(fwd_ref[...], jnp.concatenate(zs, axis=0).astype(BF16))
        hr = h_ref[0, 0].astype(F32)
        hi = h_ref[0, 1].astype(F32)
        zr = s[:n] * hr - s[n:] * hi
        zi = s[:n] * hi + s[n:] * hr
        y = _dot(inv_ref[...], jnp.concatenate([zr, zi], axis=0).astype(BF16))
        ys = (y[:l], y[l:])
    else:
        x_pk[...] = _pack_pair(zs[0], zs[1])
        _fwd_stage1(x_pk, a_pk, m1f_ref, n1, n2)

        nk = n1 // 2

        def spectrum_step(k1, carry):
            s = _dot(f2_ref[...], _load_chunk(a_pk, k1, n1, n2))
            rows = [pl.ds(pl.multiple_of(k * n2, n2), n2) for k in (k1, k1 + nk)]
            hr = jnp.concatenate([h_ref[0, 0, r, :] for r in rows], axis=1).astype(F32)
            hi = jnp.concatenate([h_ref[0, 1, r, :] for r in rows], axis=1).astype(F32)
            zr = s[:n2] * hr - s[n2:] * hi
            zi = s[:n2] * hi + s[n2:] * hr
            for r, zr_half, zi_half in zip(rows, _halves(zr), _halves(zi)):
                b_pk[r, :] = _pack_pair(zr_half, zi_half)
            return carry

        lax.fori_loop(0, nk, spectrum_step, 0, unroll=min(nk, DFT_UNROLL))

        def inverse_step(k1, carry):
            kk = (k1, k1 + nk)
            z = jnp.concatenate([_as_rows(b_pk[pl.ds(pl.multiple_of(k * n2, n2), n2), :]) for k in kk], axis=1)
            w = _as_words(_dot(f2i_ref[...], z))
            for k, ww in zip(kk, _halves(w)):
                a_pk[pl.ds(k, n2, stride=n1), :] = ww
            return carry

        lax.fori_loop(0, nk, inverse_step, 0, unroll=min(nk, DFT_UNROLL))
        _inv_stage1(a_pk, x_pk, m1i_ref, n1, n2)
        ys = _unpack_pair(x_pk[...])
    for e in range(2):
        g = _conv3_rows(g_ref[e, 0].astype(F32), cwg_ref[...], pad_ref, l)
        o_ref[e, 0] = (g * ys[e]).astype(o_ref.dtype)


def hy_conv(zsrc, zcol, p4, gcol, conv_w, spec, order, consts, n1, n2, conv_z):
    _, pairs, l, _ = zsrc.shape
    d = spec.shape[3]
    n = 2 * l
    dt = HY_DT
    nd = d // dt
    dense = n1 == 1
    once = pl.Buffered(1)
    in_specs = [pl.BlockSpec((2, 1, l, dt), lambda j, i: (0, i, 0, zcol * nd + j)),
                pl.BlockSpec((2, 1, l, dt), lambda j, i: (0, i, 0, gcol * nd + j)),
                pl.BlockSpec((CONV_W, dt), lambda j, i: (0, j)),
                pl.BlockSpec((CONV_W, dt), lambda j, i: (0, gcol * nd + j)),
                pl.BlockSpec((1, 2, n, dt), lambda j, i: (order, 0, 0, j), pipeline_mode=once)]
    if dense:
        mats = [consts["fwd"], consts["inv"]]
        scratch = []
    else:
        mats = [consts["m1f"], consts["m1i"], consts["f2"], consts["f2i"]]
        scratch = [pltpu.VMEM((l, dt), I32)] + [pltpu.VMEM((n, dt), I32)] * 2
    scratch = scratch + [pltpu.VMEM((l + 2 * SUBLANES, dt), F32)]
    in_specs += [pl.BlockSpec(m.shape, lambda j, i, nd_=m.ndim: (0,) * nd_, pipeline_mode=once) for m in mats]
    return pl.pallas_call(
        functools.partial(_hy_conv_kernel, l=l, n1=n1, n2=n2, dense=dense, conv_z=conv_z),
        grid=(nd, pairs),
        in_specs=in_specs,
        out_specs=pl.BlockSpec((2, 1, l, dt), lambda j, i: (0, i, 0, j)),
        out_shape=jax.ShapeDtypeStruct((2, pairs, l, d), BF16),
        scratch_shapes=scratch,
        compiler_params=_cparams("arbitrary", "arbitrary"),
        name="hy_conv",
    )(zsrc, p4, conv_w, conv_w, spec, *mats)


def hyena_mixer(p, conv_w, f_w1, f_b1, f_w2, f_b2, f_w3, sin_freq, skip):
    b, l, d3 = p.shape
    d = d3 // 3
    assert b % 2 == 0 and d % HY_DT == 0
    n1, n2, consts = _dft_consts(l)
    taps_f, taps_b = hy_filter_taps(l, d, f_w1, f_b1, f_w2, f_b2, f_w3, sin_freq)
    spec = hy_spectrum(taps_f, taps_b, skip, consts, n1, n2)
    p4 = p.reshape(2, b // 2, l, d3)
    z = hy_conv(p4, 0, p4, 1, conv_w, spec, 0, consts, n1, n2, True)
    z = hy_conv(z, 0, p4, 2, conv_w, spec, 1, consts, n1, n2, False)
    return z.reshape(b, l, d)


def _head_norm(x, g):
    sq = x * x
    hi = sq.astype(BF16)
    lo = (sq - hi.astype(F32)).astype(BF16)
    ones = jnp.ones((HEAD_DIM, HEAD_DIM), BF16)
    total = _dot(hi, ones) + _dot(lo, ones)
    return x * lax.rsqrt(total * (1.0 / HEAD_DIM) + EPS) * g


def _rope(x, cos, sin_a, sin_b):
    return (x * cos + pltpu.roll(x, HEAD_DIM // 4, 1) * sin_a
            + pltpu.roll(x, HEAD_DIM - HEAD_DIM // 4, 1) * sin_b)


def _qkv_kernel(x_ref, g_ref, sh_ref, sc_ref, w_ref, qg_ref, kg_ref, cos_ref, sa_ref, sb_ref,
                *out_refs, nq, rope):
    q_ref = out_refs[0] if nq else None
    k_ref, v_ref = out_refs[-2:]
    h = _norm_mod(x_ref[0], g_ref[...], sh_ref[0], sc_ref[0]).astype(BF16)
    acc = _dot(h, w_ref[...])
    hd = HEAD_DIM
    for i in range(nq + N_KV_HEADS):
        t = acc[:, i * hd:(i + 1) * hd]
        t = _head_norm(t, qg_ref[...] if i < nq else kg_ref[...])
        if rope:
            t = _rope(t, cos_ref[...], sa_ref[...], sb_ref[...])
        if i < nq:
            q_ref[0, :, i * hd:(i + 1) * hd] = (t * Q_PRESCALE).astype(BF16)
        else:
            k_ref[0, :, (i - nq) * hd:(i - nq + 1) * hd] = t.astype(BF16)
    v_ref[0] = acc[:, (nq + N_KV_HEADS) * hd:].astype(BF16)


def qkv_project(x, g, sh, sc, w, q_g, k_g, rope_tabs, nq, rope):
    b, l, d = x.shape
    n = w.shape[1]
    hd = HEAD_DIM
    kvd = N_KV_HEADS * hd
    tm = _row_tile(l)
    cos, sa, sb = rope_tabs
    tab = pl.BlockSpec((tm, hd), lambda i, j: (j, 0))
    vec = pl.BlockSpec((1, 1, d), lambda i, j: (i, 0, 0))
    widths = ([nq * hd] if nq else []) + [kvd, kvd]
    return pl.pallas_call(
        functools.partial(_qkv_kernel, nq=nq, rope=rope),
        grid=(b, l // tm),
        in_specs=[pl.BlockSpec((1, tm, d), lambda i, j: (i, j, 0)),
                  pl.BlockSpec((1, d), lambda i, j: (0, 0)), vec, vec,
                  pl.BlockSpec((d, n), lambda i, j: (0, 0)),
                  pl.BlockSpec((1, hd), lambda i, j: (0, 0)),
                  pl.BlockSpec((1, hd), lambda i, j: (0, 0)),
                  tab, tab, tab],
        out_specs=[pl.BlockSpec((1, tm, wd), lambda i, j: (i, j, 0)) for wd in widths],
        out_shape=[jax.ShapeDtypeStruct((b, l, wd), BF16) for wd in widths],
        compiler_params=_cparams("arbitrary", "arbitrary"),
        name="qkv_project",
    )(x, g.reshape(1, d), sh, sc, w, q_g.reshape(1, hd), k_g.reshape(1, hd), cos, sa, sb)


def _rope_tables(l):
    rows = l // GRID_W
    row = np.repeat(np.arange(rows, dtype=np.float64), GRID_W)
    col = np.tile(np.arange(GRID_W, dtype=np.float64), rows)
    axis = HEAD_DIM // 2
    inv = ROPE_THETA ** (-np.arange(0, axis, 2, dtype=np.float64) / axis)

    def axis_angles(pos):
        a = pos[:, None] * inv[None, :]
        return np.concatenate([a, a], axis=-1)

    ang = np.concatenate([axis_angles(row), axis_angles(col)], axis=-1)
    cos, sin = np.cos(ang), np.sin(ang)
    lane = np.arange(HEAD_DIM)[None, :] % axis
    upper = lane >= axis // 2
    sin_a = np.where(upper, sin, 0.0)
    sin_b = np.where(upper, 0.0, -sin)
    return tuple(jnp.asarray(t, F32) for t in (cos, sin_a, sin_b))


def _attn_kernel(q_ref, k_ref, vt_ref, o_ref, *, groups):
    k = k_ref[0]
    vt = vt_ref[0]
    hd = HEAD_DIM
    tq = q_ref.shape[1]
    hp = 2 if groups % 2 == 0 else 1
    def scores(g0):
        q = jnp.concatenate([q_ref[0, :, g * hd:(g + 1) * hd] for g in range(g0, g0 + hp)], axis=0)
        return lax.dot_general(k, q, (((1,), (1,)), ((), ())), preferred_element_type=F32)

    sts = [scores(g0) for g0 in range(0, groups, hp)]
    for g0, st in zip(range(0, groups, hp), sts):
        m = jnp.max(st, axis=0, keepdims=True)
        pt = jnp.exp2(st - m)
        den = jnp.sum(pt, axis=0, keepdims=True)
        ot = _dot(vt, pt.astype(BF16)) / den
        for i in range(hp):
            g = g0 + i
            o_ref[0, :, g * hd:(g + 1) * hd] = ot[:, i * tq:(i + 1) * tq].T.astype(o_ref.dtype)


def attention(q, k, v):
    b, l, qd = q.shape
    s = k.shape[1]
    hd = HEAD_DIM
    groups = qd // hd // N_KV_HEADS
    tq = _row_tile(l, 256)
    vt = jnp.swapaxes(v, 1, 2)
    return pl.pallas_call(
        functools.partial(_attn_kernel, groups=groups),
        grid=(b, N_KV_HEADS, l // tq),
        in_specs=[pl.BlockSpec((1, tq, groups * hd), lambda i, h, j: (i, j, h)),
                  pl.BlockSpec((1, s, hd), lambda i, h, j: (i, 0, h)),
                  pl.BlockSpec((1, hd, s), lambda i, h, j: (i, h, 0))],
        out_specs=pl.BlockSpec((1, tq, groups * hd), lambda i, h, j: (i, j, h)),
        out_shape=jax.ShapeDtypeStruct((b, l, qd), BF16),
        compiler_params=_cparams("arbitrary", "arbitrary", "arbitrary"),
        name="attention",
    )(q, k, vt)


def _router_kernel(x_ref, g_ref, sh_ref, sc_ref, wr_ref, h_ref, a_ref):
    h = _norm_mod(x_ref[0], g_ref[...], sh_ref[0], sc_ref[0])
    d = h.shape[1]
    logits = lax.dot_general(wr_ref[...], h, (((1,), (1,)), ((), ())), preferred_element_type=F32,
                             precision=HIGHEST)
    ex = jnp.exp(logits - jnp.max(logits, axis=0, keepdims=True))
    aff = ex / jnp.sum(ex, axis=0, keepdims=True)
    a_ref[0] = aff
    ne, tm = aff.shape
    hi = aff.astype(BF16).astype(F32)
    mid = (aff - hi).astype(BF16).astype(F32)
    lo = (aff - hi - mid).astype(BF16).astype(F32)
    pieces = jnp.concatenate([hi, mid, lo, jnp.zeros((LANES - 3 * ne, tm), F32)], axis=0)
    h_ref[0, :, 0:d] = h.astype(BF16)
    h_ref[0, :, d:d + LANES] = pieces.T.astype(BF16)


def moe_router(x, g, sh, sc, w_router_t):
    b, n, d = x.shape
    e = w_router_t.shape[0]
    assert 3 * e <= LANES and e % SUBLANES == 0
    tm = _row_tile(n)
    vec = pl.BlockSpec((1, 1, d), lambda i, j: (i, 0, 0))
    return pl.pallas_call(
        _router_kernel,
        grid=(b, n // tm),
        in_specs=[pl.BlockSpec((1, tm, d), lambda i, j: (i, j, 0)),
                  pl.BlockSpec((1, d), lambda i, j: (0, 0)), vec, vec,
                  pl.BlockSpec((e, d), lambda i, j: (0, 0))],
        out_specs=[pl.BlockSpec((1, tm, d + LANES), lambda i, j: (i, j, 0)),
                   pl.BlockSpec((1, e, tm), lambda i, j: (i, 0, j))],
        out_shape=[jax.ShapeDtypeStruct((b, n, d + LANES), BF16), jax.ShapeDtypeStruct((b, e, n), F32)],
        compiler_params=_cparams("arbitrary", "arbitrary"),
        name="moe_router",
    )(x, g.reshape(1, d), sh, sc, w_router_t)


def _cumsum_lanes(m, tri_ref, ind_ref):
    e, n = m.shape
    mb = m.astype(BF16)
    before = _dot(mb, ind_ref[...])
    outs = []
    for c in range(n // LANES):
        outs.append(_dot(mb[:, c * LANES:(c + 1) * LANES], tri_ref[...]) + before[:, c:c + 1])
    return jnp.concatenate(outs, axis=1), before


def _select_kernel(a_ref, tri_ref, ind_ref, slot_ref, before_ref, slot_t_ref, *, cap):
    aff = a_ref[0]
    bits = pltpu.bitcast(aff, I32)
    e = aff.shape[0]

    def body(i, thr):
        cand = thr | (1 << (30 - i))
        cnt = jnp.sum((bits >= cand).astype(F32), axis=1, keepdims=True)
        return jnp.where(cnt >= cap, cand, thr)

    thr = lax.fori_loop(0, 31, body, jnp.zeros((e, 1), I32))
    gt = bits > thr
    eq = bits == thr
    n_gt = jnp.sum(gt.astype(F32), axis=1, keepdims=True)
    eq_rank, _ = _cumsum_lanes(eq.astype(F32), tri_ref, ind_ref)
    sel = gt | (eq & (eq_rank <= cap - n_gt))
    pos, before = _cumsum_lanes(sel.astype(F32), tri_ref, ind_ref)
    slot = jnp.where(sel, pos - 1.0, -1.0)
    slot_ref[0] = slot
    before_ref[0] = before
    slot_t_ref[0] = slot.T


def moe_select(aff, cap):
    b, e, n = aff.shape
    assert n // LANES <= LANES
    tri = jnp.asarray(np.triu(np.ones((LANES, LANES), np.float32)), BF16)
    ind = jnp.asarray(np.arange(n)[:, None] < LANES * np.arange(LANES)[None, :], BF16)
    row = pl.BlockSpec((1, e, n), lambda i: (i, 0, 0))
    col = pl.BlockSpec((1, n, e), lambda i: (i, 0, 0))
    return pl.pallas_call(
        functools.partial(_select_kernel, cap=cap),
        grid=(b,),
        in_specs=[row, pl.BlockSpec((LANES, LANES), lambda i: (0, 0)), pl.BlockSpec((n, LANES), lambda i: (0, 0))],
        out_specs=[row, pl.BlockSpec((1, e, LANES), lambda i: (i, 0, 0)), col],
        out_shape=[jax.ShapeDtypeStruct((b, e, n), F32), jax.ShapeDtypeStruct((b, e, LANES), F32),
                   jax.ShapeDtypeStruct((b, n, e), F32)],
        compiler_params=_cparams("arbitrary"),
        name="moe_select",
    )(aff, tri, ind)


class _Windows:
    def __init__(self, before, n, cap):
        self.tile = min(n, MOE_TILE)
        self.nt = n // self.tile
        self.narrow = sorted({w for w in (min(cap, w) for w in MOE_WINDOWS)
                              if (w % LANES == 0 or LANES % w == 0) and w % BF16_ROWS == 0})
        self.wide = min(cap, self.tile + BF16_ROWS)
        first = before[:, :, 0:n // LANES:self.tile // LANES].astype(I32)
        count = jnp.concatenate([first[:, :, 1:], jnp.full_like(first[:, :, :1], cap)], axis=2) - first
        aligned = first // BF16_ROWS * BF16_ROWS
        starts = [jnp.clip(aligned, 0, cap - w) for w in self.narrow + [self.wide]]
        level = jnp.zeros(first.shape[:1] + first.shape[2:], I32)
        for i in reversed(range(len(self.narrow))):
            fits = jnp.all(first - starts[i] + count <= self.narrow[i], axis=1)
            level = jnp.where(fits, i + 1, level)
        self.args = tuple(s.reshape(-1) for s in starts) + (level.reshape(-1),)


def _gather_kernel(*refs, ne, narrow, wide):
    narrow_refs = refs[:len(narrow)]
    ww_ref, level_ref, slot_ref, h_ref, o_ref = refs[len(narrow):]
    bi, gi, ki = pl.program_id(0), pl.program_id(1), pl.program_id(2)
    nt = pl.num_programs(2)
    group, tk = slot_ref.shape[1], slot_ref.shape[2]
    level = level_ref[bi * nt + ki]

    @pl.when(ki == 0)
    def _():
        o_ref[...] = jnp.zeros_like(o_ref)

    slot = slot_ref[0].astype(I32)

    def start(ref, j):
        return pl.multiple_of(ref[(bi * ne + gi * group + j) * nt + ki], BF16_ROWS)

    def onehot(j, ws, width):
        jcol = lax.broadcasted_iota(I32, (width, tk), 0)
        return jnp.where(jcol + ws == slot[j:j + 1, :], 1.0, 0.0).astype(BF16)

    def add(j, ws, width, rows_f32):
        rows = pl.ds(ws, width)
        o_ref[0, j, rows, :] = o_ref[0, j, rows, :] + rows_f32.astype(o_ref.dtype)

    for i, (wn_ref, width) in enumerate(zip(narrow_refs, narrow)):
        @pl.when(level == i + 1)
        def _(wn_ref=wn_ref, width=width):
            starts = [start(wn_ref, j) for j in range(group)]
            lhs = jnp.concatenate([onehot(j, starts[j], width) for j in range(group)], axis=0)
            res = _dot(lhs, h_ref[0])
            for j in range(group):
                add(j, starts[j], width, res[j * width:(j + 1) * width])

    @pl.when(level == 0)
    def _():
        for j in range(group):
            ws = start(ww_ref, j)
            add(j, ws, wide, _dot(onehot(j, ws, wide), h_ref[0]))


def moe_gather(slot, win, h, cap):
    b, e, n = slot.shape
    dx = h.shape[2]
    group = math.gcd(e, MOE_GROUP)
    grid_spec = pltpu.PrefetchScalarGridSpec(
        num_scalar_prefetch=len(win.args),
        grid=(b, e // group, win.nt),
        in_specs=[pl.BlockSpec((1, group, win.tile), lambda i, g, k, *_: (i, g, k)),
                  pl.BlockSpec((1, win.tile, dx), lambda i, g, k, *_: (i, k, 0))],
        out_specs=pl.BlockSpec((1, group, cap, dx), lambda i, g, k, *_: (i, g, 0, 0)))
    return pl.pallas_call(
        functools.partial(_gather_kernel, ne=e, narrow=win.narrow, wide=win.wide),
        grid_spec=grid_spec,
        out_shape=jax.ShapeDtypeStruct((b, e, cap, dx), BF16),
        compiler_params=_cparams("arbitrary", "arbitrary", "arbitrary"),
        name="moe_gather",
    )(*win.args, slot, h)


def _ffn_kernel(*refs, ns):
    x_refs = refs[:ns]
    wg_ref, wu_ref, wd_ref = refs[ns:ns + 3]
    o_refs = refs[ns + 3:2 * ns + 3]
    wgb_ref, wub_ref, wdb_ref = refs[2 * ns + 3:]

    @pl.when(pl.program_id(1) == 0)
    def _():
        wgb_ref[...] = wg_ref[0, 0].astype(BF16)
        wub_ref[...] = wu_ref[0, 0].astype(BF16)
        wdb_ref[...] = wd_ref[0, 0].astype(BF16)

    d = wgb_ref.shape[0]
    ei, ne = pl.program_id(0), pl.num_programs(0)

    chunks = [[(0, i, r, min(r + FFN_ROWS, x_refs[0].shape[2]))]
              for i in range(x_refs[0].shape[0]) for r in range(0, x_refs[0].shape[2], FFN_ROWS)]
    for s in range(1, ns):
        chunks[-1] += [(s, i, 0, x_refs[s].shape[2]) for i in range(x_refs[s].shape[0])]

    def rows_of(chunk, c0, c1):
        parts = [x_refs[s][i, 0, r0:r1, c0:c1] for s, i, r0, r1 in chunk]
        return parts[0] if len(parts) == 1 else jnp.concatenate(parts, axis=0)

    ups = []
    for chunk in chunks:
        x = rows_of(chunk, 0, d)
        ups.append((_dot(x, wgb_ref[...]), _dot(x, wub_ref[...])))
    for chunk, (a, u) in zip(chunks, ups):
        ext = rows_of(chunk, d, d + LANES).astype(F32)
        lane = lax.broadcasted_iota(I32, ext.shape, 1)
        mine = (lane == ei) | (lane == ei + ne) | (lane == ei + 2 * ne)
        gate = jnp.sum(jnp.where(mine, ext, 0.0), axis=1, keepdims=True)
        y = _dot(((a / (1.0 + jnp.exp(-a))) * u).astype(BF16), wdb_ref[...]) * gate
        row = 0
        for s, i, r0, r1 in chunk:
            o_refs[s][i, 0, r0:r1, :] = y[row:row + r1 - r0].astype(o_refs[s].dtype)
            row += r1 - r0


def moe_ffn(xs_list, w_gate, w_up, w_down, layer):
    b, e, _, dx = xs_list[0].shape
    d, f = w_gate.shape[2], w_gate.shape[3]
    bb = 2 if b % 2 == 0 else 1

    def specs(width):
        return [pl.BlockSpec((bb, 1, xs.shape[2], width), lambda j, i: (i, j, 0, 0)) for xs in xs_list]

    return pl.pallas_call(
        functools.partial(_ffn_kernel, ns=len(xs_list)),
        grid=(e, b // bb),
        in_specs=specs(dx) + [pl.BlockSpec((1, 1, d, f), lambda j, i: (layer, j, 0, 0)),
                              pl.BlockSpec((1, 1, d, f), lambda j, i: (layer, j, 0, 0)),
                              pl.BlockSpec((1, 1, f, d), lambda j, i: (layer, j, 0, 0))],
        out_specs=specs(d),
        out_shape=[jax.ShapeDtypeStruct(xs.shape[:3] + (d,), BF16) for xs in xs_list],
        scratch_shapes=[pltpu.VMEM((d, f), BF16), pltpu.VMEM((d, f), BF16), pltpu.VMEM((f, d), BF16)],
        compiler_params=_cparams("arbitrary", "arbitrary"),
        name="moe_ffn",
    )(*xs_list, w_gate, w_up, w_down)


def _combine_kernel(*refs, ne, narrow, wide):
    narrow_refs = refs[:len(narrow)]
    ww_ref, level_ref, slot_ref, ys_hbm, lat_ref, gt_ref, o_ref, ys_buf, sem = refs[len(narrow):]
    bi, ti = pl.program_id(0), pl.program_id(1)
    nb, nt = pl.num_programs(0), pl.num_programs(1)
    cur = bi % 2

    def fetch(batch, buf):
        return pltpu.make_async_copy(ys_hbm.at[batch], ys_buf.at[buf], sem.at[buf])

    @pl.when(ti == 0)
    def _():
        @pl.when(bi == 0)
        def _():
            fetch(0, 0).start()

        fetch(bi, cur).wait()

        @pl.when(bi + 1 < nb)
        def _():
            fetch(bi + 1, 1 - cur).start()

    tn = lat_ref.shape[1]
    slot = slot_ref[0].astype(I32)

    def start(ref, e):
        return pl.multiple_of(ref[(bi * ne + e) * nt + ti], BF16_ROWS)

    def onehot(e, ws, width):
        jrow = lax.broadcasted_iota(I32, (tn, width), 1)
        return jnp.where(jrow + ws == slot[:, e:e + 1], 1.0, 0.0).astype(BF16)

    def onehot_lane_group(first, starts, width):
        lane = lax.broadcasted_iota(I32, (tn, LANES), 1)
        target = slot[:, first:first + 1] - starts[first]
        for i in range(1, LANES // width):
            e = first + i
            target = jnp.where(lane >= i * width, slot[:, e:e + 1] - (starts[e] - i * width), target)
        return jnp.where(target == lane, 1.0, 0.0).astype(BF16)

    def finish(acc):
        o_ref[0] = lat_ref[0] + gt_ref[0] * acc

    level = level_ref[bi * nt + ti]
    for i, (wn_ref, width) in enumerate(zip(narrow_refs, narrow)):
        @pl.when(level == i + 1)
        def _(wn_ref=wn_ref, width=width):
            starts = [start(wn_ref, e) for e in range(ne)]
            if width % LANES == 0:
                lhs = jnp.concatenate([onehot(e, starts[e], width) for e in range(ne)], axis=1)
            else:
                lhs = jnp.concatenate([onehot_lane_group(e, starts, width)
                                       for e in range(0, ne, LANES // width)], axis=1)
            rhs = jnp.concatenate([ys_buf[cur, e, pl.ds(starts[e], width), :] for e in range(ne)], axis=0)
            finish(_dot(lhs, rhs))

    @pl.when(level == 0)
    def _():
        acc = jnp.zeros((tn, lat_ref.shape[2]), F32)
        for e in range(ne):
            ws = start(ww_ref, e)
            acc = acc + _dot(onehot(e, ws, wide), ys_buf[cur, e, pl.ds(ws, wide), :])
        finish(acc)


def moe_combine(slot_t, win, ys, lat, gate_vec, cap):
    b, n, e = slot_t.shape
    d = lat.shape[2]
    tn = win.tile
    grid_spec = pltpu.PrefetchScalarGridSpec(
        num_scalar_prefetch=len(win.args),
        grid=(b, win.nt),
        in_specs=[pl.BlockSpec((1, tn, e), lambda i, j, *_: (i, j, 0)),
                  pl.BlockSpec(memory_space=pl.ANY),
                  pl.BlockSpec((1, tn, d), lambda i, j, *_: (i, j, 0)),
                  pl.BlockSpec((1, 1, d), lambda i, j, *_: (i, 0, 0))],
        out_specs=pl.BlockSpec((1, tn, d), lambda i, j, *_: (i, j, 0)),
        scratch_shapes=[pltpu.VMEM((2, e, cap, d), BF16), pltpu.SemaphoreType.DMA((2,))])
    return pl.pallas_call(
        functools.partial(_combine_kernel, ne=e, narrow=win.narrow, wide=win.wide),
        grid_spec=grid_spec,
        out_shape=jax.ShapeDtypeStruct((b, n, d), F32),
        compiler_params=_cparams("arbitrary", "arbitrary"),
        name="moe_combine",
    )(*win.args, slot_t, ys, lat, gate_vec)


def expert_choice_moe(streams, g, w_router, w_gate, w_up, w_down, layer):
    e = w_router.shape[1]
    routed = []
    for lat, sh, sc, _ in streams:
        cap = EC_CAPACITY_FACTOR * lat.shape[1] // e
        h, aff = moe_router(lat, g, sh, sc, w_router.T)
        slot, before, slot_t = moe_select(aff, cap)
        win = _Windows(before, lat.shape[1], cap)
        routed.append((moe_gather(slot, win, h, cap), slot_t, win, cap))
    ys_list = moe_ffn([r[0] for r in routed], w_gate, w_up, w_down, layer)
    return [moe_combine(slot_t, win, ys, lat, gate_vec, cap)
            for (lat, _, _, gate_vec), (_, slot_t, win, cap), ys in zip(streams, routed, ys_list)]


def kernel(x, c, ctx, c_ctx, ada_w, ada_b, norm_g, sc_w_in, sc_conv, sc_w_out, hy_w_in, hy_conv_w, hy_f_w1,
           hy_f_b1, hy_f_w2, hy_f_b2, hy_f_w3, hy_sin_freq, hy_skip, hy_w_out, at_w_qkv, at_q_g, at_k_g, at_w_o,
           moe_router_w, moe_w_gate, moe_w_up, moe_w_down):
    b, l, d = x.shape
    depth = ada_w.shape[0]
    assert b + 1 <= MOD_ROWS
    qd = at_w_o.shape[1]
    nq = qd // HEAD_DIM
    attn_layers = list(range(MIX_ATTN, depth, N_MIXERS))
    last_attn = max(attn_layers, default=-1)

    cond = jnp.zeros((MOD_ROWS, d), F32).at[:b].set(c).at[b].set(c_ctx)
    mods = ada_modulation(cond, ada_w, ada_b)

    lat, cx = x, ctx
    for i in range(depth):
        kind = i % N_MIXERS
        j = i // N_MIXERS
        need_ctx = i <= last_attn
        upd_ctx = i < last_attn
        m_l = mods[i, :b].reshape(b, 1, 6, d)
        sh1, sc1, g1, sh2, sc2, g2 = (m_l[:, :, k] for k in range(6))
        if need_ctx:
            m_c = jnp.broadcast_to(mods[i, b].reshape(1, 1, 6, d), (b, 1, 6, d))
            csh1, csc1, cg1, csh2, csc2, cg2 = (m_c[:, :, k] for k in range(6))
        streams = [(lat, sh1, sc1, g1)]
        if upd_ctx:
            streams.append((cx, csh1, csc1, cg1))

        outs = []
        if kind == MIX_SHORTCONV:
            w_in, w_out = sc_w_in[j].astype(BF16), sc_w_out[j].astype(BF16)
            for s, sh, sc, gt in streams:
                p = nm_matmul(s, norm_g[i, 0], sh, sc, w_in)
                outs.append(sc_out(p, sc_conv[j], w_out, s, gt))
        elif kind == MIX_HYENA:
            w_in, w_out = hy_w_in[j].astype(BF16), hy_w_out[j].astype(BF16)
            for s, sh, sc, gt in streams:
                p = nm_matmul(s, norm_g[i, 0], sh, sc, w_in)
                z = hyena_mixer(p, hy_conv_w[j], hy_f_w1[j], hy_f_b1[j], hy_f_w2[j], hy_f_b2[j], hy_f_w3[j],
                                hy_sin_freq[j], hy_skip[j])
                outs.append(proj_residual(z, w_out, s, gt))
        else:
            w_qkv, w_o = at_w_qkv[j].astype(BF16), at_w_o[j].astype(BF16)
            q_l, k_l, v_l = qkv_project(lat, norm_g[i, 0], sh1, sc1, w_qkv, at_q_g[j], at_k_g[j],
                                        _rope_tables(l), nq, True)
            lc = cx.shape[1]
            dummy = tuple(jnp.zeros((lc, HEAD_DIM), F32) for _ in range(3))
            if upd_ctx:
                q_c, k_c, v_c = qkv_project(cx, norm_g[i, 0], csh1, csc1, w_qkv, at_q_g[j], at_k_g[j],
                                            dummy, nq, False)
            else:
                k_c, v_c = qkv_project(cx, norm_g[i, 0], csh1, csc1, w_qkv[:, qd:], at_q_g[j], at_k_g[j],
                                       dummy, 0, False)
            v_all = jnp.concatenate([v_c, v_l], axis=1)
            o_l = attention(q_l, jnp.concatenate([k_c, k_l], axis=1), v_all)
            outs.append(proj_residual(o_l, w_o, lat, g1))
            if upd_ctx:
                outs.append(proj_residual(attention(q_c, k_c, v_c), w_o, cx, cg1))

        moe_streams = [(outs[0], sh2, sc2, g2)]
        if upd_ctx:
            moe_streams.append((outs[1], csh2, csc2, cg2))
        moe_out = expert_choice_moe(moe_streams, norm_g[i, 1], moe_router_w[i], moe_w_gate, moe_w_up, moe_w_down, i)
        lat = moe_out[0]
        if upd_ctx:
            cx = moe_out[1]
    return lat
```

```python
import functools
import math

import jax
import jax.numpy as jnp
import numpy as np
from jax import lax
from jax.experimental import pallas as pl
from jax.experimental.pallas import tpu as pltpu

F32 = jnp.float32
BF16 = jnp.bfloat16
I32 = jnp.int32
HIGHEST = lax.Precision.HIGHEST

N_MIXERS = 3
MIX_SHORTCONV, MIX_HYENA, MIX_ATTN = 0, 1, 2
CONV_W = 3
EPS = 1e-6
HY_ORDER = 2
HY_EMB = 33
HY_BANDS = (HY_EMB - 1) // 2
HY_FAST_DECAY = 0.3
HY_SLOW_DECAY = 1.5
HY_TARGET = 1e-2
HEAD_DIM = 128
N_KV_HEADS = 2
GRID_W = 64
ROPE_THETA = 10000.0
EC_CAPACITY_FACTOR = 2
Q_PRESCALE = HEAD_DIM ** -0.5 * math.log2(math.e)

LANES = 128
SUBLANES = 8
BF16_ROWS = 16
VMEM_LIMIT_BYTES = 56 * 1024 * 1024
ROW_TILE = 1024
MOD_ROWS = 16
FFT_N2 = 128
HY_DT = LANES
MOE_TILE = 2 * LANES
MOE_WINDOWS = (LANES // 2, LANES)
MOE_GROUP = 2 * SUBLANES
FFN_ROWS = 2 * LANES
DFT_UNROLL = 8
DFT_UNROLL_STAGE1 = 16


def _cparams(*sem):
    return pltpu.CompilerParams(dimension_semantics=sem, vmem_limit_bytes=VMEM_LIMIT_BYTES)


def _row_tile(n, cap=ROW_TILE):
    t = min(n, cap)
    assert n % t == 0
    return t


def _dot(a, b):
    return jnp.dot(a, b, preferred_element_type=F32)


def _ada_kernel(s_ref, w_ref, b_ref, o_ref):
    s = s_ref[...]
    s = s / (1.0 + jnp.exp(-s))
    o_ref[0] = jnp.dot(s, w_ref[0], preferred_element_type=F32, precision=HIGHEST) + b_ref[0]


def ada_modulation(cond, ada_w, ada_b):
    depth, d, n = ada_w.shape
    tn = 1536 if n % 1536 == 0 else n
    return pl.pallas_call(
        _ada_kernel,
        grid=(depth, n // tn),
        in_specs=[pl.BlockSpec((MOD_ROWS, d), lambda i, j: (0, 0)),
                  pl.BlockSpec((1, d, tn), lambda i, j: (i, 0, j)),
                  pl.BlockSpec((1, 1, tn), lambda i, j: (i, 0, j))],
        out_specs=pl.BlockSpec((1, MOD_ROWS, tn), lambda i, j: (i, 0, j)),
        out_shape=jax.ShapeDtypeStruct((depth, MOD_ROWS, n), F32),
        compiler_params=_cparams("arbitrary", "arbitrary"),
        name="ada_modulation",
    )(cond, ada_w, ada_b.reshape(depth, 1, n))


def _norm_mod(x, g, sh, sc):
    y = x * lax.rsqrt(jnp.mean(x * x, axis=-1, keepdims=True) + EPS)
    return (y * g) * (1.0 + sc) + sh


def _nm_matmul_kernel(x_ref, g_ref, sh_ref, sc_ref, w_ref, o_ref):
    h = _norm_mod(x_ref[0], g_ref[...], sh_ref[0], sc_ref[0]).astype(BF16)
    o_ref[0] = _dot(h, w_ref[...]).astype(o_ref.dtype)


def nm_matmul(x, g, sh, sc, w, out_dtype=BF16):
    b, l, d = x.shape
    n = w.shape[1]
    tm = _row_tile(l)
    return pl.pallas_call(
        _nm_matmul_kernel,
        grid=(b, l // tm),
        in_specs=[pl.BlockSpec((1, tm, d), lambda i, j: (i, j, 0)),
                  pl.BlockSpec((1, d), lambda i, j: (0, 0)),
                  pl.BlockSpec((1, 1, d), lambda i, j: (i, 0, 0)),
                  pl.BlockSpec((1, 1, d), lambda i, j: (i, 0, 0)),
                  pl.BlockSpec((d, n), lambda i, j: (0, 0))],
        out_specs=pl.BlockSpec((1, tm, n), lambda i, j: (i, j, 0)),
        out_shape=jax.ShapeDtypeStruct((b, l, n), out_dtype),
        compiler_params=_cparams("arbitrary", "arbitrary"),
        name="nm_matmul",
    )(x, g.reshape(1, d), sh, sc, w)


def _proj_res_kernel(a_ref, w_ref, lat_ref, gt_ref, o_ref):
    o_ref[0] = lat_ref[0] + gt_ref[0] * _dot(a_ref[0], w_ref[...])


def proj_residual(a, w, lat, gate):
    b, l, k = a.shape
    d = w.shape[1]
    tm = _row_tile(l)
    return pl.pallas_call(
        _proj_res_kernel,
        grid=(b, l // tm),
        in_specs=[pl.BlockSpec((1, tm, k), lambda i, j: (i, j, 0)),
                  pl.BlockSpec((k, d), lambda i, j: (0, 0)),
                  pl.BlockSpec((1, tm, d), lambda i, j: (i, j, 0)),
                  pl.BlockSpec((1, 1, d), lambda i, j: (i, 0, 0))],
        out_specs=pl.BlockSpec((1, tm, d), lambda i, j: (i, j, 0)),
        out_shape=jax.ShapeDtypeStruct((b, l, d), F32),
        compiler_params=_cparams("arbitrary", "arbitrary"),
        name="proj_residual",
    )(a, w, lat, gate)


def _sc_mixer_kernel(x_ref, xp_ref, xn_ref, g_ref, sh_ref, sc_ref, win_ref, cw_ref, wout_ref, gt_ref,
                     o_ref, u_ref, *, tm):
    j = pl.program_id(1)
    nj = pl.num_programs(1)
    d = u_ref.shape[1]
    x = x_ref[0]
    p = _dot(_norm_mod(x, g_ref[...], sh_ref[0], sc_ref[0]).astype(BF16), win_ref[...])
    u_ref[pl.ds(SUBLANES, tm), :] = p[:, d:2 * d] * p[:, 2 * d:]
    halo = jnp.concatenate([xp_ref[0], xn_ref[0]], axis=0)
    ph = _dot(_norm_mod(halo, g_ref[...], sh_ref[0], sc_ref[0]).astype(BF16), win_ref[:, d:])
    uh = ph[:, :d] * ph[:, d:]
    prev = jnp.where(j == 0, 0.0, uh[SUBLANES - 1:SUBLANES])
    nxt = jnp.where(j == nj - 1, 0.0, uh[SUBLANES:SUBLANES + 1])
    u_ref[pl.ds(0, SUBLANES), :] = jnp.broadcast_to(prev, (SUBLANES, d))
    u_ref[pl.ds(SUBLANES + tm, SUBLANES), :] = jnp.broadcast_to(nxt, (SUBLANES, d))
    cw = cw_ref[...]
    y = (u_ref[pl.ds(SUBLANES - 1, tm), :] * cw[0:1] + u_ref[pl.ds(SUBLANES, tm), :] * cw[1:2]
         + u_ref[pl.ds(SUBLANES + 1, tm), :] * cw[2:3])
    m = (p[:, :d] * y).astype(BF16)
    o_ref[0] = x + gt_ref[0] * _dot(m, wout_ref[...])


def sc_mixer(x, g, sh, sc, w_in, conv_w, w_out, gate):
    b, l, d = x.shape
    tm = _row_tile(l, 512)
    hb = tm // SUBLANES
    nhb = l // SUBLANES
    vec = pl.BlockSpec((1, 1, d), lambda i, j: (i, 0, 0))
    return pl.pallas_call(
        functools.partial(_sc_mixer_kernel, tm=tm),
        grid=(b, l // tm),
        in_specs=[pl.BlockSpec((1, tm, d), lambda i, j: (i, j, 0)),
                  pl.BlockSpec((1, SUBLANES, d), lambda i, j: (i, jnp.maximum(j * hb - 1, 0), 0)),
                  pl.BlockSpec((1, SUBLANES, d), lambda i, j: (i, jnp.minimum((j + 1) * hb, nhb - 1), 0)),
                  pl.BlockSpec((1, d), lambda i, j: (0, 0)), vec, vec,
                  pl.BlockSpec((d, 3 * d), lambda i, j: (0, 0)),
                  pl.BlockSpec((CONV_W, d), lambda i, j: (0, 0)),
                  pl.BlockSpec((d, d), lambda i, j: (0, 0)),
                  vec],
        out_specs=pl.BlockSpec((1, tm, d), lambda i, j: (i, j, 0)),
        out_shape=jax.ShapeDtypeStruct((b, l, d), F32),
        scratch_shapes=[pltpu.VMEM((tm + 2 * SUBLANES, d), F32)],
        compiler_params=_cparams("arbitrary", "arbitrary"),
        name="sc_mixer",
    )(x, x, x, g.reshape(1, d), sh, sc, w_in, conv_w, w_out, gate)


def _hy_filter_kernel(z_ref, w1_ref, b1_ref, w2_ref, b2_ref, fr_ref, w3f_ref, w3b_ref, t_ref, dl_ref,
                      of_ref, ob_ref, a_ref):
    @pl.when((pl.program_id(0) == 0) & (pl.program_id(1) == 0))
    def _():
        fr = fr_ref[...]
        a1 = jnp.sin(fr * (jnp.dot(z_ref[...], w1_ref[...], preferred_element_type=F32, precision=HIGHEST)
                           + b1_ref[...]))
        a_ref[...] = jnp.sin(fr * (jnp.dot(a1, w2_ref[...], preferred_element_type=F32, precision=HIGHEST)
                                   + b2_ref[...]))

    a = a_ref[...]
    decay = jnp.exp(-t_ref[...] * dl_ref[...])
    hf = jnp.dot(a, w3f_ref[...], preferred_element_type=F32, precision=HIGHEST) * decay
    hb = jnp.dot(a, w3b_ref[...], preferred_element_type=F32, precision=HIGHEST) * decay
    row = lax.broadcasted_iota(I32, hb.shape, 0)
    hb = jnp.where(row == 0, 0.0, hb)
    inv = 1.0 / (jnp.sum(jnp.abs(hf), axis=0, keepdims=True) + jnp.sum(jnp.abs(hb), axis=0, keepdims=True))
    of_ref[0] = hf * inv
    ob_ref[0] = hb * inv


def hy_filter_taps(l, d, f_w1, f_b1, f_w2, f_b2, f_w3, sin_freq):
    hid = f_w1.shape[1]
    hp = LANES
    t = np.linspace(0.0, 1.0, l)[:, None]
    bands = np.linspace(1e-4, HY_BANDS - 1, HY_BANDS)[None, :]
    ang = (2.0 * math.pi / l) * np.arange(l)[:, None] * bands
    z = np.zeros((l, hp), np.float64)
    z[:, :HY_EMB] = np.concatenate([t, np.cos(ang), -np.sin(ang)], axis=-1)
    deltas = np.abs(np.linspace(math.log(HY_TARGET) / HY_SLOW_DECAY, math.log(HY_TARGET) / HY_FAST_DECAY, d))[None, :]

    def pad2(a, r, c):
        return jnp.zeros((r, c), F32).at[:a.shape[0], :a.shape[1]].set(a.astype(F32))

    w1 = pad2(f_w1, hp, hp)
    w2 = pad2(f_w2, hp, hp)
    w3 = pad2(f_w3, hp, f_w3.shape[1])
    b1 = pad2(f_b1[None], 1, hp)
    b2 = pad2(f_b2[None], 1, hp)
    fr = pad2(sin_freq[None], 1, hp)
    tc = min(d, 2 * LANES)
    nd = d // tc
    full = lambda o, j: (0, 0)
    return pl.pallas_call(
        _hy_filter_kernel,
        grid=(HY_ORDER, nd),
        in_specs=[pl.BlockSpec((l, hp), full), pl.BlockSpec((hp, hp), full), pl.BlockSpec((1, hp), full),
                  pl.BlockSpec((hp, hp), full), pl.BlockSpec((1, hp), full), pl.BlockSpec((1, hp), full),
                  pl.BlockSpec((hp, tc), lambda o, j: (0, o * nd + j)),
                  pl.BlockSpec((hp, tc), lambda o, j: (0, (HY_ORDER + o) * nd + j)),
                  pl.BlockSpec((l, 1), full),
                  pl.BlockSpec((1, tc), lambda o, j: (0, j))],
        out_specs=[pl.BlockSpec((1, l, tc), lambda o, j: (o, 0, j)),
                   pl.BlockSpec((1, l, tc), lambda o, j: (o, 0, j))],
        out_shape=[jax.ShapeDtypeStruct((HY_ORDER, l, d), F32)] * 2,
        scratch_shapes=[pltpu.VMEM((l, hp), F32)],
        compiler_params=_cparams("arbitrary", "arbitrary"),
        name="hy_filter_taps",
    )(jnp.asarray(z, F32), w1, b1, w2, b2, fr, w3, w3, jnp.asarray(t, F32), jnp.asarray(deltas, F32))


def _dft_consts(l):
    n = 2 * l
    if n <= 4 * FFT_N2:
        n1, n2 = 1, n
    else:
        n1, n2 = n // FFT_N2, FFT_N2
    if n1 == 1:
        k = np.arange(n)[:, None]
        t = np.arange(l)[None, :]
        ang = 2.0 * np.pi * (k * t % n) / n
        c, s = np.cos(ang), np.sin(ang)
        fwd = np.block([[c, s], [-s, c]])
        inv = np.block([[c.T, -s.T], [s.T, c.T]]) / n
        return n1, n2, dict(fwd=jnp.asarray(fwd, BF16), inv=jnp.asarray(inv, BF16))
    n1h = n1 // 2
    k1 = np.arange(n1)[None, :, None]
    m1 = np.arange(n1h)[None, None, :]
    r2 = np.arange(n2)[:, None, None]
    ang = 2.0 * np.pi * ((k1 * (n2 * m1 + r2)) % n) / n
    c, s = np.cos(ang), np.sin(ang)
    m1f = np.concatenate([np.concatenate([c, s], 2), np.concatenate([-s, c], 2)], 1)
    ct, st = np.swapaxes(c, 1, 2) / n, np.swapaxes(s, 1, 2) / n
    m1i = np.concatenate([np.concatenate([ct, -st], 2), np.concatenate([st, ct], 2)], 1)
    a2 = 2.0 * np.pi * ((np.arange(n2)[:, None] * np.arange(n2)[None, :]) % n2) / n2
    c2, s2 = np.cos(a2), np.sin(a2)
    f2 = np.block([[c2, s2], [-s2, c2]])
    f2i = np.block([[c2, -s2], [s2, c2]])

    def il(m):
        return np.stack([np.arange(m), m + np.arange(m)], 1).reshape(-1)

    m1f = m1f[:, il(n1), :][:, :, il(n1h)]
    m1i = m1i[:, il(n1h), :][:, :, il(n1)]
    f2 = f2[:, il(n2)]
    f2i = f2i[il(n2), :][:, il(n2)]
    half = n2 // 2
    m1f = np.concatenate([m1f[:half], m1f[half:]], 2)
    m1i = np.concatenate([m1i[:half], m1i[half:]], 2)
    return n1, n2, dict(m1f=jnp.asarray(m1f, BF16), m1i=jnp.asarray(m1i, BF16),
                        f2=jnp.asarray(f2, BF16), f2i=jnp.asarray(f2i, BF16))


def _blockdiag(a, b):
    z = jnp.zeros_like(a)
    return jnp.concatenate([jnp.concatenate([a, z], axis=1), jnp.concatenate([z, b], axis=1)], axis=0)


def _halves(y):
    return y[:, :LANES], y[:, LANES:]


HI_HALF = -65536


def _pack_pair(re, im):
    lo = (pltpu.bitcast(re.astype(BF16).astype(F32), I32) >> 16) & 0xFFFF
    if im is None:
        return lo
    return lo | (pltpu.bitcast(im.astype(BF16).astype(F32), I32) & HI_HALF)


def _unpack_pair(w):
    return pltpu.bitcast(w << 16, F32), pltpu.bitcast(w & HI_HALF, F32)


def _as_rows(w):
    return pltpu.bitcast(w, BF16)


def _as_words(y):
    return pltpu.bitcast(y.astype(BF16), I32)


def _fwd_stage1(x_pk, a_pk, m1f_ref, n1, n2):
    n1h, half = n1 // 2, n2 // 2

    def column(r):
        return _as_rows(x_pk[pl.ds(r, n1h, stride=n2), :])

    def body(r, carry):
        w = _as_words(_dot(m1f_ref[r], _blockdiag(column(r), column(r + half))))
        for rr, ww in zip((r, r + half), _halves(w)):
            a_pk[pl.ds(pl.multiple_of(rr * n1, n1), n1), :] = ww
        return carry

    lax.fori_loop(0, half, body, 0, unroll=min(half, DFT_UNROLL_STAGE1))


def _inv_stage1(b_pk, x_pk, m1i_ref, n1, n2):
    n1h, half = n1 // 2, n2 // 2

    def column(r):
        return _as_rows(b_pk[pl.ds(pl.multiple_of(r * n1, n1), n1), :])

    def body(r, carry):
        w = _as_words(_dot(m1i_ref[r], _blockdiag(column(r), column(r + half))))
        for rr, ww in zip((r, r + half), _halves(w)):
            x_pk[pl.ds(rr, n1h, stride=n2), :] = ww
        return carry

    lax.fori_loop(0, half, body, 0, unroll=min(half, DFT_UNROLL_STAGE1))


def _load_chunk(a_pk, k1, n1, n2):
    ra = pl.ds(k1, n2, stride=n1)
    rb = pl.ds(k1 + n1 // 2, n2, stride=n1)
    return jnp.concatenate([_as_rows(a_pk[ra, :]), _as_rows(a_pk[rb, :])], axis=1)


def _hy_spec_kernel(*refs, l, n1, n2, dense):
    if dense:
        tf_ref, tb_ref, sk_ref, fwd_ref, o_ref = refs
        n = n2
        fw = fwd_ref[:, 0:l]
        sf = _dot(fw, tf_ref[0].astype(BF16))
        sb = _dot(fw, tb_ref[0].astype(BF16))
        o_ref[0, 0] = (sf[:n] + sb[:n] + sk_ref[0]).astype(o_ref.dtype)
        o_ref[0, 1] = (sf[n:] - sb[n:]).astype(o_ref.dtype)
        return
    tf_ref, tb_ref, sk_ref, m1f_ref, f2_ref, o_ref, x_pk, a_pk, h_re, h_im = refs
    for run, t_ref in enumerate((tf_ref, tb_ref)):
        x_pk[...] = _pack_pair(t_ref[0], None)
        _fwd_stage1(x_pk, a_pk, m1f_ref, n1, n2)

        def body(k1, carry, run=run):
            x = _dot(f2_ref[...], _load_chunk(a_pk, k1, n1, n2))
            for kk, xx in zip((k1, k1 + n1 // 2), _halves(x)):
                rows = pl.ds(pl.multiple_of(kk * n2, n2), n2)
                if run == 0:
                    h_re[rows, :] = xx[:n2]
                    h_im[rows, :] = xx[n2:]
                else:
                    h_re[rows, :] = h_re[rows, :] + xx[:n2]
                    h_im[rows, :] = h_im[rows, :] - xx[n2:]
            return carry

        lax.fori_loop(0, n1 // 2, body, 0, unroll=DFT_UNROLL)
    o_ref[0, 0] = (h_re[...] + sk_ref[0]).astype(o_ref.dtype)
    o_ref[0, 1] = h_im[...].astype(o_ref.dtype)


def hy_spectrum(taps_f, taps_b, skip, consts, n1, n2):
    o, l, d = taps_f.shape
    n = 2 * l
    dt = HY_DT
    dense = n1 == 1
    tap_spec = pl.BlockSpec((1, l, dt), lambda i, j: (i, 0, j))
    in_specs = [tap_spec, tap_spec, pl.BlockSpec((1, 1, dt), lambda i, j: (i, 0, j))]
    if dense:
        mats = [consts["fwd"]]
        scratch = []
    else:
        mats = [consts["m1f"], consts["f2"]]
        scratch = [pltpu.VMEM((l, dt), I32), pltpu.VMEM((n, dt), I32)] + [pltpu.VMEM((n, dt), F32)] * 2
    in_specs += [pl.BlockSpec(m.shape, lambda i, j, nd=m.ndim: (0,) * nd) for m in mats]
    return pl.pallas_call(
        functools.partial(_hy_spec_kernel, l=l, n1=n1, n2=n2, dense=dense),
        grid=(o, d // dt),
        in_specs=in_specs,
        out_specs=pl.BlockSpec((1, 2, n, dt), lambda i, j: (i, 0, 0, j)),
        out_shape=jax.ShapeDtypeStruct((o, 2, n, d), BF16),
        scratch_shapes=scratch,
        compiler_params=_cparams("arbitrary", "arbitrary"),
        name="hy_spectrum",
    )(taps_f, taps_b, skip.reshape(o, 1, d), *mats)


def _conv3_rows(src, cw, pad_ref, l):
    c = pad_ref.shape[1]
    pad_ref[pl.ds(0, SUBLANES), :] = jnp.zeros((SUBLANES, c), F32)
    pad_ref[pl.ds(SUBLANES + l, SUBLANES), :] = jnp.zeros((SUBLANES, c), F32)
    pad_ref[pl.ds(SUBLANES, l), :] = src
    return (pad_ref[pl.ds(SUBLANES - 1, l), :] * cw[0:1] + pad_ref[pl.ds(SUBLANES, l), :] * cw[1:2]
            + pad_ref[pl.ds(SUBLANES + 1, l), :] * cw[2:3])


def _hy_conv_kernel(*refs, l, n1, n2, dense, conv_z):
    if dense:
        z_ref, g_ref, cwz_ref, cwg_ref, h_ref, fwd_ref, inv_ref, o_ref, pad_ref = refs
    else:
        (z_ref, g_ref, cwz_ref, cwg_ref, h_ref, m1f_ref, m1i_ref, f2_ref, f2i_ref, o_ref,
         x_pk, a_pk, b_pk, pad_ref) = refs
    zs = []
    for e in range(2):
        z = z_ref[e, 0].astype(F32)
        zs.append(_conv3_rows(z, cwz_ref[...], pad_ref, l) if conv_z else z)
    if dense:
        n = n2
        s = _dot(fwd_ref[...], jnp.concatenate(zs, axis=0).astype(BF16))
        hr = h_ref[0, 0].astype(F32)
        hi = h_ref[0, 1].astype(F32)
        zr = s[:n] * hr - s[n:] * hi
        zi = s[:n] * hi + s[n:] * hr
        y = _dot(inv_ref[...], jnp.concatenate([zr, zi], axis=0).astype(BF16))
        ys = (y[:l], y[l:])
    else:
        x_pk[...] = _pack_pair(zs[0], zs[1])
        _fwd_stage1(x_pk, a_pk, m1f_ref, n1, n2)

        nk = n1 // 2

        def spectrum_step(k1, carry):
            s = _dot(f2_ref[...], _load_chunk(a_pk, k1, n1, n2))
            rows = [pl.ds(pl.multiple_of(k * n2, n2), n2) for k in (k1, k1 + nk)]
            hr = jnp.concatenate([h_ref[0, 0, r, :] for r in rows], axis=1).astype(F32)
            hi = jnp.concatenate([h_ref[0, 1, r, :] for r in rows], axis=1).astype(F32)
            zr = s[:n2] * hr - s[n2:] * hi
            zi = s[:n2] * hi + s[n2:] * hr
            for r, zr_half, zi_half in zip(rows, _halves(zr), _halves(zi)):
                b_pk[r, :] = _pack_pair(zr_half, zi_half)
            return carry

        lax.fori_loop(0, nk, spectrum_step, 0, unroll=min(nk, DFT_UNROLL))

        def inverse_step(k1, carry):
            kk = (k1, k1 + nk)
            z = jnp.concatenate([_as_rows(b_pk[pl.ds(pl.multiple_of(k * n2, n2), n2), :]) for k in kk], axis=1)
            w = _as_words(_dot(f2i_ref[...], z))
            for k, ww in zip(kk, _halves(w)):
                a_pk[pl.ds(k, n2, stride=n1), :] = ww
            return carry

        lax.fori_loop(0, nk, inverse_step, 0, unroll=min(nk, DFT_UNROLL))
        _inv_stage1(a_pk, x_pk, m1i_ref, n1, n2)
        ys = _unpack_pair(x_pk[...])
    for e in range(2):
        g = _conv3_rows(g_ref[e, 0].astype(F32), cwg_ref[...], pad_ref, l)
        o_ref[e, 0] = (g * ys[e]).astype(o_ref.dtype)


def hy_conv(zsrc, zcol, p4, gcol, conv_w, spec, order, consts, n1, n2, conv_z):
    _, pairs, l, _ = zsrc.shape
    d = spec.shape[3]
    n = 2 * l
    dt = HY_DT
    nd = d // dt
    dense = n1 == 1
    once = pl.Buffered(1)
    in_specs = [pl.BlockSpec((2, 1, l, dt), lambda j, i: (0, i, 0, zcol * nd + j)),
                pl.BlockSpec((2, 1, l, dt), lambda j, i: (0, i, 0, gcol * nd + j)),
                pl.BlockSpec((CONV_W, dt), lambda j, i: (0, j)),
                pl.BlockSpec((CONV_W, dt), lambda j, i: (0, gcol * nd + j)),
                pl.BlockSpec((1, 2, n, dt), lambda j, i: (order, 0, 0, j), pipeline_mode=once)]
    if dense:
        mats = [consts["fwd"], consts["inv"]]
        scratch = []
    else:
        mats = [consts["m1f"], consts["m1i"], consts["f2"], consts["f2i"]]
        scratch = [pltpu.VMEM((l, dt), I32)] + [pltpu.VMEM((n, dt), I32)] * 2
    scratch = scratch + [pltpu.VMEM((l + 2 * SUBLANES, dt), F32)]
    in_specs += [pl.BlockSpec(m.shape, lambda j, i, nd_=m.ndim: (0,) * nd_, pipeline_mode=once) for m in mats]
    return pl.pallas_call(
        functools.partial(_hy_conv_kernel, l=l, n1=n1, n2=n2, dense=dense, conv_z=conv_z),
        grid=(nd, pairs),
        in_specs=in_specs,
        out_specs=pl.BlockSpec((2, 1, l, dt), lambda j, i: (0, i, 0, j)),
        out_shape=jax.ShapeDtypeStruct((2, pairs, l, d), BF16),
        scratch_shapes=scratch,
        compiler_params=_cparams("arbitrary", "arbitrary"),
        name="hy_conv",
    )(zsrc, p4, conv_w, conv_w, spec, *mats)


def hyena_mixer(p, conv_w, f_w1, f_b1, f_w2, f_b2, f_w3, sin_freq, skip):
    b, l, d3 = p.shape
    d = d3 // 3
    assert b % 2 == 0 and d % HY_DT == 0
    n1, n2, consts = _dft_consts(l)
    taps_f, taps_b = hy_filter_taps(l, d, f_w1, f_b1, f_w2, f_b2, f_w3, sin_freq)
    spec = hy_spectrum(taps_f, taps_b, skip, consts, n1, n2)
    p4 = p.reshape(2, b // 2, l, d3)
    z = hy_conv(p4, 0, p4, 1, conv_w, spec, 0, consts, n1, n2, True)
    z = hy_conv(z, 0, p4, 2, conv_w, spec, 1, consts, n1, n2, False)
    return z.reshape(b, l, d)


def _head_norm(x, g):
    sq = x * x
    hi = sq.astype(BF16)
    lo = (sq - hi.astype(F32)).astype(BF16)
    ones = jnp.ones((HEAD_DIM, HEAD_DIM), BF16)
    total = _dot(hi, ones) + _dot(lo, ones)
    return x * lax.rsqrt(total * (1.0 / HEAD_DIM) + EPS) * g


def _rope(x, cos, sin_a, sin_b):
    return (x * cos + pltpu.roll(x, HEAD_DIM // 4, 1) * sin_a
            + pltpu.roll(x, HEAD_DIM - HEAD_DIM // 4, 1) * sin_b)


def _qkv_kernel(x_ref, g_ref, sh_ref, sc_ref, w_ref, qg_ref, kg_ref, cos_ref, sa_ref, sb_ref,
                *out_refs, nq, rope):
    q_ref = out_refs[0] if nq else None
    k_ref, v_ref = out_refs[-2:]
    h = _norm_mod(x_ref[0], g_ref[...], sh_ref[0], sc_ref[0]).astype(BF16)
    acc = _dot(h, w_ref[...])
    hd = HEAD_DIM
    for i in range(nq + N_KV_HEADS):
        t = acc[:, i * hd:(i + 1) * hd]
        t = _head_norm(t, qg_ref[...] if i < nq else kg_ref[...])
        if rope:
            t = _rope(t, cos_ref[...], sa_ref[...], sb_ref[...])
        if i < nq:
            q_ref[0, :, i * hd:(i + 1) * hd] = (t * Q_PRESCALE).astype(BF16)
        else:
            k_ref[0, :, (i - nq) * hd:(i - nq + 1) * hd] = t.astype(BF16)
    v_ref[0] = acc[:, (nq + N_KV_HEADS) * hd:].astype(BF16)


def qkv_project(x, g, sh, sc, w, q_g, k_g, rope_tabs, nq, rope):
    b, l, d = x.shape
    n = w.shape[1]
    hd = HEAD_DIM
    kvd = N_KV_HEADS * hd
    tm = _row_tile(l)
    cos, sa, sb = rope_tabs
    tab = pl.BlockSpec((tm, hd), lambda i, j: (j, 0))
    vec = pl.BlockSpec((1, 1, d), lambda i, j: (i, 0, 0))
    widths = ([nq * hd] if nq else []) + [kvd, kvd]
    return pl.pallas_call(
        functools.partial(_qkv_kernel, nq=nq, rope=rope),
        grid=(b, l // tm),
        in_specs=[pl.BlockSpec((1, tm, d), lambda i, j: (i, j, 0)),
                  pl.BlockSpec((1, d), lambda i, j: (0, 0)), vec, vec,
                  pl.BlockSpec((d, n), lambda i, j: (0, 0)),
                  pl.BlockSpec((1, hd), lambda i, j: (0, 0)),
                  pl.BlockSpec((1, hd), lambda i, j: (0, 0)),
                  tab, tab, tab],
        out_specs=[pl.BlockSpec((1, tm, wd), lambda i, j: (i, j, 0)) for wd in widths],
        out_shape=[jax.ShapeDtypeStruct((b, l, wd), BF16) for wd in widths],
        compiler_params=_cparams("arbitrary", "arbitrary"),
        name="qkv_project",
    )(x, g.reshape(1, d), sh, sc, w, q_g.reshape(1, hd), k_g.reshape(1, hd), cos, sa, sb)


def _rope_tables(l):
    rows = l // GRID_W
    row = np.repeat(np.arange(rows, dtype=np.float64), GRID_W)
    col = np.tile(np.arange(GRID_W, dtype=np.float64), rows)
    axis = HEAD_DIM // 2
    inv = ROPE_THETA ** (-np.arange(0, axis, 2, dtype=np.float64) / axis)

    def axis_angles(pos):
        a = pos[:, None] * inv[None, :]
        return np.concatenate([a, a], axis=-1)

    ang = np.concatenate([axis_angles(row), axis_angles(col)], axis=-1)
    cos, sin = np.cos(ang), np.sin(ang)
    lane = np.arange(HEAD_DIM)[None, :] % axis
    upper = lane >= axis // 2
    sin_a = np.where(upper, sin, 0.0)
    sin_b = np.where(upper, 0.0, -sin)
    return tuple(jnp.asarray(t, F32) for t in (cos, sin_a, sin_b))


def _attn_kernel(q_ref, k_ref, vt_ref, o_ref, *, groups):
    k = k_ref[0]
    vt = vt_ref[0]
    hd = HEAD_DIM
    tq = q_ref.shape[1]
    hp = 2 if groups % 2 == 0 else 1
    def scores(g0):
        q = jnp.concatenate([q_ref[0, :, g * hd:(g + 1) * hd] for g in range(g0, g0 + hp)], axis=0)
        return lax.dot_general(k, q, (((1,), (1,)), ((), ())), preferred_element_type=F32)

    sts = [scores(g0) for g0 in range(0, groups, hp)]
    for g0, st in zip(range(0, groups, hp), sts):
        m = jnp.max(st, axis=0, keepdims=True)
        pt = jnp.exp2(st - m)
        den = jnp.sum(pt, axis=0, keepdims=True)
        ot = _dot(vt, pt.astype(BF16)) / den
        for i in range(hp):
            g = g0 + i
            o_ref[0, :, g * hd:(g + 1) * hd] = ot[:, i * tq:(i + 1) * tq].T.astype(o_ref.dtype)


def attention(q, k, v):
    b, l, qd = q.shape
    s = k.shape[1]
    hd = HEAD_DIM
    groups = qd // hd // N_KV_HEADS
    tq = _row_tile(l, 256)
    vt = jnp.swapaxes(v, 1, 2)
    return pl.pallas_call(
        functools.partial(_attn_kernel, groups=groups),
        grid=(b, N_KV_HEADS, l // tq),
        in_specs=[pl.BlockSpec((1, tq, groups * hd), lambda i, h, j: (i, j, h)),
                  pl.BlockSpec((1, s, hd), lambda i, h, j: (i, 0, h)),
                  pl.BlockSpec((1, hd, s), lambda i, h, j: (i, h, 0))],
        out_specs=pl.BlockSpec((1, tq, groups * hd), lambda i, h, j: (i, j, h)),
        out_shape=jax.ShapeDtypeStruct((b, l, qd), BF16),
        compiler_params=_cparams("arbitrary", "arbitrary", "arbitrary"),
        name="attention",
    )(q, k, vt)


def _router_kernel(x_ref, g_ref, sh_ref, sc_ref, wr_ref, h_ref, a_ref):
    h = _norm_mod(x_ref[0], g_ref[...], sh_ref[0], sc_ref[0])
    d = h.shape[1]
    logits = lax.dot_general(wr_ref[...], h, (((1,), (1,)), ((), ())), preferred_element_type=F32,
                             precision=HIGHEST)
    ex = jnp.exp(logits - jnp.max(logits, axis=0, keepdims=True))
    aff = ex / jnp.sum(ex, axis=0, keepdims=True)
    a_ref[0] = aff
    ne, tm = aff.shape
    hi = aff.astype(BF16).astype(F32)
    mid = (aff - hi).astype(BF16).astype(F32)
    lo = (aff - hi - mid).astype(BF16).astype(F32)
    pieces = jnp.concatenate([hi, mid, lo, jnp.zeros((LANES - 3 * ne, tm), F32)], axis=0)
    h_ref[0, :, 0:d] = h.astype(BF16)
    h_ref[0, :, d:d + LANES] = pieces.T.astype(BF16)


def moe_router(x, g, sh, sc, w_router_t):
    b, n, d = x.shape
    e = w_router_t.shape[0]
    assert 3 * e <= LANES and e % SUBLANES == 0
    tm = _row_tile(n)
    vec = pl.BlockSpec((1, 1, d), lambda i, j: (i, 0, 0))
    return pl.pallas_call(
        _router_kernel,
        grid=(b, n // tm),
        in_specs=[pl.BlockSpec((1, tm, d), lambda i, j: (i, j, 0)),
                  pl.BlockSpec((1, d), lambda i, j: (0, 0)), vec, vec,
                  pl.BlockSpec((e, d), lambda i, j: (0, 0))],
        out_specs=[pl.BlockSpec((1, tm, d + LANES), lambda i, j: (i, j, 0)),
                   pl.BlockSpec((1, e, tm), lambda i, j: (i, 0, j))],
        out_shape=[jax.ShapeDtypeStruct((b, n, d + LANES), BF16), jax.ShapeDtypeStruct((b, e, n), F32)],
        compiler_params=_cparams("arbitrary", "arbitrary"),
        name="moe_router",
    )(x, g.reshape(1, d), sh, sc, w_router_t)


def _cumsum_lanes(m, tri_ref, ind_ref):
    e, n = m.shape
    mb = m.astype(BF16)
    before = _dot(mb, ind_ref[...])
    outs = []
    for c in range(n // LANES):
        outs.append(_dot(mb[:, c * LANES:(c + 1) * LANES], tri_ref[...]) + before[:, c:c + 1])
    return jnp.concatenate(outs, axis=1), before


def _select_kernel(a_ref, tri_ref, ind_ref, slot_ref, before_ref, slot_t_ref, *, cap):
    aff = a_ref[0]
    bits = pltpu.bitcast(aff, I32)
    e = aff.shape[0]

    def body(i, thr):
        cand = thr | (1 << (30 - i))
        cnt = jnp.sum((bits >= cand).astype(F32), axis=1, keepdims=True)
        return jnp.where(cnt >= cap, cand, thr)

    thr = lax.fori_loop(0, 31, body, jnp.zeros((e, 1), I32))
    gt = bits > thr
    eq = bits == thr
    n_gt = jnp.sum(gt.astype(F32), axis=1, keepdims=True)
    eq_rank, _ = _cumsum_lanes(eq.astype(F32), tri_ref, ind_ref)
    sel = gt | (eq & (eq_rank <= cap - n_gt))
    pos, before = _cumsum_lanes(sel.astype(F32), tri_ref, ind_ref)
    slot = jnp.where(sel, pos - 1.0, -1.0)
    slot_ref[0] = slot
    before_ref[0] = before
    slot_t_ref[0] = slot.T


def moe_select(aff, cap):
    b, e, n = aff.shape
    assert n // LANES <= LANES
    tri = jnp.asarray(np.triu(np.ones((LANES, LANES), np.float32)), BF16)
    ind = jnp.asarray(np.arange(n)[:, None] < LANES * np.arange(LANES)[None, :], BF16)
    row = pl.BlockSpec((1, e, n), lambda i: (i, 0, 0))
    col = pl.BlockSpec((1, n, e), lambda i: (i, 0, 0))
    return pl.pallas_call(
        functools.partial(_select_kernel, cap=cap),
        grid=(b,),
        in_specs=[row, pl.BlockSpec((LANES, LANES), lambda i: (0, 0)), pl.BlockSpec((n, LANES), lambda i: (0, 0))],
        out_specs=[row, pl.BlockSpec((1, e, LANES), lambda i: (i, 0, 0)), col],
        out_shape=[jax.ShapeDtypeStruct((b, e, n), F32), jax.ShapeDtypeStruct((b, e, LANES), F32),
                   jax.ShapeDtypeStruct((b, n, e), F32)],
        compiler_params=_cparams("arbitrary"),
        name="moe_select",
    )(aff, tri, ind)


class _Windows:
    def __init__(self, before, n, cap):
        self.tile = min(n, MOE_TILE)
        self.nt = n // self.tile
        self.narrow = sorted({w for w in (min(cap, w) for w in MOE_WINDOWS)
                              if (w % LANES == 0 or LANES % w == 0) and w % BF16_ROWS == 0})
        self.wide = min(cap, self.tile + BF16_ROWS)
        first = before[:, :, 0:n // LANES:self.tile // LANES].astype(I32)
        count = jnp.concatenate([first[:, :, 1:], jnp.full_like(first[:, :, :1], cap)], axis=2) - first
        aligned = first // BF16_ROWS * BF16_ROWS
        starts = [jnp.clip(aligned, 0, cap - w) for w in self.narrow + [self.wide]]
        level = jnp.zeros(first.shape[:1] + first.shape[2:], I32)
        for i in reversed(range(len(self.narrow))):
            fits = jnp.all(first - starts[i] + count <= self.narrow[i], axis=1)
            level = jnp.where(fits, i + 1, level)
        self.args = tuple(s.reshape(-1) for s in starts) + (level.reshape(-1),)


def _gather_kernel(*refs, ne, narrow, wide):
    narrow_refs = refs[:len(narrow)]
    ww_ref, level_ref, slot_ref, h_ref, o_ref = refs[len(narrow):]
    bi, gi, ki = pl.program_id(0), pl.program_id(1), pl.program_id(2)
    nt = pl.num_programs(2)
    group, tk = slot_ref.shape[1], slot_ref.shape[2]
    level = level_ref[bi * nt + ki]

    @pl.when(ki == 0)
    def _():
        o_ref[...] = jnp.zeros_like(o_ref)

    slot = slot_ref[0].astype(I32)

    def start(ref, j):
        return pl.multiple_of(ref[(bi * ne + gi * group + j) * nt + ki], BF16_ROWS)

    def onehot(j, ws, width):
        jcol = lax.broadcasted_iota(I32, (width, tk), 0)
        return jnp.where(jcol + ws == slot[j:j + 1, :], 1.0, 0.0).astype(BF16)

    def add(j, ws, width, rows_f32):
        rows = pl.ds(ws, width)
        o_ref[0, j, rows, :] = o_ref[0, j, rows, :] + rows_f32.astype(o_ref.dtype)

    for i, (wn_ref, width) in enumerate(zip(narrow_refs, narrow)):
        @pl.when(level == i + 1)
        def _(wn_ref=wn_ref, width=width):
            starts = [start(wn_ref, j) for j in range(group)]
            lhs = jnp.concatenate([onehot(j, starts[j], width) for j in range(group)], axis=0)
            res = _dot(lhs, h_ref[0])
            for j in range(group):
                add(j, starts[j], width, res[j * width:(j + 1) * width])

    @pl.when(level == 0)
    def _():
        for j in range(group):
            ws = start(ww_ref, j)
            add(j, ws, wide, _dot(onehot(j, ws, wide), h_ref[0]))


def moe_gather(slot, win, h, cap):
    b, e, n = slot.shape
    dx = h.shape[2]
    group = math.gcd(e, MOE_GROUP)
    grid_spec = pltpu.PrefetchScalarGridSpec(
        num_scalar_prefetch=len(win.args),
        grid=(b, e // group, win.nt),
        in_specs=[pl.BlockSpec((1, group, win.tile), lambda i, g, k, *_: (i, g, k)),
                  pl.BlockSpec((1, win.tile, dx), lambda i, g, k, *_: (i, k, 0))],
        out_specs=pl.BlockSpec((1, group, cap, dx), lambda i, g, k, *_: (i, g, 0, 0)))
    return pl.pallas_call(
        functools.partial(_gather_kernel, ne=e, narrow=win.narrow, wide=win.wide),
        grid_spec=grid_spec,
        out_shape=jax.ShapeDtypeStruct((b, e, cap, dx), BF16),
        compiler_params=_cparams("arbitrary", "arbitrary", "arbitrary"),
        name="moe_gather",
    )(*win.args, slot, h)


def _ffn_kernel(*refs, ns):
    x_refs = refs[:ns]
    wg_ref, wu_ref, wd_ref = refs[ns:ns + 3]
    o_refs = refs[ns + 3:2 * ns + 3]
    wgb_ref, wub_ref, wdb_ref = refs[2 * ns + 3:]

    @pl.when(pl.program_id(1) == 0)
    def _():
        wgb_ref[...] = wg_ref[0, 0].astype(BF16)
        wub_ref[...] = wu_ref[0, 0].astype(BF16)
        wdb_ref[...] = wd_ref[0, 0].astype(BF16)

    d = wgb_ref.shape[0]
    ei, ne = pl.program_id(0), pl.num_programs(0)

    chunks = [[(0, i, r, min(r + FFN_ROWS, x_refs[0].shape[2]))]
              for i in range(x_refs[0].shape[0]) for r in range(0, x_refs[0].shape[2], FFN_ROWS)]
    for s in range(1, ns):
        chunks[-1] += [(s, i, 0, x_refs[s].shape[2]) for i in range(x_refs[s].shape[0])]

    def rows_of(chunk, c0, c1):
        parts = [x_refs[s][i, 0, r0:r1, c0:c1] for s, i, r0, r1 in chunk]
        return parts[0] if len(parts) == 1 else jnp.concatenate(parts, axis=0)

    ups = []
    for chunk in chunks:
        x = rows_of(chunk, 0, d)
        ups.append((_dot(x, wgb_ref[...]), _dot(x, wub_ref[...])))
    for chunk, (a, u) in zip(chunks, ups):
        ext = rows_of(chunk, d, d + LANES).astype(F32)
        lane = lax.broadcasted_iota(I32, ext.shape, 1)
        mine = (lane == ei) | (lane == ei + ne) | (lane == ei + 2 * ne)
        gate = jnp.sum(jnp.where(mine, ext, 0.0), axis=1, keepdims=True)
        y = _dot(((a / (1.0 + jnp.exp(-a))) * u).astype(BF16), wdb_ref[...]) * gate
        row = 0
        for s, i, r0, r1 in chunk:
            o_refs[s][i, 0, r0:r1, :] = y[row:row + r1 - r0].astype(o_refs[s].dtype)
            row += r1 - r0


def moe_ffn(xs_list, w_gate, w_up, w_down, layer):
    b, e, _, dx = xs_list[0].shape
    d, f = w_gate.shape[2], w_gate.shape[3]
    bb = 2 if b % 2 == 0 else 1

    def specs(width):
        return [pl.BlockSpec((bb, 1, xs.shape[2], width), lambda j, i: (i, j, 0, 0)) for xs in xs_list]

    return pl.pallas_call(
        functools.partial(_ffn_kernel, ns=len(xs_list)),
        grid=(e, b // bb),
        in_specs=specs(dx) + [pl.BlockSpec((1, 1, d, f), lambda j, i: (layer, j, 0, 0)),
                              pl.BlockSpec((1, 1, d, f), lambda j, i: (layer, j, 0, 0)),
                              pl.BlockSpec((1, 1, f, d), lambda j, i: (layer, j, 0, 0))],
        out_specs=specs(d),
        out_shape=[jax.ShapeDtypeStruct(xs.shape[:3] + (d,), BF16) for xs in xs_list],
        scratch_shapes=[pltpu.VMEM((d, f), BF16), pltpu.VMEM((d, f), BF16), pltpu.VMEM((f, d), BF16)],
        compiler_params=_cparams("arbitrary", "arbitrary"),
        name="moe_ffn",
    )(*xs_list, w_gate, w_up, w_down)


def _combine_kernel(*refs, ne, narrow, wide):
    narrow_refs = refs[:len(narrow)]
    ww_ref, level_ref, slot_ref, ys_hbm, lat_ref, gt_ref, o_ref, ys_buf, sem = refs[len(narrow):]
    bi, ti = pl.program_id(0), pl.program_id(1)
    nb, nt = pl.num_programs(0), pl.num_programs(1)
    cur = bi % 2

    def fetch(batch, buf):
        return pltpu.make_async_copy(ys_hbm.at[batch], ys_buf.at[buf], sem.at[buf])

    @pl.when(ti == 0)
    def _():
        @pl.when(bi == 0)
        def _():
            fetch(0, 0).start()

        fetch(bi, cur).wait()

        @pl.when(bi + 1 < nb)
        def _():
            fetch(bi + 1, 1 - cur).start()

    tn = lat_ref.shape[1]
    slot = slot_ref[0].astype(I32)

    def start(ref, e):
        return pl.multiple_of(ref[(bi * ne + e) * nt + ti], BF16_ROWS)

    def onehot(e, ws, width):
        jrow = lax.broadcasted_iota(I32, (tn, width), 1)
        return jnp.where(jrow + ws == slot[:, e:e + 1], 1.0, 0.0).astype(BF16)

    def onehot_lane_group(first, starts, width):
        lane = lax.broadcasted_iota(I32, (tn, LANES), 1)
        target = slot[:, first:first + 1] - starts[first]
        for i in range(1, LANES // width):
            e = first + i
            target = jnp.where(lane >= i * width, slot[:, e:e + 1] - (starts[e] - i * width), target)
        return jnp.where(target == lane, 1.0, 0.0).astype(BF16)

    def finish(acc):
        o_ref[0] = lat_ref[0] + gt_ref[0] * acc

    level = level_ref[bi * nt + ti]
    for i, (wn_ref, width) in enumerate(zip(narrow_refs, narrow)):
        @pl.when(level == i + 1)
        def _(wn_ref=wn_ref, width=width):
            starts = [start(wn_ref, e) for e in range(ne)]
            if width % LANES == 0:
                lhs = jnp.concatenate([onehot(e, starts[e], width) for e in range(ne)], axis=1)
            else:
                lhs = jnp.concatenate([onehot_lane_group(e, starts, width)
                                       for e in range(0, ne, LANES // width)], axis=1)
            rhs = jnp.concatenate([ys_buf[cur, e, pl.ds(starts[e], width), :] for e in range(ne)], axis=0)
            finish(_dot(lhs, rhs))

    @pl.when(level == 0)
    def _():
        acc = jnp.zeros((tn, lat_ref.shape[2]), F32)
        for e in range(ne):
            ws = start(ww_ref, e)
            acc = acc + _dot(onehot(e, ws, wide), ys_buf[cur, e, pl.ds(ws, wide), :])
        finish(acc)


def moe_combine(slot_t, win, ys, lat, gate_vec, cap):
    b, n, e = slot_t.shape
    d = lat.shape[2]
    tn = win.tile
    grid_spec = pltpu.PrefetchScalarGridSpec(
        num_scalar_prefetch=len(win.args),
        grid=(b, win.nt),
        in_specs=[pl.BlockSpec((1, tn, e), lambda i, j, *_: (i, j, 0)),
                  pl.BlockSpec(memory_space=pl.ANY),
                  pl.BlockSpec((1, tn, d), lambda i, j, *_: (i, j, 0)),
                  pl.BlockSpec((1, 1, d), lambda i, j, *_: (i, 0, 0))],
        out_specs=pl.BlockSpec((1, tn, d), lambda i, j, *_: (i, j, 0)),
        scratch_shapes=[pltpu.VMEM((2, e, cap, d), BF16), pltpu.SemaphoreType.DMA((2,))])
    return pl.pallas_call(
        functools.partial(_combine_kernel, ne=e, narrow=win.narrow, wide=win.wide),
        grid_spec=grid_spec,
        out_shape=jax.ShapeDtypeStruct((b, n, d), F32),
        compiler_params=_cparams("arbitrary", "arbitrary"),
        name="moe_combine",
    )(*win.args, slot_t, ys, lat, gate_vec)


def expert_choice_moe(streams, g, w_router, w_gate, w_up, w_down, layer):
    e = w_router.shape[1]
    routed = []
    for lat, sh, sc, _ in streams:
        cap = EC_CAPACITY_FACTOR * lat.shape[1] // e
        h, aff = moe_router(lat, g, sh, sc, w_router.T)
        slot, before, slot_t = moe_select(aff, cap)
        win = _Windows(before, lat.shape[1], cap)
        routed.append((moe_gather(slot, win, h, cap), slot_t, win, cap))
    ys_list = moe_ffn([r[0] for r in routed], w_gate, w_up, w_down, layer)
    return [moe_combine(slot_t, win, ys, lat, gate_vec, cap)
            for (lat, _, _, gate_vec), (_, slot_t, win, cap), ys in zip(streams, routed, ys_list)]


def kernel(x, c, ctx, c_ctx, ada_w, ada_b, norm_g, sc_w_in, sc_conv, sc_w_out, hy_w_in, hy_conv_w, hy_f_w1,
           hy_f_b1, hy_f_w2, hy_f_b2, hy_f_w3, hy_sin_freq, hy_skip, hy_w_out, at_w_qkv, at_q_g, at_k_g, at_w_o,
           moe_router_w, moe_w_gate, moe_w_up, moe_w_down):
    b, l, d = x.shape
    depth = ada_w.shape[0]
    assert b + 1 <= MOD_ROWS
    qd = at_w_o.shape[1]
    nq = qd // HEAD_DIM
    attn_layers = list(range(MIX_ATTN, depth, N_MIXERS))
    last_attn = max(attn_layers, default=-1)

    cond = jnp.zeros((MOD_ROWS, d), F32).at[:b].set(c).at[b].set(c_ctx)
    mods = ada_modulation(cond, ada_w, ada_b)

    lat, cx = x, ctx
    for i in range(depth):
        kind = i % N_MIXERS
        j = i // N_MIXERS
        need_ctx = i <= last_attn
        upd_ctx = i < last_attn
        m_l = mods[i, :b].reshape(b, 1, 6, d)
        sh1, sc1, g1, sh2, sc2, g2 = (m_l[:, :, k] for k in range(6))
        if need_ctx:
            m_c = jnp.broadcast_to(mods[i, b].reshape(1, 1, 6, d), (b, 1, 6, d))
            csh1, csc1, cg1, csh2, csc2, cg2 = (m_c[:, :, k] for k in range(6))
        streams = [(lat, sh1, sc1, g1)]
        if upd_ctx:
            streams.append((cx, csh1, csc1, cg1))

        outs = []
        if kind == MIX_SHORTCONV:
            w_in, w_out = sc_w_in[j].astype(BF16), sc_w_out[j].astype(BF16)
            for s, sh, sc, gt in streams:
                outs.append(sc_mixer(s, norm_g[i, 0], sh, sc, w_in, sc_conv[j], w_out, gt))
        elif kind == MIX_HYENA:
            w_in, w_out = hy_w_in[j].astype(BF16), hy_w_out[j].astype(BF16)
            for s, sh, sc, gt in streams:
                p = nm_matmul(s, norm_g[i, 0], sh, sc, w_in)
                z = hyena_mixer(p, hy_conv_w[j], hy_f_w1[j], hy_f_b1[j], hy_f_w2[j], hy_f_b2[j], hy_f_w3[j],
                                hy_sin_freq[j], hy_skip[j])
                outs.append(proj_residual(z, w_out, s, gt))
        else:
            w_qkv, w_o = at_w_qkv[j].astype(BF16), at_w_o[j].astype(BF16)
            q_l, k_l, v_l = qkv_project(lat, norm_g[i, 0], sh1, sc1, w_qkv, at_q_g[j], at_k_g[j],
                                        _rope_tables(l), nq, True)
            lc = cx.shape[1]
            dummy = tuple(jnp.zeros((lc, HEAD_DIM), F32) for _ in range(3))
            if upd_ctx:
                q_c, k_c, v_c = qkv_project(cx, norm_g[i, 0], csh1, csc1, w_qkv, at_q_g[j], at_k_g[j],
                                            dummy, nq, False)
            else:
                k_c, v_c = qkv_project(cx, norm_g[i, 0], csh1, csc1, w_qkv[:, qd:], at_q_g[j], at_k_g[j],
                                       dummy, 0, False)
            v_all = jnp.concatenate([v_c, v_l], axis=1)
            o_l = attention(q_l, jnp.concatenate([k_c, k_l], axis=1), v_all)
            outs.append(proj_residual(o_l, w_o, lat, g1))
            if upd_ctx:
                outs.append(proj_residual(attention(q_c, k_c, v_c), w_o, cx, cg1))

        moe_streams = [(outs[0], sh2, sc2, g2)]
        if upd_ctx:
            moe_streams.append((outs[1], csh2, csc2, cg2))
        moe_out = expert_choice_moe(moe_streams, norm_g[i, 1], moe_router_w[i], moe_w_gate, moe_w_up, moe_w_down, i)
        lat = moe_out[0]
        if upd_ctx:
            cx = moe_out[1]
    return lat
```

```python
import functools
import math

import jax
import jax.numpy as jnp
import numpy as np
from jax import lax
from jax.experimental import pallas as pl
from jax.experimental.pallas import tpu as pltpu

F32 = jnp.float32
BF16 = jnp.bfloat16
I32 = jnp.int32
HIGHEST = lax.Precision.HIGHEST

N_MIXERS = 3
MIX_SHORTCONV, MIX_HYENA, MIX_ATTN = 0, 1, 2
CONV_W = 3
EPS = 1e-6
HY_ORDER = 2
HY_EMB = 33
HY_BANDS = (HY_EMB - 1) // 2
HY_FAST_DECAY = 0.3
HY_SLOW_DECAY = 1.5
HY_TARGET = 1e-2
HEAD_DIM = 128
N_KV_HEADS = 2
GRID_W = 64
ROPE_THETA = 10000.0
EC_CAPACITY_FACTOR = 2
Q_PRESCALE = HEAD_DIM ** -0.5 * math.log2(math.e)

LANES = 128
SUBLANES = 8
BF16_ROWS = 16
VMEM_LIMIT_BYTES = 56 * 1024 * 1024
ROW_TILE = 1024
MOD_ROWS = 16
FFT_N2 = 128
HY_DT = LANES
MOE_TILE = 2 * LANES
MOE_WINDOWS = (LANES // 2, LANES)
MOE_GROUP = 2 * SUBLANES
FFN_ROWS = 2 * LANES
DFT_UNROLL = 8
DFT_UNROLL_STAGE1 = 16


def _cparams(*sem):
    return pltpu.CompilerParams(dimension_semantics=sem, vmem_limit_bytes=VMEM_LIMIT_BYTES)


def _row_tile(n, cap=ROW_TILE):
    t = min(n, cap)
    assert n % t == 0
    return t


def _dot(a, b):
    return jnp.dot(a, b, preferred_element_type=F32)


def _ada_kernel(s_ref, w_ref, b_ref, o_ref):
    s = s_ref[...]
    s = s / (1.0 + jnp.exp(-s))
    o_ref[0] = jnp.dot(s, w_ref[0], preferred_element_type=F32, precision=HIGHEST) + b_ref[0]


def ada_modulation(cond, ada_w, ada_b):
    depth, d, n = ada_w.shape
    tn = 1536 if n % 1536 == 0 else n
    return pl.pallas_call(
        _ada_kernel,
        grid=(depth, n // tn),
        in_specs=[pl.BlockSpec((MOD_ROWS, d), lambda i, j: (0, 0)),
                  pl.BlockSpec((1, d, tn), lambda i, j: (i, 0, j)),
                  pl.BlockSpec((1, 1, tn), lambda i, j: (i, 0, j))],
        out_specs=pl.BlockSpec((1, MOD_ROWS, tn), lambda i, j: (i, 0, j)),
        out_shape=jax.ShapeDtypeStruct((depth, MOD_ROWS, n), F32),
        compiler_params=_cparams("arbitrary", "arbitrary"),
        name="ada_modulation",
    )(cond, ada_w, ada_b.reshape(depth, 1, n))


def _norm_mod(x, g, sh, sc):
    y = x * lax.rsqrt(jnp.mean(x * x, axis=-1, keepdims=True) + EPS)
    return (y * g) * (1.0 + sc) + sh


def _nm_matmul_kernel(x_ref, g_ref, sh_ref, sc_ref, w_ref, o_ref):
    h = _norm_mod(x_ref[0], g_ref[...], sh_ref[0], sc_ref[0]).astype(BF16)
    o_ref[0] = _dot(h, w_ref[...]).astype(o_ref.dtype)


def nm_matmul(x, g, sh, sc, w, out_dtype=BF16):
    b, l, d = x.shape
    n = w.shape[1]
    tm = _row_tile(l)
    return pl.pallas_call(
        _nm_matmul_kernel,
        grid=(b, l // tm),
        in_specs=[pl.BlockSpec((1, tm, d), lambda i, j: (i, j, 0)),
                  pl.BlockSpec((1, d), lambda i, j: (0, 0)),
                  pl.BlockSpec((1, 1, d), lambda i, j: (i, 0, 0)),
                  pl.BlockSpec((1, 1, d), lambda i, j: (i, 0, 0)),
                  pl.BlockSpec((d, n), lambda i, j: (0, 0))],
        out_specs=pl.BlockSpec((1, tm, n), lambda i, j: (i, j, 0)),
        out_shape=jax.ShapeDtypeStruct((b, l, n), out_dtype),
        compiler_params=_cparams("arbitrary", "arbitrary"),
        name="nm_matmul",
    )(x, g.reshape(1, d), sh, sc, w)


def _route_specs(b, l, d, e, tm):
    vec = pl.BlockSpec((1, 1, d), lambda i, j: (i, 0, 0))
    in_specs = [pl.BlockSpec((1, d), lambda i, j: (0, 0)), vec, vec, pl.BlockSpec((e, d), lambda i, j: (0, 0))]
    out_specs = [pl.BlockSpec((1, tm, d + LANES), lambda i, j: (i, j, 0)),
                 pl.BlockSpec((1, e, tm), lambda i, j: (i, 0, j))]
    out_shape = [jax.ShapeDtypeStruct((b, l, d + LANES), BF16), jax.ShapeDtypeStruct((b, e, l), F32)]
    return in_specs, out_specs, out_shape


def _proj_res_kernel(a_ref, w_ref, lat_ref, gt_ref, g2_ref, sh2_ref, sc2_ref, wr_ref, o_ref, h_ref, aff_ref):
    out = lat_ref[0] + gt_ref[0] * _dot(a_ref[0], w_ref[...])
    o_ref[0] = out
    _route(out, g2_ref, sh2_ref, sc2_ref, wr_ref, h_ref, aff_ref)


def proj_residual(a, w, lat, gate, route):
    b, l, k = a.shape
    d = w.shape[1]
    g2, sh2, sc2, wr = route
    tm = _row_tile(l, 512)
    r_in, r_out, r_shape = _route_specs(b, l, d, wr.shape[0], tm)
    return pl.pallas_call(
        _proj_res_kernel,
        grid=(b, l // tm),
        in_specs=[pl.BlockSpec((1, tm, k), lambda i, j: (i, j, 0)),
                  pl.BlockSpec((k, d), lambda i, j: (0, 0)),
                  pl.BlockSpec((1, tm, d), lambda i, j: (i, j, 0)),
                  pl.BlockSpec((1, 1, d), lambda i, j: (i, 0, 0))] + r_in,
        out_specs=[pl.BlockSpec((1, tm, d), lambda i, j: (i, j, 0))] + r_out,
        out_shape=[jax.ShapeDtypeStruct((b, l, d), F32)] + r_shape,
        compiler_params=_cparams("arbitrary", "arbitrary"),
        name="proj_residual",
    )(a, w, lat, gate, g2.reshape(1, d), sh2, sc2, wr)


def _sc_mixer_kernel(x_ref, xp_ref, xn_ref, g_ref, sh_ref, sc_ref, win_ref, cw_ref, wout_ref, gt_ref,
                     g2_ref, sh2_ref, sc2_ref, wr_ref, o_ref, h_ref, aff_ref, u_ref, *, tm):
    j = pl.program_id(1)
    nj = pl.num_programs(1)
    d = u_ref.shape[1]
    x = x_ref[0]
    p = _dot(_norm_mod(x, g_ref[...], sh_ref[0], sc_ref[0]).astype(BF16), win_ref[...])
    u_ref[pl.ds(SUBLANES, tm), :] = p[:, d:2 * d] * p[:, 2 * d:]
    halo = jnp.concatenate([xp_ref[0], xn_ref[0]], axis=0)
    ph = _dot(_norm_mod(halo, g_ref[...], sh_ref[0], sc_ref[0]).astype(BF16), win_ref[:, d:])
    uh = ph[:, :d] * ph[:, d:]
    prev = jnp.where(j == 0, 0.0, uh[SUBLANES - 1:SUBLANES])
    nxt = jnp.where(j == nj - 1, 0.0, uh[SUBLANES:SUBLANES + 1])
    u_ref[pl.ds(0, SUBLANES), :] = jnp.broadcast_to(prev, (SUBLANES, d))
    u_ref[pl.ds(SUBLANES + tm, SUBLANES), :] = jnp.broadcast_to(nxt, (SUBLANES, d))
    cw = cw_ref[...]
    y = (u_ref[pl.ds(SUBLANES - 1, tm), :] * cw[0:1] + u_ref[pl.ds(SUBLANES, tm), :] * cw[1:2]
         + u_ref[pl.ds(SUBLANES + 1, tm), :] * cw[2:3])
    m = (p[:, :d] * y).astype(BF16)
    out = x + gt_ref[0] * _dot(m, wout_ref[...])
    o_ref[0] = out
    _route(out, g2_ref, sh2_ref, sc2_ref, wr_ref, h_ref, aff_ref)


def sc_mixer(x, g, sh, sc, w_in, conv_w, w_out, gate, route):
    b, l, d = x.shape
    g2, sh2, sc2, wr = route
    tm = _row_tile(l, 512)
    hb = tm // SUBLANES
    nhb = l // SUBLANES
    vec = pl.BlockSpec((1, 1, d), lambda i, j: (i, 0, 0))
    r_in, r_out, r_shape = _route_specs(b, l, d, wr.shape[0], tm)
    return pl.pallas_call(
        functools.partial(_sc_mixer_kernel, tm=tm),
        grid=(b, l // tm),
        in_specs=[pl.BlockSpec((1, tm, d), lambda i, j: (i, j, 0)),
                  pl.BlockSpec((1, SUBLANES, d), lambda i, j: (i, jnp.maximum(j * hb - 1, 0), 0)),
                  pl.BlockSpec((1, SUBLANES, d), lambda i, j: (i, jnp.minimum((j + 1) * hb, nhb - 1), 0)),
                  pl.BlockSpec((1, d), lambda i, j: (0, 0)), vec, vec,
                  pl.BlockSpec((d, 3 * d), lambda i, j: (0, 0)),
                  pl.BlockSpec((CONV_W, d), lambda i, j: (0, 0)),
                  pl.BlockSpec((d, d), lambda i, j: (0, 0)),
                  vec] + r_in,
        out_specs=[pl.BlockSpec((1, tm, d), lambda i, j: (i, j, 0))] + r_out,
        out_shape=[jax.ShapeDtypeStruct((b, l, d), F32)] + r_shape,
        scratch_shapes=[pltpu.VMEM((tm + 2 * SUBLANES, d), F32)],
        compiler_params=_cparams("arbitrary", "arbitrary"),
        name="sc_mixer",
    )(x, x, x, g.reshape(1, d), sh, sc, w_in, conv_w, w_out, gate, g2.reshape(1, d), sh2, sc2, wr)


def _hy_filter_kernel(z_ref, w1_ref, b1_ref, w2_ref, b2_ref, fr_ref, w3f_ref, w3b_ref, t_ref, dl_ref,
                      of_ref, ob_ref, a_ref):
    @pl.when((pl.program_id(0) == 0) & (pl.program_id(1) == 0))
    def _():
        fr = fr_ref[...]
        a1 = jnp.sin(fr * (jnp.dot(z_ref[...], w1_ref[...], preferred_element_type=F32, precision=HIGHEST)
                           + b1_ref[...]))
        a_ref[...] = jnp.sin(fr * (jnp.dot(a1, w2_ref[...], preferred_element_type=F32, precision=HIGHEST)
                                   + b2_ref[...]))

    a = a_ref[...]
    decay = jnp.exp(-t_ref[...] * dl_ref[...])
    hf = jnp.dot(a, w3f_ref[...], preferred_element_type=F32, precision=HIGHEST) * decay
    hb = jnp.dot(a, w3b_ref[...], preferred_element_type=F32, precision=HIGHEST) * decay
    row = lax.broadcasted_iota(I32, hb.shape, 0)
    hb = jnp.where(row == 0, 0.0, hb)
    inv = 1.0 / (jnp.sum(jnp.abs(hf), axis=0, keepdims=True) + jnp.sum(jnp.abs(hb), axis=0, keepdims=True))
    of_ref[0] = hf * inv
    ob_ref[0] = hb * inv


def hy_filter_taps(l, d, f_w1, f_b1, f_w2, f_b2, f_w3, sin_freq):
    hid = f_w1.shape[1]
    hp = LANES
    t = np.linspace(0.0, 1.0, l)[:, None]
    bands = np.linspace(1e-4, HY_BANDS - 1, HY_BANDS)[None, :]
    ang = (2.0 * math.pi / l) * np.arange(l)[:, None] * bands
    z = np.zeros((l, hp), np.float64)
    z[:, :HY_EMB] = np.concatenate([t, np.cos(ang), -np.sin(ang)], axis=-1)
    deltas = np.abs(np.linspace(math.log(HY_TARGET) / HY_SLOW_DECAY, math.log(HY_TARGET) / HY_FAST_DECAY, d))[None, :]

    def pad2(a, r, c):
        return jnp.zeros((r, c), F32).at[:a.shape[0], :a.shape[1]].set(a.astype(F32))

    w1 = pad2(f_w1, hp, hp)
    w2 = pad2(f_w2, hp, hp)
    w3 = pad2(f_w3, hp, f_w3.shape[1])
    b1 = pad2(f_b1[None], 1, hp)
    b2 = pad2(f_b2[None], 1, hp)
    fr = pad2(sin_freq[None], 1, hp)
    tc = min(d, 2 * LANES)
    nd = d // tc
    full = lambda o, j: (0, 0)
    return pl.pallas_call(
        _hy_filter_kernel,
        grid=(HY_ORDER, nd),
        in_specs=[pl.BlockSpec((l, hp), full), pl.BlockSpec((hp, hp), full), pl.BlockSpec((1, hp), full),
                  pl.BlockSpec((hp, hp), full), pl.BlockSpec((1, hp), full), pl.BlockSpec((1, hp), full),
                  pl.BlockSpec((hp, tc), lambda o, j: (0, o * nd + j)),
                  pl.BlockSpec((hp, tc), lambda o, j: (0, (HY_ORDER + o) * nd + j)),
                  pl.BlockSpec((l, 1), full),
                  pl.BlockSpec((1, tc), lambda o, j: (0, j))],
        out_specs=[pl.BlockSpec((1, l, tc), lambda o, j: (o, 0, j)),
                   pl.BlockSpec((1, l, tc), lambda o, j: (o, 0, j))],
        out_shape=[jax.ShapeDtypeStruct((HY_ORDER, l, d), F32)] * 2,
        scratch_shapes=[pltpu.VMEM((l, hp), F32)],
        compiler_params=_cparams("arbitrary", "arbitrary"),
        name="hy_filter_taps",
    )(jnp.asarray(z, F32), w1, b1, w2, b2, fr, w3, w3, jnp.asarray(t, F32), jnp.asarray(deltas, F32))


def _dft_consts(l):
    n = 2 * l
    if n <= 4 * FFT_N2:
        n1, n2 = 1, n
    else:
        n1, n2 = n // FFT_N2, FFT_N2
    if n1 == 1:
        k = np.arange(n)[:, None]
        t = np.arange(l)[None, :]
        ang = 2.0 * np.pi * (k * t % n) / n
        c, s = np.cos(ang), np.sin(ang)
        fwd = np.block([[c, s], [-s, c]])
        inv = np.block([[c.T, -s.T], [s.T, c.T]]) / n
        return n1, n2, dict(fwd=jnp.asarray(fwd, BF16), inv=jnp.asarray(inv, BF16))
    n1h = n1 // 2
    k1 = np.arange(n1)[None, :, None]
    m1 = np.arange(n1h)[None, None, :]
    r2 = np.arange(n2)[:, None, None]
    ang = 2.0 * np.pi * ((k1 * (n2 * m1 + r2)) % n) / n
    c, s = np.cos(ang), np.sin(ang)
    m1f = np.concatenate([np.concatenate([c, s], 2), np.concatenate([-s, c], 2)], 1)
    ct, st = np.swapaxes(c, 1, 2) / n, np.swapaxes(s, 1, 2) / n
    m1i = np.concatenate([np.concatenate([ct, -st], 2), np.concatenate([st, ct], 2)], 1)
    a2 = 2.0 * np.pi * ((np.arange(n2)[:, None] * np.arange(n2)[None, :]) % n2) / n2
    c2, s2 = np.cos(a2), np.sin(a2)
    f2 = np.block([[c2, s2], [-s2, c2]])
    f2i = np.block([[c2, -s2], [s2, c2]])

    def il(m):
        return np.stack([np.arange(m), m + np.arange(m)], 1).reshape(-1)

    m1f = m1f[:, il(n1), :][:, :, il(n1h)]
    m1i = m1i[:, il(n1h), :][:, :, il(n1)]
    f2 = f2[:, il(n2)]
    f2i = f2i[il(n2), :][:, il(n2)]
    half = n2 // 2
    m1f = np.concatenate([m1f[:half], m1f[half:]], 2)
    m1i = np.concatenate([m1i[:half], m1i[half:]], 2)
    return n1, n2, dict(m1f=jnp.asarray(m1f, BF16), m1i=jnp.asarray(m1i, BF16),
                        f2=jnp.asarray(f2, BF16), f2i=jnp.asarray(f2i, BF16))


def _blockdiag(a, b):
    z = jnp.zeros_like(a)
    return jnp.concatenate([jnp.concatenate([a, z], axis=1), jnp.concatenate([z, b], axis=1)], axis=0)


def _halves(y):
    return y[:, :LANES], y[:, LANES:]


HI_HALF = -65536


def _pack_pair(re, im):
    lo = (pltpu.bitcast(re.astype(BF16).astype(F32), I32) >> 16) & 0xFFFF
    if im is None:
        return lo
    return lo | (pltpu.bitcast(im.astype(BF16).astype(F32), I32) & HI_HALF)


def _unpack_pair(w):
    return pltpu.bitcast(w << 16, F32), pltpu.bitcast(w & HI_HALF, F32)


def _as_rows(w):
    return pltpu.bitcast(w, BF16)


def _as_words(y):
    return pltpu.bitcast(y.astype(BF16), I32)


def _fwd_stage1(x_pk, a_pk, m1f_ref, n1, n2):
    n1h, half = n1 // 2, n2 // 2

    def column(r):
        return _as_rows(x_pk[pl.ds(r, n1h, stride=n2), :])

    def body(r, carry):
        w = _as_words(_dot(m1f_ref[r], _blockdiag(column(r), column(r + half))))
        for rr, ww in zip((r, r + half), _halves(w)):
            a_pk[pl.ds(pl.multiple_of(rr * n1, n1), n1), :] = ww
        return carry

    lax.fori_loop(0, half, body, 0, unroll=min(half, DFT_UNROLL_STAGE1))


def _inv_stage1(b_pk, x_pk, m1i_ref, n1, n2):
    n1h, half = n1 // 2, n2 // 2

    def column(r):
        return _as_rows(b_pk[pl.ds(pl.multiple_of(r * n1, n1), n1), :])

    def body(r, carry):
        w = _as_words(_dot(m1i_ref[r], _blockdiag(column(r), column(r + half))))
        for rr, ww in zip((r, r + half), _halves(w)):
            x_pk[pl.ds(rr, n1h, stride=n2), :] = ww
        return carry

    lax.fori_loop(0, half, body, 0, unroll=min(half, DFT_UNROLL_STAGE1))


def _load_chunk(a_pk, k1, n1, n2):
    ra = pl.ds(k1, n2, stride=n1)
    rb = pl.ds(k1 + n1 // 2, n2, stride=n1)
    return jnp.concatenate([_as_rows(a_pk[ra, :]), _as_rows(a_pk[rb, :])], axis=1)


def _hy_spec_kernel(*refs, l, n1, n2, dense):
    if dense:
        tf_ref, tb_ref, sk_ref, fwd_ref, o_ref = refs
        n = n2
        fw = fwd_ref[:, 0:l]
        sf = _dot(fw, tf_ref[0].astype(BF16))
        sb = _dot(fw, tb_ref[0].astype(BF16))
        o_ref[0, 0] = (sf[:n] + sb[:n] + sk_ref[0]).astype(o_ref.dtype)
        o_ref[0, 1] = (sf[n:] - sb[n:]).astype(o_ref.dtype)
        return
    tf_ref, tb_ref, sk_ref, m1f_ref, f2_ref, o_ref, x_pk, a_pk, h_re, h_im = refs
    for run, t_ref in enumerate((tf_ref, tb_ref)):
        x_pk[...] = _pack_pair(t_ref[0], None)
        _fwd_stage1(x_pk, a_pk, m1f_ref, n1, n2)

        def body(k1, carry, run=run):
            x = _dot(f2_ref[...], _load_chunk(a_pk, k1, n1, n2))
            for kk, xx in zip((k1, k1 + n1 // 2), _halves(x)):
                rows = pl.ds(pl.multiple_of(kk * n2, n2), n2)
                if run == 0:
                    h_re[rows, :] = xx[:n2]
                    h_im[rows, :] = xx[n2:]
                else:
                    h_re[rows, :] = h_re[rows, :] + xx[:n2]
                    h_im[rows, :] = h_im[rows, :] - xx[n2:]
            return carry

        lax.fori_loop(0, n1 // 2, body, 0, unroll=DFT_UNROLL)
    o_ref[0, 0] = (h_re[...] + sk_ref[0]).astype(o_ref.dtype)
    o_ref[0, 1] = h_im[...].astype(o_ref.dtype)


def hy_spectrum(taps_f, taps_b, skip, consts, n1, n2):
    o, l, d = taps_f.shape
    n = 2 * l
    dt = HY_DT
    dense = n1 == 1
    tap_spec = pl.BlockSpec((1, l, dt), lambda i, j: (i, 0, j))
    in_specs = [tap_spec, tap_spec, pl.BlockSpec((1, 1, dt), lambda i, j: (i, 0, j))]
    if dense:
        mats = [consts["fwd"]]
        scratch = []
    else:
        mats = [consts["m1f"], consts["f2"]]
        scratch = [pltpu.VMEM((l, dt), I32), pltpu.VMEM((n, dt), I32)] + [pltpu.VMEM((n, dt), F32)] * 2
    in_specs += [pl.BlockSpec(m.shape, lambda i, j, nd=m.ndim: (0,) * nd) for m in mats]
    return pl.pallas_call(
        functools.partial(_hy_spec_kernel, l=l, n1=n1, n2=n2, dense=dense),
        grid=(o, d // dt),
        in_specs=in_specs,
        out_specs=pl.BlockSpec((1, 2, n, dt), lambda i, j: (i, 0, 0, j)),
        out_shape=jax.ShapeDtypeStruct((o, 2, n, d), BF16),
        scratch_shapes=scratch,
        compiler_params=_cparams("arbitrary", "arbitrary"),
        name="hy_spectrum",
    )(taps_f, taps_b, skip.reshape(o, 1, d), *mats)


def _conv3_rows(src, cw, pad_ref, l):
    c = pad_ref.shape[1]
    pad_ref[pl.ds(0, SUBLANES), :] = jnp.zeros((SUBLANES, c), F32)
    pad_ref[pl.ds(SUBLANES + l, SUBLANES), :] = jnp.zeros((SUBLANES, c), F32)
    pad_ref[pl.ds(SUBLANES, l), :] = src
    return (pad_ref[pl.ds(SUBLANES - 1, l), :] * cw[0:1] + pad_ref[pl.ds(SUBLANES, l), :] * cw[1:2]
            + pad_ref[pl.ds(SUBLANES + 1, l), :] * cw[2:3])


def _hy_conv_kernel(*refs, l, n1, n2, dense, conv_z):
    if dense:
        z_ref, g_ref, cwz_ref, cwg_ref, h_ref, fwd_ref, inv_ref, o_ref, pad_ref = refs
    else:
        (z_ref, g_ref, cwz_ref, cwg_ref, h_ref, m1f_ref, m1i_ref, f2_ref, f2i_ref, o_ref,
         x_pk, a_pk, b_pk, pad_ref) = refs
    zs = []
    for e in range(2):
        z = z_ref[e, 0].astype(F32)
        zs.append(_conv3_rows(z, cwz_ref[...], pad_ref, l) if conv_z else z)
    if dense:
        n = n2
        s = _dot(fwd_ref[...], jnp.concatenate(zs, axis=0).astype(BF16))
        hr = h_ref[0, 0].astype(F32)
        hi = h_ref[0, 1].astype(F32)
        zr = s[:n] * hr - s[n:] * hi
        zi = s[:n] * hi + s[n:] * hr
        y = _dot(inv_ref[...], jnp.concatenate([zr, zi], axis=0).astype(BF16))
        ys = (y[:l], y[l:])
    else:
        x_pk[...] = _pack_pair(zs[0], zs[1])
        _fwd_stage1(x_pk, a_pk, m1f_ref, n1, n2)

        nk = n1 // 2

        def spectrum_step(k1, carry):
            s = _dot(f2_ref[...], _load_chunk(a_pk, k1, n1, n2))
            rows = [pl.ds(pl.multiple_of(k * n2, n2), n2) for k in (k1, k1 + nk)]
            hr = jnp.concatenate([h_ref[0, 0, r, :] for r in rows], axis=1).astype(F32)
            hi = jnp.concatenate([h_ref[0, 1, r, :] for r in rows], axis=1).astype(F32)
            zr = s[:n2] * hr - s[n2:] * hi
            zi = s[:n2] * hi + s[n2:] * hr
            for r, zr_half, zi_half in zip(rows, _halves(zr), _halves(zi)):
                b_pk[r, :] = _pack_pair(zr_half, zi_half)
            return carry

        lax.fori_loop(0, nk, spectrum_step, 0, unroll=min(nk, DFT_UNROLL))

        def inverse_step(k1, carry):
            kk = (k1, k1 + nk)
            z = jnp.concatenate([_as_rows(b_pk[pl.ds(pl.multiple_of(k * n2, n2), n2), :]) for k in kk], axis=1)
            w = _as_words(_dot(f2i_ref[...], z))
            for k, ww in zip(kk, _halves(w)):
                a_pk[pl.ds(k, n2, stride=n1), :] = ww
            return carry

        lax.fori_loop(0, nk, inverse_step, 0, unroll=min(nk, DFT_UNROLL))
        _inv_stage1(a_pk, x_pk, m1i_ref, n1, n2)
        ys = _unpack_pair(x_pk[...])
    for e in range(2):
        g = _conv3_rows(g_ref[e, 0].astype(F32), cwg_ref[...], pad_ref, l)
        o_ref[e, 0] = (g * ys[e]).astype(o_ref.dtype)


def hy_conv(zsrc, zcol, p4, gcol, conv_w, spec, order, consts, n1, n2, conv_z):
    _, pairs, l, _ = zsrc.shape
    d = spec.shape[3]
    n = 2 * l
    dt = HY_DT
    nd = d // dt
    dense = n1 == 1
    once = pl.Buffered(1)
    in_specs = [pl.BlockSpec((2, 1, l, dt), lambda j, i: (0, i, 0, zcol * nd + j)),
                pl.BlockSpec((2, 1, l, dt), lambda j, i: (0, i, 0, gcol * nd + j)),
                pl.BlockSpec((CONV_W, dt), lambda j, i: (0, j)),
                pl.BlockSpec((CONV_W, dt), lambda j, i: (0, gcol * nd + j)),
                pl.BlockSpec((1, 2, n, dt), lambda j, i: (order, 0, 0, j), pipeline_mode=once)]
    if dense:
        mats = [consts["fwd"], consts["inv"]]
        scratch = []
    else:
        mats = [consts["m1f"], consts["m1i"], consts["f2"], consts["f2i"]]
        scratch = [pltpu.VMEM((l, dt), I32)] + [pltpu.VMEM((n, dt), I32)] * 2
    scratch = scratch + [pltpu.VMEM((l + 2 * SUBLANES, dt), F32)]
    in_specs += [pl.BlockSpec(m.shape, lambda j, i, nd_=m.ndim: (0,) * nd_, pipeline_mode=once) for m in mats]
    return pl.pallas_call(
        functools.partial(_hy_conv_kernel, l=l, n1=n1, n2=n2, dense=dense, conv_z=conv_z),
        grid=(nd, pairs),
        in_specs=in_specs,
        out_specs=pl.BlockSpec((2, 1, l, dt), lambda j, i: (0, i, 0, j)),
        out_shape=jax.ShapeDtypeStruct((2, pairs, l, d), BF16),
        scratch_shapes=scratch,
        compiler_params=_cparams("arbitrary", "arbitrary"),
        name="hy_conv",
    )(zsrc, p4, conv_w, conv_w, spec, *mats)


def hyena_mixer(p, conv_w, f_w1, f_b1, f_w2, f_b2, f_w3, sin_freq, skip):
    b, l, d3 = p.shape
    d = d3 // 3
    assert b % 2 == 0 and d % HY_DT == 0
    n1, n2, consts = _dft_consts(l)
    taps_f, taps_b = hy_filter_taps(l, d, f_w1, f_b1, f_w2, f_b2, f_w3, sin_freq)
    spec = hy_spectrum(taps_f, taps_b, skip, consts, n1, n2)
    p4 = p.reshape(2, b // 2, l, d3)
    z = hy_conv(p4, 0, p4, 1, conv_w, spec, 0, consts, n1, n2, True)
    z = hy_conv(z, 0, p4, 2, conv_w, spec, 1, consts, n1, n2, False)
    return z.reshape(b, l, d)


def _head_norm(x, g):
    sq = x * x
    hi = sq.astype(BF16)
    lo = (sq - hi.astype(F32)).astype(BF16)
    ones = jnp.ones((HEAD_DIM, HEAD_DIM), BF16)
    total = _dot(hi, ones) + _dot(lo, ones)
    return x * lax.rsqrt(total * (1.0 / HEAD_DIM) + EPS) * g


def _rope(x, cos, sin_a, sin_b):
    return (x * cos + pltpu.roll(x, HEAD_DIM // 4, 1) * sin_a
            + pltpu.roll(x, HEAD_DIM - HEAD_DIM // 4, 1) * sin_b)


def _qkv_kernel(x_ref, g_ref, sh_ref, sc_ref, w_ref, qg_ref, kg_ref, cos_ref, sa_ref, sb_ref,
                *out_refs, nq, rope):
    q_ref = out_refs[0] if nq else None
    k_ref, v_ref = out_refs[-2:]
    h = _norm_mod(x_ref[0], g_ref[...], sh_ref[0], sc_ref[0]).astype(BF16)
    acc = _dot(h, w_ref[...])
    hd = HEAD_DIM
    for i in range(nq + N_KV_HEADS):
        t = acc[:, i * hd:(i + 1) * hd]
        t = _head_norm(t, qg_ref[...] if i < nq else kg_ref[...])
        if rope:
            t = _rope(t, cos_ref[...], sa_ref[...], sb_ref[...])
        if i < nq:
            q_ref[0, :, i * hd:(i + 1) * hd] = (t * Q_PRESCALE).astype(BF16)
        else:
            k_ref[0, :, (i - nq) * hd:(i - nq + 1) * hd] = t.astype(BF16)
    v_ref[0] = acc[:, (nq + N_KV_HEADS) * hd:].astype(BF16)


def qkv_project(x, g, sh, sc, w, q_g, k_g, rope_tabs, nq, rope):
    b, l, d = x.shape
    n = w.shape[1]
    hd = HEAD_DIM
    kvd = N_KV_HEADS * hd
    tm = _row_tile(l)
    cos, sa, sb = rope_tabs
    tab = pl.BlockSpec((tm, hd), lambda i, j: (j, 0))
    vec = pl.BlockSpec((1, 1, d), lambda i, j: (i, 0, 0))
    widths = ([nq * hd] if nq else []) + [kvd, kvd]
    return pl.pallas_call(
        functools.partial(_qkv_kernel, nq=nq, rope=rope),
        grid=(b, l // tm),
        in_specs=[pl.BlockSpec((1, tm, d), lambda i, j: (i, j, 0)),
                  pl.BlockSpec((1, d), lambda i, j: (0, 0)), vec, vec,
                  pl.BlockSpec((d, n), lambda i, j: (0, 0)),
                  pl.BlockSpec((1, hd), lambda i, j: (0, 0)),
                  pl.BlockSpec((1, hd), lambda i, j: (0, 0)),
                  tab, tab, tab],
        out_specs=[pl.BlockSpec((1, tm, wd), lambda i, j: (i, j, 0)) for wd in widths],
        out_shape=[jax.ShapeDtypeStruct((b, l, wd), BF16) for wd in widths],
        compiler_params=_cparams("arbitrary", "arbitrary"),
        name="qkv_project",
    )(x, g.reshape(1, d), sh, sc, w, q_g.reshape(1, hd), k_g.reshape(1, hd), cos, sa, sb)


def _rope_tables(l):
    rows = l // GRID_W
    row = np.repeat(np.arange(rows, dtype=np.float64), GRID_W)
    col = np.tile(np.arange(GRID_W, dtype=np.float64), rows)
    axis = HEAD_DIM // 2
    inv = ROPE_THETA ** (-np.arange(0, axis, 2, dtype=np.float64) / axis)

    def axis_angles(pos):
        a = pos[:, None] * inv[None, :]
        return np.concatenate([a, a], axis=-1)

    ang = np.concatenate([axis_angles(row), axis_angles(col)], axis=-1)
    cos, sin = np.cos(ang), np.sin(ang)
    lane = np.arange(HEAD_DIM)[None, :] % axis
    upper = lane >= axis // 2
    sin_a = np.where(upper, sin, 0.0)
    sin_b = np.where(upper, 0.0, -sin)
    return tuple(jnp.asarray(t, F32) for t in (cos, sin_a, sin_b))


def _attn_kernel(q_ref, k_ref, vt_ref, o_ref, *, groups):
    k = k_ref[0]
    vt = vt_ref[0]
    hd = HEAD_DIM
    tq = q_ref.shape[1]
    hp = 2 if groups % 2 == 0 else 1
    def scores(g0):
        q = jnp.concatenate([q_ref[0, :, g * hd:(g + 1) * hd] for g in range(g0, g0 + hp)], axis=0)
        return lax.dot_general(k, q, (((1,), (1,)), ((), ())), preferred_element_type=F32)

    sts = [scores(g0) for g0 in range(0, groups, hp)]
    for g0, st in zip(range(0, groups, hp), sts):
        m = jnp.max(st, axis=0, keepdims=True)
        pt = jnp.exp2(st - m)
        den = jnp.sum(pt, axis=0, keepdims=True)
        ot = _dot(vt, pt.astype(BF16)) / den
        for i in range(hp):
            g = g0 + i
            o_ref[0, :, g * hd:(g + 1) * hd] = ot[:, i * tq:(i + 1) * tq].T.astype(o_ref.dtype)


def attention(q, k, v):
    b, l, qd = q.shape
    s = k.shape[1]
    hd = HEAD_DIM
    groups = qd // hd // N_KV_HEADS
    tq = _row_tile(l, 256)
    vt = jnp.swapaxes(v, 1, 2)
    return pl.pallas_call(
        functools.partial(_attn_kernel, groups=groups),
        grid=(b, N_KV_HEADS, l // tq),
        in_specs=[pl.BlockSpec((1, tq, groups * hd), lambda i, h, j: (i, j, h)),
                  pl.BlockSpec((1, s, hd), lambda i, h, j: (i, 0, h)),
                  pl.BlockSpec((1, hd, s), lambda i, h, j: (i, h, 0))],
        out_specs=pl.BlockSpec((1, tq, groups * hd), lambda i, h, j: (i, j, h)),
        out_shape=jax.ShapeDtypeStruct((b, l, qd), BF16),
        compiler_params=_cparams("arbitrary", "arbitrary", "arbitrary"),
        name="attention",
    )(q, k, vt)


def _route(x, g_ref, sh_ref, sc_ref, wr_ref, h_ref, a_ref):
    h = _norm_mod(x, g_ref[...], sh_ref[0], sc_ref[0])
    d = h.shape[1]
    logits = lax.dot_general(wr_ref[...], h, (((1,), (1,)), ((), ())), preferred_element_type=F32,
                             precision=HIGHEST)
    ex = jnp.exp(logits - jnp.max(logits, axis=0, keepdims=True))
    aff = ex / jnp.sum(ex, axis=0, keepdims=True)
    a_ref[0] = aff
    ne, tm = aff.shape
    hi = aff.astype(BF16).astype(F32)
    mid = (aff - hi).astype(BF16).astype(F32)
    lo = (aff - hi - mid).astype(BF16).astype(F32)
    pieces = jnp.concatenate([hi, mid, lo, jnp.zeros((LANES - 3 * ne, tm), F32)], axis=0)
    h_ref[0, :, 0:d] = h.astype(BF16)
    h_ref[0, :, d:d + LANES] = pieces.T.astype(BF16)


def _cumsum_lanes(m, tri_ref, ind_ref):
    e, n = m.shape
    mb = m.astype(BF16)
    before = _dot(mb, ind_ref[...])
    outs = []
    for c in range(n // LANES):
        outs.append(_dot(mb[:, c * LANES:(c + 1) * LANES], tri_ref[...]) + before[:, c:c + 1])
    return jnp.concatenate(outs, axis=1), before


def _select_kernel(a_ref, tri_ref, ind_ref, slot_ref, before_ref, slot_t_ref, *, cap):
    aff = a_ref[0]
    bits = pltpu.bitcast(aff, I32)
    e = aff.shape[0]

    def body(i, thr):
        cand = thr | (1 << (30 - i))
        cnt = jnp.sum((bits >= cand).astype(F32), axis=1, keepdims=True)
        return jnp.where(cnt >= cap, cand, thr)

    thr = lax.fori_loop(0, 31, body, jnp.zeros((e, 1), I32))
    gt = bits > thr
    eq = bits == thr
    n_gt = jnp.sum(gt.astype(F32), axis=1, keepdims=True)
    eq_rank, _ = _cumsum_lanes(eq.astype(F32), tri_ref, ind_ref)
    sel = gt | (eq & (eq_rank <= cap - n_gt))
    pos, before = _cumsum_lanes(sel.astype(F32), tri_ref, ind_ref)
    slot = jnp.where(sel, pos - 1.0, -1.0)
    slot_ref[0] = slot
    before_ref[0] = before
    slot_t_ref[0] = slot.T


def moe_select(aff, cap):
    b, e, n = aff.shape
    assert n // LANES <= LANES
    tri = jnp.asarray(np.triu(np.ones((LANES, LANES), np.float32)), BF16)
    ind = jnp.asarray(np.arange(n)[:, None] < LANES * np.arange(LANES)[None, :], BF16)
    row = pl.BlockSpec((1, e, n), lambda i: (i, 0, 0))
    col = pl.BlockSpec((1, n, e), lambda i: (i, 0, 0))
    return pl.pallas_call(
        functools.partial(_select_kernel, cap=cap),
        grid=(b,),
        in_specs=[row, pl.BlockSpec((LANES, LANES), lambda i: (0, 0)), pl.BlockSpec((n, LANES), lambda i: (0, 0))],
        out_specs=[row, pl.BlockSpec((1, e, LANES), lambda i: (i, 0, 0)), col],
        out_shape=[jax.ShapeDtypeStruct((b, e, n), F32), jax.ShapeDtypeStruct((b, e, LANES), F32),
                   jax.ShapeDtypeStruct((b, n, e), F32)],
        compiler_params=_cparams("arbitrary"),
        name="moe_select",
    )(aff, tri, ind)


class _Windows:
    def __init__(self, before, n, cap):
        self.tile = min(n, MOE_TILE)
        self.nt = n // self.tile
        self.narrow = sorted({w for w in (min(cap, w) for w in MOE_WINDOWS)
                              if (w % LANES == 0 or LANES % w == 0) and w % BF16_ROWS == 0})
        self.wide = min(cap, self.tile + BF16_ROWS)
        first = before[:, :, 0:n // LANES:self.tile // LANES].astype(I32)
        count = jnp.concatenate([first[:, :, 1:], jnp.full_like(first[:, :, :1], cap)], axis=2) - first
        aligned = first // BF16_ROWS * BF16_ROWS
        starts = [jnp.clip(aligned, 0, cap - w) for w in self.narrow + [self.wide]]
        level = jnp.zeros(first.shape[:1] + first.shape[2:], I32)
        for i in reversed(range(len(self.narrow))):
            fits = jnp.all(first - starts[i] + count <= self.narrow[i], axis=1)
            level = jnp.where(fits, i + 1, level)
        self.args = tuple(s.reshape(-1) for s in starts) + (level.reshape(-1),)


def _gather_kernel(*refs, ne, narrow, wide):
    narrow_refs = refs[:len(narrow)]
    ww_ref, level_ref, slot_ref, h_ref, o_ref = refs[len(narrow):]
    bi, gi, ki = pl.program_id(0), pl.program_id(1), pl.program_id(2)
    nt = pl.num_programs(2)
    group, tk = slot_ref.shape[1], slot_ref.shape[2]
    level = level_ref[bi * nt + ki]

    @pl.when(ki == 0)
    def _():
        o_ref[...] = jnp.zeros_like(o_ref)

    slot = slot_ref[0].astype(I32)

    def start(ref, j):
        return pl.multiple_of(ref[(bi * ne + gi * group + j) * nt + ki], BF16_ROWS)

    def onehot(j, ws, width):
        jcol = lax.broadcasted_iota(I32, (width, tk), 0)
        return jnp.where(jcol + ws == slot[j:j + 1, :], 1.0, 0.0).astype(BF16)

    def add(j, ws, width, rows_f32):
        rows = pl.ds(ws, width)
        o_ref[0, j, rows, :] = o_ref[0, j, rows, :] + rows_f32.astype(o_ref.dtype)

    for i, (wn_ref, width) in enumerate(zip(narrow_refs, narrow)):
        @pl.when(level == i + 1)
        def _(wn_ref=wn_ref, width=width):
            starts = [start(wn_ref, j) for j in range(group)]
            lhs = jnp.concatenate([onehot(j, starts[j], width) for j in range(group)], axis=0)
            res = _dot(lhs, h_ref[0])
            for j in range(group):
                add(j, starts[j], width, res[j * width:(j + 1) * width])

    @pl.when(level == 0)
    def _():
        for j in range(group):
            ws = start(ww_ref, j)
            add(j, ws, wide, _dot(onehot(j, ws, wide), h_ref[0]))


def moe_gather(slot, win, h, cap):
    b, e, n = slot.shape
    dx = h.shape[2]
    group = math.gcd(e, MOE_GROUP)
    grid_spec = pltpu.PrefetchScalarGridSpec(
        num_scalar_prefetch=len(win.args),
        grid=(b, e // group, win.nt),
        in_specs=[pl.BlockSpec((1, group, win.tile), lambda i, g, k, *_: (i, g, k)),
                  pl.BlockSpec((1, win.tile, dx), lambda i, g, k, *_: (i, k, 0))],
        out_specs=pl.BlockSpec((1, group, cap, dx), lambda i, g, k, *_: (i, g, 0, 0)))
    return pl.pallas_call(
        functools.partial(_gather_kernel, ne=e, narrow=win.narrow, wide=win.wide),
        grid_spec=grid_spec,
        out_shape=jax.ShapeDtypeStruct((b, e, cap, dx), BF16),
        compiler_params=_cparams("arbitrary", "arbitrary", "arbitrary"),
        name="moe_gather",
    )(*win.args, slot, h)


def _ffn_kernel(*refs, ns):
    x_refs = refs[:ns]
    wg_ref, wu_ref, wd_ref = refs[ns:ns + 3]
    o_refs = refs[ns + 3:2 * ns + 3]
    wgb_ref, wub_ref, wdb_ref = refs[2 * ns + 3:]

    @pl.when(pl.program_id(1) == 0)
    def _():
        wgb_ref[...] = wg_ref[0, 0].astype(BF16)
        wub_ref[...] = wu_ref[0, 0].astype(BF16)
        wdb_ref[...] = wd_ref[0, 0].astype(BF16)

    d = wgb_ref.shape[0]
    ei, ne = pl.program_id(0), pl.num_programs(0)

    chunks = [[(0, i, r, min(r + FFN_ROWS, x_refs[0].shape[2]))]
              for i in range(x_refs[0].shape[0]) for r in range(0, x_refs[0].shape[2], FFN_ROWS)]
    for s in range(1, ns):
        chunks[-1] += [(s, i, 0, x_refs[s].shape[2]) for i in range(x_refs[s].shape[0])]

    def rows_of(chunk, c0, c1):
        parts = [x_refs[s][i, 0, r0:r1, c0:c1] for s, i, r0, r1 in chunk]
        return parts[0] if len(parts) == 1 else jnp.concatenate(parts, axis=0)

    ups = []
    for chunk in chunks:
        x = rows_of(chunk, 0, d)
        ups.append((_dot(x, wgb_ref[...]), _dot(x, wub_ref[...])))
    for chunk, (a, u) in zip(chunks, ups):
        ext = rows_of(chunk, d, d + LANES).astype(F32)
        lane = lax.broadcasted_iota(I32, ext.shape, 1)
        mine = (lane == ei) | (lane == ei + ne) | (lane == ei + 2 * ne)
        gate = jnp.sum(jnp.where(mine, ext, 0.0), axis=1, keepdims=True)
        y = _dot(((a / (1.0 + jnp.exp(-a))) * u).astype(BF16), wdb_ref[...]) * gate
        row = 0
        for s, i, r0, r1 in chunk:
            o_refs[s][i, 0, r0:r1, :] = y[row:row + r1 - r0].astype(o_refs[s].dtype)
            row += r1 - r0


def moe_ffn(xs_list, w_gate, w_up, w_down, layer):
    b, e, _, dx = xs_list[0].shape
    d, f = w_gate.shape[2], w_gate.shape[3]
    bb = 2 if b % 2 == 0 else 1

    def specs(width):
        return [pl.BlockSpec((bb, 1, xs.shape[2], width), lambda j, i: (i, j, 0, 0)) for xs in xs_list]

    return pl.pallas_call(
        functools.partial(_ffn_kernel, ns=len(xs_list)),
        grid=(e, b // bb),
        in_specs=specs(dx) + [pl.BlockSpec((1, 1, d, f), lambda j, i: (layer, j, 0, 0)),
                              pl.BlockSpec((1, 1, d, f), lambda j, i: (layer, j, 0, 0)),
                              pl.BlockSpec((1, 1, f, d), lambda j, i: (layer, j, 0, 0))],
        out_specs=specs(d),
        out_shape=[jax.ShapeDtypeStruct(xs.shape[:3] + (d,), BF16) for xs in xs_list],
        scratch_shapes=[pltpu.VMEM((d, f), BF16), pltpu.VMEM((d, f), BF16), pltpu.VMEM((f, d), BF16)],
        compiler_params=_cparams("arbitrary", "arbitrary"),
        name="moe_ffn",
    )(*xs_list, w_gate, w_up, w_down)


def _combine_kernel(*refs, ne, narrow, wide):
    narrow_refs = refs[:len(narrow)]
    ww_ref, level_ref, slot_ref, ys_hbm, lat_ref, gt_ref, o_ref, ys_buf, sem = refs[len(narrow):]
    bi, ti = pl.program_id(0), pl.program_id(1)
    nb, nt = pl.num_programs(0), pl.num_programs(1)
    cur = bi % 2

    def fetch(batch, buf):
        return pltpu.make_async_copy(ys_hbm.at[batch], ys_buf.at[buf], sem.at[buf])

    @pl.when(ti == 0)
    def _():
        @pl.when(bi == 0)
        def _():
            fetch(0, 0).start()

        fetch(bi, cur).wait()

        @pl.when(bi + 1 < nb)
        def _():
            fetch(bi + 1, 1 - cur).start()

    tn = lat_ref.shape[1]
    slot = slot_ref[0].astype(I32)

    def start(ref, e):
        return pl.multiple_of(ref[(bi * ne + e) * nt + ti], BF16_ROWS)

    def onehot(e, ws, width):
        jrow = lax.broadcasted_iota(I32, (tn, width), 1)
        return jnp.where(jrow + ws == slot[:, e:e + 1], 1.0, 0.0).astype(BF16)

    def onehot_lane_group(first, starts, width):
        lane = lax.broadcasted_iota(I32, (tn, LANES), 1)
        target = slot[:, first:first + 1] - starts[first]
        for i in range(1, LANES // width):
            e = first + i
            target = jnp.where(lane >= i * width, slot[:, e:e + 1] - (starts[e] - i * width), target)
        return jnp.where(target == lane, 1.0, 0.0).astype(BF16)

    def finish(acc):
        o_ref[0] = lat_ref[0] + gt_ref[0] * acc

    level = level_ref[bi * nt + ti]
    for i, (wn_ref, width) in enumerate(zip(narrow_refs, narrow)):
        @pl.when(level == i + 1)
        def _(wn_ref=wn_ref, width=width):
            starts = [start(wn_ref, e) for e in range(ne)]
            if width % LANES == 0:
                lhs = jnp.concatenate([onehot(e, starts[e], width) for e in range(ne)], axis=1)
            else:
                lhs = jnp.concatenate([onehot_lane_group(e, starts, width)
                                       for e in range(0, ne, LANES // width)], axis=1)
            rhs = jnp.concatenate([ys_buf[cur, e, pl.ds(starts[e], width), :] for e in range(ne)], axis=0)
            finish(_dot(lhs, rhs))

    @pl.when(level == 0)
    def _():
        acc = jnp.zeros((tn, lat_ref.shape[2]), F32)
        for e in range(ne):
            ws = start(ww_ref, e)
            acc = acc + _dot(onehot(e, ws, wide), ys_buf[cur, e, pl.ds(ws, wide), :])
        finish(acc)


def moe_combine(slot_t, win, ys, lat, gate_vec, cap):
    b, n, e = slot_t.shape
    d = lat.shape[2]
    tn = win.tile
    grid_spec = pltpu.PrefetchScalarGridSpec(
        num_scalar_prefetch=len(win.args),
        grid=(b, win.nt),
        in_specs=[pl.BlockSpec((1, tn, e), lambda i, j, *_: (i, j, 0)),
                  pl.BlockSpec(memory_space=pl.ANY),
                  pl.BlockSpec((1, tn, d), lambda i, j, *_: (i, j, 0)),
                  pl.BlockSpec((1, 1, d), lambda i, j, *_: (i, 0, 0))],
        out_specs=pl.BlockSpec((1, tn, d), lambda i, j, *_: (i, j, 0)),
        scratch_shapes=[pltpu.VMEM((2, e, cap, d), BF16), pltpu.SemaphoreType.DMA((2,))])
    return pl.pallas_call(
        functools.partial(_combine_kernel, ne=e, narrow=win.narrow, wide=win.wide),
        grid_spec=grid_spec,
        out_shape=jax.ShapeDtypeStruct((b, n, d), F32),
        compiler_params=_cparams("arbitrary", "arbitrary"),
        name="moe_combine",
    )(*win.args, slot_t, ys, lat, gate_vec)


def expert_choice_moe(streams, w_gate, w_up, w_down, layer):
    routed = []
    for (lat, h, aff), _ in streams:
        e = aff.shape[1]
        assert 3 * e <= LANES and e % SUBLANES == 0
        cap = EC_CAPACITY_FACTOR * lat.shape[1] // e
        slot, before, slot_t = moe_select(aff, cap)
        win = _Windows(before, lat.shape[1], cap)
        routed.append((moe_gather(slot, win, h, cap), slot_t, win, cap))
    ys_list = moe_ffn([r[0] for r in routed], w_gate, w_up, w_down, layer)
    return [moe_combine(slot_t, win, ys, lat, gate_vec, cap)
            for ((lat, _, _), gate_vec), (_, slot_t, win, cap), ys in zip(streams, routed, ys_list)]


def kernel(x, c, ctx, c_ctx, ada_w, ada_b, norm_g, sc_w_in, sc_conv, sc_w_out, hy_w_in, hy_conv_w, hy_f_w1,
           hy_f_b1, hy_f_w2, hy_f_b2, hy_f_w3, hy_sin_freq, hy_skip, hy_w_out, at_w_qkv, at_q_g, at_k_g, at_w_o,
           moe_router_w, moe_w_gate, moe_w_up, moe_w_down):
    b, l, d = x.shape
    depth = ada_w.shape[0]
    assert b + 1 <= MOD_ROWS
    qd = at_w_o.shape[1]
    nq = qd // HEAD_DIM
    attn_layers = list(range(MIX_ATTN, depth, N_MIXERS))
    last_attn = max(attn_layers, default=-1)

    cond = jnp.zeros((MOD_ROWS, d), F32).at[:b].set(c).at[b].set(c_ctx)
    mods = ada_modulation(cond, ada_w, ada_b)

    lat, cx = x, ctx
    for i in range(depth):
        kind = i % N_MIXERS
        j = i // N_MIXERS
        need_ctx = i <= last_attn
        upd_ctx = i < last_attn
        m_l = mods[i, :b].reshape(b, 1, 6, d)
        sh1, sc1, g1, sh2, sc2, g2 = (m_l[:, :, k] for k in range(6))
        if need_ctx:
            m_c = jnp.broadcast_to(mods[i, b].reshape(1, 1, 6, d), (b, 1, 6, d))
            csh1, csc1, cg1, csh2, csc2, cg2 = (m_c[:, :, k] for k in range(6))
        w_router_t = moe_router_w[i].T
        streams = [(lat, sh1, sc1, g1, (norm_g[i, 1], sh2, sc2, w_router_t))]
        if upd_ctx:
            streams.append((cx, csh1, csc1, cg1, (norm_g[i, 1], csh2, csc2, w_router_t)))

        outs = []
        if kind == MIX_SHORTCONV:
            w_in, w_out = sc_w_in[j].astype(BF16), sc_w_out[j].astype(BF16)
            for s, sh, sc, gt, route in streams:
                outs.append(sc_mixer(s, norm_g[i, 0], sh, sc, w_in, sc_conv[j], w_out, gt, route))
        elif kind == MIX_HYENA:
            w_in, w_out = hy_w_in[j].astype(BF16), hy_w_out[j].astype(BF16)
            for s, sh, sc, gt, route in streams:
                p = nm_matmul(s, norm_g[i, 0], sh, sc, w_in)
                z = hyena_mixer(p, hy_conv_w[j], hy_f_w1[j], hy_f_b1[j], hy_f_w2[j], hy_f_b2[j], hy_f_w3[j],
                                hy_sin_freq[j], hy_skip[j])
                outs.append(proj_residual(z, w_out, s, gt, route))
        else:
            w_qkv, w_o = at_w_qkv[j].astype(BF16), at_w_o[j].astype(BF16)
            q_l, k_l, v_l = qkv_project(lat, norm_g[i, 0], sh1, sc1, w_qkv, at_q_g[j], at_k_g[j],
                                        _rope_tables(l), nq, True)
            lc = cx.shape[1]
            dummy = tuple(jnp.zeros((lc, HEAD_DIM), F32) for _ in range(3))
            if upd_ctx:
                q_c, k_c, v_c = qkv_project(cx, norm_g[i, 0], csh1, csc1, w_qkv, at_q_g[j], at_k_g[j],
                                            dummy, nq, False)
            else:
                k_c, v_c = qkv_project(cx, norm_g[i, 0], csh1, csc1, w_qkv[:, qd:], at_q_g[j], at_k_g[j],
                                       dummy, 0, False)
            v_all = jnp.concatenate([v_c, v_l], axis=1)
            o_l = attention(q_l, jnp.concatenate([k_c, k_l], axis=1), v_all)
            outs.append(proj_residual(o_l, w_o, lat, g1, streams[0][4]))
            if upd_ctx:
                outs.append(proj_residual(attention(q_c, k_c, v_c), w_o, cx, cg1, streams[1][4]))

        moe_streams = [(outs[0], g2)]
        if upd_ctx:
            moe_streams.append((outs[1], cg2))
        moe_out = expert_choice_moe(moe_streams, moe_w_gate, moe_w_up, moe_w_down, i)
        lat = moe_out[0]
        if upd_ctx:
            cx = moe_out[1]
    return lat
```

```python
import functools
import math

import jax
import jax.numpy as jnp
import numpy as np
from jax import lax
from jax.experimental import pallas as pl
from jax.experimental.pallas import tpu as pltpu

F32 = jnp.float32
BF16 = jnp.bfloat16
I32 = jnp.int32
HIGHEST = lax.Precision.HIGHEST

N_MIXERS = 3
MIX_SHORTCONV, MIX_HYENA, MIX_ATTN = 0, 1, 2
CONV_W = 3
EPS = 1e-6
HY_ORDER = 2
HY_EMB = 33
HY_BANDS = (HY_EMB - 1) // 2
HY_FAST_DECAY = 0.3
HY_SLOW_DECAY = 1.5
HY_TARGET = 1e-2
HEAD_DIM = 128
N_KV_HEADS = 2
GRID_W = 64
ROPE_THETA = 10000.0
EC_CAPACITY_FACTOR = 2
Q_PRESCALE = HEAD_DIM ** -0.5 * math.log2(math.e)

LANES = 128
SUBLANES = 8
BF16_ROWS = 16
VMEM_LIMIT_BYTES = 56 * 1024 * 1024
ROW_TILE = 1024
MOD_ROWS = 16
FFT_N2 = 128
HY_DT = LANES
MOE_TILE = 2 * LANES
MOE_WINDOWS = (LANES // 2, LANES)
MOE_GROUP = 2 * SUBLANES
FFN_ROWS = 2 * LANES
DFT_UNROLL = 8
DFT_UNROLL_STAGE1 = 32


def _cparams(*sem):
    return pltpu.CompilerParams(dimension_semantics=sem, vmem_limit_bytes=VMEM_LIMIT_BYTES)


def _row_tile(n, cap=ROW_TILE):
    t = min(n, cap)
    assert n % t == 0
    return t


def _dot(a, b):
    return jnp.dot(a, b, preferred_element_type=F32)


def _ada_kernel(s_ref, w_ref, b_ref, o_ref):
    s = s_ref[...]
    s = s / (1.0 + jnp.exp(-s))
    o_ref[0] = jnp.dot(s, w_ref[0], preferred_element_type=F32, precision=HIGHEST) + b_ref[0]


def ada_modulation(cond, ada_w, ada_b):
    depth, d, n = ada_w.shape
    tn = 1536 if n % 1536 == 0 else n
    return pl.pallas_call(
        _ada_kernel,
        grid=(depth, n // tn),
        in_specs=[pl.BlockSpec((MOD_ROWS, d), lambda i, j: (0, 0)),
                  pl.BlockSpec((1, d, tn), lambda i, j: (i, 0, j)),
                  pl.BlockSpec((1, 1, tn), lambda i, j: (i, 0, j))],
        out_specs=pl.BlockSpec((1, MOD_ROWS, tn), lambda i, j: (i, 0, j)),
        out_shape=jax.ShapeDtypeStruct((depth, MOD_ROWS, n), F32),
        compiler_params=_cparams("arbitrary", "arbitrary"),
        name="ada_modulation",
    )(cond, ada_w, ada_b.reshape(depth, 1, n))


def _norm_mod(x, g, sh, sc):
    y = x * lax.rsqrt(jnp.mean(x * x, axis=-1, keepdims=True) + EPS)
    return (y * g) * (1.0 + sc) + sh


def _nm_matmul_kernel(x_ref, g_ref, sh_ref, sc_ref, w_ref, o_ref):
    h = _norm_mod(x_ref[0], g_ref[...], sh_ref[0], sc_ref[0]).astype(BF16)
    o_ref[0] = _dot(h, w_ref[...]).astype(o_ref.dtype)


def nm_matmul(x, g, sh, sc, w, out_dtype=BF16):
    b, l, d = x.shape
    n = w.shape[1]
    tm = _row_tile(l)
    return pl.pallas_call(
        _nm_matmul_kernel,
        grid=(b, l // tm),
        in_specs=[pl.BlockSpec((1, tm, d), lambda i, j: (i, j, 0)),
                  pl.BlockSpec((1, d), lambda i, j: (0, 0)),
                  pl.BlockSpec((1, 1, d), lambda i, j: (i, 0, 0)),
                  pl.BlockSpec((1, 1, d), lambda i, j: (i, 0, 0)),
                  pl.BlockSpec((d, n), lambda i, j: (0, 0))],
        out_specs=pl.BlockSpec((1, tm, n), lambda i, j: (i, j, 0)),
        out_shape=jax.ShapeDtypeStruct((b, l, n), out_dtype),
        compiler_params=_cparams("arbitrary", "arbitrary"),
        name="nm_matmul",
    )(x, g.reshape(1, d), sh, sc, w)


def _route_specs(b, l, d, e, tm):
    vec = pl.BlockSpec((1, 1, d), lambda i, j: (i, 0, 0))
    in_specs = [pl.BlockSpec((1, d), lambda i, j: (0, 0)), vec, vec, pl.BlockSpec((e, d), lambda i, j: (0, 0))]
    out_specs = [pl.BlockSpec((1, tm, d + LANES), lambda i, j: (i, j, 0)),
                 pl.BlockSpec((1, e, tm), lambda i, j: (i, 0, j))]
    out_shape = [jax.ShapeDtypeStruct((b, l, d + LANES), BF16), jax.ShapeDtypeStruct((b, e, l), F32)]
    return in_specs, out_specs, out_shape


def _proj_res_kernel(a_ref, w_ref, lat_ref, gt_ref, g2_ref, sh2_ref, sc2_ref, wr_ref, o_ref, h_ref, aff_ref):
    out = lat_ref[0] + gt_ref[0] * _dot(a_ref[0], w_ref[...])
    o_ref[0] = out
    _route(out, g2_ref, sh2_ref, sc2_ref, wr_ref, h_ref, aff_ref)


def proj_residual(a, w, lat, gate, route):
    b, l, k = a.shape
    d = w.shape[1]
    g2, sh2, sc2, wr = route
    tm = _row_tile(l)
    r_in, r_out, r_shape = _route_specs(b, l, d, wr.shape[0], tm)
    return pl.pallas_call(
        _proj_res_kernel,
        grid=(b, l // tm),
        in_specs=[pl.BlockSpec((1, tm, k), lambda i, j: (i, j, 0)),
                  pl.BlockSpec((k, d), lambda i, j: (0, 0)),
                  pl.BlockSpec((1, tm, d), lambda i, j: (i, j, 0)),
                  pl.BlockSpec((1, 1, d), lambda i, j: (i, 0, 0))] + r_in,
        out_specs=[pl.BlockSpec((1, tm, d), lambda i, j: (i, j, 0))] + r_out,
        out_shape=[jax.ShapeDtypeStruct((b, l, d), F32)] + r_shape,
        compiler_params=_cparams("arbitrary", "arbitrary"),
        name="proj_residual",
    )(a, w, lat, gate, g2.reshape(1, d), sh2, sc2, wr)


def _sc_mixer_kernel(x_ref, xp_ref, xn_ref, g_ref, sh_ref, sc_ref, win_ref, cw_ref, wout_ref, gt_ref,
                     g2_ref, sh2_ref, sc2_ref, wr_ref, o_ref, h_ref, aff_ref, u_ref, *, tm):
    j = pl.program_id(1)
    nj = pl.num_programs(1)
    d = u_ref.shape[1]
    x = x_ref[0]
    p = _dot(_norm_mod(x, g_ref[...], sh_ref[0], sc_ref[0]).astype(BF16), win_ref[...])
    u_ref[pl.ds(SUBLANES, tm), :] = p[:, d:2 * d] * p[:, 2 * d:]
    halo = jnp.concatenate([xp_ref[0], xn_ref[0]], axis=0)
    ph = _dot(_norm_mod(halo, g_ref[...], sh_ref[0], sc_ref[0]).astype(BF16), win_ref[:, d:])
    uh = ph[:, :d] * ph[:, d:]
    prev = jnp.where(j == 0, 0.0, uh[SUBLANES - 1:SUBLANES])
    nxt = jnp.where(j == nj - 1, 0.0, uh[SUBLANES:SUBLANES + 1])
    u_ref[pl.ds(0, SUBLANES), :] = jnp.broadcast_to(prev, (SUBLANES, d))
    u_ref[pl.ds(SUBLANES + tm, SUBLANES), :] = jnp.broadcast_to(nxt, (SUBLANES, d))
    cw = cw_ref[...]
    y = (u_ref[pl.ds(SUBLANES - 1, tm), :] * cw[0:1] + u_ref[pl.ds(SUBLANES, tm), :] * cw[1:2]
         + u_ref[pl.ds(SUBLANES + 1, tm), :] * cw[2:3])
    m = (p[:, :d] * y).astype(BF16)
    out = x + gt_ref[0] * _dot(m, wout_ref[...])
    o_ref[0] = out
    _route(out, g2_ref, sh2_ref, sc2_ref, wr_ref, h_ref, aff_ref)


def sc_mixer(x, g, sh, sc, w_in, conv_w, w_out, gate, route):
    b, l, d = x.shape
    g2, sh2, sc2, wr = route
    tm = _row_tile(l, 512)
    hb = tm // SUBLANES
    nhb = l // SUBLANES
    vec = pl.BlockSpec((1, 1, d), lambda i, j: (i, 0, 0))
    r_in, r_out, r_shape = _route_specs(b, l, d, wr.shape[0], tm)
    return pl.pallas_call(
        functools.partial(_sc_mixer_kernel, tm=tm),
        grid=(b, l // tm),
        in_specs=[pl.BlockSpec((1, tm, d), lambda i, j: (i, j, 0)),
                  pl.BlockSpec((1, SUBLANES, d), lambda i, j: (i, jnp.maximum(j * hb - 1, 0), 0)),
                  pl.BlockSpec((1, SUBLANES, d), lambda i, j: (i, jnp.minimum((j + 1) * hb, nhb - 1), 0)),
                  pl.BlockSpec((1, d), lambda i, j: (0, 0)), vec, vec,
                  pl.BlockSpec((d, 3 * d), lambda i, j: (0, 0)),
                  pl.BlockSpec((CONV_W, d), lambda i, j: (0, 0)),
                  pl.BlockSpec((d, d), lambda i, j: (0, 0)),
                  vec] + r_in,
        out_specs=[pl.BlockSpec((1, tm, d), lambda i, j: (i, j, 0))] + r_out,
        out_shape=[jax.ShapeDtypeStruct((b, l, d), F32)] + r_shape,
        scratch_shapes=[pltpu.VMEM((tm + 2 * SUBLANES, d), F32)],
        compiler_params=_cparams("arbitrary", "arbitrary"),
        name="sc_mixer",
    )(x, x, x, g.reshape(1, d), sh, sc, w_in, conv_w, w_out, gate, g2.reshape(1, d), sh2, sc2, wr)


def _hy_filter_kernel(z_ref, w1_ref, b1_ref, w2_ref, b2_ref, fr_ref, w3f_ref, w3b_ref, t_ref, dl_ref,
                      of_ref, ob_ref, a_ref):
    @pl.when((pl.program_id(0) == 0) & (pl.program_id(1) == 0))
    def _():
        fr = fr_ref[...]
        a1 = jnp.sin(fr * (jnp.dot(z_ref[...], w1_ref[...], preferred_element_type=F32, precision=HIGHEST)
                           + b1_ref[...]))
        a_ref[...] = jnp.sin(fr * (jnp.dot(a1, w2_ref[...], preferred_element_type=F32, precision=HIGHEST)
                                   + b2_ref[...]))

    a = a_ref[...]
    decay = jnp.exp(-t_ref[...] * dl_ref[...])
    hf = jnp.dot(a, w3f_ref[...], preferred_element_type=F32, precision=HIGHEST) * decay
    hb = jnp.dot(a, w3b_ref[...], preferred_element_type=F32, precision=HIGHEST) * decay
    row = lax.broadcasted_iota(I32, hb.shape, 0)
    hb = jnp.where(row == 0, 0.0, hb)
    inv = 1.0 / (jnp.sum(jnp.abs(hf), axis=0, keepdims=True) + jnp.sum(jnp.abs(hb), axis=0, keepdims=True))
    of_ref[0] = hf * inv
    ob_ref[0] = hb * inv


def hy_filter_taps(l, d, f_w1, f_b1, f_w2, f_b2, f_w3, sin_freq):
    hid = f_w1.shape[1]
    hp = LANES
    t = np.linspace(0.0, 1.0, l)[:, None]
    bands = np.linspace(1e-4, HY_BANDS - 1, HY_BANDS)[None, :]
    ang = (2.0 * math.pi / l) * np.arange(l)[:, None] * bands
    z = np.zeros((l, hp), np.float64)
    z[:, :HY_EMB] = np.concatenate([t, np.cos(ang), -np.sin(ang)], axis=-1)
    deltas = np.abs(np.linspace(math.log(HY_TARGET) / HY_SLOW_DECAY, math.log(HY_TARGET) / HY_FAST_DECAY, d))[None, :]

    def pad2(a, r, c):
        return jnp.zeros((r, c), F32).at[:a.shape[0], :a.shape[1]].set(a.astype(F32))

    w1 = pad2(f_w1, hp, hp)
    w2 = pad2(f_w2, hp, hp)
    w3 = pad2(f_w3, hp, f_w3.shape[1])
    b1 = pad2(f_b1[None], 1, hp)
    b2 = pad2(f_b2[None], 1, hp)
    fr = pad2(sin_freq[None], 1, hp)
    tc = min(d, 2 * LANES)
    nd = d // tc
    full = lambda o, j: (0, 0)
    return pl.pallas_call(
        _hy_filter_kernel,
        grid=(HY_ORDER, nd),
        in_specs=[pl.BlockSpec((l, hp), full), pl.BlockSpec((hp, hp), full), pl.BlockSpec((1, hp), full),
                  pl.BlockSpec((hp, hp), full), pl.BlockSpec((1, hp), full), pl.BlockSpec((1, hp), full),
                  pl.BlockSpec((hp, tc), lambda o, j: (0, o * nd + j)),
                  pl.BlockSpec((hp, tc), lambda o, j: (0, (HY_ORDER + o) * nd + j)),
                  pl.BlockSpec((l, 1), full),
                  pl.BlockSpec((1, tc), lambda o, j: (0, j))],
        out_specs=[pl.BlockSpec((1, l, tc), lambda o, j: (o, 0, j)),
                   pl.BlockSpec((1, l, tc), lambda o, j: (o, 0, j))],
        out_shape=[jax.ShapeDtypeStruct((HY_ORDER, l, d), F32)] * 2,
        scratch_shapes=[pltpu.VMEM((l, hp), F32)],
        compiler_params=_cparams("arbitrary", "arbitrary"),
        name="hy_filter_taps",
    )(jnp.asarray(z, F32), w1, b1, w2, b2, fr, w3, w3, jnp.asarray(t, F32), jnp.asarray(deltas, F32))


def _dft_consts(l):
    n = 2 * l
    if n <= 4 * FFT_N2:
        n1, n2 = 1, n
    else:
        n1, n2 = n // FFT_N2, FFT_N2
    if n1 == 1:
        k = np.arange(n)[:, None]
        t = np.arange(l)[None, :]
        ang = 2.0 * np.pi * (k * t % n) / n
        c, s = np.cos(ang), np.sin(ang)
        fwd = np.block([[c, s], [-s, c]])
        inv = np.block([[c.T, -s.T], [s.T, c.T]]) / n
        return n1, n2, dict(fwd=jnp.asarray(fwd, BF16), inv=jnp.asarray(inv, BF16))
    n1h = n1 // 2
    k1 = np.arange(n1)[None, :, None]
    m1 = np.arange(n1h)[None, None, :]
    r2 = np.arange(n2)[:, None, None]
    ang = 2.0 * np.pi * ((k1 * (n2 * m1 + r2)) % n) / n
    c, s = np.cos(ang), np.sin(ang)
    m1f = np.concatenate([np.concatenate([c, s], 2), np.concatenate([-s, c], 2)], 1)
    ct, st = np.swapaxes(c, 1, 2) / n, np.swapaxes(s, 1, 2) / n
    m1i = np.concatenate([np.concatenate([ct, -st], 2), np.concatenate([st, ct], 2)], 1)
    a2 = 2.0 * np.pi * ((np.arange(n2)[:, None] * np.arange(n2)[None, :]) % n2) / n2
    c2, s2 = np.cos(a2), np.sin(a2)
    f2 = np.block([[c2, s2], [-s2, c2]])
    f2i = np.block([[c2, -s2], [s2, c2]])

    def il(m):
        return np.stack([np.arange(m), m + np.arange(m)], 1).reshape(-1)

    m1f = m1f[:, il(n1), :][:, :, il(n1h)]
    m1i = m1i[:, il(n1h), :][:, :, il(n1)]
    f2 = f2[:, il(n2)]
    f2i = f2i[il(n2), :][:, il(n2)]
    half = n2 // 2
    m1f = np.concatenate([m1f[:half], m1f[half:]], 2)
    m1i = np.concatenate([m1i[:half], m1i[half:]], 2)
    return n1, n2, dict(m1f=jnp.asarray(m1f, BF16), m1i=jnp.asarray(m1i, BF16),
                        f2=jnp.asarray(f2, BF16), f2i=jnp.asarray(f2i, BF16))


def _blockdiag(a, b):
    z = jnp.zeros_like(a)
    return jnp.concatenate([jnp.concatenate([a, z], axis=1), jnp.concatenate([z, b], axis=1)], axis=0)


def _halves(y):
    return y[:, :LANES], y[:, LANES:]


HI_HALF = -65536


def _pack_pair(re, im):
    lo = (pltpu.bitcast(re.astype(BF16).astype(F32), I32) >> 16) & 0xFFFF
    if im is None:
        return lo
    return lo | (pltpu.bitcast(im.astype(BF16).astype(F32), I32) & HI_HALF)


def _unpack_pair(w):
    return pltpu.bitcast(w << 16, F32), pltpu.bitcast(w & HI_HALF, F32)


def _as_rows(w):
    return pltpu.bitcast(w, BF16)


def _as_words(y):
    return pltpu.bitcast(y.astype(BF16), I32)


def _fwd_stage1(x_pk, a_pk, m1f_ref, n1, n2):
    n1h, half = n1 // 2, n2 // 2

    def column(r):
        return _as_rows(x_pk[pl.ds(r, n1h, stride=n2), :])

    def body(r, carry):
        w = _as_words(_dot(m1f_ref[r], _blockdiag(column(r), column(r + half))))
        for rr, ww in zip((r, r + half), _halves(w)):
            a_pk[pl.ds(pl.multiple_of(rr * n1, n1), n1), :] = ww
        return carry

    lax.fori_loop(0, half, body, 0, unroll=min(half, DFT_UNROLL_STAGE1))


def _inv_stage1(b_pk, x_pk, m1i_ref, n1, n2):
    n1h, half = n1 // 2, n2 // 2

    def column(r):
        return _as_rows(b_pk[pl.ds(pl.multiple_of(r * n1, n1), n1), :])

    def body(r, carry):
        w = _as_words(_dot(m1i_ref[r], _blockdiag(column(r), column(r + half))))
        for rr, ww in zip((r, r + half), _halves(w)):
            x_pk[pl.ds(rr, n1h, stride=n2), :] = ww
        return carry

    lax.fori_loop(0, half, body, 0, unroll=min(half, DFT_UNROLL_STAGE1))


def _load_chunk(a_pk, k1, n1, n2):
    ra = pl.ds(k1, n2, stride=n1)
    rb = pl.ds(k1 + n1 // 2, n2, stride=n1)
    return jnp.concatenate([_as_rows(a_pk[ra, :]), _as_rows(a_pk[rb, :])], axis=1)


def _hy_spec_kernel(*refs, l, n1, n2, dense):
    if dense:
        tf_ref, tb_ref, sk_ref, fwd_ref, o_ref = refs
        n = n2
        fw = fwd_ref[:, 0:l]
        sf = _dot(fw, tf_ref[0].astype(BF16))
        sb = _dot(fw, tb_ref[0].astype(BF16))
        o_ref[0, 0] = (sf[:n] + sb[:n] + sk_ref[0]).astype(o_ref.dtype)
        o_ref[0, 1] = (sf[n:] - sb[n:]).astype(o_ref.dtype)
        return
    tf_ref, tb_ref, sk_ref, m1f_ref, f2_ref, o_ref, x_pk, a_pk, h_re, h_im = refs
    for run, t_ref in enumerate((tf_ref, tb_ref)):
        x_pk[...] = _pack_pair(t_ref[0], None)
        _fwd_stage1(x_pk, a_pk, m1f_ref, n1, n2)

        def body(k1, carry, run=run):
            x = _dot(f2_ref[...], _load_chunk(a_pk, k1, n1, n2))
            for kk, xx in zip((k1, k1 + n1 // 2), _halves(x)):
                rows = pl.ds(pl.multiple_of(kk * n2, n2), n2)
                if run == 0:
                    h_re[rows, :] = xx[:n2]
                    h_im[rows, :] = xx[n2:]
                else:
                    h_re[rows, :] = h_re[rows, :] + xx[:n2]
                    h_im[rows, :] = h_im[rows, :] - xx[n2:]
            return carry

        lax.fori_loop(0, n1 // 2, body, 0, unroll=DFT_UNROLL)
    o_ref[0, 0] = (h_re[...] + sk_ref[0]).astype(o_ref.dtype)
    o_ref[0, 1] = h_im[...].astype(o_ref.dtype)


def hy_spectrum(taps_f, taps_b, skip, consts, n1, n2):
    o, l, d = taps_f.shape
    n = 2 * l
    dt = HY_DT
    dense = n1 == 1
    tap_spec = pl.BlockSpec((1, l, dt), lambda i, j: (i, 0, j))
    in_specs = [tap_spec, tap_spec, pl.BlockSpec((1, 1, dt), lambda i, j: (i, 0, j))]
    if dense:
        mats = [consts["fwd"]]
        scratch = []
    else:
        mats = [consts["m1f"], consts["f2"]]
        scratch = [pltpu.VMEM((l, dt), I32), pltpu.VMEM((n, dt), I32)] + [pltpu.VMEM((n, dt), F32)] * 2
    in_specs += [pl.BlockSpec(m.shape, lambda i, j, nd=m.ndim: (0,) * nd) for m in mats]
    return pl.pallas_call(
        functools.partial(_hy_spec_kernel, l=l, n1=n1, n2=n2, dense=dense),
        grid=(o, d // dt),
        in_specs=in_specs,
        out_specs=pl.BlockSpec((1, 2, n, dt), lambda i, j: (i, 0, 0, j)),
        out_shape=jax.ShapeDtypeStruct((o, 2, n, d), BF16),
        scratch_shapes=scratch,
        compiler_params=_cparams("arbitrary", "arbitrary"),
        name="hy_spectrum",
    )(taps_f, taps_b, skip.reshape(o, 1, d), *mats)


def _conv3_rows(src, cw, pad_ref, l):
    c = pad_ref.shape[1]
    pad_ref[pl.ds(0, SUBLANES), :] = jnp.zeros((SUBLANES, c), F32)
    pad_ref[pl.ds(SUBLANES + l, SUBLANES), :] = jnp.zeros((SUBLANES, c), F32)
    pad_ref[pl.ds(SUBLANES, l), :] = src
    return (pad_ref[pl.ds(SUBLANES - 1, l), :] * cw[0:1] + pad_ref[pl.ds(SUBLANES, l), :] * cw[1:2]
            + pad_ref[pl.ds(SUBLANES + 1, l), :] * cw[2:3])


def _hy_conv_kernel(*refs, l, n1, n2, dense, conv_z):
    if dense:
        z_ref, g_ref, cwz_ref, cwg_ref, h_ref, fwd_ref, inv_ref, o_ref, pad_ref = refs
    else:
        (z_ref, g_ref, cwz_ref, cwg_ref, h_ref, m1f_ref, m1i_ref, f2_ref, f2i_ref, o_ref,
         x_pk, a_pk, b_pk, pad_ref) = refs
    zs = []
    for e in range(2):
        z = z_ref[e, 0].astype(F32)
        zs.append(_conv3_rows(z, cwz_ref[...], pad_ref, l) if conv_z else z)
    if dense:
        n = n2
        s = _dot(fwd_ref[...], jnp.concatenate(zs, axis=0).astype(BF16))
        hr = h_ref[0, 0].astype(F32)
        hi = h_ref[0, 1].astype(F32)
        zr = s[:n] * hr - s[n:] * hi
        zi = s[:n] * hi + s[n:] * hr
        y = _dot(inv_ref[...], jnp.concatenate([zr, zi], axis=0).astype(BF16))
        ys = (y[:l], y[l:])
    else:
        x_pk[...] = _pack_pair(zs[0], zs[1])
        _fwd_stage1(x_pk, a_pk, m1f_ref, n1, n2)

        nk = n1 // 2

        def spectrum_step(k1, carry):
            s = _dot(f2_ref[...], _load_chunk(a_pk, k1, n1, n2))
            rows = [pl.ds(pl.multiple_of(k * n2, n2), n2) for k in (k1, k1 + nk)]
            hr = jnp.concatenate([h_ref[0, 0, r, :] for r in rows], axis=1).astype(F32)
            hi = jnp.concatenate([h_ref[0, 1, r, :] for r in rows], axis=1).astype(F32)
            zr = s[:n2] * hr - s[n2:] * hi
            zi = s[:n2] * hi + s[n2:] * hr
            for r, zr_half, zi_half in zip(rows, _halves(zr), _halves(zi)):
                b_pk[r, :] = _pack_pair(zr_half, zi_half)
            return carry

        lax.fori_loop(0, nk, spectrum_step, 0, unroll=min(nk, DFT_UNROLL))

        def inverse_step(k1, carry):
            kk = (k1, k1 + nk)
            z = jnp.concatenate([_as_rows(b_pk[pl.ds(pl.multiple_of(k * n2, n2), n2), :]) for k in kk], axis=1)
            w = _as_words(_dot(f2i_ref[...], z))
            for k, ww in zip(kk, _halves(w)):
                a_pk[pl.ds(k, n2, stride=n1), :] = ww
            return carry

        lax.fori_loop(0, nk, inverse_step, 0, unroll=min(nk, DFT_UNROLL))
        _inv_stage1(a_pk, x_pk, m1i_ref, n1, n2)
        ys = _unpack_pair(x_pk[...])
    for e in range(2):
        g = _conv3_rows(g_ref[e, 0].astype(F32), cwg_ref[...], pad_ref, l)
        o_ref[e, 0] = (g * ys[e]).astype(o_ref.dtype)


def hy_conv(zsrc, zcol, p4, gcol, conv_w, spec, order, consts, n1, n2, conv_z):
    _, pairs, l, _ = zsrc.shape
    d = spec.shape[3]
    n = 2 * l
    dt = HY_DT
    nd = d // dt
    dense = n1 == 1
    once = pl.Buffered(1)
    in_specs = [pl.BlockSpec((2, 1, l, dt), lambda j, i: (0, i, 0, zcol * nd + j)),
                pl.BlockSpec((2, 1, l, dt), lambda j, i: (0, i, 0, gcol * nd + j)),
                pl.BlockSpec((CONV_W, dt), lambda j, i: (0, j)),
                pl.BlockSpec((CONV_W, dt), lambda j, i: (0, gcol * nd + j)),
                pl.BlockSpec((1, 2, n, dt), lambda j, i: (order, 0, 0, j), pipeline_mode=once)]
    if dense:
        mats = [consts["fwd"], consts["inv"]]
        scratch = []
    else:
        mats = [consts["m1f"], consts["m1i"], consts["f2"], consts["f2i"]]
        scratch = [pltpu.VMEM((l, dt), I32)] + [pltpu.VMEM((n, dt), I32)] * 2
    scratch = scratch + [pltpu.VMEM((l + 2 * SUBLANES, dt), F32)]
    in_specs += [pl.BlockSpec(m.shape, lambda j, i, nd_=m.ndim: (0,) * nd_, pipeline_mode=once) for m in mats]
    return pl.pallas_call(
        functools.partial(_hy_conv_kernel, l=l, n1=n1, n2=n2, dense=dense, conv_z=conv_z),
        grid=(nd, pairs),
        in_specs=in_specs,
        out_specs=pl.BlockSpec((2, 1, l, dt), lambda j, i: (0, i, 0, j)),
        out_shape=jax.ShapeDtypeStruct((2, pairs, l, d), BF16),
        scratch_shapes=scratch,
        compiler_params=_cparams("arbitrary", "arbitrary"),
        name="hy_conv",
    )(zsrc, p4, conv_w, conv_w, spec, *mats)


def hyena_mixer(p, conv_w, f_w1, f_b1, f_w2, f_b2, f_w3, sin_freq, skip):
    b, l, d3 = p.shape
    d = d3 // 3
    assert b % 2 == 0 and d % HY_DT == 0
    n1, n2, consts = _dft_consts(l)
    taps_f, taps_b = hy_filter_taps(l, d, f_w1, f_b1, f_w2, f_b2, f_w3, sin_freq)
    spec = hy_spectrum(taps_f, taps_b, skip, consts, n1, n2)
    p4 = p.reshape(2, b // 2, l, d3)
    z = hy_conv(p4, 0, p4, 1, conv_w, spec, 0, consts, n1, n2, True)
    z = hy_conv(z, 0, p4, 2, conv_w, spec, 1, consts, n1, n2, False)
    return z.reshape(b, l, d)


def _head_norm(x, g):
    sq = x * x
    hi = sq.astype(BF16)
    lo = (sq - hi.astype(F32)).astype(BF16)
    ones = jnp.ones((HEAD_DIM, HEAD_DIM), BF16)
    total = _dot(hi, ones) + _dot(lo, ones)
    return x * lax.rsqrt(total * (1.0 / HEAD_DIM) + EPS) * g


def _rope(x, cos, sin_a, sin_b):
    return (x * cos + pltpu.roll(x, HEAD_DIM // 4, 1) * sin_a
            + pltpu.roll(x, HEAD_DIM - HEAD_DIM // 4, 1) * sin_b)


def _qkv_kernel(x_ref, g_ref, sh_ref, sc_ref, w_ref, qg_ref, kg_ref, cos_ref, sa_ref, sb_ref,
                *out_refs, nq, rope):
    q_ref = out_refs[0] if nq else None
    k_ref, v_ref = out_refs[-2:]
    h = _norm_mod(x_ref[0], g_ref[...], sh_ref[0], sc_ref[0]).astype(BF16)
    acc = _dot(h, w_ref[...])
    hd = HEAD_DIM
    for i in range(nq + N_KV_HEADS):
        t = acc[:, i * hd:(i + 1) * hd]
        t = _head_norm(t, qg_ref[...] if i < nq else kg_ref[...])
        if rope:
            t = _rope(t, cos_ref[...], sa_ref[...], sb_ref[...])
        if i < nq:
            q_ref[0, :, i * hd:(i + 1) * hd] = (t * Q_PRESCALE).astype(BF16)
        else:
            k_ref[0, :, (i - nq) * hd:(i - nq + 1) * hd] = t.astype(BF16)
    v_ref[0] = acc[:, (nq + N_KV_HEADS) * hd:].astype(BF16)


def qkv_project(x, g, sh, sc, w, q_g, k_g, rope_tabs, nq, rope):
    b, l, d = x.shape
    n = w.shape[1]
    hd = HEAD_DIM
    kvd = N_KV_HEADS * hd
    tm = _row_tile(l)
    cos, sa, sb = rope_tabs
    tab = pl.BlockSpec((tm, hd), lambda i, j: (j, 0))
    vec = pl.BlockSpec((1, 1, d), lambda i, j: (i, 0, 0))
    widths = ([nq * hd] if nq else []) + [kvd, kvd]
    return pl.pallas_call(
        functools.partial(_qkv_kernel, nq=nq, rope=rope),
        grid=(b, l // tm),
        in_specs=[pl.BlockSpec((1, tm, d), lambda i, j: (i, j, 0)),
                  pl.BlockSpec((1, d), lambda i, j: (0, 0)), vec, vec,
                  pl.BlockSpec((d, n), lambda i, j: (0, 0)),
                  pl.BlockSpec((1, hd), lambda i, j: (0, 0)),
                  pl.BlockSpec((1, hd), lambda i, j: (0, 0)),
                  tab, tab, tab],
        out_specs=[pl.BlockSpec((1, tm, wd), lambda i, j: (i, j, 0)) for wd in widths],
        out_shape=[jax.ShapeDtypeStruct((b, l, wd), BF16) for wd in widths],
        compiler_params=_cparams("arbitrary", "arbitrary"),
        name="qkv_project",
    )(x, g.reshape(1, d), sh, sc, w, q_g.reshape(1, hd), k_g.reshape(1, hd), cos, sa, sb)


def _rope_tables(l):
    rows = l // GRID_W
    row = np.repeat(np.arange(rows, dtype=np.float64), GRID_W)
    col = np.tile(np.arange(GRID_W, dtype=np.float64), rows)
    axis = HEAD_DIM // 2
    inv = ROPE_THETA ** (-np.arange(0, axis, 2, dtype=np.float64) / axis)

    def axis_angles(pos):
        a = pos[:, None] * inv[None, :]
        return np.concatenate([a, a], axis=-1)

    ang = np.concatenate([axis_angles(row), axis_angles(col)], axis=-1)
    cos, sin = np.cos(ang), np.sin(ang)
    lane = np.arange(HEAD_DIM)[None, :] % axis
    upper = lane >= axis // 2
    sin_a = np.where(upper, sin, 0.0)
    sin_b = np.where(upper, 0.0, -sin)
    return tuple(jnp.asarray(t, F32) for t in (cos, sin_a, sin_b))


def _attn_kernel(q_ref, k_ref, vt_ref, o_ref, *, groups):
    k = k_ref[0]
    vt = vt_ref[0]
    hd = HEAD_DIM
    tq = q_ref.shape[1]
    hp = 2 if groups % 2 == 0 else 1
    def scores(g0):
        q = jnp.concatenate([q_ref[0, :, g * hd:(g + 1) * hd] for g in range(g0, g0 + hp)], axis=0)
        return lax.dot_general(k, q, (((1,), (1,)), ((), ())), preferred_element_type=F32)

    sts = [scores(g0) for g0 in range(0, groups, hp)]
    for g0, st in zip(range(0, groups, hp), sts):
        m = jnp.max(st, axis=0, keepdims=True)
        pt = jnp.exp2(st - m)
        den = jnp.sum(pt, axis=0, keepdims=True)
        ot = _dot(vt, pt.astype(BF16)) / den
        for i in range(hp):
            g = g0 + i
            o_ref[0, :, g * hd:(g + 1) * hd] = ot[:, i * tq:(i + 1) * tq].T.astype(o_ref.dtype)


def attention(q, k, v):
    b, l, qd = q.shape
    s = k.shape[1]
    hd = HEAD_DIM
    groups = qd // hd // N_KV_HEADS
    tq = _row_tile(l, 256)
    vt = jnp.swapaxes(v, 1, 2)
    return pl.pallas_call(
        functools.partial(_attn_kernel, groups=groups),
        grid=(b, N_KV_HEADS, l // tq),
        in_specs=[pl.BlockSpec((1, tq, groups * hd), lambda i, h, j: (i, j, h)),
                  pl.BlockSpec((1, s, hd), lambda i, h, j: (i, 0, h)),
                  pl.BlockSpec((1, hd, s), lambda i, h, j: (i, h, 0))],
        out_specs=pl.BlockSpec((1, tq, groups * hd), lambda i, h, j: (i, j, h)),
        out_shape=jax.ShapeDtypeStruct((b, l, qd), BF16),
        compiler_params=_cparams("arbitrary", "arbitrary", "arbitrary"),
        name="attention",
    )(q, k, vt)


def _route(x, g_ref, sh_ref, sc_ref, wr_ref, h_ref, a_ref):
    h = _norm_mod(x, g_ref[...], sh_ref[0], sc_ref[0])
    d = h.shape[1]
    logits = lax.dot_general(wr_ref[...], h, (((1,), (1,)), ((), ())), preferred_element_type=F32,
                             precision=HIGHEST)
    ex = jnp.exp(logits - jnp.max(logits, axis=0, keepdims=True))
    aff = ex / jnp.sum(ex, axis=0, keepdims=True)
    a_ref[0] = aff
    ne, tm = aff.shape
    hi = aff.astype(BF16).astype(F32)
    mid = (aff - hi).astype(BF16).astype(F32)
    lo = (aff - hi - mid).astype(BF16).astype(F32)
    pieces = jnp.concatenate([hi, mid, lo, jnp.zeros((LANES - 3 * ne, tm), F32)], axis=0)
    h_ref[0, :, 0:d] = h.astype(BF16)
    h_ref[0, :, d:d + LANES] = pieces.T.astype(BF16)


def _cumsum_lanes(m, tri_ref, ind_ref):
    e, n = m.shape
    mb = m.astype(BF16)
    before = _dot(mb, ind_ref[...])
    outs = []
    for c in range(n // LANES):
        outs.append(_dot(mb[:, c * LANES:(c + 1) * LANES], tri_ref[...]) + before[:, c:c + 1])
    return jnp.concatenate(outs, axis=1), before


def _select_kernel(a_ref, tri_ref, ind_ref, slot_ref, before_ref, slot_t_ref, *, cap):
    aff = a_ref[0]
    bits = pltpu.bitcast(aff, I32)
    e = aff.shape[0]

    def body(i, thr):
        cand = thr | (1 << (30 - i))
        cnt = jnp.sum((bits >= cand).astype(F32), axis=1, keepdims=True)
        return jnp.where(cnt >= cap, cand, thr)

    thr = lax.fori_loop(0, 31, body, jnp.zeros((e, 1), I32))
    gt = bits > thr
    eq = bits == thr
    n_gt = jnp.sum(gt.astype(F32), axis=1, keepdims=True)
    eq_rank, _ = _cumsum_lanes(eq.astype(F32), tri_ref, ind_ref)
    sel = gt | (eq & (eq_rank <= cap - n_gt))
    pos, before = _cumsum_lanes(sel.astype(F32), tri_ref, ind_ref)
    slot = jnp.where(sel, pos - 1.0, -1.0)
    slot_ref[0] = slot
    before_ref[0] = before
    slot_t_ref[0] = slot.T


def moe_select(aff, cap):
    b, e, n = aff.shape
    assert n // LANES <= LANES
    tri = jnp.asarray(np.triu(np.ones((LANES, LANES), np.float32)), BF16)
    ind = jnp.asarray(np.arange(n)[:, None] < LANES * np.arange(LANES)[None, :], BF16)
    row = pl.BlockSpec((1, e, n), lambda i: (i, 0, 0))
    col = pl.BlockSpec((1, n, e), lambda i: (i, 0, 0))
    return pl.pallas_call(
        functools.partial(_select_kernel, cap=cap),
        grid=(b,),
        in_specs=[row, pl.BlockSpec((LANES, LANES), lambda i: (0, 0)), pl.BlockSpec((n, LANES), lambda i: (0, 0))],
        out_specs=[row, pl.BlockSpec((1, e, LANES), lambda i: (i, 0, 0)), col],
        out_shape=[jax.ShapeDtypeStruct((b, e, n), F32), jax.ShapeDtypeStruct((b, e, LANES), F32),
                   jax.ShapeDtypeStruct((b, n, e), F32)],
        compiler_params=_cparams("arbitrary"),
        name="moe_select",
    )(aff, tri, ind)


class _Windows:
    def __init__(self, before, n, cap):
        self.tile = min(n, MOE_TILE)
        self.nt = n // self.tile
        self.narrow = sorted({w for w in (min(cap, w) for w in MOE_WINDOWS)
                              if (w % LANES == 0 or LANES % w == 0) and w % BF16_ROWS == 0})
        self.wide = min(cap, self.tile + BF16_ROWS)
        first = before[:, :, 0:n // LANES:self.tile // LANES].astype(I32)
        count = jnp.concatenate([first[:, :, 1:], jnp.full_like(first[:, :, :1], cap)], axis=2) - first
        aligned = first // BF16_ROWS * BF16_ROWS
        starts = [jnp.clip(aligned, 0, cap - w) for w in self.narrow + [self.wide]]
        level = jnp.zeros(first.shape[:1] + first.shape[2:], I32)
        for i in reversed(range(len(self.narrow))):
            fits = jnp.all(first - starts[i] + count <= self.narrow[i], axis=1)
            level = jnp.where(fits, i + 1, level)
        self.args = tuple(s.reshape(-1) for s in starts) + (level.reshape(-1),)


def _gather_kernel(*refs, ne, narrow, wide):
    narrow_refs = refs[:len(narrow)]
    ww_ref, level_ref, slot_ref, h_ref, o_ref = refs[len(narrow):]
    bi, gi, ki = pl.program_id(0), pl.program_id(1), pl.program_id(2)
    nt = pl.num_programs(2)
    group, tk = slot_ref.shape[1], slot_ref.shape[2]
    level = level_ref[bi * nt + ki]

    @pl.when(ki == 0)
    def _():
        o_ref[...] = jnp.zeros_like(o_ref)

    slot = slot_ref[0].astype(I32)

    def start(ref, j):
        return pl.multiple_of(ref[(bi * ne + gi * group + j) * nt + ki], BF16_ROWS)

    def onehot(j, ws, width):
        jcol = lax.broadcasted_iota(I32, (width, tk), 0)
        return jnp.where(jcol + ws == slot[j:j + 1, :], 1.0, 0.0).astype(BF16)

    def add(j, ws, width, rows_f32):
        rows = pl.ds(ws, width)
        o_ref[0, j, rows, :] = o_ref[0, j, rows, :] + rows_f32.astype(o_ref.dtype)

    for i, (wn_ref, width) in enumerate(zip(narrow_refs, narrow)):
        @pl.when(level == i + 1)
        def _(wn_ref=wn_ref, width=width):
            starts = [start(wn_ref, j) for j in range(group)]
            lhs = jnp.concatenate([onehot(j, starts[j], width) for j in range(group)], axis=0)
            res = _dot(lhs, h_ref[0])
            for j in range(group):
                add(j, starts[j], width, res[j * width:(j + 1) * width])

    @pl.when(level == 0)
    def _():
        for j in range(group):
            ws = start(ww_ref, j)
            add(j, ws, wide, _dot(onehot(j, ws, wide), h_ref[0]))


def moe_gather(slot, win, h, cap):
    b, e, n = slot.shape
    dx = h.shape[2]
    group = math.gcd(e, MOE_GROUP)
    grid_spec = pltpu.PrefetchScalarGridSpec(
        num_scalar_prefetch=len(win.args),
        grid=(b, e // group, win.nt),
        in_specs=[pl.BlockSpec((1, group, win.tile), lambda i, g, k, *_: (i, g, k)),
                  pl.BlockSpec((1, win.tile, dx), lambda i, g, k, *_: (i, k, 0))],
        out_specs=pl.BlockSpec((1, group, cap, dx), lambda i, g, k, *_: (i, g, 0, 0)))
    return pl.pallas_call(
        functools.partial(_gather_kernel, ne=e, narrow=win.narrow, wide=win.wide),
        grid_spec=grid_spec,
        out_shape=jax.ShapeDtypeStruct((b, e, cap, dx), BF16),
        compiler_params=_cparams("arbitrary", "arbitrary", "arbitrary"),
        name="moe_gather",
    )(*win.args, slot, h)


def _ffn_kernel(*refs, ns):
    x_refs = refs[:ns]
    wg_ref, wu_ref, wd_ref = refs[ns:ns + 3]
    o_refs = refs[ns + 3:2 * ns + 3]
    wgb_ref, wub_ref, wdb_ref = refs[2 * ns + 3:]

    @pl.when(pl.program_id(1) == 0)
    def _():
        wgb_ref[...] = wg_ref[0, 0].astype(BF16)
        wub_ref[...] = wu_ref[0, 0].astype(BF16)
        wdb_ref[...] = wd_ref[0, 0].astype(BF16)

    d = wgb_ref.shape[0]
    ei, ne = pl.program_id(0), pl.num_programs(0)

    chunks = [[(0, i, r, min(r + FFN_ROWS, x_refs[0].shape[2]))]
              for i in range(x_refs[0].shape[0]) for r in range(0, x_refs[0].shape[2], FFN_ROWS)]
    for s in range(1, ns):
        chunks[-1] += [(s, i, 0, x_refs[s].shape[2]) for i in range(x_refs[s].shape[0])]

    def rows_of(chunk, c0, c1):
        parts = [x_refs[s][i, 0, r0:r1, c0:c1] for s, i, r0, r1 in chunk]
        return parts[0] if len(parts) == 1 else jnp.concatenate(parts, axis=0)

    ups = []
    for chunk in chunks:
        x = rows_of(chunk, 0, d)
        ups.append((_dot(x, wgb_ref[...]), _dot(x, wub_ref[...])))
    for chunk, (a, u) in zip(chunks, ups):
        ext = rows_of(chunk, d, d + LANES).astype(F32)
        lane = lax.broadcasted_iota(I32, ext.shape, 1)
        mine = (lane == ei) | (lane == ei + ne) | (lane == ei + 2 * ne)
        gate = jnp.sum(jnp.where(mine, ext, 0.0), axis=1, keepdims=True)
        y = _dot(((a / (1.0 + jnp.exp(-a))) * u).astype(BF16), wdb_ref[...]) * gate
        row = 0
        for s, i, r0, r1 in chunk:
            o_refs[s][i, 0, r0:r1, :] = y[row:row + r1 - r0].astype(o_refs[s].dtype)
            row += r1 - r0


def moe_ffn(xs_list, w_gate, w_up, w_down, layer):
    b, e, _, dx = xs_list[0].shape
    d, f = w_gate.shape[2], w_gate.shape[3]
    bb = 2 if b % 2 == 0 else 1

    def specs(width):
        return [pl.BlockSpec((bb, 1, xs.shape[2], width), lambda j, i: (i, j, 0, 0)) for xs in xs_list]

    return pl.pallas_call(
        functools.partial(_ffn_kernel, ns=len(xs_list)),
        grid=(e, b // bb),
        in_specs=specs(dx) + [pl.BlockSpec((1, 1, d, f), lambda j, i: (layer, j, 0, 0)),
                              pl.BlockSpec((1, 1, d, f), lambda j, i: (layer, j, 0, 0)),
                              pl.BlockSpec((1, 1, f, d), lambda j, i: (layer, j, 0, 0))],
        out_specs=specs(d),
        out_shape=[jax.ShapeDtypeStruct(xs.shape[:3] + (d,), BF16) for xs in xs_list],
        scratch_shapes=[pltpu.VMEM((d, f), BF16), pltpu.VMEM((d, f), BF16), pltpu.VMEM((f, d), BF16)],
        compiler_params=_cparams("arbitrary", "arbitrary"),
        name="moe_ffn",
    )(*xs_list, w_gate, w_up, w_down)


def _combine_kernel(*refs, ne, narrow, wide):
    narrow_refs = refs[:len(narrow)]
    ww_ref, level_ref, slot_ref, ys_hbm, lat_ref, gt_ref, o_ref, ys_buf, sem = refs[len(narrow):]
    bi, ti = pl.program_id(0), pl.program_id(1)
    nb, nt = pl.num_programs(0), pl.num_programs(1)
    cur = bi % 2

    def fetch(batch, buf):
        return pltpu.make_async_copy(ys_hbm.at[batch], ys_buf.at[buf], sem.at[buf])

    @pl.when(ti == 0)
    def _():
        @pl.when(bi == 0)
        def _():
            fetch(0, 0).start()

        fetch(bi, cur).wait()

        @pl.when(bi + 1 < nb)
        def _():
            fetch(bi + 1, 1 - cur).start()

    tn = lat_ref.shape[1]
    slot = slot_ref[0].astype(I32)

    def start(ref, e):
        return pl.multiple_of(ref[(bi * ne + e) * nt + ti], BF16_ROWS)

    def onehot(e, ws, width):
        jrow = lax.broadcasted_iota(I32, (tn, width), 1)
        return jnp.where(jrow + ws == slot[:, e:e + 1], 1.0, 0.0).astype(BF16)

    def onehot_lane_group(first, starts, width):
        lane = lax.broadcasted_iota(I32, (tn, LANES), 1)
        target = slot[:, first:first + 1] - starts[first]
        for i in range(1, LANES // width):
            e = first + i
            target = jnp.where(lane >= i * width, slot[:, e:e + 1] - (starts[e] - i * width), target)
        return jnp.where(target == lane, 1.0, 0.0).astype(BF16)

    def finish(acc):
        o_ref[0] = lat_ref[0] + gt_ref[0] * acc

    level = level_ref[bi * nt + ti]
    for i, (wn_ref, width) in enumerate(zip(narrow_refs, narrow)):
        @pl.when(level == i + 1)
        def _(wn_ref=wn_ref, width=width):
            starts = [start(wn_ref, e) for e in range(ne)]
            if width % LANES == 0:
                lhs = jnp.concatenate([onehot(e, starts[e], width) for e in range(ne)], axis=1)
            else:
                lhs = jnp.concatenate([onehot_lane_group(e, starts, width)
                                       for e in range(0, ne, LANES // width)], axis=1)
            rhs = jnp.concatenate([ys_buf[cur, e, pl.ds(starts[e], width), :] for e in range(ne)], axis=0)
            finish(_dot(lhs, rhs))

    @pl.when(level == 0)
    def _():
        acc = jnp.zeros((tn, lat_ref.shape[2]), F32)
        for e in range(ne):
            ws = start(ww_ref, e)
            acc = acc + _dot(onehot(e, ws, wide), ys_buf[cur, e, pl.ds(ws, wide), :])
        finish(acc)


def moe_combine(slot_t, win, ys, lat, gate_vec, cap):
    b, n, e = slot_t.shape
    d = lat.shape[2]
    tn = win.tile
    grid_spec = pltpu.PrefetchScalarGridSpec(
        num_scalar_prefetch=len(win.args),
        grid=(b, win.nt),
        in_specs=[pl.BlockSpec((1, tn, e), lambda i, j, *_: (i, j, 0)),
                  pl.BlockSpec(memory_space=pl.ANY),
                  pl.BlockSpec((1, tn, d), lambda i, j, *_: (i, j, 0)),
                  pl.BlockSpec((1, 1, d), lambda i, j, *_: (i, 0, 0))],
        out_specs=pl.BlockSpec((1, tn, d), lambda i, j, *_: (i, j, 0)),
        scratch_shapes=[pltpu.VMEM((2, e, cap, d), BF16), pltpu.SemaphoreType.DMA((2,))])
    return pl.pallas_call(
        functools.partial(_combine_kernel, ne=e, narrow=win.narrow, wide=win.wide),
        grid_spec=grid_spec,
        out_shape=jax.ShapeDtypeStruct((b, n, d), F32),
        compiler_params=_cparams("arbitrary", "arbitrary"),
        name="moe_combine",
    )(*win.args, slot_t, ys, lat, gate_vec)


def expert_choice_moe(streams, w_gate, w_up, w_down, layer):
    routed = []
    for (lat, h, aff), _ in streams:
        e = aff.shape[1]
        assert 3 * e <= LANES and e % SUBLANES == 0
        cap = EC_CAPACITY_FACTOR * lat.shape[1] // e
        slot, before, slot_t = moe_select(aff, cap)
        win = _Windows(before, lat.shape[1], cap)
        routed.append((moe_gather(slot, win, h, cap), slot_t, win, cap))
    ys_list = moe_ffn([r[0] for r in routed], w_gate, w_up, w_down, layer)
    return [moe_combine(slot_t, win, ys, lat, gate_vec, cap)
            for ((lat, _, _), gate_vec), (_, slot_t, win, cap), ys in zip(streams, routed, ys_list)]


def kernel(x, c, ctx, c_ctx, ada_w, ada_b, norm_g, sc_w_in, sc_conv, sc_w_out, hy_w_in, hy_conv_w, hy_f_w1,
           hy_f_b1, hy_f_w2, hy_f_b2, hy_f_w3, hy_sin_freq, hy_skip, hy_w_out, at_w_qkv, at_q_g, at_k_g, at_w_o,
           moe_router_w, moe_w_gate, moe_w_up, moe_w_down):
    b, l, d = x.shape
    depth = ada_w.shape[0]
    assert b + 1 <= MOD_ROWS
    qd = at_w_o.shape[1]
    nq = qd // HEAD_DIM
    attn_layers = list(range(MIX_ATTN, depth, N_MIXERS))
    last_attn = max(attn_layers, default=-1)

    cond = jnp.zeros((MOD_ROWS, d), F32).at[:b].set(c).at[b].set(c_ctx)
    mods = ada_modulation(cond, ada_w, ada_b)

    lat, cx = x, ctx
    for i in range(depth):
        kind = i % N_MIXERS
        j = i // N_MIXERS
        need_ctx = i <= last_attn
        upd_ctx = i < last_attn
        m_l = mods[i, :b].reshape(b, 1, 6, d)
        sh1, sc1, g1, sh2, sc2, g2 = (m_l[:, :, k] for k in range(6))
        if need_ctx:
            m_c = jnp.broadcast_to(mods[i, b].reshape(1, 1, 6, d), (b, 1, 6, d))
            csh1, csc1, cg1, csh2, csc2, cg2 = (m_c[:, :, k] for k in range(6))
        w_router_t = moe_router_w[i].T
        streams = [(lat, sh1, sc1, g1, (norm_g[i, 1], sh2, sc2, w_router_t))]
        if upd_ctx:
            streams.append((cx, csh1, csc1, cg1, (norm_g[i, 1], csh2, csc2, w_router_t)))

        outs = []
        if kind == MIX_SHORTCONV:
            w_in, w_out = sc_w_in[j].astype(BF16), sc_w_out[j].astype(BF16)
            for s, sh, sc, gt, route in streams:
                outs.append(sc_mixer(s, norm_g[i, 0], sh, sc, w_in, sc_conv[j], w_out, gt, route))
        elif kind == MIX_HYENA:
            w_in, w_out = hy_w_in[j].astype(BF16), hy_w_out[j].astype(BF16)
            for s, sh, sc, gt, route in streams:
                p = nm_matmul(s, norm_g[i, 0], sh, sc, w_in)
                z = hyena_mixer(p, hy_conv_w[j], hy_f_w1[j], hy_f_b1[j], hy_f_w2[j], hy_f_b2[j], hy_f_w3[j],
                                hy_sin_freq[j], hy_skip[j])
                outs.append(proj_residual(z, w_out, s, gt, route))
        else:
            w_qkv, w_o = at_w_qkv[j].astype(BF16), at_w_o[j].astype(BF16)
            q_l, k_l, v_l = qkv_project(lat, norm_g[i, 0], sh1, sc1, w_qkv, at_q_g[j], at_k_g[j],
                                        _rope_tables(l), nq, True)
            lc = cx.shape[1]
            dummy = tuple(jnp.zeros((lc, HEAD_DIM), F32) for _ in range(3))
            if upd_ctx:
                q_c, k_c, v_c = qkv_project(cx, norm_g[i, 0], csh1, csc1, w_qkv, at_q_g[j], at_k_g[j],
                                            dummy, nq, False)
            else:
                k_c, v_c = qkv_project(cx, norm_g[i, 0], csh1, csc1, w_qkv[:, qd:], at_q_g[j], at_k_g[j],
                                       dummy, 0, False)
            v_all = jnp.concatenate([v_c, v_l], axis=1)
            o_l = attention(q_l, jnp.concatenate([k_c, k_l], axis=1), v_all)
            outs.append(proj_residual(o_l, w_o, lat, g1, streams[0][4]))
            if upd_ctx:
                outs.append(proj_residual(attention(q_c, k_c, v_c), w_o, cx, cg1, streams[1][4]))

        moe_streams = [(outs[0], g2)]
        if upd_ctx:
            moe_streams.append((outs[1], cg2))
        moe_out = expert_choice_moe(moe_streams, moe_w_gate, moe_w_up, moe_w_down, i)
        lat = moe_out[0]
        if upd_ctx:
            cx = moe_out[1]
    return lat
```

```python
import functools
import math

import jax
import jax.numpy as jnp
import numpy as np
from jax import lax
from jax.experimental import pallas as pl
from jax.experimental.pallas import tpu as pltpu

F32 = jnp.float32
BF16 = jnp.bfloat16
I32 = jnp.int32
HIGHEST = lax.Precision.HIGHEST

N_MIXERS = 3
MIX_SHORTCONV, MIX_HYENA, MIX_ATTN = 0, 1, 2
CONV_W = 3
EPS = 1e-6
HY_ORDER = 2
HY_EMB = 33
HY_BANDS = (HY_EMB - 1) // 2
HY_FAST_DECAY = 0.3
HY_SLOW_DECAY = 1.5
HY_TARGET = 1e-2
HEAD_DIM = 128
N_KV_HEADS = 2
GRID_W = 64
ROPE_THETA = 10000.0
EC_CAPACITY_FACTOR = 2
Q_PRESCALE = HEAD_DIM ** -0.5 * math.log2(math.e)

LANES = 128
SUBLANES = 8
BF16_ROWS = 16
VMEM_LIMIT_BYTES = 56 * 1024 * 1024
ROW_TILE = 1024
MOD_ROWS = 16
FFT_N2 = 128
HY_DT = LANES
MOE_TILE = 2 * LANES
MOE_WINDOWS = (LANES // 2, LANES)
MOE_GROUP = 2 * SUBLANES
FFN_ROWS = 2 * LANES
DFT_UNROLL = 8
DFT_UNROLL_STAGE1 = 32


def _cparams(*sem):
    return pltpu.CompilerParams(dimension_semantics=sem, vmem_limit_bytes=VMEM_LIMIT_BYTES)


def _row_tile(n, cap=ROW_TILE):
    t = min(n, cap)
    assert n % t == 0
    return t


def _dot(a, b):
    return jnp.dot(a, b, preferred_element_type=F32)


def _ada_kernel(s_ref, w_ref, b_ref, o_ref):
    s = s_ref[...]
    s = s / (1.0 + jnp.exp(-s))
    o_ref[0] = jnp.dot(s, w_ref[0], preferred_element_type=F32, precision=HIGHEST) + b_ref[0]


def ada_modulation(cond, ada_w, ada_b):
    depth, d, n = ada_w.shape
    tn = 1536 if n % 1536 == 0 else n
    return pl.pallas_call(
        _ada_kernel,
        grid=(depth, n // tn),
        in_specs=[pl.BlockSpec((MOD_ROWS, d), lambda i, j: (0, 0)),
                  pl.BlockSpec((1, d, tn), lambda i, j: (i, 0, j)),
                  pl.BlockSpec((1, 1, tn), lambda i, j: (i, 0, j))],
        out_specs=pl.BlockSpec((1, MOD_ROWS, tn), lambda i, j: (i, 0, j)),
        out_shape=jax.ShapeDtypeStruct((depth, MOD_ROWS, n), F32),
        compiler_params=_cparams("arbitrary", "arbitrary"),
        name="ada_modulation",
    )(cond, ada_w, ada_b.reshape(depth, 1, n))


def _norm_mod(x, g, sh, sc):
    y = x * lax.rsqrt(jnp.mean(x * x, axis=-1, keepdims=True) + EPS)
    return (y * g) * (1.0 + sc) + sh


def _nm_matmul_kernel(x_ref, g_ref, sh_ref, sc_ref, w_ref, o_ref):
    h = _norm_mod(x_ref[0], g_ref[...], sh_ref[0], sc_ref[0]).astype(BF16)
    o_ref[0] = _dot(h, w_ref[...]).astype(o_ref.dtype)


def nm_matmul(x, g, sh, sc, w, out_dtype=BF16):
    b, l, d = x.shape
    n = w.shape[1]
    tm = _row_tile(l)
    return pl.pallas_call(
        _nm_matmul_kernel,
        grid=(b, l // tm),
        in_specs=[pl.BlockSpec((1, tm, d), lambda i, j: (i, j, 0)),
                  pl.BlockSpec((1, d), lambda i, j: (0, 0)),
                  pl.BlockSpec((1, 1, d), lambda i, j: (i, 0, 0)),
                  pl.BlockSpec((1, 1, d), lambda i, j: (i, 0, 0)),
                  pl.BlockSpec((d, n), lambda i, j: (0, 0))],
        out_specs=pl.BlockSpec((1, tm, n), lambda i, j: (i, j, 0)),
        out_shape=jax.ShapeDtypeStruct((b, l, n), out_dtype),
        compiler_params=_cparams("arbitrary", "arbitrary"),
        name="nm_matmul",
    )(x, g.reshape(1, d), sh, sc, w)


def _route_specs(b, l, d, e, tm):
    vec = pl.BlockSpec((1, 1, d), lambda i, j: (i, 0, 0))
    in_specs = [pl.BlockSpec((1, d), lambda i, j: (0, 0)), vec, vec, pl.BlockSpec((e, d), lambda i, j: (0, 0))]
    out_specs = [pl.BlockSpec((1, tm, d + LANES), lambda i, j: (i, j, 0)),
                 pl.BlockSpec((1, e, tm), lambda i, j: (i, 0, j))]
    out_shape = [jax.ShapeDtypeStruct((b, l, d + LANES), BF16), jax.ShapeDtypeStruct((b, e, l), F32)]
    return in_specs, out_specs, out_shape


def _proj_res_kernel(a_ref, w_ref, lat_ref, gt_ref, g2_ref, sh2_ref, sc2_ref, wr_ref, o_ref, h_ref, aff_ref):
    out = lat_ref[0] + gt_ref[0] * _dot(a_ref[0], w_ref[...])
    o_ref[0] = out
    _route(out, g2_ref, sh2_ref, sc2_ref, wr_ref, h_ref, aff_ref)


def proj_residual(a, w, lat, gate, route):
    b, l, k = a.shape
    d = w.shape[1]
    g2, sh2, sc2, wr = route
    tm = _row_tile(l)
    r_in, r_out, r_shape = _route_specs(b, l, d, wr.shape[0], tm)
    return pl.pallas_call(
        _proj_res_kernel,
        grid=(b, l // tm),
        in_specs=[pl.BlockSpec((1, tm, k), lambda i, j: (i, j, 0)),
                  pl.BlockSpec((k, d), lambda i, j: (0, 0)),
                  pl.BlockSpec((1, tm, d), lambda i, j: (i, j, 0)),
                  pl.BlockSpec((1, 1, d), lambda i, j: (i, 0, 0))] + r_in,
        out_specs=[pl.BlockSpec((1, tm, d), lambda i, j: (i, j, 0))] + r_out,
        out_shape=[jax.ShapeDtypeStruct((b, l, d), F32)] + r_shape,
        compiler_params=_cparams("arbitrary", "arbitrary"),
        name="proj_residual",
    )(a, w, lat, gate, g2.reshape(1, d), sh2, sc2, wr)


def _sc_mixer_kernel(x_ref, xp_ref, xn_ref, g_ref, sh_ref, sc_ref, win_ref, cw_ref, wout_ref, gt_ref,
                     g2_ref, sh2_ref, sc2_ref, wr_ref, o_ref, h_ref, aff_ref, u_ref, *, tm):
    j = pl.program_id(1)
    nj = pl.num_programs(1)
    d = u_ref.shape[1]
    x = x_ref[0]
    p = _dot(_norm_mod(x, g_ref[...], sh_ref[0], sc_ref[0]).astype(BF16), win_ref[...])
    u_ref[pl.ds(SUBLANES, tm), :] = p[:, d:2 * d] * p[:, 2 * d:]
    halo = jnp.concatenate([xp_ref[0], xn_ref[0]], axis=0)
    ph = _dot(_norm_mod(halo, g_ref[...], sh_ref[0], sc_ref[0]).astype(BF16), win_ref[:, d:])
    uh = ph[:, :d] * ph[:, d:]
    prev = jnp.where(j == 0, 0.0, uh[SUBLANES - 1:SUBLANES])
    nxt = jnp.where(j == nj - 1, 0.0, uh[SUBLANES:SUBLANES + 1])
    u_ref[pl.ds(0, SUBLANES), :] = jnp.broadcast_to(prev, (SUBLANES, d))
    u_ref[pl.ds(SUBLANES + tm, SUBLANES), :] = jnp.broadcast_to(nxt, (SUBLANES, d))
    cw = cw_ref[...]
    y = (u_ref[pl.ds(SUBLANES - 1, tm), :] * cw[0:1] + u_ref[pl.ds(SUBLANES, tm), :] * cw[1:2]
         + u_ref[pl.ds(SUBLANES + 1, tm), :] * cw[2:3])
    m = (p[:, :d] * y).astype(BF16)
    out = x + gt_ref[0] * _dot(m, wout_ref[...])
    o_ref[0] = out
    _route(out, g2_ref, sh2_ref, sc2_ref, wr_ref, h_ref, aff_ref)


def sc_mixer(x, g, sh, sc, w_in, conv_w, w_out, gate, route):
    b, l, d = x.shape
    g2, sh2, sc2, wr = route
    tm = _row_tile(l, 512)
    hb = tm // SUBLANES
    nhb = l // SUBLANES
    vec = pl.BlockSpec((1, 1, d), lambda i, j: (i, 0, 0))
    r_in, r_out, r_shape = _route_specs(b, l, d, wr.shape[0], tm)
    return pl.pallas_call(
        functools.partial(_sc_mixer_kernel, tm=tm),
        grid=(b, l // tm),
        in_specs=[pl.BlockSpec((1, tm, d), lambda i, j: (i, j, 0)),
                  pl.BlockSpec((1, SUBLANES, d), lambda i, j: (i, jnp.maximum(j * hb - 1, 0), 0)),
                  pl.BlockSpec((1, SUBLANES, d), lambda i, j: (i, jnp.minimum((j + 1) * hb, nhb - 1), 0)),
                  pl.BlockSpec((1, d), lambda i, j: (0, 0)), vec, vec,
                  pl.BlockSpec((d, 3 * d), lambda i, j: (0, 0)),
                  pl.BlockSpec((CONV_W, d), lambda i, j: (0, 0)),
                  pl.BlockSpec((d, d), lambda i, j: (0, 0)),
                  vec] + r_in,
        out_specs=[pl.BlockSpec((1, tm, d), lambda i, j: (i, j, 0))] + r_out,
        out_shape=[jax.ShapeDtypeStruct((b, l, d), F32)] + r_shape,
        scratch_shapes=[pltpu.VMEM((tm + 2 * SUBLANES, d), F32)],
        compiler_params=_cparams("arbitrary", "arbitrary"),
        name="sc_mixer",
    )(x, x, x, g.reshape(1, d), sh, sc, w_in, conv_w, w_out, gate, g2.reshape(1, d), sh2, sc2, wr)


def _hy_filter_kernel(z_ref, w1_ref, b1_ref, w2_ref, b2_ref, fr_ref, w3f_ref, w3b_ref, t_ref, dl_ref,
                      of_ref, ob_ref, a_ref):
    @pl.when((pl.program_id(0) == 0) & (pl.program_id(1) == 0))
    def _():
        fr = fr_ref[...]
        a1 = jnp.sin(fr * (jnp.dot(z_ref[...], w1_ref[...], preferred_element_type=F32, precision=HIGHEST)
                           + b1_ref[...]))
        a_ref[...] = jnp.sin(fr * (jnp.dot(a1, w2_ref[...], preferred_element_type=F32, precision=HIGHEST)
                                   + b2_ref[...]))

    a = a_ref[...]
    decay = jnp.exp(-t_ref[...] * dl_ref[...])
    hf = jnp.dot(a, w3f_ref[...], preferred_element_type=F32, precision=HIGHEST) * decay
    hb = jnp.dot(a, w3b_ref[...], preferred_element_type=F32, precision=HIGHEST) * decay
    row = lax.broadcasted_iota(I32, hb.shape, 0)
    hb = jnp.where(row == 0, 0.0, hb)
    inv = 1.0 / (jnp.sum(jnp.abs(hf), axis=0, keepdims=True) + jnp.sum(jnp.abs(hb), axis=0, keepdims=True))
    of_ref[0] = hf * inv
    ob_ref[0] = hb * inv


def hy_filter_taps(l, d, f_w1, f_b1, f_w2, f_b2, f_w3, sin_freq):
    hid = f_w1.shape[1]
    hp = LANES
    t = np.linspace(0.0, 1.0, l)[:, None]
    bands = np.linspace(1e-4, HY_BANDS - 1, HY_BANDS)[None, :]
    ang = (2.0 * math.pi / l) * np.arange(l)[:, None] * bands
    z = np.zeros((l, hp), np.float64)
    z[:, :HY_EMB] = np.concatenate([t, np.cos(ang), -np.sin(ang)], axis=-1)
    deltas = np.abs(np.linspace(math.log(HY_TARGET) / HY_SLOW_DECAY, math.log(HY_TARGET) / HY_FAST_DECAY, d))[None, :]

    def pad2(a, r, c):
        return jnp.zeros((r, c), F32).at[:a.shape[0], :a.shape[1]].set(a.astype(F32))

    w1 = pad2(f_w1, hp, hp)
    w2 = pad2(f_w2, hp, hp)
    w3 = pad2(f_w3, hp, f_w3.shape[1])
    b1 = pad2(f_b1[None], 1, hp)
    b2 = pad2(f_b2[None], 1, hp)
    fr = pad2(sin_freq[None], 1, hp)
    tc = min(d, 2 * LANES)
    nd = d // tc
    full = lambda o, j: (0, 0)
    return pl.pallas_call(
        _hy_filter_kernel,
        grid=(HY_ORDER, nd),
        in_specs=[pl.BlockSpec((l, hp), full), pl.BlockSpec((hp, hp), full), pl.BlockSpec((1, hp), full),
                  pl.BlockSpec((hp, hp), full), pl.BlockSpec((1, hp), full), pl.BlockSpec((1, hp), full),
                  pl.BlockSpec((hp, tc), lambda o, j: (0, o * nd + j)),
                  pl.BlockSpec((hp, tc), lambda o, j: (0, (HY_ORDER + o) * nd + j)),
                  pl.BlockSpec((l, 1), full),
                  pl.BlockSpec((1, tc), lambda o, j: (0, j))],
        out_specs=[pl.BlockSpec((1, l, tc), lambda o, j: (o, 0, j)),
                   pl.BlockSpec((1, l, tc), lambda o, j: (o, 0, j))],
        out_shape=[jax.ShapeDtypeStruct((HY_ORDER, l, d), F32)] * 2,
        scratch_shapes=[pltpu.VMEM((l, hp), F32)],
        compiler_params=_cparams("arbitrary", "arbitrary"),
        name="hy_filter_taps",
    )(jnp.asarray(z, F32), w1, b1, w2, b2, fr, w3, w3, jnp.asarray(t, F32), jnp.asarray(deltas, F32))


def _dft_consts(l):
    n = 2 * l
    if n <= 4 * FFT_N2:
        n1, n2 = 1, n
    else:
        n1, n2 = n // FFT_N2, FFT_N2
    if n1 == 1:
        k = np.arange(n)[:, None]
        t = np.arange(l)[None, :]
        ang = 2.0 * np.pi * (k * t % n) / n
        c, s = np.cos(ang), np.sin(ang)
        fwd = np.block([[c, s], [-s, c]])
        inv = np.block([[c.T, -s.T], [s.T, c.T]]) / n
        return n1, n2, dict(fwd=jnp.asarray(fwd, BF16), inv=jnp.asarray(inv, BF16))
    n1h = n1 // 2
    k1 = np.arange(n1)[None, :, None]
    m1 = np.arange(n1h)[None, None, :]
    r2 = np.arange(n2)[:, None, None]
    ang = 2.0 * np.pi * ((k1 * (n2 * m1 + r2)) % n) / n
    c, s = np.cos(ang), np.sin(ang)
    m1f = np.concatenate([np.concatenate([c, s], 2), np.concatenate([-s, c], 2)], 1)
    ct, st = np.swapaxes(c, 1, 2) / n, np.swapaxes(s, 1, 2) / n
    m1i = np.concatenate([np.concatenate([ct, -st], 2), np.concatenate([st, ct], 2)], 1)
    a2 = 2.0 * np.pi * ((np.arange(n2)[:, None] * np.arange(n2)[None, :]) % n2) / n2
    c2, s2 = np.cos(a2), np.sin(a2)
    f2 = np.block([[c2, s2], [-s2, c2]])
    f2i = np.block([[c2, -s2], [s2, c2]])

    def il(m):
        return np.stack([np.arange(m), m + np.arange(m)], 1).reshape(-1)

    m1f = m1f[:, il(n1), :][:, :, il(n1h)]
    m1i = m1i[:, il(n1h), :][:, :, il(n1)]
    f2 = f2[:, il(n2)]
    f2i = f2i[il(n2), :][:, il(n2)]
    half = n2 // 2
    m1f = np.concatenate([m1f[:half], m1f[half:]], 2)
    m1i = np.concatenate([m1i[:half], m1i[half:]], 2)
    return n1, n2, dict(m1f=jnp.asarray(m1f, BF16), m1i=jnp.asarray(m1i, BF16),
                        f2=jnp.asarray(f2, BF16), f2i=jnp.asarray(f2i, BF16))


def _blockdiag(a, b):
    z = jnp.zeros_like(a)
    return jnp.concatenate([jnp.concatenate([a, z], axis=1), jnp.concatenate([z, b], axis=1)], axis=0)


def _halves(y):
    return y[:, :LANES], y[:, LANES:]


HI_HALF = -65536


def _pack_pair(re, im):
    lo = (pltpu.bitcast(re.astype(BF16).astype(F32), I32) >> 16) & 0xFFFF
    if im is None:
        return lo
    return lo | (pltpu.bitcast(im.astype(BF16).astype(F32), I32) & HI_HALF)


def _unpack_pair(w):
    return pltpu.bitcast(w << 16, F32), pltpu.bitcast(w & HI_HALF, F32)


def _as_rows(w):
    return pltpu.bitcast(w, BF16)


def _as_words(y):
    return pltpu.bitcast(y.astype(BF16), I32)


def _fwd_stage1(x_pk, a_pk, m1f_ref, n1, n2):
    n1h, half = n1 // 2, n2 // 2

    def column(r):
        return _as_rows(x_pk[pl.ds(r, n1h, stride=n2), :])

    def body(r, carry):
        w = _as_words(_dot(m1f_ref[r], _blockdiag(column(r), column(r + half))))
        for rr, ww in zip((r, r + half), _halves(w)):
            a_pk[pl.ds(pl.multiple_of(rr * n1, n1), n1), :] = ww
        return carry

    lax.fori_loop(0, half, body, 0, unroll=min(half, DFT_UNROLL_STAGE1))


def _inv_stage1(b_pk, x_pk, m1i_ref, n1, n2):
    n1h, half = n1 // 2, n2 // 2

    def column(r):
        return _as_rows(b_pk[pl.ds(pl.multiple_of(r * n1, n1), n1), :])

    def body(r, carry):
        w = _as_words(_dot(m1i_ref[r], _blockdiag(column(r), column(r + half))))
        for rr, ww in zip((r, r + half), _halves(w)):
            x_pk[pl.ds(rr, n1h, stride=n2), :] = ww
        return carry

    lax.fori_loop(0, half, body, 0, unroll=min(half, DFT_UNROLL_STAGE1))


def _load_chunk(a_pk, k1, n1, n2):
    ra = pl.ds(k1, n2, stride=n1)
    rb = pl.ds(k1 + n1 // 2, n2, stride=n1)
    return jnp.concatenate([_as_rows(a_pk[ra, :]), _as_rows(a_pk[rb, :])], axis=1)


def _hy_spec_kernel(*refs, l, n1, n2, dense):
    if dense:
        tf_ref, tb_ref, sk_ref, fwd_ref, o_ref = refs
        n = n2
        fw = fwd_ref[:, 0:l]
        sf = _dot(fw, tf_ref[0].astype(BF16))
        sb = _dot(fw, tb_ref[0].astype(BF16))
        o_ref[0, 0] = (sf[:n] + sb[:n] + sk_ref[0]).astype(o_ref.dtype)
        o_ref[0, 1] = (sf[n:] - sb[n:]).astype(o_ref.dtype)
        return
    tf_ref, tb_ref, sk_ref, m1f_ref, f2_ref, o_ref, x_pk, a_pk, h_re, h_im = refs
    for run, t_ref in enumerate((tf_ref, tb_ref)):
        x_pk[...] = _pack_pair(t_ref[0], None)
        _fwd_stage1(x_pk, a_pk, m1f_ref, n1, n2)

        def body(k1, carry, run=run):
            x = _dot(f2_ref[...], _load_chunk(a_pk, k1, n1, n2))
            for kk, xx in zip((k1, k1 + n1 // 2), _halves(x)):
                rows = pl.ds(pl.multiple_of(kk * n2, n2), n2)
                if run == 0:
                    h_re[rows, :] = xx[:n2]
                    h_im[rows, :] = xx[n2:]
                else:
                    h_re[rows, :] = h_re[rows, :] + xx[:n2]
                    h_im[rows, :] = h_im[rows, :] - xx[n2:]
            return carry

        lax.fori_loop(0, n1 // 2, body, 0, unroll=DFT_UNROLL)
    o_ref[0, 0] = (h_re[...] + sk_ref[0]).astype(o_ref.dtype)
    o_ref[0, 1] = h_im[...].astype(o_ref.dtype)


def hy_spectrum(taps_f, taps_b, skip, consts, n1, n2):
    o, l, d = taps_f.shape
    n = 2 * l
    dt = HY_DT
    dense = n1 == 1
    tap_spec = pl.BlockSpec((1, l, dt), lambda i, j: (i, 0, j))
    in_specs = [tap_spec, tap_spec, pl.BlockSpec((1, 1, dt), lambda i, j: (i, 0, j))]
    if dense:
        mats = [consts["fwd"]]
        scratch = []
    else:
        mats = [consts["m1f"], consts["f2"]]
        scratch = [pltpu.VMEM((l, dt), I32), pltpu.VMEM((n, dt), I32)] + [pltpu.VMEM((n, dt), F32)] * 2
    in_specs += [pl.BlockSpec(m.shape, lambda i, j, nd=m.ndim: (0,) * nd) for m in mats]
    return pl.pallas_call(
        functools.partial(_hy_spec_kernel, l=l, n1=n1, n2=n2, dense=dense),
        grid=(o, d // dt),
        in_specs=in_specs,
        out_specs=pl.BlockSpec((1, 2, n, dt), lambda i, j: (i, 0, 0, j)),
        out_shape=jax.ShapeDtypeStruct((o, 2, n, d), BF16),
        scratch_shapes=scratch,
        compiler_params=_cparams("arbitrary", "arbitrary"),
        name="hy_spectrum",
    )(taps_f, taps_b, skip.reshape(o, 1, d), *mats)


def _conv3_rows(src, cw, pad_ref, l):
    c = pad_ref.shape[1]
    pad_ref[pl.ds(0, SUBLANES), :] = jnp.zeros((SUBLANES, c), F32)
    pad_ref[pl.ds(SUBLANES + l, SUBLANES), :] = jnp.zeros((SUBLANES, c), F32)
    pad_ref[pl.ds(SUBLANES, l), :] = src
    return (pad_ref[pl.ds(SUBLANES - 1, l), :] * cw[0:1] + pad_ref[pl.ds(SUBLANES, l), :] * cw[1:2]
            + pad_ref[pl.ds(SUBLANES + 1, l), :] * cw[2:3])


def _hy_conv_kernel(*refs, l, n1, n2, dense, conv_z):
    if dense:
        z_ref, g_ref, cwz_ref, cwg_ref, h_ref, fwd_ref, inv_ref, o_ref, pad_ref = refs
    else:
        (z_ref, g_ref, cwz_ref, cwg_ref, h_ref, m1f_ref, m1i_ref, f2_ref, f2i_ref, o_ref,
         x_pk, a_pk, b_pk, pad_ref) = refs
    zs = []
    for e in range(2):
        z = z_ref[e, 0].astype(F32)
        zs.append(_conv3_rows(z, cwz_ref[...], pad_ref, l) if conv_z else z)
    if dense:
        n = n2
        s = _dot(fwd_ref[...], jnp.concatenate(zs, axis=0).astype(BF16))
        hr = h_ref[0, 0].astype(F32)
        hi = h_ref[0, 1].astype(F32)
        zr = s[:n] * hr - s[n:] * hi
        zi = s[:n] * hi + s[n:] * hr
        y = _dot(inv_ref[...], jnp.concatenate([zr, zi], axis=0).astype(BF16))
        ys = (y[:l], y[l:])
    else:
        x_pk[...] = _pack_pair(zs[0], zs[1])
        _fwd_stage1(x_pk, a_pk, m1f_ref, n1, n2)

        nk = n1 // 2

        def spectrum_step(k1, carry):
            s = _dot(f2_ref[...], _load_chunk(a_pk, k1, n1, n2))
            rows = [pl.ds(pl.multiple_of(k * n2, n2), n2) for k in (k1, k1 + nk)]
            hr = jnp.concatenate([h_ref[0, 0, r, :] for r in rows], axis=1).astype(F32)
            hi = jnp.concatenate([h_ref[0, 1, r, :] for r in rows], axis=1).astype(F32)
            zr = s[:n2] * hr - s[n2:] * hi
            zi = s[:n2] * hi + s[n2:] * hr
            for r, zr_half, zi_half in zip(rows, _halves(zr), _halves(zi)):
                b_pk[r, :] = _pack_pair(zr_half, zi_half)
            return carry

        lax.fori_loop(0, nk, spectrum_step, 0, unroll=min(nk, DFT_UNROLL))

        def inverse_step(k1, carry):
            kk = (k1, k1 + nk)
            z = jnp.concatenate([_as_rows(b_pk[pl.ds(pl.multiple_of(k * n2, n2), n2), :]) for k in kk], axis=1)
            w = _as_words(_dot(f2i_ref[...], z))
            for k, ww in zip(kk, _halves(w)):
                a_pk[pl.ds(k, n2, stride=n1), :] = ww
            return carry

        lax.fori_loop(0, nk, inverse_step, 0, unroll=min(nk, DFT_UNROLL))
        _inv_stage1(a_pk, x_pk, m1i_ref, n1, n2)
        ys = _unpack_pair(x_pk[...])
    for e in range(2):
        g = _conv3_rows(g_ref[e, 0].astype(F32), cwg_ref[...], pad_ref, l)
        o_ref[e, 0] = (g * ys[e]).astype(o_ref.dtype)


def hy_conv(zsrc, zcol, p4, gcol, conv_w, spec, order, consts, n1, n2, conv_z):
    _, pairs, l, _ = zsrc.shape
    d = spec.shape[3]
    n = 2 * l
    dt = HY_DT
    nd = d // dt
    dense = n1 == 1
    once = pl.Buffered(1)
    in_specs = [pl.BlockSpec((2, 1, l, dt), lambda j, i: (0, i, 0, zcol * nd + j)),
                pl.BlockSpec((2, 1, l, dt), lambda j, i: (0, i, 0, gcol * nd + j)),
                pl.BlockSpec((CONV_W, dt), lambda j, i: (0, j)),
                pl.BlockSpec((CONV_W, dt), lambda j, i: (0, gcol * nd + j)),
                pl.BlockSpec((1, 2, n, dt), lambda j, i: (order, 0, 0, j), pipeline_mode=once)]
    if dense:
        mats = [consts["fwd"], consts["inv"]]
        scratch = []
    else:
        mats = [consts["m1f"], consts["m1i"], consts["f2"], consts["f2i"]]
        scratch = [pltpu.VMEM((l, dt), I32)] + [pltpu.VMEM((n, dt), I32)] * 2
    scratch = scratch + [pltpu.VMEM((l + 2 * SUBLANES, dt), F32)]
    in_specs += [pl.BlockSpec(m.shape, lambda j, i, nd_=m.ndim: (0,) * nd_, pipeline_mode=once) for m in mats]
    return pl.pallas_call(
        functools.partial(_hy_conv_kernel, l=l, n1=n1, n2=n2, dense=dense, conv_z=conv_z),
        grid=(nd, pairs),
        in_specs=in_specs,
        out_specs=pl.BlockSpec((2, 1, l, dt), lambda j, i: (0, i, 0, j)),
        out_shape=jax.ShapeDtypeStruct((2, pairs, l, d), BF16),
        scratch_shapes=scratch,
        compiler_params=_cparams("arbitrary", "arbitrary"),
        name="hy_conv",
    )(zsrc, p4, conv_w, conv_w, spec, *mats)


def hyena_mixer(p, conv_w, f_w1, f_b1, f_w2, f_b2, f_w3, sin_freq, skip):
    b, l, d3 = p.shape
    d = d3 // 3
    assert b % 2 == 0 and d % HY_DT == 0
    n1, n2, consts = _dft_consts(l)
    taps_f, taps_b = hy_filter_taps(l, d, f_w1, f_b1, f_w2, f_b2, f_w3, sin_freq)
    spec = hy_spectrum(taps_f, taps_b, skip, consts, n1, n2)
    p4 = p.reshape(2, b // 2, l, d3)
    z = hy_conv(p4, 0, p4, 1, conv_w, spec, 0, consts, n1, n2, True)
    z = hy_conv(z, 0, p4, 2, conv_w, spec, 1, consts, n1, n2, False)
    return z.reshape(b, l, d)


def _head_norm(x, g):
    sq = x * x
    hi = sq.astype(BF16)
    lo = (sq - hi.astype(F32)).astype(BF16)
    ones = jnp.ones((HEAD_DIM, HEAD_DIM), BF16)
    total = _dot(hi, ones) + _dot(lo, ones)
    return x * lax.rsqrt(total * (1.0 / HEAD_DIM) + EPS) * g


def _rope(x, cos, sin_a, sin_b):
    return (x * cos + pltpu.roll(x, HEAD_DIM // 4, 1) * sin_a
            + pltpu.roll(x, HEAD_DIM - HEAD_DIM // 4, 1) * sin_b)


def _qkv_kernel(x_ref, g_ref, sh_ref, sc_ref, w_ref, qg_ref, kg_ref, cos_ref, sa_ref, sb_ref,
                *out_refs, nq, rope):
    q_ref = out_refs[0] if nq else None
    k_ref, v_ref = out_refs[-2:]
    h = _norm_mod(x_ref[0], g_ref[...], sh_ref[0], sc_ref[0]).astype(BF16)
    acc = _dot(h, w_ref[...])
    hd = HEAD_DIM
    for i in range(nq + N_KV_HEADS):
        t = acc[:, i * hd:(i + 1) * hd]
        t = _head_norm(t, qg_ref[...] if i < nq else kg_ref[...])
        if rope:
            t = _rope(t, cos_ref[...], sa_ref[...], sb_ref[...])
        if i < nq:
            q_ref[0, :, i * hd:(i + 1) * hd] = (t * Q_PRESCALE).astype(BF16)
        else:
            k_ref[0, :, (i - nq) * hd:(i - nq + 1) * hd] = t.astype(BF16)
    v_ref[0] = acc[:, (nq + N_KV_HEADS) * hd:].astype(BF16)


def qkv_project(x, g, sh, sc, w, q_g, k_g, rope_tabs, nq, rope):
    b, l, d = x.shape
    n = w.shape[1]
    hd = HEAD_DIM
    kvd = N_KV_HEADS * hd
    tm = _row_tile(l)
    cos, sa, sb = rope_tabs
    tab = pl.BlockSpec((tm, hd), lambda i, j: (j, 0))
    vec = pl.BlockSpec((1, 1, d), lambda i, j: (i, 0, 0))
    widths = ([nq * hd] if nq else []) + [kvd, kvd]
    return pl.pallas_call(
        functools.partial(_qkv_kernel, nq=nq, rope=rope),
        grid=(b, l // tm),
        in_specs=[pl.BlockSpec((1, tm, d), lambda i, j: (i, j, 0)),
                  pl.BlockSpec((1, d), lambda i, j: (0, 0)), vec, vec,
                  pl.BlockSpec((d, n), lambda i, j: (0, 0)),
                  pl.BlockSpec((1, hd), lambda i, j: (0, 0)),
                  pl.BlockSpec((1, hd), lambda i, j: (0, 0)),
                  tab, tab, tab],
        out_specs=[pl.BlockSpec((1, tm, wd), lambda i, j: (i, j, 0)) for wd in widths],
        out_shape=[jax.ShapeDtypeStruct((b, l, wd), BF16) for wd in widths],
        compiler_params=_cparams("arbitrary", "arbitrary"),
        name="qkv_project",
    )(x, g.reshape(1, d), sh, sc, w, q_g.reshape(1, hd), k_g.reshape(1, hd), cos, sa, sb)


def _rope_tables(l):
    rows = l // GRID_W
    row = np.repeat(np.arange(rows, dtype=np.float64), GRID_W)
    col = np.tile(np.arange(GRID_W, dtype=np.float64), rows)
    axis = HEAD_DIM // 2
    inv = ROPE_THETA ** (-np.arange(0, axis, 2, dtype=np.float64) / axis)

    def axis_angles(pos):
        a = pos[:, None] * inv[None, :]
        return np.concatenate([a, a], axis=-1)

    ang = np.concatenate([axis_angles(row), axis_angles(col)], axis=-1)
    cos, sin = np.cos(ang), np.sin(ang)
    lane = np.arange(HEAD_DIM)[None, :] % axis
    upper = lane >= axis // 2
    sin_a = np.where(upper, sin, 0.0)
    sin_b = np.where(upper, 0.0, -sin)
    return tuple(jnp.asarray(t, F32) for t in (cos, sin_a, sin_b))


def _attn_kernel(q_ref, k_ref, vt_ref, o_ref, *, groups):
    k = k_ref[0]
    vt = vt_ref[0]
    hd = HEAD_DIM
    tq = q_ref.shape[1]
    hp = 2 if groups % 2 == 0 else 1
    def scores(g0):
        q = jnp.concatenate([q_ref[0, :, g * hd:(g + 1) * hd] for g in range(g0, g0 + hp)], axis=0)
        return lax.dot_general(k, q, (((1,), (1,)), ((), ())), preferred_element_type=F32)

    sts = [scores(g0) for g0 in range(0, groups, hp)]
    for g0, st in zip(range(0, groups, hp), sts):
        m = jnp.max(st, axis=0, keepdims=True)
        pt = jnp.exp2(st - m)
        den = jnp.sum(pt, axis=0, keepdims=True)
        ot = _dot(vt, pt.astype(BF16)) / den
        for i in range(hp):
            g = g0 + i
            o_ref[0, :, g * hd:(g + 1) * hd] = ot[:, i * tq:(i + 1) * tq].T.astype(o_ref.dtype)


def attention(q, k, v):
    b, l, qd = q.shape
    s = k.shape[1]
    hd = HEAD_DIM
    groups = qd // hd // N_KV_HEADS
    tq = _row_tile(l, 256)
    vt = jnp.swapaxes(v, 1, 2)
    return pl.pallas_call(
        functools.partial(_attn_kernel, groups=groups),
        grid=(b, N_KV_HEADS, l // tq),
        in_specs=[pl.BlockSpec((1, tq, groups * hd), lambda i, h, j: (i, j, h)),
                  pl.BlockSpec((1, s, hd), lambda i, h, j: (i, 0, h)),
                  pl.BlockSpec((1, hd, s), lambda i, h, j: (i, h, 0))],
        out_specs=pl.BlockSpec((1, tq, groups * hd), lambda i, h, j: (i, j, h)),
        out_shape=jax.ShapeDtypeStruct((b, l, qd), BF16),
        compiler_params=_cparams("arbitrary", "arbitrary", "arbitrary"),
        name="attention",
    )(q, k, vt)


def _route(x, g_ref, sh_ref, sc_ref, wr_ref, h_ref, a_ref):
    h = _norm_mod(x, g_ref[...], sh_ref[0], sc_ref[0])
    d = h.shape[1]
    logits = lax.dot_general(wr_ref[...], h, (((1,), (1,)), ((), ())), preferred_element_type=F32,
                             precision=HIGHEST)
    ex = jnp.exp(logits - jnp.max(logits, axis=0, keepdims=True))
    aff = ex / jnp.sum(ex, axis=0, keepdims=True)
    a_ref[0] = aff
    ne, tm = aff.shape
    hi = aff.astype(BF16).astype(F32)
    mid = (aff - hi).astype(BF16).astype(F32)
    lo = (aff - hi - mid).astype(BF16).astype(F32)
    pieces = jnp.concatenate([hi, mid, lo, jnp.zeros((LANES - 3 * ne, tm), F32)], axis=0)
    h_ref[0, :, 0:d] = h.astype(BF16)
    h_ref[0, :, d:d + LANES] = pieces.T.astype(BF16)


def _cumsum_lanes(m, tri_ref, ind_ref):
    e, n = m.shape
    mb = m.astype(BF16)
    before = _dot(mb, ind_ref[...])
    outs = []
    for c in range(n // LANES):
        outs.append(_dot(mb[:, c * LANES:(c + 1) * LANES], tri_ref[...]) + before[:, c:c + 1])
    return jnp.concatenate(outs, axis=1), before


def _select_kernel(a_ref, tri_ref, ind_ref, slot_ref, before_ref, slot_t_ref, *, cap):
    aff = a_ref[0]
    bits = pltpu.bitcast(aff, I32)
    e = aff.shape[0]

    def body(i, thr):
        cand = thr | (1 << (30 - i))
        cnt = jnp.sum((bits >= cand).astype(F32), axis=1, keepdims=True)
        return jnp.where(cnt >= cap, cand, thr)

    thr = lax.fori_loop(0, 31, body, jnp.zeros((e, 1), I32))
    gt = bits > thr
    eq = bits == thr
    n_gt = jnp.sum(gt.astype(F32), axis=1, keepdims=True)
    eq_rank, _ = _cumsum_lanes(eq.astype(F32), tri_ref, ind_ref)
    sel = gt | (eq & (eq_rank <= cap - n_gt))
    pos, before = _cumsum_lanes(sel.astype(F32), tri_ref, ind_ref)
    slot = jnp.where(sel, pos - 1.0, -1.0)
    slot_ref[0] = slot
    before_ref[0] = before
    slot_t_ref[0] = slot.T


def moe_select(aff, cap):
    b, e, n = aff.shape
    assert n // LANES <= LANES
    tri = jnp.asarray(np.triu(np.ones((LANES, LANES), np.float32)), BF16)
    ind = jnp.asarray(np.arange(n)[:, None] < LANES * np.arange(LANES)[None, :], BF16)
    row = pl.BlockSpec((1, e, n), lambda i: (i, 0, 0))
    col = pl.BlockSpec((1, n, e), lambda i: (i, 0, 0))
    return pl.pallas_call(
        functools.partial(_select_kernel, cap=cap),
        grid=(b,),
        in_specs=[row, pl.BlockSpec((LANES, LANES), lambda i: (0, 0)), pl.BlockSpec((n, LANES), lambda i: (0, 0))],
        out_specs=[row, pl.BlockSpec((1, e, LANES), lambda i: (i, 0, 0)), col],
        out_shape=[jax.ShapeDtypeStruct((b, e, n), F32), jax.ShapeDtypeStruct((b, e, LANES), F32),
                   jax.ShapeDtypeStruct((b, n, e), F32)],
        compiler_params=_cparams("arbitrary"),
        name="moe_select",
    )(aff, tri, ind)


class _Windows:
    def __init__(self, before, n, cap):
        self.tile = min(n, MOE_TILE)
        self.nt = n // self.tile
        self.narrow = sorted({w for w in (min(cap, w) for w in MOE_WINDOWS)
                              if (w % LANES == 0 or LANES % w == 0) and w % BF16_ROWS == 0})
        self.wide = min(cap, self.tile + BF16_ROWS)
        first = before[:, :, 0:n // LANES:self.tile // LANES].astype(I32)
        count = jnp.concatenate([first[:, :, 1:], jnp.full_like(first[:, :, :1], cap)], axis=2) - first
        aligned = first // BF16_ROWS * BF16_ROWS
        starts = [jnp.clip(aligned, 0, cap - w) for w in self.narrow + [self.wide]]
        level = jnp.zeros(first.shape[:1] + first.shape[2:], I32)
        for i in reversed(range(len(self.narrow))):
            fits = jnp.all(first - starts[i] + count <= self.narrow[i], axis=1)
            level = jnp.where(fits, i + 1, level)
        self.args = tuple(s.reshape(-1) for s in starts) + (level.reshape(-1),)


def _gather_kernel(*refs, ne, narrow, wide):
    narrow_refs = refs[:len(narrow)]
    ww_ref, level_ref, slot_ref, h_ref, o_ref = refs[len(narrow):]
    bi, gi, ki = pl.program_id(0), pl.program_id(1), pl.program_id(2)
    nt = pl.num_programs(2)
    group, tk = slot_ref.shape[1], slot_ref.shape[2]
    level = level_ref[bi * nt + ki]

    @pl.when(ki == 0)
    def _():
        o_ref[...] = jnp.zeros_like(o_ref)

    slot = slot_ref[0].astype(I32)

    def start(ref, j):
        return pl.multiple_of(ref[(bi * ne + gi * group + j) * nt + ki], BF16_ROWS)

    def onehot(j, ws, width):
        jcol = lax.broadcasted_iota(I32, (width, tk), 0)
        return jnp.where(jcol + ws == slot[j:j + 1, :], 1.0, 0.0).astype(BF16)

    def add(j, ws, width, rows_f32):
        rows = pl.ds(ws, width)
        o_ref[0, j, rows, :] = o_ref[0, j, rows, :] + rows_f32.astype(o_ref.dtype)

    for i, (wn_ref, width) in enumerate(zip(narrow_refs, narrow)):
        @pl.when(level == i + 1)
        def _(wn_ref=wn_ref, width=width):
            starts = [start(wn_ref, j) for j in range(group)]
            lhs = jnp.concatenate([onehot(j, starts[j], width) for j in range(group)], axis=0)
            res = _dot(lhs, h_ref[0])
            for j in range(group):
                add(j, starts[j], width, res[j * width:(j + 1) * width])

    @pl.when(level == 0)
    def _():
        for j in range(group):
            ws = start(ww_ref, j)
            add(j, ws, wide, _dot(onehot(j, ws, wide), h_ref[0]))


def moe_gather(slot, win, h, cap):
    b, e, n = slot.shape
    dx = h.shape[2]
    group = math.gcd(e, MOE_GROUP)
    grid_spec = pltpu.PrefetchScalarGridSpec(
        num_scalar_prefetch=len(win.args),
        grid=(b, e // group, win.nt),
        in_specs=[pl.BlockSpec((1, group, win.tile), lambda i, g, k, *_: (i, g, k)),
                  pl.BlockSpec((1, win.tile, dx), lambda i, g, k, *_: (i, k, 0))],
        out_specs=pl.BlockSpec((1, group, cap, dx), lambda i, g, k, *_: (i, g, 0, 0)))
    return pl.pallas_call(
        functools.partial(_gather_kernel, ne=e, narrow=win.narrow, wide=win.wide),
        grid_spec=grid_spec,
        out_shape=jax.ShapeDtypeStruct((b, e, cap, dx), BF16),
        compiler_params=_cparams("arbitrary", "arbitrary", "arbitrary"),
        name="moe_gather",
    )(*win.args, slot, h)


def _ffn_kernel(*refs, ns, layer):
    x_refs = refs[:ns]
    w_hbm = refs[ns:ns + 3]
    o_refs = refs[ns + 3:2 * ns + 3]
    w_bf = refs[2 * ns + 3:2 * ns + 6]
    w_stage = refs[2 * ns + 6:2 * ns + 9]
    sem = refs[2 * ns + 9]
    cur = pl.program_id(0) % 2

    def fetch(expert, buf):
        return [pltpu.make_async_copy(w.at[layer, expert], st.at[buf], sem.at[buf, k])
                for k, (w, st) in enumerate(zip(w_hbm, w_stage))]

    @pl.when(pl.program_id(1) == 0)
    def _():
        @pl.when(pl.program_id(0) == 0)
        def _():
            for c in fetch(0, 0):
                c.start()

        for c in fetch(pl.program_id(0), cur):
            c.wait()

        @pl.when(pl.program_id(0) + 1 < pl.num_programs(0))
        def _():
            for c in fetch(pl.program_id(0) + 1, 1 - cur):
                c.start()

        for dst, st in zip(w_bf, w_stage):
            dst[...] = st[cur].astype(BF16)

    wgb_ref, wub_ref, wdb_ref = w_bf
    d = wgb_ref.shape[0]
    ei, ne = pl.program_id(0), pl.num_programs(0)

    chunks = [[(0, i, r, min(r + FFN_ROWS, x_refs[0].shape[2]))]
              for i in range(x_refs[0].shape[0]) for r in range(0, x_refs[0].shape[2], FFN_ROWS)]
    for s in range(1, ns):
        chunks[-1] += [(s, i, 0, x_refs[s].shape[2]) for i in range(x_refs[s].shape[0])]

    def rows_of(chunk, c0, c1):
        parts = [x_refs[s][i, 0, r0:r1, c0:c1] for s, i, r0, r1 in chunk]
        return parts[0] if len(parts) == 1 else jnp.concatenate(parts, axis=0)

    ups = []
    for chunk in chunks:
        x = rows_of(chunk, 0, d)
        ups.append((_dot(x, wgb_ref[...]), _dot(x, wub_ref[...])))
    for chunk, (a, u) in zip(chunks, ups):
        ext = rows_of(chunk, d, d + LANES).astype(F32)
        lane = lax.broadcasted_iota(I32, ext.shape, 1)
        mine = (lane == ei) | (lane == ei + ne) | (lane == ei + 2 * ne)
        gate = jnp.sum(jnp.where(mine, ext, 0.0), axis=1, keepdims=True)
        y = _dot(((a / (1.0 + jnp.exp(-a))) * u).astype(BF16), wdb_ref[...]) * gate
        row = 0
        for s, i, r0, r1 in chunk:
            o_refs[s][i, 0, r0:r1, :] = y[row:row + r1 - r0].astype(o_refs[s].dtype)
            row += r1 - r0


def moe_ffn(xs_list, w_gate, w_up, w_down, layer):
    b, e, _, dx = xs_list[0].shape
    d, f = w_gate.shape[2], w_gate.shape[3]
    bb = 2 if b % 2 == 0 else 1

    def specs(width):
        return [pl.BlockSpec((bb, 1, xs.shape[2], width), lambda j, i: (i, j, 0, 0)) for xs in xs_list]

    return pl.pallas_call(
        functools.partial(_ffn_kernel, ns=len(xs_list), layer=layer),
        grid=(e, b // bb),
        in_specs=specs(dx) + [pl.BlockSpec(memory_space=pl.ANY)] * 3,
        out_specs=specs(d),
        out_shape=[jax.ShapeDtypeStruct(xs.shape[:3] + (d,), BF16) for xs in xs_list],
        scratch_shapes=[pltpu.VMEM((d, f), BF16), pltpu.VMEM((d, f), BF16), pltpu.VMEM((f, d), BF16),
                        pltpu.VMEM((2, d, f), F32), pltpu.VMEM((2, d, f), F32), pltpu.VMEM((2, f, d), F32),
                        pltpu.SemaphoreType.DMA((2, 3))],
        compiler_params=_cparams("arbitrary", "arbitrary"),
        name="moe_ffn",
    )(*xs_list, w_gate, w_up, w_down)


def _combine_kernel(*refs, ne, narrow, wide):
    narrow_refs = refs[:len(narrow)]
    ww_ref, level_ref, slot_ref, ys_hbm, lat_ref, gt_ref, o_ref, ys_buf, sem = refs[len(narrow):]
    bi, ti = pl.program_id(0), pl.program_id(1)
    nb, nt = pl.num_programs(0), pl.num_programs(1)
    cur = bi % 2

    def fetch(batch, buf):
        return pltpu.make_async_copy(ys_hbm.at[batch], ys_buf.at[buf], sem.at[buf])

    @pl.when(ti == 0)
    def _():
        @pl.when(bi == 0)
        def _():
            fetch(0, 0).start()

        fetch(bi, cur).wait()

        @pl.when(bi + 1 < nb)
        def _():
            fetch(bi + 1, 1 - cur).start()

    tn = lat_ref.shape[1]
    slot = slot_ref[0].astype(I32)

    def start(ref, e):
        return pl.multiple_of(ref[(bi * ne + e) * nt + ti], BF16_ROWS)

    def onehot(e, ws, width):
        jrow = lax.broadcasted_iota(I32, (tn, width), 1)
        return jnp.where(jrow + ws == slot[:, e:e + 1], 1.0, 0.0).astype(BF16)

    def onehot_lane_group(first, starts, width):
        lane = lax.broadcasted_iota(I32, (tn, LANES), 1)
        target = slot[:, first:first + 1] - starts[first]
        for i in range(1, LANES // width):
            e = first + i
            target = jnp.where(lane >= i * width, slot[:, e:e + 1] - (starts[e] - i * width), target)
        return jnp.where(target == lane, 1.0, 0.0).astype(BF16)

    def finish(acc):
        o_ref[0] = lat_ref[0] + gt_ref[0] * acc

    level = level_ref[bi * nt + ti]
    for i, (wn_ref, width) in enumerate(zip(narrow_refs, narrow)):
        @pl.when(level == i + 1)
        def _(wn_ref=wn_ref, width=width):
            starts = [start(wn_ref, e) for e in range(ne)]
            if width % LANES == 0:
                lhs = jnp.concatenate([onehot(e, starts[e], width) for e in range(ne)], axis=1)
            else:
                lhs = jnp.concatenate([onehot_lane_group(e, starts, width)
                                       for e in range(0, ne, LANES // width)], axis=1)
            rhs = jnp.concatenate([ys_buf[cur, e, pl.ds(starts[e], width), :] for e in range(ne)], axis=0)
            finish(_dot(lhs, rhs))

    @pl.when(level == 0)
    def _():
        acc = jnp.zeros((tn, lat_ref.shape[2]), F32)
        for e in range(ne):
            ws = start(ww_ref, e)
            acc = acc + _dot(onehot(e, ws, wide), ys_buf[cur, e, pl.ds(ws, wide), :])
        finish(acc)


def moe_combine(slot_t, win, ys, lat, gate_vec, cap):
    b, n, e = slot_t.shape
    d = lat.shape[2]
    tn = win.tile
    grid_spec = pltpu.PrefetchScalarGridSpec(
        num_scalar_prefetch=len(win.args),
        grid=(b, win.nt),
        in_specs=[pl.BlockSpec((1, tn, e), lambda i, j, *_: (i, j, 0)),
                  pl.BlockSpec(memory_space=pl.ANY),
                  pl.BlockSpec((1, tn, d), lambda i, j, *_: (i, j, 0)),
                  pl.BlockSpec((1, 1, d), lambda i, j, *_: (i, 0, 0))],
        out_specs=pl.BlockSpec((1, tn, d), lambda i, j, *_: (i, j, 0)),
        scratch_shapes=[pltpu.VMEM((2, e, cap, d), BF16), pltpu.SemaphoreType.DMA((2,))])
    return pl.pallas_call(
        functools.partial(_combine_kernel, ne=e, narrow=win.narrow, wide=win.wide),
        grid_spec=grid_spec,
        out_shape=jax.ShapeDtypeStruct((b, n, d), F32),
        compiler_params=_cparams("arbitrary", "arbitrary"),
        name="moe_combine",
    )(*win.args, slot_t, ys, lat, gate_vec)


def expert_choice_moe(streams, w_gate, w_up, w_down, layer):
    routed = []
    for (lat, h, aff), _ in streams:
        e = aff.shape[1]
        assert 3 * e <= LANES and e % SUBLANES == 0
        cap = EC_CAPACITY_FACTOR * lat.shape[1] // e
        slot, before, slot_t = moe_select(aff, cap)
        win = _Windows(before, lat.shape[1], cap)
        routed.append((moe_gather(slot, win, h, cap), slot_t, win, cap))
    ys_list = moe_ffn([r[0] for r in routed], w_gate, w_up, w_down, layer)
    return [moe_combine(slot_t, win, ys, lat, gate_vec, cap)
            for ((lat, _, _), gate_vec), (_, slot_t, win, cap), ys in zip(streams, routed, ys_list)]


def kernel(x, c, ctx, c_ctx, ada_w, ada_b, norm_g, sc_w_in, sc_conv, sc_w_out, hy_w_in, hy_conv_w, hy_f_w1,
           hy_f_b1, hy_f_w2, hy_f_b2, hy_f_w3, hy_sin_freq, hy_skip, hy_w_out, at_w_qkv, at_q_g, at_k_g, at_w_o,
           moe_router_w, moe_w_gate, moe_w_up, moe_w_down):
    b, l, d = x.shape
    depth = ada_w.shape[0]
    assert b + 1 <= MOD_ROWS
    qd = at_w_o.shape[1]
    nq = qd // HEAD_DIM
    attn_layers = list(range(MIX_ATTN, depth, N_MIXERS))
    last_attn = max(attn_layers, default=-1)

    cond = jnp.zeros((MOD_ROWS, d), F32).at[:b].set(c).at[b].set(c_ctx)
    mods = ada_modulation(cond, ada_w, ada_b)

    lat, cx = x, ctx
    for i in range(depth):
        kind = i % N_MIXERS
        j = i // N_MIXERS
        need_ctx = i <= last_attn
        upd_ctx = i < last_attn
        m_l = mods[i, :b].reshape(b, 1, 6, d)
        sh1, sc1, g1, sh2, sc2, g2 = (m_l[:, :, k] for k in range(6))
        if need_ctx:
            m_c = jnp.broadcast_to(mods[i, b].reshape(1, 1, 6, d), (b, 1, 6, d))
            csh1, csc1, cg1, csh2, csc2, cg2 = (m_c[:, :, k] for k in range(6))
        w_router_t = moe_router_w[i].T
        streams = [(lat, sh1, sc1, g1, (norm_g[i, 1], sh2, sc2, w_router_t))]
        if upd_ctx:
            streams.append((cx, csh1, csc1, cg1, (norm_g[i, 1], csh2, csc2, w_router_t)))

        outs = []
        if kind == MIX_SHORTCONV:
            w_in, w_out = sc_w_in[j].astype(BF16), sc_w_out[j].astype(BF16)
            for s, sh, sc, gt, route in streams:
                outs.append(sc_mixer(s, norm_g[i, 0], sh, sc, w_in, sc_conv[j], w_out, gt, route))
        elif kind == MIX_HYENA:
            w_in, w_out = hy_w_in[j].astype(BF16), hy_w_out[j].astype(BF16)
            for s, sh, sc, gt, route in streams:
                p = nm_matmul(s, norm_g[i, 0], sh, sc, w_in)
                z = hyena_mixer(p, hy_conv_w[j], hy_f_w1[j], hy_f_b1[j], hy_f_w2[j], hy_f_b2[j], hy_f_w3[j],
                                hy_sin_freq[j], hy_skip[j])
                outs.append(proj_residual(z, w_out, s, gt, route))
        else:
            w_qkv, w_o = at_w_qkv[j].astype(BF16), at_w_o[j].astype(BF16)
            q_l, k_l, v_l = qkv_project(lat, norm_g[i, 0], sh1, sc1, w_qkv, at_q_g[j], at_k_g[j],
                                        _rope_tables(l), nq, True)
            lc = cx.shape[1]
            dummy = tuple(jnp.zeros((lc, HEAD_DIM), F32) for _ in range(3))
            if upd_ctx:
                q_c, k_c, v_c = qkv_project(cx, norm_g[i, 0], csh1, csc1, w_qkv, at_q_g[j], at_k_g[j],
                                            dummy, nq, False)
            else:
                k_c, v_c = qkv_project(cx, norm_g[i, 0], csh1, csc1, w_qkv[:, qd:], at_q_g[j], at_k_g[j],
                                       dummy, 0, False)
            v_all = jnp.concatenate([v_c, v_l], axis=1)
            o_l = attention(q_l, jnp.concatenate([k_c, k_l], axis=1), v_all)
            outs.append(proj_residual(o_l, w_o, lat, g1, streams[0][4]))
            if upd_ctx:
                outs.append(proj_residual(attention(q_c, k_c, v_c), w_o, cx, cg1, streams[1][4]))

        moe_streams = [(outs[0], g2)]
        if upd_ctx:
            moe_streams.append((outs[1], cg2))
        moe_out = expert_choice_moe(moe_streams, moe_w_gate, moe_w_up, moe_w_down, i)
        lat = moe_out[0]
        if upd_ctx:
            cx = moe_out[1]
    return lat
```
